```python
import math
import jax, jax.numpy as jnp
from jax import lax
import numpy as np

D_MODEL = 1024
BATCH = 8
SEQ = 2048
DEPTH = 2
DEC_BATCH = 128
DEC_SEQ = 8
PAST_LEN = 2048
PAGE_SIZE = 128

N_EVEN = (DEPTH + 1) // 2
N_ODD = DEPTH // 2
EPS = 1e-6
ROPE_THETA = 10000.0
Q_BLOCK = 128
NEG_INF = -1e30
FORCE = 1e4

RET_HEADS = 4
RET_DK = 64
RET_DV = 128
RET_CHUNK = 128
A_WIDTH = RET_HEADS * RET_DV

NSA_HEADS = 8
NSA_KV_HEADS = 2
NSA_DH = 64
NSA_HPG = NSA_HEADS // NSA_KV_HEADS
B_WIDTH = NSA_HEADS * NSA_DH
L_CMP = 32
CMP_STRIDE = 16
CMP_RATIO = L_CMP // CMP_STRIDE
L_SEL = 64
N_SEL = 8
WINDOW = 512
N_FULL_KV = 4
N_WIN_KV = 2

S5_WIDTH = D_MODEL
S5_GROUP = 16
S5_GROUPS = S5_WIDTH // S5_GROUP
S5_STATE = 64

EVEN_SPLITS = (RET_HEADS * RET_DK, RET_HEADS * RET_DK, A_WIDTH, A_WIDTH, B_WIDTH, 6 * NSA_KV_HEADS * NSA_DH, 3 * NSA_HEADS, B_WIDTH)
EVEN_IN = 2 * RET_HEADS * RET_DK + 2 * A_WIDTH + 2 * B_WIDTH + 6 * NSA_KV_HEADS * NSA_DH + 3 * NSA_HEADS
MIX_EVEN = A_WIDTH + B_WIDTH

kernel_name = 'hybrid_retention_nsa_s5_step'


def _rmsnorm(x, g):
    xf = x.astype(jnp.float32)
    y = xf * lax.rsqrt(jnp.mean(xf * xf, axis=-1, keepdims=True) + EPS)
    return (y * g.astype(jnp.float32)).astype(x.dtype)


def _rope(x, pos):
    half = x.shape[-1] // 2
    inv = ROPE_THETA ** (-jnp.arange(half, dtype=jnp.float32) / half)
    ang = pos.astype(jnp.float32)[:, None] * inv[None, :]
    cos = jnp.cos(ang)[:, None, :]
    sin = jnp.sin(ang)[:, None, :]
    xf = x.astype(jnp.float32)
    x1, x2 = xf[..., :half], xf[..., half:]
    return jnp.concatenate([x1 * cos - x2 * sin, x2 * cos + x1 * sin], axis=-1).astype(x.dtype)


def _split(x, sizes):
    outs, o = [], 0
    for s in sizes:
        outs.append(x[..., o:o + s])
        o += s
    return outs


def _head_groupnorm(o, g):
    mu = jnp.mean(o, axis=-1, keepdims=True)
    var = jnp.mean(jnp.square(o - mu), axis=-1, keepdims=True)
    y = (o - mu) * lax.rsqrt(var + EPS)
    B, L, H, dv = o.shape
    return y.reshape(B, L, H * dv) * g.astype(jnp.float32)


def _retention(q, k, v, s0):
    f32 = jnp.float32
    q, k, v, s0 = q.astype(f32), k.astype(f32), v.astype(f32), s0.astype(f32)
    B, L, H, dk = q.shape
    dv = v.shape[-1]
    C = min(RET_CHUNK, L)
    n = L // C
    log_g = jnp.log(1.0 - 2.0 ** (-5.0 - jnp.arange(H, dtype=f32)))
    idx = jnp.arange(C, dtype=f32)
    diff = idx[:, None] - idx[None, :]
    causal = diff >= 0
    decay_mat = jnp.exp(jnp.where(causal, diff, 0.0)[None] * log_g[:, None, None]) * causal[None]
    q_decay = jnp.exp((idx + 1.0)[None, :] * log_g[:, None]).T[None, :, :, None]
    k_decay = jnp.exp((C - 1.0 - idx)[None, :] * log_g[:, None]).T[None, :, :, None]
    chunk_decay = jnp.exp(C * log_g)[None, :, None, None]

    def step(S, xs):
        qc, kc, vc = xs
        scores = jnp.einsum('bihd,bjhd->bhij', qc, kc) * decay_mat[None]
        intra = jnp.einsum('bhij,bjhe->bihe', scores, vc)
        cross = jnp.einsum('bihd,bhde->bihe', qc * q_decay, S)
        S_new = S * chunk_decay + jnp.einsum('bjhd,bjhe->bhde', kc * k_decay, vc)
        return S_new, intra + cross

    to_chunks = lambda a: a.reshape(B, n, C, H, a.shape[-1]).transpose(1, 0, 2, 3, 4)
    S_fin, o = lax.scan(step, s0, (to_chunks(q), to_chunks(k), to_chunks(v)))
    o = o.transpose(1, 0, 2, 3, 4).reshape(B, L, H, dv)
    return o, S_fin


def _nsa(q, kv_full, kv_win, gates, cmp_pos, cmp_w):
    f32 = jnp.float32
    B, Tq, H, dh = q.shape
    Tk = kv_full.shape[1]
    Tw = kv_win.shape[1]
    G = NSA_KV_HEADS
    q_pos0 = Tk - Tq
    kw_start = Tk - Tw
    scale = dh ** -0.5

    nh = Tk // CMP_STRIDE
    n_c = nh - CMP_RATIO + 1
    halves = kv_full[:, :nh * CMP_STRIDE, :2].reshape(B, nh, CMP_STRIDE, 2, G, dh)
    comp = None
    for r in range(CMP_RATIO):
        w_r = cmp_w[:, r * CMP_STRIDE:(r + 1) * CMP_STRIDE]
        p_r = cmp_pos[:, r * CMP_STRIDE:(r + 1) * CMP_STRIDE]
        part = jnp.einsum('bnlcgd,clde->bncge', halves, w_r) + jnp.einsum('cld,clde->ce', p_r, w_r)[None, None, :, None, :]
        part = part[:, r:r + n_c]
        comp = part if comp is None else comp + part
    ck, cv = comp[:, :, 0], comp[:, :, 1]
    cmp_end = jnp.arange(n_c, dtype=jnp.int32) * CMP_STRIDE + (L_CMP - 1)

    n_s = -(-Tk // L_SEL)
    tk_pad = n_s * L_SEL
    sel_kv = jnp.pad(kv_full[:, :, 2:4], ((0, 0), (0, tk_pad - Tk), (0, 0), (0, 0), (0, 0)))
    sel_kv = sel_kv.reshape(B, n_s, L_SEL, 2, G, dh).transpose(0, 4, 1, 2, 3, 5)
    ks_blk, vs_blk = sel_kv[..., 0, :], sel_kv[..., 1, :]
    c_start = jnp.arange(n_c, dtype=jnp.int32) * CMP_STRIDE
    s_start = jnp.arange(n_s, dtype=jnp.int32) * L_SEL
    overlap = ((c_start[:, None] < s_start[None, :] + L_SEL) & (s_start[None, :] < c_start[:, None] + L_CMP)).astype(f32)
    n_top = min(N_SEL, n_s)
    blk_ids = jnp.arange(n_s, dtype=jnp.int32)
    bi = jnp.arange(B)[:, None, None, None]
    gi = jnp.arange(G)[None, :, None, None]

    kwp = jnp.pad(kv_win, ((0, 0), (WINDOW, 0), (0, 0), (0, 0), (0, 0)))

    qb = min(Q_BLOCK, Tq)
    nqb = Tq // qb
    qg = q.reshape(B, nqb, qb, G, NSA_HPG, dh).transpose(1, 0, 2, 3, 4, 5)
    gg = gates.reshape(B, nqb, qb, G, NSA_HPG, 3).transpose(1, 0, 2, 3, 4, 5)
    starts = q_pos0 + jnp.arange(nqb, dtype=jnp.int32) * qb

    def block(args):
        qblk, gblk, p0 = args
        tpos = p0 + jnp.arange(qb, dtype=jnp.int32)
        s1 = jnp.einsum('bqghd,bngd->bqghn', qblk, ck).astype(f32) * scale
        vmask = (cmp_end[None, :] <= tpos[:, None])[None, :, None, None, :]
        p1 = jax.nn.softmax(jnp.where(vmask, s1, NEG_INF), axis=-1) * vmask
        o_cmp = jnp.einsum('bqghn,bngd->bqghd', p1.astype(cv.dtype), cv)
        imp = jnp.einsum('bqghn,ns->bqgs', p1, overlap)
        cur = tpos // L_SEL
        forced = (blk_ids[None, :] == 0) | (blk_ids[None, :] == cur[:, None]) | (blk_ids[None, :] == cur[:, None] - 1)
        valid_s = s_start[None, :] <= tpos[:, None]
        score = jnp.where(forced[None, :, None, :], FORCE, imp)
        score = jnp.where(valid_s[None, :, None, :], score, -FORCE)
        _, sel = lax.top_k(score, n_top)
        sel = sel.transpose(0, 2, 1, 3)
        kg = ks_blk[bi, gi, sel].reshape(B, G, qb, n_top * L_SEL, dh)
        vg = vs_blk[bi, gi, sel].reshape(B, G, qb, n_top * L_SEL, dh)
        kpos = (sel[..., None] * L_SEL + jnp.arange(L_SEL, dtype=jnp.int32)).reshape(B, G, qb, n_top * L_SEL)
        m2 = (kpos <= tpos[None, None, :, None]).transpose(0, 2, 1, 3)[:, :, :, None, :]
        s2 = jnp.einsum('bqghd,bgqkd->bqghk', qblk, kg).astype(f32) * scale
        p2 = jax.nn.softmax(jnp.where(m2, s2, NEG_INF), axis=-1)
        o_sel = jnp.einsum('bqghk,bgqkd->bqghd', p2.astype(vg.dtype), vg)
        kwb = lax.dynamic_slice_in_dim(kwp, p0 - kw_start, WINDOW + qb, axis=1)
        kpos_w = p0 - WINDOW + jnp.arange(WINDOW + qb, dtype=jnp.int32)
        m3 = (kpos_w[None, :] <= tpos[:, None]) & (kpos_w[None, :] > tpos[:, None] - WINDOW) & (kpos_w[None, :] >= 0)
        s3 = jnp.einsum('bqghd,bkgd->bqghk', qblk, kwb[:, :, 0]).astype(f32) * scale
        p3 = jax.nn.softmax(jnp.where(m3[None, :, None, None, :], s3, NEG_INF), axis=-1)
        o_win = jnp.einsum('bqghk,bkgd->bqghd', p3.astype(kwb.dtype), kwb[:, :, 1])
        o = gblk[..., 0:1] * o_cmp + gblk[..., 1:2] * o_sel + gblk[..., 2:3] * o_win
        return o.astype(qblk.dtype)

    out = lax.map(block, (qg, gg, starts))
    return out.transpose(1, 0, 2, 3, 4, 5).reshape(B, Tq, H * dh)


def _even_layer(x, pos0, s_ret, kv_past, win_past, norm_g, w_in, w_out, gn_gain, q_norm, k_norm, cmp_pos, cmp_w):
    B, L, _ = x.shape
    pos = pos0 + jnp.arange(L, dtype=jnp.int32)
    h = _rmsnorm(x, norm_g)
    proj = jnp.einsum('bld,de->ble', h, w_in)
    qa, ka, va, za, qn, kvb, gl, zb = _split(proj, EVEN_SPLITS)
    qa = _rope(qa.reshape(B, L, RET_HEADS, RET_DK), pos)
    ka = _rope(ka.reshape(B, L, RET_HEADS, RET_DK), pos) * (RET_DK ** -0.5)
    va = va.reshape(B, L, RET_HEADS, RET_DV)
    oa, s_ret_new = _retention(qa, ka, va, s_ret)
    oa = _head_groupnorm(oa, gn_gain).astype(x.dtype) * jax.nn.silu(za)
    qn = _rope(_rmsnorm(qn.reshape(B, L, NSA_HEADS, NSA_DH), q_norm), pos)
    kvb = kvb.reshape(B, L, 6, NSA_KV_HEADS, NSA_DH)
    k_c = _rope(_rmsnorm(kvb[:, :, 0], k_norm[0]), pos)
    k_s = _rope(_rmsnorm(kvb[:, :, 2], k_norm[1]), pos)
    k_w = _rope(_rmsnorm(kvb[:, :, 4], k_norm[2]), pos)
    full_new = jnp.stack([k_c, kvb[:, :, 1], k_s, kvb[:, :, 3]], axis=2)
    win_new = jnp.stack([k_w, kvb[:, :, 5]], axis=2)
    kv_full = full_new if kv_past is None else jnp.concatenate([kv_past.astype(full_new.dtype), full_new], axis=1)
    kv_win = win_new if win_past is None else jnp.concatenate([win_past.astype(win_new.dtype), win_new], axis=1)
    gates = jax.nn.sigmoid(gl.reshape(B, L, NSA_HEADS, 3))
    ob = _nsa(qn, kv_full, kv_win, gates, cmp_pos, cmp_w) * jax.nn.silu(zb)
    y = x + jnp.einsum('ble,ed->bld', jnp.concatenate([oa, ob], axis=-1), w_out)
    keep = min(WINDOW, kv_win.shape[1])
    return y, s_ret_new, full_new, kv_win[:, kv_win.shape[1] - keep:]


def _complex_affine_combine(e1, e2):
    a1r, a1i, b1r, b1i = e1
    a2r, a2i, b2r, b2i = e2
    return (a1r * a2r - a1i * a2i, a1r * a2i + a1i * a2r,
            a2r * b1r - a2i * b1i + b2r, a2r * b1i + a2i * b1r + b2i)


def _s5(u, x0_re, x0_im, lam_re, lam_im, b_re, b_im, c_re, c_im, d, log_step):
    B, L, E = u.shape
    ug = u.reshape(B, L, S5_GROUPS, S5_GROUP)
    dt = jnp.exp(log_step)[:, None]
    mag = jnp.exp(lam_re * dt)
    ang = lam_im * dt
    ab_re, ab_im = mag * jnp.cos(ang), mag * jnp.sin(ang)
    den = lam_re * lam_re + lam_im * lam_im
    nr = ab_re - 1.0
    f_re = (nr * lam_re + ab_im * lam_im) / den
    f_im = (ab_im * lam_re - nr * lam_im) / den
    bb_re = f_re[..., None] * b_re - f_im[..., None] * b_im
    bb_im = f_re[..., None] * b_im + f_im[..., None] * b_re
    bu_re = jnp.einsum('blgc,gpc->blgp', ug, bb_re)
    bu_im = jnp.einsum('blgc,gpc->blgp', ug, bb_im)
    bu_re = bu_re.at[:, 0].add(ab_re * x0_re - ab_im * x0_im)
    bu_im = bu_im.at[:, 0].add(ab_re * x0_im + ab_im * x0_re)
    a_re = jnp.broadcast_to(ab_re, bu_re.shape)
    a_im = jnp.broadcast_to(ab_im, bu_im.shape)
    _, _, xr, xi = lax.associative_scan(_complex_affine_combine, (a_re, a_im, bu_re, bu_im), axis=1)
    y = jnp.einsum('blgp,gcp->blgc', xr, c_re) - jnp.einsum('blgp,gcp->blgc', xi, c_im)
    y = y.reshape(B, L, E) + d * u
    return y, xr[:, -1], xi[:, -1]


def _odd_layer(x, s_re, s_im, norm_g, w_in, lam_re, lam_im, b_re, b_im, c_re, c_im, d, log_step, glu_w1, glu_w2, w_out):
    f32 = jnp.float32
    h = _rmsnorm(x, norm_g)
    u, z = _split(jnp.einsum('bld,de->ble', h, w_in), (S5_WIDTH, S5_WIDTH))
    y, f_re, f_im = _s5(u.astype(f32), s_re.astype(f32), s_im.astype(f32), lam_re.astype(f32), lam_im.astype(f32),
                        b_re.astype(f32), b_im.astype(f32), c_re.astype(f32), c_im.astype(f32), d.astype(f32), log_step.astype(f32))
    y = jax.nn.gelu(y).astype(x.dtype)
    y = jnp.einsum('ble,ef->blf', y, glu_w1) * jax.nn.sigmoid(jnp.einsum('ble,ef->blf', y, glu_w2))
    y = x + jnp.einsum('ble,ed->bld', y * jax.nn.silu(z), w_out)
    return y, f_re, f_im


def setup_inputs(seed: int = 0) -> dict:
    key = jax.random.key(seed)
    ks = jax.random.split(key, 32)
    f32 = jnp.float32
    n_pages = PAST_LEN // PAGE_SIZE
    n_used = DEC_BATCH * n_pages
    n_phys = n_used + n_used // 4
    w_buf = min(WINDOW, PAST_LEN)
    nrm = lambda k, shape, s: jax.random.normal(k, shape, f32) * s
    gain = lambda k, shape: 1.0 + 0.02 * jax.random.normal(k, shape, f32)
    page_table = jax.random.permutation(ks[0], n_phys)[:n_used].reshape(DEC_BATCH, n_pages).astype(jnp.int32)
    lam_re = -0.5 + 0.01 * jax.random.normal(ks[1], (N_ODD, S5_GROUPS, S5_STATE), f32)
    lam_im = jnp.broadcast_to(math.pi * jnp.arange(S5_STATE, dtype=f32), (N_ODD, S5_GROUPS, S5_STATE))
    log_step = jax.random.uniform(ks[2], (N_ODD, S5_GROUPS), f32, math.log(1e-3), math.log(1e-1))
    return {
        'x_prompt': nrm(ks[3], (BATCH, SEQ, D_MODEL), 1.0),
        'x_sample': nrm(ks[4], (DEC_BATCH, DEC_SEQ, D_MODEL), 1.0),
        'cache_nsa_kv': nrm(ks[5], (N_EVEN, n_phys, PAGE_SIZE, N_FULL_KV, NSA_KV_HEADS, NSA_DH), 1.0),
        'cache_nsa_win': nrm(ks[6], (N_EVEN, DEC_BATCH, w_buf, N_WIN_KV, NSA_KV_HEADS, NSA_DH), 1.0),
        'state_ret': nrm(ks[7], (N_EVEN, DEC_BATCH, RET_HEADS, RET_DK, RET_DV), 0.5),
        'state_ssm_re': nrm(ks[8], (N_ODD, DEC_BATCH, S5_GROUPS, S5_STATE), 0.5),
        'state_ssm_im': nrm(ks[9], (N_ODD, DEC_BATCH, S5_GROUPS, S5_STATE), 0.5),
        'page_table': page_table,
        'norm_even': gain(ks[10], (N_EVEN, D_MODEL)),
        'w_in_even': nrm(ks[11], (N_EVEN, D_MODEL, EVEN_IN), D_MODEL ** -0.5),
        'w_out_even': nrm(ks[12], (N_EVEN, MIX_EVEN, D_MODEL), MIX_EVEN ** -0.5),
        'ret_gn_gain': gain(ks[13], (N_EVEN, A_WIDTH)),
        'nsa_q_norm': gain(ks[14], (N_EVEN, NSA_DH)),
        'nsa_k_norm': gain(ks[15], (N_EVEN, 3, NSA_DH)),
        'nsa_cmp_pos': nrm(ks[16], (N_EVEN, 2, L_CMP, NSA_DH), 0.02),
        'nsa_cmp_w': nrm(ks[17], (N_EVEN, 2, L_CMP, NSA_DH, NSA_DH), (L_CMP * NSA_DH) ** -0.5),
        'norm_odd': gain(ks[18], (N_ODD, D_MODEL)),
        'w_in_odd': nrm(ks[19], (N_ODD, D_MODEL, 2 * S5_WIDTH), D_MODEL ** -0.5),
        'ssm_lambda_re': lam_re,
        'ssm_lambda_im': lam_im,
        'ssm_b_re': nrm(ks[20], (N_ODD, S5_GROUPS, S5_STATE, S5_GROUP), (2 * S5_GROUP) ** -0.5),
        'ssm_b_im': nrm(ks[21], (N_ODD, S5_GROUPS, S5_STATE, S5_GROUP), (2 * S5_GROUP) ** -0.5),
        'ssm_c_re': nrm(ks[22], (N_ODD, S5_GROUPS, S5_GROUP, S5_STATE), S5_STATE ** -0.5),
        'ssm_c_im': nrm(ks[23], (N_ODD, S5_GROUPS, S5_GROUP, S5_STATE), S5_STATE ** -0.5),
        'ssm_d': nrm(ks[24], (N_ODD, S5_WIDTH), 1.0),
        'ssm_log_step': log_step,
        'glu_w1': nrm(ks[25], (N_ODD, S5_WIDTH, S5_WIDTH), S5_WIDTH ** -0.5),
        'glu_w2': nrm(ks[26], (N_ODD, S5_WIDTH, S5_WIDTH), S5_WIDTH ** -0.5),
        'w_out_odd': nrm(ks[27], (N_ODD, S5_WIDTH, D_MODEL), S5_WIDTH ** -0.5),
    }


def reference(x_prompt, x_sample, cache_nsa_kv, cache_nsa_win, state_ret, state_ssm_re, state_ssm_im, page_table,
              norm_even, w_in_even, w_out_even, ret_gn_gain, nsa_q_norm, nsa_k_norm, nsa_cmp_pos, nsa_cmp_w,
              norm_odd, w_in_odd, ssm_lambda_re, ssm_lambda_im, ssm_b_re, ssm_b_im, ssm_c_re, ssm_c_im, ssm_d,
              ssm_log_step, glu_w1, glu_w2, w_out_odd):
    bp = x_prompt.shape[0]
    db = x_sample.shape[0]
    n_pages = page_table.shape[1]
    past_len = n_pages * PAGE_SIZE
    yp, ys = x_prompt, x_sample
    ret_p, ret_s, kv_p, kv_s, win_p, win_s = [], [], [], [], [], []
    sre_p, sim_p, sre_s, sim_s = [], [], [], []
    for layer in range(DEPTH):
        li = layer // 2
        if layer % 2 == 0:
            ew = (norm_even[li], w_in_even[li], w_out_even[li], ret_gn_gain[li], nsa_q_norm[li], nsa_k_norm[li],
                  nsa_cmp_pos[li], nsa_cmp_w[li])
            s0 = jnp.zeros((bp, RET_HEADS, RET_DK, RET_DV), jnp.float32)
            yp, sr, kvr, wr = _even_layer(yp, 0, s0, None, None, *ew)
            ret_p.append(sr); kv_p.append(kvr); win_p.append(wr)
            past = cache_nsa_kv[li][page_table].reshape(db, past_len, N_FULL_KV, NSA_KV_HEADS, NSA_DH)
            ys, sr2, kvr2, wr2 = _even_layer(ys, past_len, state_ret[li], past, cache_nsa_win[li], *ew)
            ret_s.append(sr2); kv_s.append(kvr2); win_s.append(wr2)
        else:
            ow = (norm_odd[li], w_in_odd[li], ssm_lambda_re[li], ssm_lambda_im[li], ssm_b_re[li], ssm_b_im[li],
                  ssm_c_re[li], ssm_c_im[li], ssm_d[li], ssm_log_step[li], glu_w1[li], glu_w2[li], w_out_odd[li])
            z0 = jnp.zeros((bp, S5_GROUPS, S5_STATE), jnp.float32)
            yp, fr, fi = _odd_layer(yp, z0, z0, *ow)
            sre_p.append(fr); sim_p.append(fi)
            ys, fr2, fi2 = _odd_layer(ys, state_ssm_re[li], state_ssm_im[li], *ow)
            sre_s.append(fr2); sim_s.append(fi2)
    return (yp, ys, jnp.stack(ret_p), jnp.stack(ret_s), jnp.stack(kv_p), jnp.stack(kv_s), jnp.stack(win_p),
            jnp.stack(win_s), jnp.stack(sre_p), jnp.stack(sim_p), jnp.stack(sre_s), jnp.stack(sim_s))
```

```python
import functools
import math

import jax
import jax.numpy as jnp
from jax import lax
from jax.experimental import pallas as pl
from jax.experimental.pallas import tpu as pltpu

F32 = jnp.float32
BF16 = jnp.bfloat16

LANES = 128
SUBLANES = 8
VMEM_LIMIT_BYTES = 48 * 2**20

EPS = 1e-6
ROPE_THETA = 10000.0
NEG_INF = -1e30
FORCE = 1e4

RET_HEADS = 4
RET_DK = 64
RET_DV = 128
RET_CHUNK = 128
A_WIDTH = RET_HEADS * RET_DV

NSA_HEADS = 8
NSA_KV_HEADS = 2
NSA_DH = 64
NSA_HPG = NSA_HEADS // NSA_KV_HEADS
B_WIDTH = NSA_HEADS * NSA_DH
L_CMP = 32
CMP_STRIDE = 16
CMP_RATIO = L_CMP // CMP_STRIDE
L_SEL = 64
N_SEL = 8
WINDOW = 512
PAGE_SIZE = 128
KV_ROW = 4 * NSA_KV_HEADS * NSA_DH
WIN_ROW = 2 * NSA_KV_HEADS * NSA_DH
KEY_TILE = 128
WIN_KEYS = WINDOW + KEY_TILE

S5_GROUP = 16
S5_STATE = 64
S5_LANE_GROUPS = LANES // S5_GROUP
S5_BLOCK_STATE = S5_LANE_GROUPS * S5_STATE

QA0, KA0, VA0, ZA0, QN0, KVB0, ZB0, GL0 = 0, 256, 512, 1024, 1536, 2048, 2816, 3328
EVEN_COLS = GL0 + LANES
N_GATES = 3 * NSA_HEADS

ROW_TILE = 256


def _cparams(*sem):
    return pltpu.CompilerParams(dimension_semantics=sem, vmem_limit_bytes=VMEM_LIMIT_BYTES)


def _lane_iota(shape):
    return lax.broadcasted_iota(jnp.int32, shape, len(shape) - 1)


def _row_iota(shape):
    return lax.broadcasted_iota(jnp.int32, shape, len(shape) - 2)


def _dot(a, b):
    return jnp.dot(a, b, preferred_element_type=F32)


def _dot_nt(a, b):
    return lax.dot_general(a, b, (((1,), (1,)), ((), ())), preferred_element_type=F32)


def _rmsnorm_rows(x, g):
    return x * lax.rsqrt(jnp.mean(x * x, axis=-1, keepdims=True) + EPS) * g


def _swap_halves(x):
    return pltpu.roll(x, NSA_DH, axis=1)


def _rope_block(x, cos, sin_signed):
    half = NSA_DH // 2
    lane = _lane_iota(x.shape)
    first = (lane % NSA_DH) < half
    partner = jnp.where(first, pltpu.roll(x, LANES - half, axis=1), pltpu.roll(x, half, axis=1))
    return x * cos + partner * sin_signed


def _head_rms_block(x, g):
    lane = _lane_iota(x.shape)
    lo = lane < NSA_DH
    sq = x * x
    s_lo = jnp.sum(jnp.where(lo, sq, 0.0), axis=-1, keepdims=True)
    s_hi = jnp.sum(jnp.where(lo, 0.0, sq), axis=-1, keepdims=True)
    ms = jnp.where(lo, s_lo, s_hi) * (1.0 / NSA_DH)
    return x * lax.rsqrt(ms + EPS) * g


def _even_in_kernel(x_ref, g_ref, w_ref, cos_ref, sin_ref, qg_ref, kg_ref,
                    qa_ref, ka_ref, va_ref, za_ref, qn_ref, full_ref, win_ref, zb_ref, gl_ref):
    hb = _rmsnorm_rows(x_ref[...], g_ref[...]).astype(BF16)
    cos = cos_ref[...]
    sin = sin_ref[...]

    def proj(c0):
        return _dot(hb, w_ref[:, c0:c0 + LANES])

    for j in range(RET_HEADS * RET_DK // LANES):
        c = j * LANES
        qa_ref[:, c:c + LANES] = _rope_block(proj(QA0 + c), cos, sin)
        ka_ref[:, c:c + LANES] = _rope_block(proj(KA0 + c), cos, sin) * (RET_DK ** -0.5)
    for j in range(A_WIDTH // LANES):
        c = j * LANES
        va_ref[:, c:c + LANES] = proj(VA0 + c)
        za_ref[:, c:c + LANES] = jax.nn.silu(proj(ZA0 + c))
    for j in range(B_WIDTH // LANES):
        c = j * LANES
        qn_ref[:, c:c + LANES] = _rope_block(_head_rms_block(proj(QN0 + c), qg_ref[...]), cos, sin)
        zb_ref[:, c:c + LANES] = jax.nn.silu(proj(ZB0 + c))
    for j in range(6):
        y = proj(KVB0 + j * LANES)
        if j % 2 == 0:
            y = _rope_block(_head_rms_block(y, kg_ref[j // 2:j // 2 + 1, :]), cos, sin)
        if j < 4:
            full_ref[:, j * LANES:(j + 1) * LANES] = y
        else:
            win_ref[:, (j - 4) * LANES:(j - 3) * LANES] = y
    gl_ref[...] = jax.nn.sigmoid(proj(GL0))


def _even_in(x2d, gain, w_bf, cos_t, sin_t, qg, kg, n_table_blocks):
    m, d = x2d.shape
    grid = (m // ROW_TILE,)
    row = lambda i: (i, 0)
    fixed = lambda i: (0, 0)
    table = lambda i: (i % n_table_blocks, 0)
    widths = (RET_HEADS * RET_DK, RET_HEADS * RET_DK, A_WIDTH, A_WIDTH, B_WIDTH, KV_ROW, WIN_ROW, B_WIDTH, LANES)
    return pl.pallas_call(
        _even_in_kernel,
        grid=grid,
        in_specs=[
            pl.BlockSpec((ROW_TILE, d), row),
            pl.BlockSpec((1, d), fixed),
            pl.BlockSpec((d, EVEN_COLS), fixed),
            pl.BlockSpec((ROW_TILE, LANES), table),
            pl.BlockSpec((ROW_TILE, LANES), table),
            pl.BlockSpec((1, LANES), fixed),
            pl.BlockSpec((3, LANES), fixed),
        ],
        out_specs=[pl.BlockSpec((ROW_TILE, w), row) for w in widths],
        out_shape=[jax.ShapeDtypeStruct((m, w), F32) for w in widths],
        compiler_params=_cparams("parallel"),
        name="even_in_proj",
    )(x2d, gain, w_bf, cos_t, sin_t, qg, kg)


def _retention_kernel(q_ref, k_ref, v_ref, z_ref, gn_ref, s0_ref, dmat_ref, qdec_ref, kdec_ref, cdec_ref,
                      o_ref, sfin_ref, s_scr, *, rows):
    c = pl.program_id(1)

    @pl.when(c == 0)
    def _():
        s_scr[...] = s0_ref[0]

    def padded(ref):
        x = ref[...]
        if rows == RET_CHUNK:
            return x
        return jnp.concatenate([x, jnp.zeros((RET_CHUNK - rows, x.shape[1]), F32)], axis=0)

    q = padded(q_ref)
    k = padded(k_ref)
    v = padded(v_ref)
    qd = q * qdec_ref[...]
    kd = k * kdec_ref[...]
    lane = _lane_iota((RET_CHUNK, LANES))
    for pair in range(RET_HEADS // 2):
        cols = slice(pair * LANES, (pair + 1) * LANES)
        q2, k2b, qd2, kd2 = q[:, cols], k[:, cols].astype(BF16), qd[:, cols], kd[:, cols]
        kd2_t = kd2.T
        s_pair = jnp.concatenate([s_scr[2 * pair], s_scr[2 * pair + 1]], axis=0).astype(BF16)
        for sub in range(2):
            h = 2 * pair + sub
            mine = (lane >= sub * RET_DK) & (lane < (sub + 1) * RET_DK)
            qm = jnp.where(mine, q2, 0.0).astype(BF16)
            qdm = jnp.where(mine, qd2, 0.0).astype(BF16)
            vh = v[:, h * RET_DV:(h + 1) * RET_DV]
            vhb = vh.astype(BF16)
            scores = _dot_nt(qm, k2b) * dmat_ref[h]
            out = _dot(scores.astype(BF16), vhb) + _dot(qdm, s_pair)
            kt = kd2_t[sub * RET_DK:(sub + 1) * RET_DK, :].astype(BF16)
            s_scr[h] = s_scr[h] * cdec_ref[h] + _dot(kt, vhb)
            mu = jnp.mean(out, axis=-1, keepdims=True)
            cen = out - mu
            var = jnp.mean(cen * cen, axis=-1, keepdims=True)
            y = cen * lax.rsqrt(var + EPS) * gn_ref[:, h * RET_DV:(h + 1) * RET_DV]
            o_ref[:, h * RET_DV:(h + 1) * RET_DV] = y[:rows] * z_ref[:, h * RET_DV:(h + 1) * RET_DV]

    @pl.when(c == pl.num_programs(1) - 1)
    def _():
        sfin_ref[0] = s_scr[...]


def _retention_tables(rows):
    log_g = jnp.log(1.0 - 2.0 ** (-5.0 - jnp.arange(RET_HEADS, dtype=F32)))
    idx = jnp.arange(RET_CHUNK, dtype=F32)
    live = idx < rows
    diff = idx[:, None] - idx[None, :]
    causal = (diff >= 0) & live[:, None] & live[None, :]
    dmat = jnp.exp(jnp.where(causal, diff, 0.0)[None] * log_g[:, None, None]) * causal[None]
    qdec = jnp.exp((idx + 1.0)[:, None] * log_g[None, :]) * live[:, None]
    kdec = jnp.exp((rows - 1.0 - idx)[:, None] * log_g[None, :]) * live[:, None]
    qdec = jnp.repeat(qdec, RET_DK, axis=1)
    kdec = jnp.repeat(kdec, RET_DK, axis=1)
    cdec = jnp.broadcast_to(jnp.exp(rows * log_g)[:, None, None], (RET_HEADS, RET_DK, RET_DV))
    return dmat, qdec, kdec, cdec


def _retention(qa, ka, va, za, gn_gain, s0, batch, seq):
    rows = min(RET_CHUNK, seq)
    n_chunks = seq // rows
    dmat, qdec, kdec, cdec = _retention_tables(rows)
    tok = lambda b, c: (b * n_chunks + c, 0)
    fixed2 = lambda b, c: (0, 0)
    fixed3 = lambda b, c: (0, 0, 0)
    state = lambda b, c: (b, 0, 0, 0)
    qk_w = RET_HEADS * RET_DK
    return pl.pallas_call(
        functools.partial(_retention_kernel, rows=rows),
        grid=(batch, n_chunks),
        in_specs=[
            pl.BlockSpec((rows, qk_w), tok),
            pl.BlockSpec((rows, qk_w), tok),
            pl.BlockSpec((rows, A_WIDTH), tok),
            pl.BlockSpec((rows, A_WIDTH), tok),
            pl.BlockSpec((1, A_WIDTH), fixed2),
            pl.BlockSpec((1, RET_HEADS, RET_DK, RET_DV), state),
            pl.BlockSpec((RET_HEADS, RET_CHUNK, RET_CHUNK), fixed3),
            pl.BlockSpec((RET_CHUNK, qk_w), fixed2),
            pl.BlockSpec((RET_CHUNK, qk_w), fixed2),
            pl.BlockSpec((RET_HEADS, RET_DK, RET_DV), fixed3),
        ],
        out_specs=[
            pl.BlockSpec((rows, A_WIDTH), tok),
            pl.BlockSpec((1, RET_HEADS, RET_DK, RET_DV), state),
        ],
        out_shape=[
            jax.ShapeDtypeStruct((batch * seq, A_WIDTH), F32),
            jax.ShapeDtypeStruct((batch, RET_HEADS, RET_DK, RET_DV), F32),
        ],
        scratch_shapes=[pltpu.VMEM((RET_HEADS, RET_DK, RET_DV), F32)],
        compiler_params=_cparams("parallel", "arbitrary"),
        name="retention",
    )(qa, ka, va, za, gn_gain, s0, dmat, qdec, kdec, cdec)


def _compress_kernel(*refs, n_pref, n_src, src_rows):
    refs = refs[n_pref:]
    src_k = refs[:n_src]
    src_v = refs[n_src:2 * n_src]
    pe_ref, w_ref, o_ref = refs[2 * n_src:]
    per_src = src_rows // CMP_STRIDE
    n_half = n_src * per_src
    acc = [jnp.zeros((n_half, 2 * LANES), F32) for _ in range(CMP_RATIO)]

    def strided_rows(srcs, l):
        rows = [s[0, pl.ds(l, per_src, stride=CMP_STRIDE), :] for s in srcs]
        return rows[0] if n_src == 1 else jnp.concatenate(rows, axis=0)

    for l in range(CMP_STRIDE):
        x = jnp.concatenate([strided_rows(src_k, l), strided_rows(src_v, l)], axis=1)
        for r in range(CMP_RATIO):
            i = r * CMP_STRIDE + l
            acc[r] = acc[r] + _dot((x + pe_ref[i:i + 1, :]).astype(BF16), w_ref[i])
    out = acc[0]
    for r in range(1, CMP_RATIO):
        out = out + pltpu.roll(acc[r], n_half - r, axis=0)
    live = _row_iota(out.shape) < n_half - CMP_RATIO + 1
    o_ref[0] = jnp.where(live, out, 0.0)


def _compress(src_arrays, src_specs, grid, extra_prefetch, pe, w_bd, batch, src_rows, n_half):
    n_src = len(src_specs) // 2
    n_pref = len(extra_prefetch)
    kern = functools.partial(_compress_kernel, n_pref=n_pref, n_src=n_src, src_rows=src_rows)
    fixed2 = lambda *a: (0, 0)
    fixed3 = lambda *a: (0, 0, 0)
    out_map = lambda b, *a: (b, 0, 0)
    return pl.pallas_call(
        kern,
        grid_spec=pltpu.PrefetchScalarGridSpec(
            num_scalar_prefetch=n_pref,
            grid=grid,
            in_specs=list(src_specs) + [
                pl.BlockSpec((L_CMP, 2 * LANES), fixed2),
                pl.BlockSpec((L_CMP, 2 * LANES, 2 * LANES), fixed3),
            ],
            out_specs=pl.BlockSpec((1, n_half, 2 * LANES), out_map),
        ),
        out_shape=jax.ShapeDtypeStruct((batch, n_half, 2 * LANES), F32),
        compiler_params=_cparams("parallel"),
        name="kv_compress",
    )(*extra_prefetch, *src_arrays, pe, w_bd)


def _softmax_rows(s):
    m = jnp.max(s, axis=-1, keepdims=True)
    e = jnp.exp(s - m)
    return e / jnp.sum(e, axis=-1, keepdims=True)


def _split3_bf16(x):
    hi = x.astype(BF16)
    r1 = x - hi.astype(F32)
    mid = r1.astype(BF16)
    lo = (r1 - mid.astype(F32)).astype(BF16)
    return hi, mid, lo


def _nsa_body(q, gates, zs, cmp_blk, sel_k, sel_v, win_k, win_v, tpos, sel_kpos, win_kpos,
              overlap, expand, n_blocks, o_ref):
    tq = q.shape[0]
    n_cmp = cmp_blk.shape[0]
    lane = _lane_iota((tq, LANES))
    ck = cmp_blk[:, :LANES].astype(BF16)
    cv = cmp_blk[:, LANES:].astype(BF16)
    sel_kb, sel_vb = sel_k.astype(BF16), sel_v.astype(BF16)
    win_kb, win_vb = win_k.astype(BF16), win_v.astype(BF16)
    tpos4 = jnp.concatenate([tpos] * NSA_HPG, axis=0)

    cmp_end = _lane_iota((1, n_cmp)) * CMP_STRIDE + (L_CMP - 1)
    cmp_ok4 = cmp_end <= tpos4
    win_ok4 = (win_kpos <= tpos4) & (win_kpos > tpos4 - WINDOW)
    causal_sel = sel_kpos <= tpos

    blk = _lane_iota((tq, LANES))
    cur = tpos >> int(math.log2(L_SEL))
    forced = (blk == 0) | (blk == cur) | (blk == cur - 1)
    valid = (blk * L_SEL <= tpos) & (blk < n_blocks)

    ext = []
    for g in range(NSA_KV_HEADS):
        in_g = (lane >= g * NSA_DH) & (lane < (g + 1) * NSA_DH)
        parts = []
        for hh in range(NSA_HPG):
            h = g * NSA_HPG + hh
            two = q[:, (h // 2) * LANES:(h // 2 + 1) * LANES]
            if h % 2 != g:
                two = _swap_halves(two)
            parts.append(jnp.where(in_g, two, 0.0))
        qs = (jnp.concatenate(parts, axis=0) * (NSA_DH ** -0.5)).astype(BF16)

        s1 = _dot_nt(qs, ck)
        p1 = _softmax_rows(jnp.where(cmp_ok4, s1, NEG_INF)) * cmp_ok4.astype(F32)
        o_cmp = _dot(p1.astype(BF16), cv)

        psum = p1[0:tq]
        for hh in range(1, NSA_HPG):
            psum = psum + p1[hh * tq:(hh + 1) * tq]
        imp = sum(_dot(t, overlap) for t in _split3_bf16(psum))
        score = jnp.where(valid, jnp.where(forced, FORCE, imp), -FORCE)
        rank = jnp.zeros((tq, LANES), F32)
        for s in range(n_blocks):
            col = score[:, s:s + 1]
            ahead = (col > score) | ((col == score) & (blk > s))
            rank = rank + ahead.astype(F32)
        chosen = ((rank < N_SEL) & valid).astype(BF16)
        key_on = (_dot(chosen, expand) > 0.5) & causal_sel
        bias = jnp.where(key_on, 0.0, NEG_INF)
        bias4 = jnp.concatenate([bias] * NSA_HPG, axis=0)

        p2 = _softmax_rows(_dot_nt(qs, sel_kb) + bias4)
        o_sel = _dot(p2.astype(BF16), sel_vb)

        s3 = _dot_nt(qs, win_kb)
        p3 = _softmax_rows(jnp.where(win_ok4, s3, NEG_INF))
        o_win = _dot(p3.astype(BF16), win_vb)

        for hh in range(NSA_HPG):
            h = g * NSA_HPG + hh
            rows = slice(hh * tq, (hh + 1) * tq)
            o = (gates[:, 3 * h:3 * h + 1] * o_cmp[rows] + gates[:, 3 * h + 1:3 * h + 2] * o_sel[rows]
                 + gates[:, 3 * h + 2:3 * h + 3] * o_win[rows])
            if h % 2 != g:
                o = _swap_halves(o)
            ext.append(o)

    for j in range(NSA_HEADS // 2):
        both = jnp.where(lane < NSA_DH, ext[2 * j], ext[2 * j + 1])
        o_ref[:, j * LANES:(j + 1) * LANES] = both * zs[:, j * LANES:(j + 1) * LANES]


def _nsa_prompt_kernel(q_ref, g_ref, z_ref, cmp_ref, full_ref, win_ref, ov_ref, ex_ref, o_ref, *, n_blocks):
    i = pl.program_id(1)
    tq = q_ref.shape[0]
    n_keys = full_ref.shape[1]
    tpos = i * tq + _row_iota((tq, 1))
    sel_kpos = _lane_iota((1, n_keys))
    w0 = pl.multiple_of(jnp.maximum(i * tq - WINDOW, 0), KEY_TILE)
    win_kpos = w0 + _lane_iota((1, WIN_KEYS))
    _nsa_body(q_ref[...], g_ref[...], z_ref[...], cmp_ref[0],
              full_ref[0, :, 0:LANES], full_ref[0, :, LANES:2 * LANES],
              win_ref[0, pl.ds(w0, WIN_KEYS), 0:LANES], win_ref[0, pl.ds(w0, WIN_KEYS), LANES:2 * LANES],
              tpos, sel_kpos, win_kpos, ov_ref[...], ex_ref[...], n_blocks, o_ref)


def _nsa_prompt(qn, gates, zb, cmp_kv, full3, win3, overlap, expand, batch, seq):
    tq = KEY_TILE
    nq = seq // tq
    tok = lambda b, i: (b * nq + i, 0)
    per_b = lambda b, i: (b, 0, 0)
    sel_half = lambda b, i: (b, 0, 1)
    fixed = lambda b, i: (0, 0)
    n_blocks = -(-seq // L_SEL)
    return pl.pallas_call(
        functools.partial(_nsa_prompt_kernel, n_blocks=n_blocks),
        grid=(batch, nq),
        in_specs=[
            pl.BlockSpec((tq, B_WIDTH), tok),
            pl.BlockSpec((tq, LANES), tok),
            pl.BlockSpec((tq, B_WIDTH), tok),
            pl.BlockSpec((1, cmp_kv.shape[1], 2 * LANES), per_b),
            pl.BlockSpec((1, seq, 2 * LANES), sel_half),
            pl.BlockSpec((1, seq, WIN_ROW), per_b),
            pl.BlockSpec(overlap.shape, fixed),
            pl.BlockSpec(expand.shape, fixed),
        ],
        out_specs=pl.BlockSpec((tq, B_WIDTH), tok),
        out_shape=jax.ShapeDtypeStruct((batch * seq, B_WIDTH), F32),
        compiler_params=_cparams("parallel", "arbitrary"),
        name="nsa_prompt",
    )(qn, gates, zb, cmp_kv, full3, win3, overlap, expand)


def _nsa_sample_kernel(*refs, n_pages, past_len, n_blocks):
    pt_ref = refs[0]
    del pt_ref
    q_ref, g_ref, z_ref, cmp_ref, new_full_ref, new_win_ref, winp_ref = refs[1:8]
    pages = refs[8:8 + n_pages]
    ov_ref, ex_ref, o_ref, wout_ref = refs[8 + n_pages:]
    tq = q_ref.shape[0]
    w_buf = winp_ref.shape[1]
    pad = jnp.zeros((KEY_TILE - tq, LANES), F32)
    tpos = past_len + _row_iota((tq, 1))
    new_sel = new_full_ref[:, 2 * LANES:4 * LANES]
    sel_k = jnp.concatenate([p[0, :, 0:LANES] for p in pages] + [new_sel[:, :LANES], pad], axis=0)
    sel_v = jnp.concatenate([p[0, :, LANES:2 * LANES] for p in pages] + [new_sel[:, LANES:], pad], axis=0)
    new_win = new_win_ref[...]
    win_k = jnp.concatenate([winp_ref[0, :, 0:LANES], new_win[:, :LANES], pad], axis=0)
    win_v = jnp.concatenate([winp_ref[0, :, LANES:2 * LANES], new_win[:, LANES:], pad], axis=0)
    sel_kpos = _lane_iota((1, past_len + KEY_TILE))
    win_kpos = (past_len - w_buf) + _lane_iota((1, w_buf + KEY_TILE))
    _nsa_body(q_ref[...], g_ref[...], z_ref[...], cmp_ref[0], sel_k, sel_v, win_k, win_v,
              tpos, sel_kpos, win_kpos, ov_ref[...], ex_ref[...], n_blocks, o_ref)
    wout_ref[0, 0:w_buf - tq, :] = winp_ref[0, tq:w_buf, :]
    wout_ref[0, w_buf - tq:w_buf, :] = new_win


def _nsa_sample(page_table, qn, gates, zb, cmp_kv, full_new, win_new, win_past, cache, overlap, expand,
                batch, tq, n_pages):
    past_len = n_pages * PAGE_SIZE
    w_buf = win_past.shape[1]
    n_blocks = -(-(past_len + tq) // L_SEL)
    tok = lambda b, pt: (b, 0)
    per_b = lambda b, pt: (b, 0, 0)
    fixed = lambda b, pt: (0, 0)

    def page_spec(p):
        return pl.BlockSpec((1, PAGE_SIZE, 2 * LANES), lambda b, pt: (pt[b, p], 0, 1))

    return pl.pallas_call(
        functools.partial(_nsa_sample_kernel, n_pages=n_pages, past_len=past_len, n_blocks=n_blocks),
        grid_spec=pltpu.PrefetchScalarGridSpec(
            num_scalar_prefetch=1,
            grid=(batch,),
            in_specs=[
                pl.BlockSpec((tq, B_WIDTH), tok),
                pl.BlockSpec((tq, LANES), tok),
                pl.BlockSpec((tq, B_WIDTH), tok),
                pl.BlockSpec((1, cmp_kv.shape[1], 2 * LANES), per_b),
                pl.BlockSpec((tq, KV_ROW), tok),
                pl.BlockSpec((tq, WIN_ROW), tok),
                pl.BlockSpec((1, w_buf, WIN_ROW), per_b),
            ] + [page_spec(p) for p in range(n_pages)] + [
                pl.BlockSpec(overlap.shape, fixed),
                pl.BlockSpec(expand.shape, fixed),
            ],
            out_specs=[
                pl.BlockSpec((tq, B_WIDTH), tok),
                pl.BlockSpec((1, w_buf, WIN_ROW), per_b),
            ],
        ),
        out_shape=[
            jax.ShapeDtypeStruct((batch * tq, B_WIDTH), F32),
            jax.ShapeDtypeStruct((batch, w_buf, WIN_ROW), F32),
        ],
        compiler_params=_cparams("parallel"),
        name="nsa_sample",
    )(page_table, qn, gates, zb, cmp_kv, full_new, win_new, win_past, *([cache] * n_pages), overlap, expand)


def _selection_tables(n_cmp_rows, n_sel_keys):
    n = jnp.arange(n_cmp_rows, dtype=jnp.int32)[:, None]
    s = jnp.arange(LANES, dtype=jnp.int32)[None, :]
    c_start = n * CMP_STRIDE
    s_start = s * L_SEL
    overlap = ((c_start < s_start + L_SEL) & (s_start < c_start + L_CMP)).astype(BF16)
    key = jnp.arange(n_sel_keys, dtype=jnp.int32)[None, :]
    expand = ((key // L_SEL) == jnp.arange(LANES, dtype=jnp.int32)[:, None]).astype(BF16)
    return overlap, expand


def _even_out_kernel(x_ref, oa_ref, ob_ref, wa_ref, wb_ref, y_ref):
    y_ref[...] = (x_ref[...] + _dot(oa_ref[...].astype(BF16), wa_ref[...])
                  + _dot(ob_ref[...].astype(BF16), wb_ref[...]))


def _even_out(x2d, oa, ob, wa, wb):
    m, d = x2d.shape
    row = lambda i: (i, 0)
    fixed = lambda i: (0, 0)
    return pl.pallas_call(
        _even_out_kernel,
        grid=(m // ROW_TILE,),
        in_specs=[
            pl.BlockSpec((ROW_TILE, d), row),
            pl.BlockSpec((ROW_TILE, A_WIDTH), row),
            pl.BlockSpec((ROW_TILE, B_WIDTH), row),
            pl.BlockSpec((A_WIDTH, d), fixed),
            pl.BlockSpec((B_WIDTH, d), fixed),
        ],
        out_specs=pl.BlockSpec((ROW_TILE, d), row),
        out_shape=jax.ShapeDtypeStruct((m, d), F32),
        compiler_params=_cparams("parallel"),
        name="even_out_proj",
    )(x2d, oa, ob, wa, wb)


def _odd_in_kernel(x_ref, g_ref, w_ref, u_ref, z_ref):
    hb = _rmsnorm_rows(x_ref[...], g_ref[...]).astype(BF16)
    e = u_ref.shape[1]
    u_ref[...] = _dot(hb, w_ref[:, :e])
    z_ref[...] = jax.nn.silu(_dot(hb, w_ref[:, e:]))


def _odd_in(x2d, gain, w_bf):
    m, d = x2d.shape
    e = w_bf.shape[1] // 2
    row = lambda i: (i, 0)
    fixed = lambda i: (0, 0)
    return pl.pallas_call(
        _odd_in_kernel,
        grid=(m // ROW_TILE,),
        in_specs=[
            pl.BlockSpec((ROW_TILE, d), row),
            pl.BlockSpec((1, d), fixed),
            pl.BlockSpec((d, 2 * e), fixed),
        ],
        out_specs=[pl.BlockSpec((ROW_TILE, e), row), pl.BlockSpec((ROW_TILE, e), row)],
        out_shape=[jax.ShapeDtypeStruct((m, e), F32), jax.ShapeDtypeStruct((m, e), F32)],
        compiler_params=_cparams("parallel"),
        name="odd_in_proj",
    )(x2d, gain, w_bf)


def _s5_kernel(u_ref, x0r_ref, x0i_ref, ar_ref, ai_ref, bm_ref, cm_ref, d_ref,
               y_ref, fr_ref, fi_ref, st_re, st_im, xbuf, *, nb, tt):
    t_idx = pl.program_id(1)
    n_blk = bm_ref.shape[0]
    half = S5_BLOCK_STATE
    n_chunk = half // LANES

    @pl.when(t_idx == 0)
    def _():
        for kb in range(n_blk):
            st_re[kb] = x0r_ref[:, kb * half:(kb + 1) * half]
            st_im[kb] = x0i_ref[:, kb * half:(kb + 1) * half]

    for kb in range(n_blk):
        lanes = slice(kb * LANES, (kb + 1) * LANES)
        u_blk = u_ref[:, :, lanes].reshape(nb * tt, LANES)
        bu = _dot(u_blk.astype(BF16), bm_ref[kb])
        for j in range(2 * n_chunk):
            xbuf[j] = bu[:, j * LANES:(j + 1) * LANES]
        a_re = jnp.broadcast_to(ar_ref[kb], (nb, half))
        a_im = jnp.broadcast_to(ai_ref[kb], (nb, half))

        def step(t, carry):
            s_re, s_im = carry
            rows = pl.ds(t, nb, stride=tt)
            b_re = jnp.concatenate([xbuf[j, rows, :] for j in range(n_chunk)], axis=1)
            b_im = jnp.concatenate([xbuf[n_chunk + j, rows, :] for j in range(n_chunk)], axis=1)
            n_re = a_re * s_re - a_im * s_im + b_re
            n_im = a_re * s_im + a_im * s_re + b_im
            for j in range(n_chunk):
                xbuf[j, rows, :] = n_re[:, j * LANES:(j + 1) * LANES]
                xbuf[n_chunk + j, rows, :] = n_im[:, j * LANES:(j + 1) * LANES]
            return n_re, n_im

        s_re, s_im = lax.fori_loop(0, tt, step, (st_re[kb], st_im[kb]))
        st_re[kb] = s_re
        st_im[kb] = s_im
        states = jnp.concatenate([xbuf[j] for j in range(2 * n_chunk)], axis=1)
        y = _dot(states.astype(BF16), cm_ref[kb]) + d_ref[:, lanes] * u_blk
        y_ref[:, :, lanes] = y.reshape(nb, tt, LANES)

    @pl.when(t_idx == pl.num_programs(1) - 1)
    def _():
        for kb in range(n_blk):
            fr_ref[:, kb * half:(kb + 1) * half] = st_re[kb]
            fi_ref[:, kb * half:(kb + 1) * half] = st_im[kb]


def _s5(u3, x0_re, x0_im, a_re, a_im, bmat, cmat, d_row, tt):
    batch, seq, e = u3.shape
    nb = SUBLANES
    n_blk = e // LANES
    n_state = n_blk * S5_BLOCK_STATE
    seq_map = lambda b, t: (b, t, 0)
    st_map = lambda b, t: (b, 0)
    fixed2 = lambda b, t: (0, 0)
    fixed3 = lambda b, t: (0, 0, 0)
    return pl.pallas_call(
        functools.partial(_s5_kernel, nb=nb, tt=tt),
        grid=(batch // nb, seq // tt),
        in_specs=[
            pl.BlockSpec((nb, tt, e), seq_map),
            pl.BlockSpec((nb, n_state), st_map),
            pl.BlockSpec((nb, n_state), st_map),
            pl.BlockSpec((n_blk, 1, S5_BLOCK_STATE), fixed3),
            pl.BlockSpec((n_blk, 1, S5_BLOCK_STATE), fixed3),
            pl.BlockSpec((n_blk, LANES, 2 * S5_BLOCK_STATE), fixed3),
            pl.BlockSpec((n_blk, 2 * S5_BLOCK_STATE, LANES), fixed3),
            pl.BlockSpec((1, e), fixed2),
        ],
        out_specs=[
            pl.BlockSpec((nb, tt, e), seq_map),
            pl.BlockSpec((nb, n_state), st_map),
            pl.BlockSpec((nb, n_state), st_map),
        ],
        out_shape=[
            jax.ShapeDtypeStruct((batch, seq, e), F32),
            jax.ShapeDtypeStruct((batch, n_state), F32),
            jax.ShapeDtypeStruct((batch, n_state), F32),
        ],
        scratch_shapes=[
            pltpu.VMEM((n_blk, nb, S5_BLOCK_STATE), F32),
            pltpu.VMEM((n_blk, nb, S5_BLOCK_STATE), F32),
            pltpu.VMEM((2 * S5_BLOCK_STATE // LANES, nb * tt, LANES), F32),
        ],
        compiler_params=_cparams("parallel", "arbitrary"),
        name="s5_scan",
    )(u3, x0_re, x0_im, a_re, a_im, bmat, cmat, d_row)


def _s5_params(lam_re, lam_im, b_re, b_im, c_re, c_im, log_step):
    n_groups = lam_re.shape[0]
    n_blk = n_groups // S5_LANE_GROUPS
    dt = jnp.exp(log_step)[:, None]
    mag = jnp.exp(lam_re * dt)
    ang = lam_im * dt
    ab_re, ab_im = mag * jnp.cos(ang), mag * jnp.sin(ang)
    den = lam_re * lam_re + lam_im * lam_im
    nr = ab_re - 1.0
    f_re = (nr * lam_re + ab_im * lam_im) / den
    f_im = (ab_im * lam_re - nr * lam_im) / den
    bb_re = f_re[..., None] * b_re - f_im[..., None] * b_im
    bb_im = f_re[..., None] * b_im + f_im[..., None] * b_re
    eye = jnp.eye(S5_LANE_GROUPS, dtype=lam_re.dtype)

    def in_map(bb):
        bb = bb.reshape(n_blk, S5_LANE_GROUPS, S5_STATE, S5_GROUP)
        m = jnp.einsum('kgpc,gh->kgchp', bb, eye)
        return m.reshape(n_blk, LANES, S5_BLOCK_STATE)

    def out_map(cc):
        cc = cc.reshape(n_blk, S5_LANE_GROUPS, S5_GROUP, S5_STATE)
        m = jnp.einsum('kgcp,gh->kgphc', cc, eye)
        return m.reshape(n_blk, S5_BLOCK_STATE, LANES)

    bmat = jnp.concatenate([in_map(bb_re), in_map(bb_im)], axis=2).astype(BF16)
    cmat = jnp.concatenate([out_map(c_re), out_map(-c_im)], axis=1).astype(BF16)
    a_re = ab_re.reshape(n_blk, 1, S5_BLOCK_STATE)
    a_im = ab_im.reshape(n_blk, 1, S5_BLOCK_STATE)
    return a_re, a_im, bmat, cmat


def _odd_out_kernel(x_ref, y_ref, z_ref, w1_ref, w2_ref, wo_ref, o_ref):
    yb = jax.nn.gelu(y_ref[...]).astype(BF16)
    t = _dot(yb, w1_ref[...]) * jax.nn.sigmoid(_dot(yb, w2_ref[...])) * z_ref[...]
    o_ref[...] = x_ref[...] + _dot(t.astype(BF16), wo_ref[...])


def _odd_out(x2d, y2d, z2d, w1, w2, wo):
    m, d = x2d.shape
    e = y2d.shape[1]
    row = lambda i: (i, 0)
    fixed = lambda i: (0, 0)
    return pl.pallas_call(
        _odd_out_kernel,
        grid=(m // ROW_TILE,),
        in_specs=[
            pl.BlockSpec((ROW_TILE, d), row),
            pl.BlockSpec((ROW_TILE, e), row),
            pl.BlockSpec((ROW_TILE, e), row),
            pl.BlockSpec((e, e), fixed),
            pl.BlockSpec((e, e), fixed),
            pl.BlockSpec((e, d), fixed),
        ],
        out_specs=pl.BlockSpec((ROW_TILE, d), row),
        out_shape=jax.ShapeDtypeStruct((m, d), F32),
        compiler_params=_cparams("parallel"),
        name="odd_out_proj",
    )(x2d, y2d, z2d, w1, w2, wo)


def _rope_tables(pos):
    half = NSA_DH // 2
    inv = ROPE_THETA ** (-jnp.arange(half, dtype=F32) / half)
    ang = pos.astype(F32)[:, None] * inv[None, :]
    cos, sin = jnp.cos(ang), jnp.sin(ang)
    reps = LANES // NSA_DH
    return jnp.tile(cos, (1, 2 * reps)), jnp.tile(jnp.concatenate([-sin, sin], axis=1), (1, reps))


def _even_weights(norm_g, w_in, w_out, q_norm, k_norm, cmp_pos, cmp_w):
    d = w_in.shape[0]
    sizes = (RET_HEADS * RET_DK, RET_HEADS * RET_DK, A_WIDTH, A_WIDTH, B_WIDTH, 6 * NSA_KV_HEADS * NSA_DH,
             N_GATES, B_WIDTH)
    parts, o = [], 0
    for s in sizes:
        parts.append(w_in[:, o:o + s])
        o += s
    qa, ka, va, za, qn, kvb, gl, zb = parts
    w_perm = jnp.concatenate([qa, ka, va, za, qn, kvb, zb, gl, jnp.zeros((d, LANES - N_GATES), w_in.dtype)],
                             axis=1).astype(BF16)
    reps = LANES // NSA_DH
    qg = jnp.tile(q_norm[None, :], (1, reps))
    kg = jnp.tile(k_norm, (1, reps))
    eye = jnp.eye(NSA_KV_HEADS, dtype=cmp_w.dtype)
    w_bd = jnp.einsum('clde,gh,ck->lcgdkhe', cmp_w, eye, jnp.eye(2, dtype=cmp_w.dtype))
    w_bd = w_bd.reshape(L_CMP, 2 * LANES, 2 * LANES).astype(BF16)
    pe = jnp.broadcast_to(cmp_pos.transpose(1, 0, 2)[:, :, None, :], (L_CMP, 2, NSA_KV_HEADS, NSA_DH))
    pe = pe.reshape(L_CMP, 2 * LANES)
    wa = w_out[:A_WIDTH].astype(BF16)
    wb = w_out[A_WIDTH:].astype(BF16)
    return norm_g[None, :], w_perm, qg, kg, w_bd, pe, wa, wb


def _even_layer(x, pos0, s_ret, ew, gn_gain, cache=None, page_table=None, win_past=None):
    batch, seq, d = x.shape
    gain, w_perm, qg, kg, w_bd, pe, wa, wb = ew
    x2d = x.reshape(batch * seq, d)
    pos = pos0 + jnp.arange(seq, dtype=jnp.int32)
    cos_t, sin_t = _rope_tables(pos)
    if seq >= ROW_TILE:
        n_table_blocks = seq // ROW_TILE
    else:
        cos_t = jnp.tile(cos_t, (ROW_TILE // seq, 1))
        sin_t = jnp.tile(sin_t, (ROW_TILE // seq, 1))
        n_table_blocks = 1
    qa, ka, va, za, qn, full_new, win_new, zb, gates = _even_in(x2d, gain, w_perm, cos_t, sin_t, qg, kg,
                                                                 n_table_blocks)
    oa, s_fin = _retention(qa, ka, va, za, gn_gain[None, :], s_ret, batch, seq)

    if cache is None:
        full3 = full_new.reshape(batch, seq, KV_ROW)
        n_half = seq // CMP_STRIDE
        src_specs = [pl.BlockSpec((1, seq, LANES), lambda b: (b, 0, 0)),
                     pl.BlockSpec((1, seq, LANES), lambda b: (b, 0, 1))]
        cmp_kv = _compress([full3, full3], src_specs, (batch,), (), pe, w_bd, batch, seq, n_half)
        overlap, expand = _selection_tables(n_half, seq)
        win3 = win_new.reshape(batch, seq, WIN_ROW)
        ob = _nsa_prompt(qn, gates, zb, cmp_kv, full3, win3, overlap, expand, batch, seq)
        keep = min(WINDOW, seq)
        win_out = win3[:, seq - keep:]
    else:
        n_pages = page_table.shape[1]
        past_len = n_pages * PAGE_SIZE
        n_half = past_len // CMP_STRIDE
        specs = [pl.BlockSpec((1, PAGE_SIZE, LANES), functools.partial(lambda b, pt, p, c: (pt[b, p], 0, c), p=p, c=c))
                 for c in range(2) for p in range(n_pages)]
        cmp_kv = _compress([cache] * (2 * n_pages), specs, (batch,), (page_table,), pe, w_bd, batch, PAGE_SIZE,
                           n_half)
        overlap, expand = _selection_tables(n_half, past_len + KEY_TILE)
        ob, win_out = _nsa_sample(page_table, qn, gates, zb, cmp_kv, full_new, win_new, win_past, cache,
                                  overlap, expand, batch, seq, n_pages)
    y = _even_out(x2d, oa, ob, wa, wb).reshape(batch, seq, d)
    g, dh = NSA_KV_HEADS, NSA_DH
    return (y, s_fin, full_new.reshape(batch, seq, 4, g, dh),
            win_out.reshape(batch, win_out.shape[1], 2, g, dh))


def _odd_layer(x, s_re, s_im, gain, w_in_bf, s5p, d_row, w1, w2, wo, tt):
    batch, seq, d = x.shape
    x2d = x.reshape(batch * seq, d)
    u, zs = _odd_in(x2d, gain, w_in_bf)
    e = u.shape[1]
    a_re, a_im, bmat, cmat = s5p
    n_groups, n_state = s_re.shape[1], s_re.shape[2]
    y, f_re, f_im = _s5(u.reshape(batch, seq, e), s_re.reshape(batch, n_groups * n_state),
                        s_im.reshape(batch, n_groups * n_state), a_re, a_im, bmat, cmat, d_row, tt)
    out = _odd_out(x2d, y.reshape(batch * seq, e), zs, w1, w2, wo).reshape(batch, seq, d)
    return out, f_re.reshape(batch, n_groups, n_state), f_im.reshape(batch, n_groups, n_state)


S5_TIME_TILE = 128


def kernel(x_prompt, x_sample, cache_nsa_kv, cache_nsa_win, state_ret, state_ssm_re, state_ssm_im, page_table,
           norm_even, w_in_even, w_out_even, ret_gn_gain, nsa_q_norm, nsa_k_norm, nsa_cmp_pos, nsa_cmp_w,
           norm_odd, w_in_odd, ssm_lambda_re, ssm_lambda_im, ssm_b_re, ssm_b_im, ssm_c_re, ssm_c_im, ssm_d,
           ssm_log_step, glu_w1, glu_w2, w_out_odd):
    bp, seq_p, _ = x_prompt.shape
    db, seq_s, _ = x_sample.shape
    n_pages = page_table.shape[1]
    past_len = n_pages * PAGE_SIZE
    depth = norm_even.shape[0] + norm_odd.shape[0]
    yp, ys = x_prompt, x_sample
    ret_p, ret_s, kv_p, kv_s, win_p, win_s = [], [], [], [], [], []
    sre_p, sim_p, sre_s, sim_s = [], [], [], []
    for layer in range(depth):
        li = layer // 2
        if layer % 2 == 0:
            ew = _even_weights(norm_even[li], w_in_even[li], w_out_even[li], nsa_q_norm[li], nsa_k_norm[li],
                               nsa_cmp_pos[li], nsa_cmp_w[li])
            s0 = jnp.zeros((bp, RET_HEADS, RET_DK, RET_DV), F32)
            yp, sr, kvr, wr = _even_layer(yp, 0, s0, ew, ret_gn_gain[li])
            ret_p.append(sr); kv_p.append(kvr); win_p.append(wr)
            cache2 = cache_nsa_kv[li].reshape(cache_nsa_kv.shape[1], PAGE_SIZE, KV_ROW)
            winp = cache_nsa_win[li].reshape(db, cache_nsa_win.shape[2], WIN_ROW)
            ys, sr2, kvr2, wr2 = _even_layer(ys, past_len, state_ret[li], ew, ret_gn_gain[li],
                                             cache=cache2, page_table=page_table, win_past=winp)
            ret_s.append(sr2); kv_s.append(kvr2); win_s.append(wr2)
        else:
            s5p = _s5_params(ssm_lambda_re[li], ssm_lambda_im[li], ssm_b_re[li], ssm_b_im[li],
                             ssm_c_re[li], ssm_c_im[li], ssm_log_step[li])
            gain = norm_odd[li][None, :]
            w_in_bf = w_in_odd[li].astype(BF16)
            w1, w2, wo = glu_w1[li].astype(BF16), glu_w2[li].astype(BF16), w_out_odd[li].astype(BF16)
            d_row = ssm_d[li][None, :]
            n_groups = ssm_lambda_re.shape[1]
            z0 = jnp.zeros((bp, n_groups, S5_STATE), F32)
            yp, fr, fi = _odd_layer(yp, z0, z0, gain, w_in_bf, s5p, d_row, w1, w2, wo, min(S5_TIME_TILE, seq_p))
            sre_p.append(fr); sim_p.append(fi)
            ys, fr2, fi2 = _odd_layer(ys, state_ssm_re[li], state_ssm_im[li], gain, w_in_bf, s5p, d_row,
                                      w1, w2, wo, min(S5_TIME_TILE, seq_s))
            sre_s.append(fr2); sim_s.append(fi2)
    return (yp, ys, jnp.stack(ret_p), jnp.stack(ret_s), jnp.stack(kv_p), jnp.stack(kv_s), jnp.stack(win_p),
            jnp.stack(win_s), jnp.stack(sre_p), jnp.stack(sim_p), jnp.stack(sre_s), jnp.stack(sim_s))
```

```python
import functools
import math

import jax
import jax.numpy as jnp
from jax import lax
from jax.experimental import pallas as pl
from jax.experimental.pallas import tpu as pltpu

F32 = jnp.float32
BF16 = jnp.bfloat16

LANES = 128
SUBLANES = 8
VMEM_LIMIT_BYTES = 48 * 2**20

EPS = 1e-6
ROPE_THETA = 10000.0
NEG_INF = -1e30
FORCE = 1e4

RET_HEADS = 4
RET_DK = 64
RET_DV = 128
RET_CHUNK = 128
A_WIDTH = RET_HEADS * RET_DV

NSA_HEADS = 8
NSA_KV_HEADS = 2
NSA_DH = 64
NSA_HPG = NSA_HEADS // NSA_KV_HEADS
B_WIDTH = NSA_HEADS * NSA_DH
L_CMP = 32
CMP_STRIDE = 16
CMP_RATIO = L_CMP // CMP_STRIDE
L_SEL = 64
N_SEL = 8
WINDOW = 512
PAGE_SIZE = 128
KV_ROW = 4 * NSA_KV_HEADS * NSA_DH
WIN_ROW = 2 * NSA_KV_HEADS * NSA_DH
KEY_TILE = 128
WIN_KEYS = WINDOW + KEY_TILE
SEL_CHUNK = 512

S5_GROUP = 16
S5_STATE = 64
S5_LANE_GROUPS = LANES // S5_GROUP
S5_BLOCK_STATE = S5_LANE_GROUPS * S5_STATE
S5_ROW_PAD = 4
S5_SCAN_UNROLL = 4

QA0, KA0, VA0, ZA0, QN0, KVB0, ZB0, GL0 = 0, 256, 512, 1024, 1536, 2048, 2816, 3328
EVEN_COLS = GL0 + LANES
N_GATES = 3 * NSA_HEADS

ROW_TILE = 256


def _cparams(*sem):
    return pltpu.CompilerParams(dimension_semantics=sem, vmem_limit_bytes=VMEM_LIMIT_BYTES)


def _lane_iota(shape):
    return lax.broadcasted_iota(jnp.int32, shape, len(shape) - 1)


def _row_iota(shape):
    return lax.broadcasted_iota(jnp.int32, shape, len(shape) - 2)


def _dot(a, b):
    return jnp.dot(a, b, preferred_element_type=F32)


def _dot_nt(a, b):
    return lax.dot_general(a, b, (((1,), (1,)), ((), ())), preferred_element_type=F32)


def _rmsnorm_rows(x, g):
    return x * lax.rsqrt(jnp.mean(x * x, axis=-1, keepdims=True) + EPS) * g


def _swap_halves(x):
    return pltpu.roll(x, NSA_DH, axis=1)


def _rope_block(x, cos, sin_signed):
    half = NSA_DH // 2
    lane = _lane_iota(x.shape)
    first = (lane % NSA_DH) < half
    partner = jnp.where(first, pltpu.roll(x, LANES - half, axis=1), pltpu.roll(x, half, axis=1))
    return x * cos + partner * sin_signed


def _head_rms_block(x, g):
    lane = _lane_iota(x.shape)
    lo = lane < NSA_DH
    sq = x * x
    s_lo = jnp.sum(jnp.where(lo, sq, 0.0), axis=-1, keepdims=True)
    s_hi = jnp.sum(jnp.where(lo, 0.0, sq), axis=-1, keepdims=True)
    ms = jnp.where(lo, s_lo, s_hi) * (1.0 / NSA_DH)
    return x * lax.rsqrt(ms + EPS) * g


def _even_in_kernel(x_ref, g_ref, w_ref, cos_ref, sin_ref, qg_ref, kg_ref, *out_refs, transposed_kv):
    if transposed_kv:
        qa_ref, ka_ref, va_ref, za_ref, qn_ref, full_ref, win_ref, zb_ref, gl_ref, cmp_ref = out_refs
    else:
        qa_ref, ka_ref, va_ref, za_ref, qn_ref, full_ref, win_ref, zb_ref, gl_ref = out_refs
    hb = _rmsnorm_rows(x_ref[...], g_ref[...]).astype(BF16)
    cos = cos_ref[...]
    sin = sin_ref[...]

    def proj(c0):
        return _dot(hb, w_ref[:, c0:c0 + LANES])

    for j in range(RET_HEADS * RET_DK // LANES):
        c = j * LANES
        qa_ref[:, c:c + LANES] = _rope_block(proj(QA0 + c), cos, sin)
        ka_ref[:, c:c + LANES] = _rope_block(proj(KA0 + c), cos, sin) * (RET_DK ** -0.5)
    for j in range(A_WIDTH // LANES):
        c = j * LANES
        va_ref[:, c:c + LANES] = proj(VA0 + c)
        za_ref[:, c:c + LANES] = jax.nn.silu(proj(ZA0 + c))
    for j in range(B_WIDTH // LANES):
        c = j * LANES
        qn_ref[:, c:c + LANES] = _rope_block(_head_rms_block(proj(QN0 + c), qg_ref[...]), cos, sin)
        zb_ref[:, c:c + LANES] = jax.nn.silu(proj(ZB0 + c))
    for j in range(6):
        y = proj(KVB0 + j * LANES)
        if j % 2 == 0:
            y = _rope_block(_head_rms_block(y, kg_ref[j // 2:j // 2 + 1, :]), cos, sin)
        if transposed_kv:
            if j < 4:
                full_ref[0, j * LANES:(j + 1) * LANES, :] = y.T
            else:
                win_ref[0, (j - 4) * LANES:(j - 3) * LANES, :] = y.T
            if j < 2:
                cmp_ref[:, j * LANES:(j + 1) * LANES] = y
        elif j < 4:
            full_ref[:, j * LANES:(j + 1) * LANES] = y
        else:
            win_ref[:, (j - 4) * LANES:(j - 3) * LANES] = y
    gl_ref[...] = jax.nn.sigmoid(proj(GL0))


def _even_in(x2d, gain, w_bf, cos_t, sin_t, qg, kg, n_table_blocks, seq_tiles=None):
    m, d = x2d.shape
    grid = (m // ROW_TILE,)
    row = lambda i: (i, 0)
    fixed = lambda i: (0, 0)
    table = lambda i: (i % n_table_blocks, 0)
    widths = (RET_HEADS * RET_DK, RET_HEADS * RET_DK, A_WIDTH, A_WIDTH, B_WIDTH, KV_ROW, WIN_ROW, B_WIDTH, LANES)
    out_specs = [pl.BlockSpec((ROW_TILE, w), row) for w in widths]
    out_shape = [jax.ShapeDtypeStruct((m, w), F32) for w in widths]
    if seq_tiles is not None:
        batch, seq = grid[0] // seq_tiles, seq_tiles * ROW_TILE
        fmajor = lambda i: (i // seq_tiles, 0, i % seq_tiles)
        for idx, w in ((5, KV_ROW), (6, WIN_ROW)):
            out_specs[idx] = pl.BlockSpec((1, w, ROW_TILE), fmajor)
            out_shape[idx] = jax.ShapeDtypeStruct((batch, w, seq), F32)
        out_specs.append(pl.BlockSpec((ROW_TILE, 2 * LANES), row))
        out_shape.append(jax.ShapeDtypeStruct((m, 2 * LANES), F32))
    return pl.pallas_call(
        functools.partial(_even_in_kernel, transposed_kv=seq_tiles is not None),
        grid=grid,
        in_specs=[
            pl.BlockSpec((ROW_TILE, d), row),
            pl.BlockSpec((1, d), fixed),
            pl.BlockSpec((d, EVEN_COLS), fixed),
            pl.BlockSpec((ROW_TILE, LANES), table),
            pl.BlockSpec((ROW_TILE, LANES), table),
            pl.BlockSpec((1, LANES), fixed),
            pl.BlockSpec((3, LANES), fixed),
        ],
        out_specs=out_specs,
        out_shape=out_shape,
        compiler_params=_cparams("parallel"),
        name="even_in_proj",
    )(x2d, gain, w_bf, cos_t, sin_t, qg, kg)


def _retention_kernel(q_ref, k_ref, v_ref, z_ref, gn_ref, s0_ref, dmat_ref, qdec_ref, kdec_ref, cdec_ref,
                      o_ref, sfin_ref, s_scr, *, rows):
    c = pl.program_id(1)

    @pl.when(c == 0)
    def _():
        s_scr[...] = s0_ref[0]

    def padded(ref):
        x = ref[...]
        if rows == RET_CHUNK:
            return x
        return jnp.concatenate([x, jnp.zeros((RET_CHUNK - rows, x.shape[1]), F32)], axis=0)

    q = padded(q_ref)
    k = padded(k_ref)
    v = padded(v_ref)
    qd = q * qdec_ref[...]
    kd = k * kdec_ref[...]
    lane = _lane_iota((RET_CHUNK, LANES))
    for pair in range(RET_HEADS // 2):
        cols = slice(pair * LANES, (pair + 1) * LANES)
        q2, k2b, qd2, kd2 = q[:, cols], k[:, cols].astype(BF16), qd[:, cols], kd[:, cols]
        kd2_t = kd2.T
        s_pair = jnp.concatenate([s_scr[2 * pair], s_scr[2 * pair + 1]], axis=0).astype(BF16)
        for sub in range(2):
            h = 2 * pair + sub
            mine = (lane >= sub * RET_DK) & (lane < (sub + 1) * RET_DK)
            qm = jnp.where(mine, q2, 0.0).astype(BF16)
            qdm = jnp.where(mine, qd2, 0.0).astype(BF16)
            vh = v[:, h * RET_DV:(h + 1) * RET_DV]
            vhb = vh.astype(BF16)
            scores = _dot_nt(qm, k2b) * dmat_ref[h]
            out = _dot(scores.astype(BF16), vhb) + _dot(qdm, s_pair)
            kt = kd2_t[sub * RET_DK:(sub + 1) * RET_DK, :].astype(BF16)
            s_scr[h] = s_scr[h] * cdec_ref[h] + _dot(kt, vhb)
            mu = jnp.mean(out, axis=-1, keepdims=True)
            cen = out - mu
            var = jnp.mean(cen * cen, axis=-1, keepdims=True)
            y = cen * lax.rsqrt(var + EPS) * gn_ref[:, h * RET_DV:(h + 1) * RET_DV]
            o_ref[:, h * RET_DV:(h + 1) * RET_DV] = y[:rows] * z_ref[:, h * RET_DV:(h + 1) * RET_DV]

    @pl.when(c == pl.num_programs(1) - 1)
    def _():
        sfin_ref[0] = s_scr[...]


def _retention_tables(rows):
    log_g = jnp.log(1.0 - 2.0 ** (-5.0 - jnp.arange(RET_HEADS, dtype=F32)))
    idx = jnp.arange(RET_CHUNK, dtype=F32)
    live = idx < rows
    diff = idx[:, None] - idx[None, :]
    causal = (diff >= 0) & live[:, None] & live[None, :]
    dmat = jnp.exp(jnp.where(causal, diff, 0.0)[None] * log_g[:, None, None]) * causal[None]
    qdec = jnp.exp((idx + 1.0)[:, None] * log_g[None, :]) * live[:, None]
    kdec = jnp.exp((rows - 1.0 - idx)[:, None] * log_g[None, :]) * live[:, None]
    qdec = jnp.repeat(qdec, RET_DK, axis=1)
    kdec = jnp.repeat(kdec, RET_DK, axis=1)
    cdec = jnp.broadcast_to(jnp.exp(rows * log_g)[:, None, None], (RET_HEADS, RET_DK, RET_DV))
    return dmat, qdec, kdec, cdec


def _retention(qa, ka, va, za, gn_gain, s0, batch, seq):
    rows = min(RET_CHUNK, seq)
    n_chunks = seq // rows
    dmat, qdec, kdec, cdec = _retention_tables(rows)
    tok = lambda b, c: (b * n_chunks + c, 0)
    fixed2 = lambda b, c: (0, 0)
    fixed3 = lambda b, c: (0, 0, 0)
    state = lambda b, c: (b, 0, 0, 0)
    qk_w = RET_HEADS * RET_DK
    return pl.pallas_call(
        functools.partial(_retention_kernel, rows=rows),
        grid=(batch, n_chunks),
        in_specs=[
            pl.BlockSpec((rows, qk_w), tok),
            pl.BlockSpec((rows, qk_w), tok),
            pl.BlockSpec((rows, A_WIDTH), tok),
            pl.BlockSpec((rows, A_WIDTH), tok),
            pl.BlockSpec((1, A_WIDTH), fixed2),
            pl.BlockSpec((1, RET_HEADS, RET_DK, RET_DV), state),
            pl.BlockSpec((RET_HEADS, RET_CHUNK, RET_CHUNK), fixed3),
            pl.BlockSpec((RET_CHUNK, qk_w), fixed2),
            pl.BlockSpec((RET_CHUNK, qk_w), fixed2),
            pl.BlockSpec((RET_HEADS, RET_DK, RET_DV), fixed3),
        ],
        out_specs=[
            pl.BlockSpec((rows, A_WIDTH), tok),
            pl.BlockSpec((1, RET_HEADS, RET_DK, RET_DV), state),
        ],
        out_shape=[
            jax.ShapeDtypeStruct((batch * seq, A_WIDTH), F32),
            jax.ShapeDtypeStruct((batch, RET_HEADS, RET_DK, RET_DV), F32),
        ],
        scratch_shapes=[pltpu.VMEM((RET_HEADS, RET_DK, RET_DV), F32)],
        compiler_params=_cparams("parallel", "arbitrary"),
        name="retention",
    )(qa, ka, va, za, gn_gain, s0, dmat, qdec, kdec, cdec)


def _compress_kernel(*refs, n_pref, n_src, src_rows):
    refs = refs[n_pref:]
    src_k = refs[:n_src]
    src_v = refs[n_src:2 * n_src]
    pe_ref, w_ref, o_ref = refs[2 * n_src:]
    per_src = src_rows // CMP_STRIDE
    n_half = n_src * per_src
    acc = [jnp.zeros((n_half, 2 * LANES), F32) for _ in range(CMP_RATIO)]

    def strided_rows(srcs, l):
        rows = [s[0, pl.ds(l, per_src, stride=CMP_STRIDE), :] for s in srcs]
        return rows[0] if n_src == 1 else jnp.concatenate(rows, axis=0)

    for l in range(CMP_STRIDE):
        x = jnp.concatenate([strided_rows(src_k, l), strided_rows(src_v, l)], axis=1)
        for r in range(CMP_RATIO):
            i = r * CMP_STRIDE + l
            acc[r] = acc[r] + _dot((x + pe_ref[i:i + 1, :]).astype(BF16), w_ref[i])
    out = acc[0]
    for r in range(1, CMP_RATIO):
        out = out + pltpu.roll(acc[r], n_half - r, axis=0)
    live = _row_iota(out.shape) < n_half - CMP_RATIO + 1
    o_ref[0] = jnp.where(live, out, 0.0)


def _compress(src_arrays, src_specs, grid, extra_prefetch, pe, w_bd, batch, src_rows, n_half):
    n_src = len(src_specs) // 2
    n_pref = len(extra_prefetch)
    kern = functools.partial(_compress_kernel, n_pref=n_pref, n_src=n_src, src_rows=src_rows)
    fixed2 = lambda *a: (0, 0)
    fixed3 = lambda *a: (0, 0, 0)
    out_map = lambda b, *a: (b, 0, 0)
    return pl.pallas_call(
        kern,
        grid_spec=pltpu.PrefetchScalarGridSpec(
            num_scalar_prefetch=n_pref,
            grid=grid,
            in_specs=list(src_specs) + [
                pl.BlockSpec((L_CMP, 2 * LANES), fixed2),
                pl.BlockSpec((L_CMP, 2 * LANES, 2 * LANES), fixed3),
            ],
            out_specs=pl.BlockSpec((1, n_half, 2 * LANES), out_map),
        ),
        out_shape=jax.ShapeDtypeStruct((batch, n_half, 2 * LANES), F32),
        compiler_params=_cparams("parallel"),
        name="kv_compress",
    )(*extra_prefetch, *src_arrays, pe, w_bd)


def _softmax_rows(s):
    m = jnp.max(s, axis=-1, keepdims=True)
    e = jnp.exp(s - m)
    return e / jnp.sum(e, axis=-1, keepdims=True)


def _split3_bf16(x):
    hi = x.astype(BF16)
    r1 = x - hi.astype(F32)
    mid = r1.astype(BF16)
    lo = (r1 - mid.astype(F32)).astype(BF16)
    return hi, mid, lo


def _nsa_body(q, gates, zs, cmp_blk, sel_k, sel_v, win_k, win_v, tpos, sel_kpos, win_kpos,
              overlap, expand, n_blocks, o_ref):
    tq = q.shape[0]
    n_cmp = cmp_blk.shape[0]
    lane = _lane_iota((tq, LANES))
    ck = cmp_blk[:, :LANES].astype(BF16)
    cv = cmp_blk[:, LANES:].astype(BF16)
    sel_kb, sel_vb = sel_k.astype(BF16), sel_v.astype(BF16)
    win_kb, win_vb = win_k.astype(BF16), win_v.astype(BF16)
    tpos4 = jnp.concatenate([tpos] * NSA_HPG, axis=0)

    cmp_end = _lane_iota((1, n_cmp)) * CMP_STRIDE + (L_CMP - 1)
    cmp_ok4 = cmp_end <= tpos4
    win_ok4 = (win_kpos <= tpos4) & (win_kpos > tpos4 - WINDOW)
    causal_sel = sel_kpos <= tpos

    blk = _lane_iota((tq, LANES))
    cur = tpos >> int(math.log2(L_SEL))
    forced = (blk == 0) | (blk == cur) | (blk == cur - 1)
    valid = (blk * L_SEL <= tpos) & (blk < n_blocks)

    ext = []
    for g in range(NSA_KV_HEADS):
        in_g = (lane >= g * NSA_DH) & (lane < (g + 1) * NSA_DH)
        parts = []
        for hh in range(NSA_HPG):
            h = g * NSA_HPG + hh
            two = q[:, (h // 2) * LANES:(h // 2 + 1) * LANES]
            if h % 2 != g:
                two = _swap_halves(two)
            parts.append(jnp.where(in_g, two, 0.0))
        qs = (jnp.concatenate(parts, axis=0) * (NSA_DH ** -0.5)).astype(BF16)

        s1 = _dot_nt(qs, ck)
        p1 = _softmax_rows(jnp.where(cmp_ok4, s1, NEG_INF)) * cmp_ok4.astype(F32)
        o_cmp = _dot(p1.astype(BF16), cv)

        psum = p1[0:tq]
        for hh in range(1, NSA_HPG):
            psum = psum + p1[hh * tq:(hh + 1) * tq]
        imp = sum(_dot(t, overlap) for t in _split3_bf16(psum))
        score = jnp.where(valid, jnp.where(forced, FORCE, imp), -FORCE)
        rank = jnp.zeros((tq, LANES), F32)
        for s in range(n_blocks):
            col = score[:, s:s + 1]
            ahead = (col > score) | ((col == score) & (blk > s))
            rank = rank + ahead.astype(F32)
        chosen = ((rank < N_SEL) & valid).astype(BF16)
        key_on = (_dot(chosen, expand) > 0.5) & causal_sel
        bias = jnp.where(key_on, 0.0, NEG_INF)
        bias4 = jnp.concatenate([bias] * NSA_HPG, axis=0)

        p2 = _softmax_rows(_dot_nt(qs, sel_kb) + bias4)
        o_sel = _dot(p2.astype(BF16), sel_vb)

        s3 = _dot_nt(qs, win_kb)
        p3 = _softmax_rows(jnp.where(win_ok4, s3, NEG_INF))
        o_win = _dot(p3.astype(BF16), win_vb)

        for hh in range(NSA_HPG):
            h = g * NSA_HPG + hh
            rows = slice(hh * tq, (hh + 1) * tq)
            o = (gates[:, 3 * h:3 * h + 1] * o_cmp[rows] + gates[:, 3 * h + 1:3 * h + 2] * o_sel[rows]
                 + gates[:, 3 * h + 2:3 * h + 3] * o_win[rows])
            if h % 2 != g:
                o = _swap_halves(o)
            ext.append(o)

    for j in range(NSA_HEADS // 2):
        both = jnp.where(lane < NSA_DH, ext[2 * j], ext[2 * j + 1])
        o_ref[:, j * LANES:(j + 1) * LANES] = both * zs[:, j * LANES:(j + 1) * LANES]


def _stack_group_queries(q, g):
    lane = _lane_iota((q.shape[0], LANES))
    in_g = (lane >= g * NSA_DH) & (lane < (g + 1) * NSA_DH)
    parts = []
    for hh in range(NSA_HPG):
        h = g * NSA_HPG + hh
        two = q[:, (h // 2) * LANES:(h // 2 + 1) * LANES]
        if h % 2 != g:
            two = _swap_halves(two)
        parts.append(jnp.where(in_g, two, 0.0))
    return (jnp.concatenate(parts, axis=0) * (NSA_DH ** -0.5)).astype(BF16)


def _compressed_branch(qs, ck, cv, cmp_ok4):
    s1 = jnp.where(cmp_ok4, _dot_nt(qs, ck), NEG_INF)
    e = jnp.exp(s1 - jnp.max(s1, axis=-1, keepdims=True))
    p1 = jnp.where(cmp_ok4, e * (1.0 / jnp.sum(e, axis=-1, keepdims=True)), 0.0)
    return _dot(p1.astype(BF16), cv), p1


def _choose_blocks(p1, tq, tpos, overlap, n_blocks):
    blk = _lane_iota((tq, LANES))
    cur = tpos >> int(math.log2(L_SEL))
    forced = (blk == 0) | (blk == cur) | (blk == cur - 1)
    valid = (blk * L_SEL <= tpos) & (blk < n_blocks)
    psum = p1[0:tq]
    for hh in range(1, NSA_HPG):
        psum = psum + p1[hh * tq:(hh + 1) * tq]
    imp = sum(_dot(t, overlap) for t in _split3_bf16(psum))
    score = jnp.where(valid, jnp.where(forced, FORCE, imp), -FORCE)
    rank = jnp.zeros((tq, LANES), F32)
    for s in range(n_blocks):
        col = score[:, s:s + 1]
        ahead = (col > score) | ((col == score) & (blk > s))
        rank = rank + ahead.astype(F32)
    return ((rank < N_SEL) & valid).astype(BF16)


def _gate_and_store(gates, zs, branch_outs, o_ref):
    tq = gates.shape[0]
    lane = _lane_iota((tq, LANES))
    ext = []
    for g in range(NSA_KV_HEADS):
        o_cmp, o_sel, o_win = branch_outs[g]
        for hh in range(NSA_HPG):
            h = g * NSA_HPG + hh
            rows = slice(hh * tq, (hh + 1) * tq)
            o = (gates[:, 3 * h:3 * h + 1] * o_cmp[rows] + gates[:, 3 * h + 1:3 * h + 2] * o_sel[rows]
                 + gates[:, 3 * h + 2:3 * h + 3] * o_win[rows])
            ext.append(_swap_halves(o) if h % 2 != g else o)
    for j in range(NSA_HEADS // 2):
        both = jnp.where(lane < NSA_DH, ext[2 * j], ext[2 * j + 1])
        o_ref[:, j * LANES:(j + 1) * LANES] = both * zs[:, j * LANES:(j + 1) * LANES]


def _nsa_prompt_kernel(q_ref, g_ref, z_ref, cmp_ref, sel_ref, win_ref, ov_ref, ex_ref, o_ref, *, n_blocks):
    i = pl.program_id(1)
    tq = q_ref.shape[0]
    q = q_ref[...]
    tpos = i * tq + _row_iota((tq, 1))
    tpos4 = jnp.concatenate([tpos] * NSA_HPG, axis=0)
    cmp_blk = cmp_ref[0]
    ck = cmp_blk[:, :LANES].astype(BF16)
    cv = cmp_blk[:, LANES:].astype(BF16)
    cmp_ok4 = (_lane_iota((1, cmp_blk.shape[0])) * CMP_STRIDE + (L_CMP - 1)) <= tpos4

    qs, o_cmp, chosen = [], [], []
    for g in range(NSA_KV_HEADS):
        qs.append(_stack_group_queries(q, g))
        o, p1 = _compressed_branch(qs[g], ck, cv, cmp_ok4)
        o_cmp.append(o)
        chosen.append(_choose_blocks(p1, tq, tpos, ov_ref[...], n_blocks))

    def sel_chunk(c, carry):
        k0 = pl.multiple_of(c * SEL_CHUNK, SEL_CHUNK)
        k_t = sel_ref[0, 0:LANES, pl.ds(k0, SEL_CHUNK)].astype(BF16)
        v_t = sel_ref[0, LANES:2 * LANES, pl.ds(k0, SEL_CHUNK)].astype(BF16)
        causal = (k0 + _lane_iota((1, SEL_CHUNK))) <= tpos
        expand = ex_ref[:, pl.ds(k0, SEL_CHUNK)]
        out = []
        for g in range(NSA_KV_HEADS):
            m, l, acc = carry[g]
            key_on = (_dot(chosen[g], expand) > 0.5) & causal
            bias = jnp.where(key_on, 0.0, NEG_INF)
            s = _dot(qs[g], k_t) + jnp.concatenate([bias] * NSA_HPG, axis=0)
            m_new = jnp.maximum(m, jnp.max(s, axis=-1, keepdims=True))
            alpha = jnp.exp(m - m_new)
            e = jnp.exp(s - m_new)
            l_new = alpha * l + jnp.sum(e, axis=-1, keepdims=True)
            out.append((m_new, l_new, alpha * acc + _dot_nt(e.astype(BF16), v_t)))
        return tuple(out)

    rows = NSA_HPG * tq
    init = (jnp.full((rows, 1), NEG_INF, F32), jnp.zeros((rows, 1), F32), jnp.zeros((rows, LANES), F32))
    n_chunks = (i * tq) // SEL_CHUNK + 1
    sel_state = lax.fori_loop(0, n_chunks, sel_chunk, (init,) * NSA_KV_HEADS)

    w0 = pl.multiple_of(jnp.maximum(i * tq - WINDOW, 0), KEY_TILE)
    wk_t = win_ref[0, 0:LANES, pl.ds(w0, WIN_KEYS)].astype(BF16)
    wv_t = win_ref[0, LANES:2 * LANES, pl.ds(w0, WIN_KEYS)].astype(BF16)
    win_kpos = w0 + _lane_iota((1, WIN_KEYS))
    win_ok4 = (win_kpos <= tpos4) & (win_kpos > tpos4 - WINDOW)

    branch_outs = []
    for g in range(NSA_KV_HEADS):
        _, l, acc = sel_state[g]
        s3 = jnp.where(win_ok4, _dot(qs[g], wk_t), NEG_INF)
        e3 = jnp.exp(s3 - jnp.max(s3, axis=-1, keepdims=True))
        o_win = _dot_nt(e3.astype(BF16), wv_t) / jnp.sum(e3, axis=-1, keepdims=True)
        branch_outs.append((o_cmp[g], acc / l, o_win))
    _gate_and_store(g_ref[...], z_ref[...], branch_outs, o_ref)


def _nsa_prompt(qn, gates, zb, cmp_kv, full_t, win_t, overlap, expand, batch, seq):
    tq = KEY_TILE
    nq = seq // tq
    tok = lambda b, i: (b * nq + i, 0)
    per_b = lambda b, i: (b, 0, 0)
    sel_half = lambda b, i: (b, 1, 0)
    fixed = lambda b, i: (0, 0)
    n_blocks = -(-seq // L_SEL)
    return pl.pallas_call(
        functools.partial(_nsa_prompt_kernel, n_blocks=n_blocks),
        grid=(batch, nq),
        in_specs=[
            pl.BlockSpec((tq, B_WIDTH), tok),
            pl.BlockSpec((tq, LANES), tok),
            pl.BlockSpec((tq, B_WIDTH), tok),
            pl.BlockSpec((1, cmp_kv.shape[1], 2 * LANES), per_b),
            pl.BlockSpec((1, 2 * LANES, seq), sel_half),
            pl.BlockSpec((1, WIN_ROW, seq), per_b),
            pl.BlockSpec(overlap.shape, fixed),
            pl.BlockSpec(expand.shape, fixed),
        ],
        out_specs=pl.BlockSpec((tq, B_WIDTH), tok),
        out_shape=jax.ShapeDtypeStruct((batch * seq, B_WIDTH), F32),
        compiler_params=_cparams("parallel", "arbitrary"),
        name="nsa_prompt",
    )(qn, gates, zb, cmp_kv, full_t, win_t, overlap, expand)


def _nsa_sample_kernel(*refs, n_pages, past_len, n_blocks):
    pt_ref = refs[0]
    del pt_ref
    q_ref, g_ref, z_ref, cmp_ref, new_full_ref, new_win_ref, winp_ref = refs[1:8]
    pages = refs[8:8 + n_pages]
    ov_ref, ex_ref, o_ref, wout_ref = refs[8 + n_pages:]
    tq = q_ref.shape[0]
    w_buf = winp_ref.shape[1]
    pad = jnp.zeros((KEY_TILE - tq, LANES), F32)
    tpos = past_len + _row_iota((tq, 1))
    new_sel = new_full_ref[:, 2 * LANES:4 * LANES]
    sel_k = jnp.concatenate([p[0, :, 0:LANES] for p in pages] + [new_sel[:, :LANES], pad], axis=0)
    sel_v = jnp.concatenate([p[0, :, LANES:2 * LANES] for p in pages] + [new_sel[:, LANES:], pad], axis=0)
    new_win = new_win_ref[...]
    win_k = jnp.concatenate([winp_ref[0, :, 0:LANES], new_win[:, :LANES], pad], axis=0)
    win_v = jnp.concatenate([winp_ref[0, :, LANES:2 * LANES], new_win[:, LANES:], pad], axis=0)
    sel_kpos = _lane_iota((1, past_len + KEY_TILE))
    win_kpos = (past_len - w_buf) + _lane_iota((1, w_buf + KEY_TILE))
    _nsa_body(q_ref[...], g_ref[...], z_ref[...], cmp_ref[0], sel_k, sel_v, win_k, win_v,
              tpos, sel_kpos, win_kpos, ov_ref[...], ex_ref[...], n_blocks, o_ref)
    wout_ref[0, 0:w_buf - tq, :] = winp_ref[0, tq:w_buf, :]
    wout_ref[0, w_buf - tq:w_buf, :] = new_win


def _nsa_sample(page_table, qn, gates, zb, cmp_kv, full_new, win_new, win_past, cache, overlap, expand,
                batch, tq, n_pages):
    past_len = n_pages * PAGE_SIZE
    w_buf = win_past.shape[1]
    n_blocks = -(-(past_len + tq) // L_SEL)
    tok = lambda b, pt: (b, 0)
    per_b = lambda b, pt: (b, 0, 0)
    fixed = lambda b, pt: (0, 0)

    def page_spec(p):
        return pl.BlockSpec((1, PAGE_SIZE, 2 * LANES), lambda b, pt: (pt[b, p], 0, 1))

    return pl.pallas_call(
        functools.partial(_nsa_sample_kernel, n_pages=n_pages, past_len=past_len, n_blocks=n_blocks),
        grid_spec=pltpu.PrefetchScalarGridSpec(
            num_scalar_prefetch=1,
            grid=(batch,),
            in_specs=[
                pl.BlockSpec((tq, B_WIDTH), tok),
                pl.BlockSpec((tq, LANES), tok),
                pl.BlockSpec((tq, B_WIDTH), tok),
                pl.BlockSpec((1, cmp_kv.shape[1], 2 * LANES), per_b),
                pl.BlockSpec((tq, KV_ROW), tok),
                pl.BlockSpec((tq, WIN_ROW), tok),
                pl.BlockSpec((1, w_buf, WIN_ROW), per_b),
            ] + [page_spec(p) for p in range(n_pages)] + [
                pl.BlockSpec(overlap.shape, fixed),
                pl.BlockSpec(expand.shape, fixed),
            ],
            out_specs=[
                pl.BlockSpec((tq, B_WIDTH), tok),
                pl.BlockSpec((1, w_buf, WIN_ROW), per_b),
            ],
        ),
        out_shape=[
            jax.ShapeDtypeStruct((batch * tq, B_WIDTH), F32),
            jax.ShapeDtypeStruct((batch, w_buf, WIN_ROW), F32),
        ],
        compiler_params=_cparams("parallel"),
        name="nsa_sample",
    )(page_table, qn, gates, zb, cmp_kv, full_new, win_new, win_past, *([cache] * n_pages), overlap, expand)


def _selection_tables(n_cmp_rows, n_sel_keys):
    n = jnp.arange(n_cmp_rows, dtype=jnp.int32)[:, None]
    s = jnp.arange(LANES, dtype=jnp.int32)[None, :]
    c_start = n * CMP_STRIDE
    s_start = s * L_SEL
    overlap = ((c_start < s_start + L_SEL) & (s_start < c_start + L_CMP)).astype(BF16)
    key = jnp.arange(n_sel_keys, dtype=jnp.int32)[None, :]
    expand = ((key // L_SEL) == jnp.arange(LANES, dtype=jnp.int32)[:, None]).astype(BF16)
    return overlap, expand


def _even_out_kernel(x_ref, oa_ref, ob_ref, wa_ref, wb_ref, y_ref):
    y_ref[...] = (x_ref[...] + _dot(oa_ref[...].astype(BF16), wa_ref[...])
                  + _dot(ob_ref[...].astype(BF16), wb_ref[...]))


def _even_out(x2d, oa, ob, wa, wb):
    m, d = x2d.shape
    row = lambda i: (i, 0)
    fixed = lambda i: (0, 0)
    return pl.pallas_call(
        _even_out_kernel,
        grid=(m // ROW_TILE,),
        in_specs=[
            pl.BlockSpec((ROW_TILE, d), row),
            pl.BlockSpec((ROW_TILE, A_WIDTH), row),
            pl.BlockSpec((ROW_TILE, B_WIDTH), row),
            pl.BlockSpec((A_WIDTH, d), fixed),
            pl.BlockSpec((B_WIDTH, d), fixed),
        ],
        out_specs=pl.BlockSpec((ROW_TILE, d), row),
        out_shape=jax.ShapeDtypeStruct((m, d), F32),
        compiler_params=_cparams("parallel"),
        name="even_out_proj",
    )(x2d, oa, ob, wa, wb)


def _odd_in_kernel(x_ref, g_ref, w_ref, u_ref, z_ref):
    hb = _rmsnorm_rows(x_ref[...], g_ref[...]).astype(BF16)
    e = u_ref.shape[1]
    u_ref[...] = _dot(hb, w_ref[:, :e])
    z_ref[...] = jax.nn.silu(_dot(hb, w_ref[:, e:]))


def _odd_in(x2d, gain, w_bf):
    m, d = x2d.shape
    e = w_bf.shape[1] // 2
    row = lambda i: (i, 0)
    fixed = lambda i: (0, 0)
    return pl.pallas_call(
        _odd_in_kernel,
        grid=(m // ROW_TILE,),
        in_specs=[
            pl.BlockSpec((ROW_TILE, d), row),
            pl.BlockSpec((1, d), fixed),
            pl.BlockSpec((d, 2 * e), fixed),
        ],
        out_specs=[pl.BlockSpec((ROW_TILE, e), row), pl.BlockSpec((ROW_TILE, e), row)],
        out_shape=[jax.ShapeDtypeStruct((m, e), F32), jax.ShapeDtypeStruct((m, e), F32)],
        compiler_params=_cparams("parallel"),
        name="odd_in_proj",
    )(x2d, gain, w_bf)


def _s5_kernel(u_ref, x0r_ref, x0i_ref, ar_ref, ai_ref, bm_ref, cm_ref, d_ref,
               y_ref, fr_ref, fi_ref, st_re, st_im, ubuf, xbuf, *, nb, tt):
    t_idx = pl.program_id(1)
    n_blk = bm_ref.shape[0]
    half = S5_BLOCK_STATE
    n_chunk = half // LANES
    pitch = tt + S5_ROW_PAD

    @pl.when(t_idx == 0)
    def _():
        for kb in range(n_blk):
            st_re[kb] = x0r_ref[:, kb * half:(kb + 1) * half]
            st_im[kb] = x0i_ref[:, kb * half:(kb + 1) * half]

    for b in range(nb):
        ubuf[b * pitch + tt:(b + 1) * pitch, :] = jnp.zeros((S5_ROW_PAD, LANES), F32)

    for kb in range(n_blk):
        lanes = slice(kb * LANES, (kb + 1) * LANES)
        for b in range(nb):
            ubuf[b * pitch:b * pitch + tt, :] = u_ref[b, :, lanes]
        u_blk = ubuf[...]
        bu = _dot(u_blk.astype(BF16), bm_ref[kb])
        for j in range(2 * n_chunk):
            xbuf[j] = bu[:, j * LANES:(j + 1) * LANES]
        a_re = jnp.broadcast_to(ar_ref[kb], (nb, half))
        a_im = jnp.broadcast_to(ai_ref[kb], (nb, half))

        def step(t, carry):
            s_re, s_im = carry
            rows = pl.ds(t, nb, stride=pitch)
            b_re = jnp.concatenate([xbuf[j, rows, :] for j in range(n_chunk)], axis=1)
            b_im = jnp.concatenate([xbuf[n_chunk + j, rows, :] for j in range(n_chunk)], axis=1)
            n_re = a_re * s_re - a_im * s_im + b_re
            n_im = a_re * s_im + a_im * s_re + b_im
            for j in range(n_chunk):
                xbuf[j, rows, :] = n_re[:, j * LANES:(j + 1) * LANES]
                xbuf[n_chunk + j, rows, :] = n_im[:, j * LANES:(j + 1) * LANES]
            return n_re, n_im

        s_re, s_im = lax.fori_loop(0, tt, step, (st_re[kb], st_im[kb]), unroll=S5_SCAN_UNROLL)
        st_re[kb] = s_re
        st_im[kb] = s_im
        states = jnp.concatenate([xbuf[j] for j in range(2 * n_chunk)], axis=1)
        ubuf[...] = _dot(states.astype(BF16), cm_ref[kb]) + d_ref[:, lanes] * u_blk
        for b in range(nb):
            y_ref[b, :, lanes] = ubuf[b * pitch:b * pitch + tt, :]

    @pl.when(t_idx == pl.num_programs(1) - 1)
    def _():
        for kb in range(n_blk):
            fr_ref[:, kb * half:(kb + 1) * half] = st_re[kb]
            fi_ref[:, kb * half:(kb + 1) * half] = st_im[kb]


def _s5(u3, x0_re, x0_im, a_re, a_im, bmat, cmat, d_row, tt):
    batch, seq, e = u3.shape
    nb = SUBLANES
    n_blk = e // LANES
    n_state = n_blk * S5_BLOCK_STATE
    seq_map = lambda b, t: (b, t, 0)
    st_map = lambda b, t: (b, 0)
    fixed2 = lambda b, t: (0, 0)
    fixed3 = lambda b, t: (0, 0, 0)
    return pl.pallas_call(
        functools.partial(_s5_kernel, nb=nb, tt=tt),
        grid=(batch // nb, seq // tt),
        in_specs=[
            pl.BlockSpec((nb, tt, e), seq_map),
            pl.BlockSpec((nb, n_state), st_map),
            pl.BlockSpec((nb, n_state), st_map),
            pl.BlockSpec((n_blk, 1, S5_BLOCK_STATE), fixed3),
            pl.BlockSpec((n_blk, 1, S5_BLOCK_STATE), fixed3),
            pl.BlockSpec((n_blk, LANES, 2 * S5_BLOCK_STATE), fixed3),
            pl.BlockSpec((n_blk, 2 * S5_BLOCK_STATE, LANES), fixed3),
            pl.BlockSpec((1, e), fixed2),
        ],
        out_specs=[
            pl.BlockSpec((nb, tt, e), seq_map),
            pl.BlockSpec((nb, n_state), st_map),
            pl.BlockSpec((nb, n_state), st_map),
        ],
        out_shape=[
            jax.ShapeDtypeStruct((batch, seq, e), F32),
            jax.ShapeDtypeStruct((batch, n_state), F32),
            jax.ShapeDtypeStruct((batch, n_state), F32),
        ],
        scratch_shapes=[
            pltpu.VMEM((n_blk, nb, S5_BLOCK_STATE), F32),
            pltpu.VMEM((n_blk, nb, S5_BLOCK_STATE), F32),
            pltpu.VMEM((nb * (tt + S5_ROW_PAD), LANES), F32),
            pltpu.VMEM((2 * S5_BLOCK_STATE // LANES, nb * (tt + S5_ROW_PAD), LANES), F32),
        ],
        compiler_params=_cparams("parallel", "arbitrary"),
        name="s5_scan",
    )(u3, x0_re, x0_im, a_re, a_im, bmat, cmat, d_row)


def _s5_params(lam_re, lam_im, b_re, b_im, c_re, c_im, log_step):
    n_groups = lam_re.shape[0]
    n_blk = n_groups // S5_LANE_GROUPS
    dt = jnp.exp(log_step)[:, None]
    mag = jnp.exp(lam_re * dt)
    ang = lam_im * dt
    ab_re, ab_im = mag * jnp.cos(ang), mag * jnp.sin(ang)
    den = lam_re * lam_re + lam_im * lam_im
    nr = ab_re - 1.0
    f_re = (nr * lam_re + ab_im * lam_im) / den
    f_im = (ab_im * lam_re - nr * lam_im) / den
    bb_re = f_re[..., None] * b_re - f_im[..., None] * b_im
    bb_im = f_re[..., None] * b_im + f_im[..., None] * b_re
    eye = jnp.eye(S5_LANE_GROUPS, dtype=lam_re.dtype)

    def in_map(bb):
        bb = bb.reshape(n_blk, S5_LANE_GROUPS, S5_STATE, S5_GROUP)
        m = jnp.einsum('kgpc,gh->kgchp', bb, eye)
        return m.reshape(n_blk, LANES, S5_BLOCK_STATE)

    def out_map(cc):
        cc = cc.reshape(n_blk, S5_LANE_GROUPS, S5_GROUP, S5_STATE)
        m = jnp.einsum('kgcp,gh->kgphc', cc, eye)
        return m.reshape(n_blk, S5_BLOCK_STATE, LANES)

    bmat = jnp.concatenate([in_map(bb_re), in_map(bb_im)], axis=2).astype(BF16)
    cmat = jnp.concatenate([out_map(c_re), out_map(-c_im)], axis=1).astype(BF16)
    a_re = ab_re.reshape(n_blk, 1, S5_BLOCK_STATE)
    a_im = ab_im.reshape(n_blk, 1, S5_BLOCK_STATE)
    return a_re, a_im, bmat, cmat


def _odd_out_kernel(x_ref, y_ref, z_ref, w1_ref, w2_ref, wo_ref, o_ref):
    yb = jax.nn.gelu(y_ref[...]).astype(BF16)
    t = _dot(yb, w1_ref[...]) * jax.nn.sigmoid(_dot(yb, w2_ref[...])) * z_ref[...]
    o_ref[...] = x_ref[...] + _dot(t.astype(BF16), wo_ref[...])


def _odd_out(x2d, y2d, z2d, w1, w2, wo):
    m, d = x2d.shape
    e = y2d.shape[1]
    row = lambda i: (i, 0)
    fixed = lambda i: (0, 0)
    return pl.pallas_call(
        _odd_out_kernel,
        grid=(m // ROW_TILE,),
        in_specs=[
            pl.BlockSpec((ROW_TILE, d), row),
            pl.BlockSpec((ROW_TILE, e), row),
            pl.BlockSpec((ROW_TILE, e), row),
            pl.BlockSpec((e, e), fixed),
            pl.BlockSpec((e, e), fixed),
            pl.BlockSpec((e, d), fixed),
        ],
        out_specs=pl.BlockSpec((ROW_TILE, d), row),
        out_shape=jax.ShapeDtypeStruct((m, d), F32),
        compiler_params=_cparams("parallel"),
        name="odd_out_proj",
    )(x2d, y2d, z2d, w1, w2, wo)


def _rope_tables(pos):
    half = NSA_DH // 2
    inv = ROPE_THETA ** (-jnp.arange(half, dtype=F32) / half)
    ang = pos.astype(F32)[:, None] * inv[None, :]
    cos, sin = jnp.cos(ang), jnp.sin(ang)
    reps = LANES // NSA_DH
    return jnp.tile(cos, (1, 2 * reps)), jnp.tile(jnp.concatenate([-sin, sin], axis=1), (1, reps))


def _even_weights(norm_g, w_in, w_out, q_norm, k_norm, cmp_pos, cmp_w):
    d = w_in.shape[0]
    sizes = (RET_HEADS * RET_DK, RET_HEADS * RET_DK, A_WIDTH, A_WIDTH, B_WIDTH, 6 * NSA_KV_HEADS * NSA_DH,
             N_GATES, B_WIDTH)
    parts, o = [], 0
    for s in sizes:
        parts.append(w_in[:, o:o + s])
        o += s
    qa, ka, va, za, qn, kvb, gl, zb = parts
    w_perm = jnp.concatenate([qa, ka, va, za, qn, kvb, zb, gl, jnp.zeros((d, LANES - N_GATES), w_in.dtype)],
                             axis=1).astype(BF16)
    reps = LANES // NSA_DH
    qg = jnp.tile(q_norm[None, :], (1, reps))
    kg = jnp.tile(k_norm, (1, reps))
    eye = jnp.eye(NSA_KV_HEADS, dtype=cmp_w.dtype)
    w_bd = jnp.einsum('clde,gh,ck->lcgdkhe', cmp_w, eye, jnp.eye(2, dtype=cmp_w.dtype))
    w_bd = w_bd.reshape(L_CMP, 2 * LANES, 2 * LANES).astype(BF16)
    pe = jnp.broadcast_to(cmp_pos.transpose(1, 0, 2)[:, :, None, :], (L_CMP, 2, NSA_KV_HEADS, NSA_DH))
    pe = pe.reshape(L_CMP, 2 * LANES)
    wa = w_out[:A_WIDTH].astype(BF16)
    wb = w_out[A_WIDTH:].astype(BF16)
    return norm_g[None, :], w_perm, qg, kg, w_bd, pe, wa, wb


def _even_layer(x, pos0, s_ret, ew, gn_gain, cache=None, page_table=None, win_past=None):
    batch, seq, d = x.shape
    gain, w_perm, qg, kg, w_bd, pe, wa, wb = ew
    x2d = x.reshape(batch * seq, d)
    pos = pos0 + jnp.arange(seq, dtype=jnp.int32)
    cos_t, sin_t = _rope_tables(pos)
    if seq >= ROW_TILE:
        n_table_blocks = seq // ROW_TILE
    else:
        cos_t = jnp.tile(cos_t, (ROW_TILE // seq, 1))
        sin_t = jnp.tile(sin_t, (ROW_TILE // seq, 1))
        n_table_blocks = 1
    g, dh = NSA_KV_HEADS, NSA_DH
    if cache is None:
        qa, ka, va, za, qn, full_t, win_t, zb, gates, cmp_rows = _even_in(
            x2d, gain, w_perm, cos_t, sin_t, qg, kg, n_table_blocks, seq_tiles=seq // ROW_TILE)
        oa, s_fin = _retention(qa, ka, va, za, gn_gain[None, :], s_ret, batch, seq)
        cmp3 = cmp_rows.reshape(batch, seq, 2 * LANES)
        n_half = seq // CMP_STRIDE
        src_specs = [pl.BlockSpec((1, seq, LANES), lambda b: (b, 0, 0)),
                     pl.BlockSpec((1, seq, LANES), lambda b: (b, 0, 1))]
        cmp_kv = _compress([cmp3, cmp3], src_specs, (batch,), (), pe, w_bd, batch, seq, n_half)
        overlap, expand = _selection_tables(n_half, seq)
        ob = _nsa_prompt(qn, gates, zb, cmp_kv, full_t, win_t, overlap, expand, batch, seq)
        keep = min(WINDOW, seq)
        y = _even_out(x2d, oa, ob, wa, wb).reshape(batch, seq, d)
        full_rows = full_t.reshape(batch, 4, g, dh, seq).transpose(0, 4, 1, 2, 3)
        win_rows = win_t[:, :, seq - keep:].reshape(batch, 2, g, dh, keep).transpose(0, 4, 1, 2, 3)
        return y, s_fin, full_rows, win_rows
    else:
        qa, ka, va, za, qn, full_new, win_new, zb, gates = _even_in(x2d, gain, w_perm, cos_t, sin_t, qg, kg,
                                                                     n_table_blocks)
        oa, s_fin = _retention(qa, ka, va, za, gn_gain[None, :], s_ret, batch, seq)
        n_pages = page_table.shape[1]
        past_len = n_pages * PAGE_SIZE
        n_half = past_len // CMP_STRIDE
        specs = [pl.BlockSpec((1, PAGE_SIZE, LANES), functools.partial(lambda b, pt, p, c: (pt[b, p], 0, c), p=p, c=c))
                 for c in range(2) for p in range(n_pages)]
        cmp_kv = _compress([cache] * (2 * n_pages), specs, (batch,), (page_table,), pe, w_bd, batch, PAGE_SIZE,
                           n_half)
        overlap, expand = _selection_tables(n_half, past_len + KEY_TILE)
        ob, win_out = _nsa_sample(page_table, qn, gates, zb, cmp_kv, full_new, win_new, win_past, cache,
                                  overlap, expand, batch, seq, n_pages)
    y = _even_out(x2d, oa, ob, wa, wb).reshape(batch, seq, d)
    return (y, s_fin, full_new.reshape(batch, seq, 4, g, dh),
            win_out.reshape(batch, win_out.shape[1], 2, g, dh))


def _odd_layer(x, s_re, s_im, gain, w_in_bf, s5p, d_row, w1, w2, wo, tt):
    batch, seq, d = x.shape
    x2d = x.reshape(batch * seq, d)
    u, zs = _odd_in(x2d, gain, w_in_bf)
    e = u.shape[1]
    a_re, a_im, bmat, cmat = s5p
    n_groups, n_state = s_re.shape[1], s_re.shape[2]
    y, f_re, f_im = _s5(u.reshape(batch, seq, e), s_re.reshape(batch, n_groups * n_state),
                        s_im.reshape(batch, n_groups * n_state), a_re, a_im, bmat, cmat, d_row, tt)
    out = _odd_out(x2d, y.reshape(batch * seq, e), zs, w1, w2, wo).reshape(batch, seq, d)
    return out, f_re.reshape(batch, n_groups, n_state), f_im.reshape(batch, n_groups, n_state)


S5_TIME_TILE = 128


def kernel(x_prompt, x_sample, cache_nsa_kv, cache_nsa_win, state_ret, state_ssm_re, state_ssm_im, page_table,
           norm_even, w_in_even, w_out_even, ret_gn_gain, nsa_q_norm, nsa_k_norm, nsa_cmp_pos, nsa_cmp_w,
           norm_odd, w_in_odd, ssm_lambda_re, ssm_lambda_im, ssm_b_re, ssm_b_im, ssm_c_re, ssm_c_im, ssm_d,
           ssm_log_step, glu_w1, glu_w2, w_out_odd):
    bp, seq_p, _ = x_prompt.shape
    db, seq_s, _ = x_sample.shape
    n_pages = page_table.shape[1]
    past_len = n_pages * PAGE_SIZE
    depth = norm_even.shape[0] + norm_odd.shape[0]
    yp, ys = x_prompt, x_sample
    ret_p, ret_s, kv_p, kv_s, win_p, win_s = [], [], [], [], [], []
    sre_p, sim_p, sre_s, sim_s = [], [], [], []
    for layer in range(depth):
        li = layer // 2
        if layer % 2 == 0:
            ew = _even_weights(norm_even[li], w_in_even[li], w_out_even[li], nsa_q_norm[li], nsa_k_norm[li],
                               nsa_cmp_pos[li], nsa_cmp_w[li])
            s0 = jnp.zeros((bp, RET_HEADS, RET_DK, RET_DV), F32)
            yp, sr, kvr, wr = _even_layer(yp, 0, s0, ew, ret_gn_gain[li])
            ret_p.append(sr); kv_p.append(kvr); win_p.append(wr)
            cache2 = cache_nsa_kv[li].reshape(cache_nsa_kv.shape[1], PAGE_SIZE, KV_ROW)
            winp = cache_nsa_win[li].reshape(db, cache_nsa_win.shape[2], WIN_ROW)
            ys, sr2, kvr2, wr2 = _even_layer(ys, past_len, state_ret[li], ew, ret_gn_gain[li],
                                             cache=cache2, page_table=page_table, win_past=winp)
            ret_s.append(sr2); kv_s.append(kvr2); win_s.append(wr2)
        else:
            s5p = _s5_params(ssm_lambda_re[li], ssm_lambda_im[li], ssm_b_re[li], ssm_b_im[li],
                             ssm_c_re[li], ssm_c_im[li], ssm_log_step[li])
            gain = norm_odd[li][None, :]
            w_in_bf = w_in_odd[li].astype(BF16)
            w1, w2, wo = glu_w1[li].astype(BF16), glu_w2[li].astype(BF16), w_out_odd[li].astype(BF16)
            d_row = ssm_d[li][None, :]
            n_groups = ssm_lambda_re.shape[1]
            z0 = jnp.zeros((bp, n_groups, S5_STATE), F32)
            yp, fr, fi = _odd_layer(yp, z0, z0, gain, w_in_bf, s5p, d_row, w1, w2, wo, min(S5_TIME_TILE, seq_p))
            sre_p.append(fr); sim_p.append(fi)
            ys, fr2, fi2 = _odd_layer(ys, state_ssm_re[li], state_ssm_im[li], gain, w_in_bf, s5p, d_row,
                                      w1, w2, wo, min(S5_TIME_TILE, seq_s))
            sre_s.append(fr2); sim_s.append(fi2)
    return (yp, ys, jnp.stack(ret_p), jnp.stack(ret_s), jnp.stack(kv_p), jnp.stack(kv_s), jnp.stack(win_p),
            jnp.stack(win_s), jnp.stack(sre_p), jnp.stack(sim_p), jnp.stack(sre_s), jnp.stack(sim_s))
```

```python
import functools
import math

import jax
import jax.numpy as jnp
from jax import lax
from jax.experimental import pallas as pl
from jax.experimental.pallas import tpu as pltpu

F32 = jnp.float32
BF16 = jnp.bfloat16

LANES = 128
SUBLANES = 8
VMEM_LIMIT_BYTES = 48 * 2**20

EPS = 1e-6
ROPE_THETA = 10000.0
NEG_INF = -1e30
FORCE = 1e4

RET_HEADS = 4
RET_DK = 64
RET_DV = 128
RET_CHUNK = 128
A_WIDTH = RET_HEADS * RET_DV

NSA_HEADS = 8
NSA_KV_HEADS = 2
NSA_DH = 64
NSA_HPG = NSA_HEADS // NSA_KV_HEADS
B_WIDTH = NSA_HEADS * NSA_DH
L_CMP = 32
CMP_STRIDE = 16
CMP_RATIO = L_CMP // CMP_STRIDE
L_SEL = 64
N_SEL = 8
WINDOW = 512
PAGE_SIZE = 128
KV_ROW = 4 * NSA_KV_HEADS * NSA_DH
WIN_ROW = 2 * NSA_KV_HEADS * NSA_DH
KEY_TILE = 128
WIN_KEYS = WINDOW + KEY_TILE
SEL_CHUNK = 512
CMP_SLOT = CMP_STRIDE + 4
SAMPLE_REQS_PER_STEP = 2

S5_GROUP = 16
S5_STATE = 64
S5_LANE_GROUPS = LANES // S5_GROUP
S5_BLOCK_STATE = S5_LANE_GROUPS * S5_STATE
S5_ROW_PAD = 4
S5_SCAN_UNROLL = 4

QA0, KA0, VA0, ZA0, QN0, KVB0, ZB0, GL0 = 0, 256, 512, 1024, 1536, 2048, 2816, 3328
EVEN_COLS = GL0 + LANES
N_GATES = 3 * NSA_HEADS

ROW_TILE = 256


def _cparams(*sem):
    return pltpu.CompilerParams(dimension_semantics=sem, vmem_limit_bytes=VMEM_LIMIT_BYTES)


def _lane_iota(shape):
    return lax.broadcasted_iota(jnp.int32, shape, len(shape) - 1)


def _row_iota(shape):
    return lax.broadcasted_iota(jnp.int32, shape, len(shape) - 2)


def _dot(a, b):
    return jnp.dot(a, b, preferred_element_type=F32)


def _dot_nt(a, b):
    return lax.dot_general(a, b, (((1,), (1,)), ((), ())), preferred_element_type=F32)


def _rmsnorm_rows(x, g):
    return x * lax.rsqrt(jnp.mean(x * x, axis=-1, keepdims=True) + EPS) * g


def _swap_halves(x):
    return pltpu.roll(x, NSA_DH, axis=1)


def _rope_block(x, cos, sin_signed):
    half = NSA_DH // 2
    lane = _lane_iota(x.shape)
    first = (lane % NSA_DH) < half
    partner = jnp.where(first, pltpu.roll(x, LANES - half, axis=1), pltpu.roll(x, half, axis=1))
    return x * cos + partner * sin_signed


def _head_rms_block(x, g):
    lane = _lane_iota(x.shape)
    lo = lane < NSA_DH
    sq = x * x
    s_lo = jnp.sum(jnp.where(lo, sq, 0.0), axis=-1, keepdims=True)
    s_hi = jnp.sum(jnp.where(lo, 0.0, sq), axis=-1, keepdims=True)
    ms = jnp.where(lo, s_lo, s_hi) * (1.0 / NSA_DH)
    return x * lax.rsqrt(ms + EPS) * g


def _even_in_kernel(x_ref, g_ref, w_ref, cos_ref, sin_ref, qg_ref, kg_ref, *out_refs, transposed_kv):
    if transposed_kv:
        qa_ref, ka_ref, va_ref, za_ref, qn_ref, full_ref, win_ref, zb_ref, gl_ref, cmp_ref = out_refs
    else:
        qa_ref, ka_ref, va_ref, za_ref, qn_ref, full_ref, win_ref, zb_ref, gl_ref = out_refs
    hb = _rmsnorm_rows(x_ref[...], g_ref[...]).astype(BF16)
    cos = cos_ref[...]
    sin = sin_ref[...]

    def proj(c0):
        return _dot(hb, w_ref[:, c0:c0 + LANES])

    for j in range(RET_HEADS * RET_DK // LANES):
        c = j * LANES
        qa_ref[:, c:c + LANES] = _rope_block(proj(QA0 + c), cos, sin)
        ka_ref[:, c:c + LANES] = _rope_block(proj(KA0 + c), cos, sin) * (RET_DK ** -0.5)
    for j in range(A_WIDTH // LANES):
        c = j * LANES
        va_ref[:, c:c + LANES] = proj(VA0 + c)
        za_ref[:, c:c + LANES] = jax.nn.silu(proj(ZA0 + c))
    for j in range(B_WIDTH // LANES):
        c = j * LANES
        qn_ref[:, c:c + LANES] = _rope_block(_head_rms_block(proj(QN0 + c), qg_ref[...]), cos, sin)
        zb_ref[:, c:c + LANES] = jax.nn.silu(proj(ZB0 + c))
    for j in range(6):
        y = proj(KVB0 + j * LANES)
        if j % 2 == 0:
            y = _rope_block(_head_rms_block(y, kg_ref[j // 2:j // 2 + 1, :]), cos, sin)
        if transposed_kv:
            if j < 4:
                full_ref[0, j * LANES:(j + 1) * LANES, :] = y.T
            else:
                win_ref[0, (j - 4) * LANES:(j - 3) * LANES, :] = y.T
            if j < 2:
                cmp_ref[:, j * LANES:(j + 1) * LANES] = y
        elif j < 4:
            full_ref[:, j * LANES:(j + 1) * LANES] = y
        else:
            win_ref[:, (j - 4) * LANES:(j - 3) * LANES] = y
    gl_ref[...] = jax.nn.sigmoid(proj(GL0))


def _even_in(x2d, gain, w_bf, cos_t, sin_t, qg, kg, n_table_blocks, seq_tiles=None):
    m, d = x2d.shape
    grid = (m // ROW_TILE,)
    row = lambda i: (i, 0)
    fixed = lambda i: (0, 0)
    table = lambda i: (i % n_table_blocks, 0)
    widths = (RET_HEADS * RET_DK, RET_HEADS * RET_DK, A_WIDTH, A_WIDTH, B_WIDTH, KV_ROW, WIN_ROW, B_WIDTH, LANES)
    out_specs = [pl.BlockSpec((ROW_TILE, w), row) for w in widths]
    out_shape = [jax.ShapeDtypeStruct((m, w), F32) for w in widths]
    if seq_tiles is not None:
        batch, seq = grid[0] // seq_tiles, seq_tiles * ROW_TILE
        fmajor = lambda i: (i // seq_tiles, 0, i % seq_tiles)
        for idx, w in ((5, KV_ROW), (6, WIN_ROW)):
            out_specs[idx] = pl.BlockSpec((1, w, ROW_TILE), fmajor)
            out_shape[idx] = jax.ShapeDtypeStruct((batch, w, seq), F32)
        out_specs.append(pl.BlockSpec((ROW_TILE, 2 * LANES), row))
        out_shape.append(jax.ShapeDtypeStruct((m, 2 * LANES), F32))
    return pl.pallas_call(
        functools.partial(_even_in_kernel, transposed_kv=seq_tiles is not None),
        grid=grid,
        in_specs=[
            pl.BlockSpec((ROW_TILE, d), row),
            pl.BlockSpec((1, d), fixed),
            pl.BlockSpec((d, EVEN_COLS), fixed),
            pl.BlockSpec((ROW_TILE, LANES), table),
            pl.BlockSpec((ROW_TILE, LANES), table),
            pl.BlockSpec((1, LANES), fixed),
            pl.BlockSpec((3, LANES), fixed),
        ],
        out_specs=out_specs,
        out_shape=out_shape,
        compiler_params=_cparams("parallel"),
        name="even_in_proj",
    )(x2d, gain, w_bf, cos_t, sin_t, qg, kg)


def _retention_kernel(q_ref, k_ref, v_ref, z_ref, gn_ref, s0_ref, dmat_ref, qdec_ref, kdec_ref, cdec_ref,
                      o_ref, sfin_ref, s_scr, *, rows):
    c = pl.program_id(1)

    @pl.when(c == 0)
    def _():
        s_scr[...] = s0_ref[0]

    def padded(ref):
        x = ref[...]
        if rows == RET_CHUNK:
            return x
        return jnp.concatenate([x, jnp.zeros((RET_CHUNK - rows, x.shape[1]), F32)], axis=0)

    q = padded(q_ref)
    k = padded(k_ref)
    v = padded(v_ref)
    qd = q * qdec_ref[...]
    kd = k * kdec_ref[...]
    lane = _lane_iota((RET_CHUNK, LANES))
    for pair in range(RET_HEADS // 2):
        cols = slice(pair * LANES, (pair + 1) * LANES)
        q2, k2b, qd2, kd2 = q[:, cols], k[:, cols].astype(BF16), qd[:, cols], kd[:, cols]
        kd2_t = kd2.T
        s_pair = jnp.concatenate([s_scr[2 * pair], s_scr[2 * pair + 1]], axis=0).astype(BF16)
        for sub in range(2):
            h = 2 * pair + sub
            mine = (lane >= sub * RET_DK) & (lane < (sub + 1) * RET_DK)
            qm = jnp.where(mine, q2, 0.0).astype(BF16)
            qdm = jnp.where(mine, qd2, 0.0).astype(BF16)
            vh = v[:, h * RET_DV:(h + 1) * RET_DV]
            vhb = vh.astype(BF16)
            scores = _dot_nt(qm, k2b) * dmat_ref[h]
            out = _dot(scores.astype(BF16), vhb) + _dot(qdm, s_pair)
            kt = kd2_t[sub * RET_DK:(sub + 1) * RET_DK, :].astype(BF16)
            s_scr[h] = s_scr[h] * cdec_ref[h] + _dot(kt, vhb)
            mu = jnp.mean(out, axis=-1, keepdims=True)
            cen = out - mu
            var = jnp.mean(cen * cen, axis=-1, keepdims=True)
            y = cen * lax.rsqrt(var + EPS) * gn_ref[:, h * RET_DV:(h + 1) * RET_DV]
            o_ref[:, h * RET_DV:(h + 1) * RET_DV] = y[:rows] * z_ref[:, h * RET_DV:(h + 1) * RET_DV]

    @pl.when(c == pl.num_programs(1) - 1)
    def _():
        sfin_ref[0] = s_scr[...]


def _retention_tables(rows):
    log_g = jnp.log(1.0 - 2.0 ** (-5.0 - jnp.arange(RET_HEADS, dtype=F32)))
    idx = jnp.arange(RET_CHUNK, dtype=F32)
    live = idx < rows
    diff = idx[:, None] - idx[None, :]
    causal = (diff >= 0) & live[:, None] & live[None, :]
    dmat = jnp.exp(jnp.where(causal, diff, 0.0)[None] * log_g[:, None, None]) * causal[None]
    qdec = jnp.exp((idx + 1.0)[:, None] * log_g[None, :]) * live[:, None]
    kdec = jnp.exp((rows - 1.0 - idx)[:, None] * log_g[None, :]) * live[:, None]
    qdec = jnp.repeat(qdec, RET_DK, axis=1)
    kdec = jnp.repeat(kdec, RET_DK, axis=1)
    cdec = jnp.broadcast_to(jnp.exp(rows * log_g)[:, None, None], (RET_HEADS, RET_DK, RET_DV))
    return dmat, qdec, kdec, cdec


def _retention(qa, ka, va, za, gn_gain, s0, batch, seq):
    rows = min(RET_CHUNK, seq)
    n_chunks = seq // rows
    dmat, qdec, kdec, cdec = _retention_tables(rows)
    tok = lambda b, c: (b * n_chunks + c, 0)
    fixed2 = lambda b, c: (0, 0)
    fixed3 = lambda b, c: (0, 0, 0)
    state = lambda b, c: (b, 0, 0, 0)
    qk_w = RET_HEADS * RET_DK
    return pl.pallas_call(
        functools.partial(_retention_kernel, rows=rows),
        grid=(batch, n_chunks),
        in_specs=[
            pl.BlockSpec((rows, qk_w), tok),
            pl.BlockSpec((rows, qk_w), tok),
            pl.BlockSpec((rows, A_WIDTH), tok),
            pl.BlockSpec((rows, A_WIDTH), tok),
            pl.BlockSpec((1, A_WIDTH), fixed2),
            pl.BlockSpec((1, RET_HEADS, RET_DK, RET_DV), state),
            pl.BlockSpec((RET_HEADS, RET_CHUNK, RET_CHUNK), fixed3),
            pl.BlockSpec((RET_CHUNK, qk_w), fixed2),
            pl.BlockSpec((RET_CHUNK, qk_w), fixed2),
            pl.BlockSpec((RET_HEADS, RET_DK, RET_DV), fixed3),
        ],
        out_specs=[
            pl.BlockSpec((rows, A_WIDTH), tok),
            pl.BlockSpec((1, RET_HEADS, RET_DK, RET_DV), state),
        ],
        out_shape=[
            jax.ShapeDtypeStruct((batch * seq, A_WIDTH), F32),
            jax.ShapeDtypeStruct((batch, RET_HEADS, RET_DK, RET_DV), F32),
        ],
        scratch_shapes=[pltpu.VMEM((RET_HEADS, RET_DK, RET_DV), F32)],
        compiler_params=_cparams("parallel", "arbitrary"),
        name="retention",
    )(qa, ka, va, za, gn_gain, s0, dmat, qdec, kdec, cdec)


def _compress_kernel(*refs, n_pref, n_src, src_rows):
    refs = refs[n_pref:]
    src_k = refs[:n_src]
    src_v = refs[n_src:2 * n_src]
    pe_ref, w_ref, o_ref = refs[2 * n_src:]
    per_src = src_rows // CMP_STRIDE
    n_half = n_src * per_src
    acc = [jnp.zeros((n_half, 2 * LANES), F32) for _ in range(CMP_RATIO)]

    def strided_rows(srcs, l):
        rows = [s[0, pl.ds(l, per_src, stride=CMP_STRIDE), :] for s in srcs]
        return rows[0] if n_src == 1 else jnp.concatenate(rows, axis=0)

    for l in range(CMP_STRIDE):
        x = jnp.concatenate([strided_rows(src_k, l), strided_rows(src_v, l)], axis=1)
        for r in range(CMP_RATIO):
            i = r * CMP_STRIDE + l
            acc[r] = acc[r] + _dot((x + pe_ref[i:i + 1, :]).astype(BF16), w_ref[i])
    out = acc[0]
    for r in range(1, CMP_RATIO):
        out = out + pltpu.roll(acc[r], n_half - r, axis=0)
    live = _row_iota(out.shape) < n_half - CMP_RATIO + 1
    o_ref[0] = jnp.where(live, out, 0.0)


def _compress(src_arrays, src_specs, grid, extra_prefetch, pe, w_bd, batch, src_rows, n_half):
    n_src = len(src_specs) // 2
    n_pref = len(extra_prefetch)
    kern = functools.partial(_compress_kernel, n_pref=n_pref, n_src=n_src, src_rows=src_rows)
    fixed2 = lambda *a: (0, 0)
    fixed3 = lambda *a: (0, 0, 0)
    out_map = lambda b, *a: (b, 0, 0)
    return pl.pallas_call(
        kern,
        grid_spec=pltpu.PrefetchScalarGridSpec(
            num_scalar_prefetch=n_pref,
            grid=grid,
            in_specs=list(src_specs) + [
                pl.BlockSpec((L_CMP, 2 * LANES), fixed2),
                pl.BlockSpec((L_CMP, 2 * LANES, 2 * LANES), fixed3),
            ],
            out_specs=pl.BlockSpec((1, n_half, 2 * LANES), out_map),
        ),
        out_shape=jax.ShapeDtypeStruct((batch, n_half, 2 * LANES), F32),
        compiler_params=_cparams("parallel"),
        name="kv_compress",
    )(*extra_prefetch, *src_arrays, pe, w_bd)


def _split3_bf16(x):
    hi = x.astype(BF16)
    r1 = x - hi.astype(F32)
    mid = r1.astype(BF16)
    lo = (r1 - mid.astype(F32)).astype(BF16)
    return hi, mid, lo


def _stack_group_queries(q, g):
    lane = _lane_iota((q.shape[0], LANES))
    in_g = (lane >= g * NSA_DH) & (lane < (g + 1) * NSA_DH)
    parts = []
    for hh in range(NSA_HPG):
        h = g * NSA_HPG + hh
        two = q[:, (h // 2) * LANES:(h // 2 + 1) * LANES]
        if h % 2 != g:
            two = _swap_halves(two)
        parts.append(jnp.where(in_g, two, 0.0))
    return (jnp.concatenate(parts, axis=0) * (NSA_DH ** -0.5)).astype(BF16)


def _compressed_branch(qs, ck, cv, cmp_ok4):
    s1 = jnp.where(cmp_ok4, _dot_nt(qs, ck), NEG_INF)
    e = jnp.exp(s1 - jnp.max(s1, axis=-1, keepdims=True))
    p1 = jnp.where(cmp_ok4, e * (1.0 / jnp.sum(e, axis=-1, keepdims=True)), 0.0)
    return _dot(p1.astype(BF16), cv), p1


def _sum_heads(p1, tq):
    psum = p1[0:tq]
    for hh in range(1, NSA_HPG):
        psum = psum + p1[hh * tq:(hh + 1) * tq]
    return psum


def _choose_blocks(psum, tpos, overlap, n_blocks):
    tq = psum.shape[0]
    blk = _lane_iota((tq, LANES))
    cur = tpos >> int(math.log2(L_SEL))
    forced = (blk == 0) | (blk == cur) | (blk == cur - 1)
    valid = (blk * L_SEL <= tpos) & (blk < n_blocks)
    imp = sum(_dot(t, overlap) for t in _split3_bf16(psum))
    score = jnp.where(valid, jnp.where(forced, FORCE, imp), -FORCE)
    rank = jnp.zeros((tq, LANES), F32)
    for s in range(n_blocks):
        col = score[:, s:s + 1]
        ahead = (col > score) | ((col == score) & (blk > s))
        rank = rank + ahead.astype(F32)
    return ((rank < N_SEL) & valid).astype(BF16)


def _gate_and_store(gates, zs, branch_outs, o_ref, row0=0):
    tq = gates.shape[0]
    lane = _lane_iota((tq, LANES))
    ext = []
    for g in range(NSA_KV_HEADS):
        o_cmp, o_sel, o_win = branch_outs[g]
        for hh in range(NSA_HPG):
            h = g * NSA_HPG + hh
            rows = slice(hh * tq, (hh + 1) * tq)
            o = (gates[:, 3 * h:3 * h + 1] * o_cmp[rows] + gates[:, 3 * h + 1:3 * h + 2] * o_sel[rows]
                 + gates[:, 3 * h + 2:3 * h + 3] * o_win[rows])
            ext.append(_swap_halves(o) if h % 2 != g else o)
    for j in range(NSA_HEADS // 2):
        both = jnp.where(lane < NSA_DH, ext[2 * j], ext[2 * j + 1])
        o_ref[row0:row0 + tq, j * LANES:(j + 1) * LANES] = both * zs[:, j * LANES:(j + 1) * LANES]


def _nsa_prompt_kernel(q_ref, g_ref, z_ref, cmp_ref, sel_ref, win_ref, ov_ref, ex_ref, o_ref, *, n_blocks):
    i = pl.program_id(1)
    tq = q_ref.shape[0]
    q = q_ref[...]
    tpos = i * tq + _row_iota((tq, 1))
    tpos4 = jnp.concatenate([tpos] * NSA_HPG, axis=0)
    cmp_blk = cmp_ref[0]
    ck = cmp_blk[:, :LANES].astype(BF16)
    cv = cmp_blk[:, LANES:].astype(BF16)
    cmp_ok4 = (_lane_iota((1, cmp_blk.shape[0])) * CMP_STRIDE + (L_CMP - 1)) <= tpos4

    qs, o_cmp, chosen = [], [], []
    for g in range(NSA_KV_HEADS):
        qs.append(_stack_group_queries(q, g))
        o, p1 = _compressed_branch(qs[g], ck, cv, cmp_ok4)
        o_cmp.append(o)
        chosen.append(_choose_blocks(_sum_heads(p1, tq), tpos, ov_ref[...], n_blocks))

    def sel_chunk(c, carry):
        k0 = pl.multiple_of(c * SEL_CHUNK, SEL_CHUNK)
        k_t = sel_ref[0, 0:LANES, pl.ds(k0, SEL_CHUNK)].astype(BF16)
        v_t = sel_ref[0, LANES:2 * LANES, pl.ds(k0, SEL_CHUNK)].astype(BF16)
        causal = (k0 + _lane_iota((1, SEL_CHUNK))) <= tpos
        expand = ex_ref[:, pl.ds(k0, SEL_CHUNK)]
        out = []
        for g in range(NSA_KV_HEADS):
            m, l, acc = carry[g]
            key_on = (_dot(chosen[g], expand) > 0.5) & causal
            bias = jnp.where(key_on, 0.0, NEG_INF)
            s = _dot(qs[g], k_t) + jnp.concatenate([bias] * NSA_HPG, axis=0)
            m_new = jnp.maximum(m, jnp.max(s, axis=-1, keepdims=True))
            alpha = jnp.exp(m - m_new)
            e = jnp.exp(s - m_new)
            l_new = alpha * l + jnp.sum(e, axis=-1, keepdims=True)
            out.append((m_new, l_new, alpha * acc + _dot_nt(e.astype(BF16), v_t)))
        return tuple(out)

    rows = NSA_HPG * tq
    init = (jnp.full((rows, 1), NEG_INF, F32), jnp.zeros((rows, 1), F32), jnp.zeros((rows, LANES), F32))
    n_chunks = (i * tq) // SEL_CHUNK + 1
    sel_state = lax.fori_loop(0, n_chunks, sel_chunk, (init,) * NSA_KV_HEADS)

    w0 = pl.multiple_of(jnp.maximum(i * tq - WINDOW, 0), KEY_TILE)
    wk_t = win_ref[0, 0:LANES, pl.ds(w0, WIN_KEYS)].astype(BF16)
    wv_t = win_ref[0, LANES:2 * LANES, pl.ds(w0, WIN_KEYS)].astype(BF16)
    win_kpos = w0 + _lane_iota((1, WIN_KEYS))
    win_ok4 = (win_kpos <= tpos4) & (win_kpos > tpos4 - WINDOW)

    branch_outs = []
    for g in range(NSA_KV_HEADS):
        _, l, acc = sel_state[g]
        s3 = jnp.where(win_ok4, _dot(qs[g], wk_t), NEG_INF)
        e3 = jnp.exp(s3 - jnp.max(s3, axis=-1, keepdims=True))
        o_win = _dot_nt(e3.astype(BF16), wv_t) / jnp.sum(e3, axis=-1, keepdims=True)
        branch_outs.append((o_cmp[g], acc / l, o_win))
    _gate_and_store(g_ref[...], z_ref[...], branch_outs, o_ref)


def _nsa_prompt(qn, gates, zb, cmp_kv, full_t, win_t, overlap, expand, batch, seq):
    tq = KEY_TILE
    nq = seq // tq
    tok = lambda b, i: (b * nq + i, 0)
    per_b = lambda b, i: (b, 0, 0)
    sel_half = lambda b, i: (b, 1, 0)
    fixed = lambda b, i: (0, 0)
    n_blocks = -(-seq // L_SEL)
    return pl.pallas_call(
        functools.partial(_nsa_prompt_kernel, n_blocks=n_blocks),
        grid=(batch, nq),
        in_specs=[
            pl.BlockSpec((tq, B_WIDTH), tok),
            pl.BlockSpec((tq, LANES), tok),
            pl.BlockSpec((tq, B_WIDTH), tok),
            pl.BlockSpec((1, cmp_kv.shape[1], 2 * LANES), per_b),
            pl.BlockSpec((1, 2 * LANES, seq), sel_half),
            pl.BlockSpec((1, WIN_ROW, seq), per_b),
            pl.BlockSpec(overlap.shape, fixed),
            pl.BlockSpec(expand.shape, fixed),
        ],
        out_specs=pl.BlockSpec((tq, B_WIDTH), tok),
        out_shape=jax.ShapeDtypeStruct((batch * seq, B_WIDTH), F32),
        compiler_params=_cparams("parallel", "arbitrary"),
        name="nsa_prompt",
    )(qn, gates, zb, cmp_kv, full_t, win_t, overlap, expand)


def _nsa_sample_kernel(*refs, nb, n_pages, past_len, n_blocks):
    q_ref, g_ref, z_ref, newf_ref, neww_ref, winp_ref = refs[1:7]
    pages = refs[7:7 + nb * n_pages]
    pe_ref, w_ref, ov_ref, ex_ref, o_ref, wout_ref, xs_ref = refs[7 + nb * n_pages:]
    tq = q_ref.shape[0] // nb
    w_buf = winp_ref.shape[3]
    halves_per_page = PAGE_SIZE // CMP_STRIDE
    n_half = past_len // CMP_STRIDE
    total_half = nb * n_half

    for j in range(nb * n_pages):
        for c in range(2):
            x = pages[j][0, c].T
            for hb in range(halves_per_page):
                slot = (j * halves_per_page + hb) * CMP_SLOT
                xs_ref[c, slot:slot + CMP_STRIDE, :] = x[hb * CMP_STRIDE:(hb + 1) * CMP_STRIDE]

    acc = [jnp.zeros((total_half, 2 * LANES), F32) for _ in range(CMP_RATIO)]
    for l in range(CMP_STRIDE):
        x = jnp.concatenate([xs_ref[c, pl.ds(l, total_half, stride=CMP_SLOT), :] for c in range(2)], axis=1)
        for r in range(CMP_RATIO):
            i = r * CMP_STRIDE + l
            acc[r] = acc[r] + _dot((x + pe_ref[i:i + 1, :]).astype(BF16), w_ref[i])
    cmp_all = acc[0]
    for r in range(1, CMP_RATIO):
        cmp_all = cmp_all + pltpu.roll(acc[r], total_half - r, axis=0)
    live = _row_iota((n_half, 2 * LANES)) < n_half - CMP_RATIO + 1

    pad = jnp.zeros((KEY_TILE - tq, LANES), F32)
    lane = _lane_iota((LANES, LANES))
    per_req = NSA_HEADS * tq
    tpos = past_len + _row_iota((tq, 1))
    tpos_all = jnp.concatenate([tpos] * (nb * NSA_HEADS), axis=0)
    tpos_grp = jnp.concatenate([tpos] * (nb * NSA_KV_HEADS), axis=0)

    def padded_rows(x):
        return jnp.concatenate([x, pad], axis=0)

    def per_request(fn):
        return jnp.concatenate([fn(j, slice(j * per_req, (j + 1) * per_req)) for j in range(nb)], axis=0)

    tok = [slice(j * tq, (j + 1) * tq) for j in range(nb)]
    qs = jnp.concatenate([_stack_group_queries(q_ref[tok[j], :], g)
                          for j in range(nb) for g in range(NSA_KV_HEADS)], axis=0)
    cmp_blk = [jnp.where(live, cmp_all[j * n_half:(j + 1) * n_half], 0.0).astype(BF16) for j in range(nb)]
    mine = [pages[j * n_pages:(j + 1) * n_pages] for j in range(nb)]
    newf = [newf_ref[tok[j], :] for j in range(nb)]
    neww = [neww_ref[tok[j], :] for j in range(nb)]

    cmp_ok = (_lane_iota((1, n_half)) * CMP_STRIDE + (L_CMP - 1)) <= tpos_all
    s1 = jnp.where(cmp_ok, per_request(lambda j, r: _dot_nt(qs[r], cmp_blk[j][:, :LANES])), NEG_INF)
    e1 = jnp.exp(s1 - jnp.max(s1, axis=-1, keepdims=True))
    p1 = jnp.where(cmp_ok, e1 * (1.0 / jnp.sum(e1, axis=-1, keepdims=True)), 0.0)
    p1b = p1.astype(BF16)
    o_cmp = per_request(lambda j, r: _dot(p1b[r], cmp_blk[j][:, LANES:]))

    grp_rows = NSA_HPG * tq
    psum = jnp.concatenate([_sum_heads(p1[i * grp_rows:(i + 1) * grp_rows], tq)
                            for i in range(nb * NSA_KV_HEADS)], axis=0)
    chosen = _choose_blocks(psum, tpos_grp, ov_ref[...], n_blocks)
    key_on = (_dot(chosen, ex_ref[...]) > 0.5) & (_lane_iota((1, past_len + KEY_TILE)) <= tpos_grp)
    bias = jnp.where(key_on, 0.0, NEG_INF)
    bias = jnp.concatenate([bias[i * tq:(i + 1) * tq] for i in range(nb * NSA_KV_HEADS)
                            for _ in range(NSA_HPG)], axis=0)

    def sel_scores(j, r):
        k_t = jnp.concatenate([p[0, 2].astype(BF16) for p in mine[j]], axis=1)
        k_new = padded_rows(newf[j][:, 2 * LANES:3 * LANES]).astype(BF16)
        return jnp.concatenate([_dot(qs[r], k_t), _dot_nt(qs[r], k_new)], axis=1)

    s2 = per_request(sel_scores) + bias
    e2 = jnp.exp(s2 - jnp.max(s2, axis=-1, keepdims=True))
    l2 = jnp.sum(e2, axis=-1, keepdims=True)
    e2 = e2.astype(BF16)

    def sel_values(j, r):
        v_t = jnp.concatenate([p[0, 3].astype(BF16) for p in mine[j]], axis=1)
        v_new = padded_rows(newf[j][:, 3 * LANES:4 * LANES]).astype(BF16)
        return _dot_nt(e2[r, :past_len], v_t) + _dot(e2[r, past_len:], v_new)

    o_sel = per_request(sel_values) / l2

    win_kpos = (past_len - w_buf) + _lane_iota((1, w_buf + KEY_TILE))
    win_ok = (win_kpos <= tpos_all) & (win_kpos > tpos_all - WINDOW)
    s3 = per_request(lambda j, r: jnp.concatenate(
        [_dot(qs[r], winp_ref[j, 0].astype(BF16)),
         _dot_nt(qs[r], padded_rows(neww[j][:, :LANES]).astype(BF16))], axis=1))
    s3 = jnp.where(win_ok, s3, NEG_INF)
    e3 = jnp.exp(s3 - jnp.max(s3, axis=-1, keepdims=True))
    l3 = jnp.sum(e3, axis=-1, keepdims=True)
    e3 = e3.astype(BF16)
    o_win = per_request(lambda j, r: _dot_nt(e3[r, :w_buf], winp_ref[j, 1].astype(BF16))
                        + _dot(e3[r, w_buf:], padded_rows(neww[j][:, LANES:]).astype(BF16))) / l3

    for j in range(nb):
        branch_outs = []
        for g in range(NSA_KV_HEADS):
            r = slice((j * NSA_KV_HEADS + g) * grp_rows, (j * NSA_KV_HEADS + g + 1) * grp_rows)
            branch_outs.append((o_cmp[r], o_sel[r], o_win[r]))
        _gate_and_store(g_ref[tok[j], :], z_ref[tok[j], :], branch_outs, o_ref, row0=j * tq)

        for c in range(2):
            shifted = pltpu.roll(winp_ref[j, c], w_buf - tq, axis=1)
            new_cols = pltpu.roll(padded_rows(neww[j][:, c * LANES:(c + 1) * LANES]).T, LANES - tq, axis=1)
            wout_ref[j, c, :, 0:w_buf - LANES] = shifted[:, 0:w_buf - LANES]
            wout_ref[j, c, :, w_buf - LANES:w_buf] = jnp.where(lane >= LANES - tq, new_cols,
                                                               shifted[:, w_buf - LANES:w_buf])


def _nsa_sample(page_table, qn, gates, zb, full_new, win_new, win_t, cache_t, pe, w_bd, overlap, expand,
                batch, tq, n_pages):
    nb = SAMPLE_REQS_PER_STEP
    past_len = n_pages * PAGE_SIZE
    w_buf = win_t.shape[3]
    n_blocks = -(-(past_len + tq) // L_SEL)
    tok = lambda b, pt: (b, 0)
    per_b = lambda b, pt: (b, 0, 0, 0)
    fixed2 = lambda b, pt: (0, 0)
    fixed3 = lambda b, pt: (0, 0, 0)

    def page_spec(j, p):
        return pl.BlockSpec((1, 4, LANES, PAGE_SIZE), lambda b, pt: (pt[b * nb + j, p], 0, 0, 0))

    n_slots = nb * n_pages * (PAGE_SIZE // CMP_STRIDE)
    return pl.pallas_call(
        functools.partial(_nsa_sample_kernel, nb=nb, n_pages=n_pages, past_len=past_len, n_blocks=n_blocks),
        grid_spec=pltpu.PrefetchScalarGridSpec(
            num_scalar_prefetch=1,
            grid=(batch // nb,),
            in_specs=[
                pl.BlockSpec((nb * tq, B_WIDTH), tok),
                pl.BlockSpec((nb * tq, LANES), tok),
                pl.BlockSpec((nb * tq, B_WIDTH), tok),
                pl.BlockSpec((nb * tq, KV_ROW), tok),
                pl.BlockSpec((nb * tq, WIN_ROW), tok),
                pl.BlockSpec((nb, 2, LANES, w_buf), per_b),
            ] + [page_spec(j, p) for j in range(nb) for p in range(n_pages)] + [
                pl.BlockSpec(pe.shape, fixed2),
                pl.BlockSpec(w_bd.shape, fixed3),
                pl.BlockSpec(overlap.shape, fixed2),
                pl.BlockSpec(expand.shape, fixed2),
            ],
            out_specs=[
                pl.BlockSpec((nb * tq, B_WIDTH), tok),
                pl.BlockSpec((nb, 2, LANES, w_buf), per_b),
            ],
            scratch_shapes=[pltpu.VMEM((2, n_slots * CMP_SLOT, LANES), F32)],
        ),
        out_shape=[
            jax.ShapeDtypeStruct((batch * tq, B_WIDTH), F32),
            jax.ShapeDtypeStruct((batch, 2, LANES, w_buf), F32),
        ],
        compiler_params=_cparams("parallel"),
        name="nsa_sample",
    )(page_table, qn, gates, zb, full_new, win_new, win_t, *([cache_t] * (nb * n_pages)), pe, w_bd,
      overlap, expand)


def _selection_tables(n_cmp_rows, n_sel_keys):
    n = jnp.arange(n_cmp_rows, dtype=jnp.int32)[:, None]
    s = jnp.arange(LANES, dtype=jnp.int32)[None, :]
    c_start = n * CMP_STRIDE
    s_start = s * L_SEL
    overlap = ((c_start < s_start + L_SEL) & (s_start < c_start + L_CMP)).astype(BF16)
    key = jnp.arange(n_sel_keys, dtype=jnp.int32)[None, :]
    expand = ((key // L_SEL) == jnp.arange(LANES, dtype=jnp.int32)[:, None]).astype(BF16)
    return overlap, expand


def _even_out_kernel(x_ref, oa_ref, ob_ref, wa_ref, wb_ref, y_ref):
    y_ref[...] = (x_ref[...] + _dot(oa_ref[...].astype(BF16), wa_ref[...])
                  + _dot(ob_ref[...].astype(BF16), wb_ref[...]))


def _even_out(x2d, oa, ob, wa, wb):
    m, d = x2d.shape
    row = lambda i: (i, 0)
    fixed = lambda i: (0, 0)
    return pl.pallas_call(
        _even_out_kernel,
        grid=(m // ROW_TILE,),
        in_specs=[
            pl.BlockSpec((ROW_TILE, d), row),
            pl.BlockSpec((ROW_TILE, A_WIDTH), row),
            pl.BlockSpec((ROW_TILE, B_WIDTH), row),
            pl.BlockSpec((A_WIDTH, d), fixed),
            pl.BlockSpec((B_WIDTH, d), fixed),
        ],
        out_specs=pl.BlockSpec((ROW_TILE, d), row),
        out_shape=jax.ShapeDtypeStruct((m, d), F32),
        compiler_params=_cparams("parallel"),
        name="even_out_proj",
    )(x2d, oa, ob, wa, wb)


def _odd_in_kernel(x_ref, g_ref, w_ref, u_ref, z_ref):
    hb = _rmsnorm_rows(x_ref[...], g_ref[...]).astype(BF16)
    e = u_ref.shape[1]
    u_ref[...] = _dot(hb, w_ref[:, :e])
    z_ref[...] = jax.nn.silu(_dot(hb, w_ref[:, e:]))


def _odd_in(x2d, gain, w_bf):
    m, d = x2d.shape
    e = w_bf.shape[1] // 2
    row = lambda i: (i, 0)
    fixed = lambda i: (0, 0)
    return pl.pallas_call(
        _odd_in_kernel,
        grid=(m // ROW_TILE,),
        in_specs=[
            pl.BlockSpec((ROW_TILE, d), row),
            pl.BlockSpec((1, d), fixed),
            pl.BlockSpec((d, 2 * e), fixed),
        ],
        out_specs=[pl.BlockSpec((ROW_TILE, e), row), pl.BlockSpec((ROW_TILE, e), row)],
        out_shape=[jax.ShapeDtypeStruct((m, e), F32), jax.ShapeDtypeStruct((m, e), F32)],
        compiler_params=_cparams("parallel"),
        name="odd_in_proj",
    )(x2d, gain, w_bf)


def _s5_kernel(u_ref, x0r_ref, x0i_ref, ar_ref, ai_ref, bm_ref, cm_ref, d_ref,
               y_ref, fr_ref, fi_ref, st_re, st_im, ubuf, xbuf, *, nb, tt):
    t_idx = pl.program_id(1)
    n_blk = bm_ref.shape[0]
    half = S5_BLOCK_STATE
    n_chunk = half // LANES
    pitch = tt + S5_ROW_PAD

    @pl.when(t_idx == 0)
    def _():
        for kb in range(n_blk):
            st_re[kb] = x0r_ref[:, kb * half:(kb + 1) * half]
            st_im[kb] = x0i_ref[:, kb * half:(kb + 1) * half]

    for b in range(nb):
        ubuf[b * pitch + tt:(b + 1) * pitch, :] = jnp.zeros((S5_ROW_PAD, LANES), F32)

    for kb in range(n_blk):
        lanes = slice(kb * LANES, (kb + 1) * LANES)
        for b in range(nb):
            ubuf[b * pitch:b * pitch + tt, :] = u_ref[b, :, lanes]
        u_blk = ubuf[...]
        bu = _dot(u_blk.astype(BF16), bm_ref[kb])
        for j in range(2 * n_chunk):
            xbuf[j] = bu[:, j * LANES:(j + 1) * LANES]
        a_re = jnp.broadcast_to(ar_ref[kb], (nb, half))
        a_im = jnp.broadcast_to(ai_ref[kb], (nb, half))

        def step(t, carry):
            s_re, s_im = carry
            rows = pl.ds(t, nb, stride=pitch)
            b_re = jnp.concatenate([xbuf[j, rows, :] for j in range(n_chunk)], axis=1)
            b_im = jnp.concatenate([xbuf[n_chunk + j, rows, :] for j in range(n_chunk)], axis=1)
            n_re = a_re * s_re - a_im * s_im + b_re
            n_im = a_re * s_im + a_im * s_re + b_im
            for j in range(n_chunk):
                xbuf[j, rows, :] = n_re[:, j * LANES:(j + 1) * LANES]
                xbuf[n_chunk + j, rows, :] = n_im[:, j * LANES:(j + 1) * LANES]
            return n_re, n_im

        s_re, s_im = lax.fori_loop(0, tt, step, (st_re[kb], st_im[kb]), unroll=S5_SCAN_UNROLL)
        st_re[kb] = s_re
        st_im[kb] = s_im
        states = jnp.concatenate([xbuf[j] for j in range(2 * n_chunk)], axis=1)
        ubuf[...] = _dot(states.astype(BF16), cm_ref[kb]) + d_ref[:, lanes] * u_blk
        for b in range(nb):
            y_ref[b, :, lanes] = ubuf[b * pitch:b * pitch + tt, :]

    @pl.when(t_idx == pl.num_programs(1) - 1)
    def _():
        for kb in range(n_blk):
            fr_ref[:, kb * half:(kb + 1) * half] = st_re[kb]
            fi_ref[:, kb * half:(kb + 1) * half] = st_im[kb]


def _s5(u3, x0_re, x0_im, a_re, a_im, bmat, cmat, d_row, tt):
    batch, seq, e = u3.shape
    nb = SUBLANES
    n_blk = e // LANES
    n_state = n_blk * S5_BLOCK_STATE
    seq_map = lambda b, t: (b, t, 0)
    st_map = lambda b, t: (b, 0)
    fixed2 = lambda b, t: (0, 0)
    fixed3 = lambda b, t: (0, 0, 0)
    return pl.pallas_call(
        functools.partial(_s5_kernel, nb=nb, tt=tt),
        grid=(batch // nb, seq // tt),
        in_specs=[
            pl.BlockSpec((nb, tt, e), seq_map),
            pl.BlockSpec((nb, n_state), st_map),
            pl.BlockSpec((nb, n_state), st_map),
            pl.BlockSpec((n_blk, 1, S5_BLOCK_STATE), fixed3),
            pl.BlockSpec((n_blk, 1, S5_BLOCK_STATE), fixed3),
            pl.BlockSpec((n_blk, LANES, 2 * S5_BLOCK_STATE), fixed3),
            pl.BlockSpec((n_blk, 2 * S5_BLOCK_STATE, LANES), fixed3),
            pl.BlockSpec((1, e), fixed2),
        ],
        out_specs=[
            pl.BlockSpec((nb, tt, e), seq_map),
            pl.BlockSpec((nb, n_state), st_map),
            pl.BlockSpec((nb, n_state), st_map),
        ],
        out_shape=[
            jax.ShapeDtypeStruct((batch, seq, e), F32),
            jax.ShapeDtypeStruct((batch, n_state), F32),
            jax.ShapeDtypeStruct((batch, n_state), F32),
        ],
        scratch_shapes=[
            pltpu.VMEM((n_blk, nb, S5_BLOCK_STATE), F32),
            pltpu.VMEM((n_blk, nb, S5_BLOCK_STATE), F32),
            pltpu.VMEM((nb * (tt + S5_ROW_PAD), LANES), F32),
            pltpu.VMEM((2 * S5_BLOCK_STATE // LANES, nb * (tt + S5_ROW_PAD), LANES), F32),
        ],
        compiler_params=_cparams("parallel", "arbitrary"),
        name="s5_scan",
    )(u3, x0_re, x0_im, a_re, a_im, bmat, cmat, d_row)


def _s5_params(lam_re, lam_im, b_re, b_im, c_re, c_im, log_step):
    n_groups = lam_re.shape[0]
    n_blk = n_groups // S5_LANE_GROUPS
    dt = jnp.exp(log_step)[:, None]
    mag = jnp.exp(lam_re * dt)
    ang = lam_im * dt
    ab_re, ab_im = mag * jnp.cos(ang), mag * jnp.sin(ang)
    den = lam_re * lam_re + lam_im * lam_im
    nr = ab_re - 1.0
    f_re = (nr * lam_re + ab_im * lam_im) / den
    f_im = (ab_im * lam_re - nr * lam_im) / den
    bb_re = f_re[..., None] * b_re - f_im[..., None] * b_im
    bb_im = f_re[..., None] * b_im + f_im[..., None] * b_re
    eye = jnp.eye(S5_LANE_GROUPS, dtype=lam_re.dtype)

    def in_map(bb):
        bb = bb.reshape(n_blk, S5_LANE_GROUPS, S5_STATE, S5_GROUP)
        m = jnp.einsum('kgpc,gh->kgchp', bb, eye)
        return m.reshape(n_blk, LANES, S5_BLOCK_STATE)

    def out_map(cc):
        cc = cc.reshape(n_blk, S5_LANE_GROUPS, S5_GROUP, S5_STATE)
        m = jnp.einsum('kgcp,gh->kgphc', cc, eye)
        return m.reshape(n_blk, S5_BLOCK_STATE, LANES)

    bmat = jnp.concatenate([in_map(bb_re), in_map(bb_im)], axis=2).astype(BF16)
    cmat = jnp.concatenate([out_map(c_re), out_map(-c_im)], axis=1).astype(BF16)
    a_re = ab_re.reshape(n_blk, 1, S5_BLOCK_STATE)
    a_im = ab_im.reshape(n_blk, 1, S5_BLOCK_STATE)
    return a_re, a_im, bmat, cmat


def _odd_out_kernel(x_ref, y_ref, z_ref, w1_ref, w2_ref, wo_ref, o_ref):
    yb = jax.nn.gelu(y_ref[...]).astype(BF16)
    t = _dot(yb, w1_ref[...]) * jax.nn.sigmoid(_dot(yb, w2_ref[...])) * z_ref[...]
    o_ref[...] = x_ref[...] + _dot(t.astype(BF16), wo_ref[...])


def _odd_out(x2d, y2d, z2d, w1, w2, wo):
    m, d = x2d.shape
    e = y2d.shape[1]
    row = lambda i: (i, 0)
    fixed = lambda i: (0, 0)
    return pl.pallas_call(
        _odd_out_kernel,
        grid=(m // ROW_TILE,),
        in_specs=[
            pl.BlockSpec((ROW_TILE, d), row),
            pl.BlockSpec((ROW_TILE, e), row),
            pl.BlockSpec((ROW_TILE, e), row),
            pl.BlockSpec((e, e), fixed),
            pl.BlockSpec((e, e), fixed),
            pl.BlockSpec((e, d), fixed),
        ],
        out_specs=pl.BlockSpec((ROW_TILE, d), row),
        out_shape=jax.ShapeDtypeStruct((m, d), F32),
        compiler_params=_cparams("parallel"),
        name="odd_out_proj",
    )(x2d, y2d, z2d, w1, w2, wo)


def _rope_tables(pos):
    half = NSA_DH // 2
    inv = ROPE_THETA ** (-jnp.arange(half, dtype=F32) / half)
    ang = pos.astype(F32)[:, None] * inv[None, :]
    cos, sin = jnp.cos(ang), jnp.sin(ang)
    reps = LANES // NSA_DH
    return jnp.tile(cos, (1, 2 * reps)), jnp.tile(jnp.concatenate([-sin, sin], axis=1), (1, reps))


def _even_weights(norm_g, w_in, w_out, q_norm, k_norm, cmp_pos, cmp_w):
    d = w_in.shape[0]
    sizes = (RET_HEADS * RET_DK, RET_HEADS * RET_DK, A_WIDTH, A_WIDTH, B_WIDTH, 6 * NSA_KV_HEADS * NSA_DH,
             N_GATES, B_WIDTH)
    parts, o = [], 0
    for s in sizes:
        parts.append(w_in[:, o:o + s])
        o += s
    qa, ka, va, za, qn, kvb, gl, zb = parts
    w_perm = jnp.concatenate([qa, ka, va, za, qn, kvb, zb, gl, jnp.zeros((d, LANES - N_GATES), w_in.dtype)],
                             axis=1).astype(BF16)
    reps = LANES // NSA_DH
    qg = jnp.tile(q_norm[None, :], (1, reps))
    kg = jnp.tile(k_norm, (1, reps))
    eye = jnp.eye(NSA_KV_HEADS, dtype=cmp_w.dtype)
    w_bd = jnp.einsum('clde,gh,ck->lcgdkhe', cmp_w, eye, jnp.eye(2, dtype=cmp_w.dtype))
    w_bd = w_bd.reshape(L_CMP, 2 * LANES, 2 * LANES).astype(BF16)
    pe = jnp.broadcast_to(cmp_pos.transpose(1, 0, 2)[:, :, None, :], (L_CMP, 2, NSA_KV_HEADS, NSA_DH))
    pe = pe.reshape(L_CMP, 2 * LANES)
    wa = w_out[:A_WIDTH].astype(BF16)
    wb = w_out[A_WIDTH:].astype(BF16)
    return norm_g[None, :], w_perm, qg, kg, w_bd, pe, wa, wb


def _even_layer(x, pos0, s_ret, ew, gn_gain, cache=None, page_table=None, win_past=None):
    batch, seq, d = x.shape
    gain, w_perm, qg, kg, w_bd, pe, wa, wb = ew
    x2d = x.reshape(batch * seq, d)
    pos = pos0 + jnp.arange(seq, dtype=jnp.int32)
    cos_t, sin_t = _rope_tables(pos)
    if seq >= ROW_TILE:
        n_table_blocks = seq // ROW_TILE
    else:
        cos_t = jnp.tile(cos_t, (ROW_TILE // seq, 1))
        sin_t = jnp.tile(sin_t, (ROW_TILE // seq, 1))
        n_table_blocks = 1
    g, dh = NSA_KV_HEADS, NSA_DH
    if cache is None:
        qa, ka, va, za, qn, full_t, win_t, zb, gates, cmp_rows = _even_in(
            x2d, gain, w_perm, cos_t, sin_t, qg, kg, n_table_blocks, seq_tiles=seq // ROW_TILE)
        oa, s_fin = _retention(qa, ka, va, za, gn_gain[None, :], s_ret, batch, seq)
        cmp3 = cmp_rows.reshape(batch, seq, 2 * LANES)
        n_half = seq // CMP_STRIDE
        src_specs = [pl.BlockSpec((1, seq, LANES), lambda b: (b, 0, 0)),
                     pl.BlockSpec((1, seq, LANES), lambda b: (b, 0, 1))]
        cmp_kv = _compress([cmp3, cmp3], src_specs, (batch,), (), pe, w_bd, batch, seq, n_half)
        overlap, expand = _selection_tables(n_half, seq)
        ob = _nsa_prompt(qn, gates, zb, cmp_kv, full_t, win_t, overlap, expand, batch, seq)
        keep = min(WINDOW, seq)
        y = _even_out(x2d, oa, ob, wa, wb).reshape(batch, seq, d)
        full_rows = full_t.reshape(batch, 4, g, dh, seq).transpose(0, 4, 1, 2, 3)
        win_rows = win_t[:, :, seq - keep:].reshape(batch, 2, g, dh, keep).transpose(0, 4, 1, 2, 3)
        return y, s_fin, full_rows, win_rows
    else:
        qa, ka, va, za, qn, full_new, win_new, zb, gates = _even_in(x2d, gain, w_perm, cos_t, sin_t, qg, kg,
                                                                     n_table_blocks)
        oa, s_fin = _retention(qa, ka, va, za, gn_gain[None, :], s_ret, batch, seq)
        n_pages = page_table.shape[1]
        past_len = n_pages * PAGE_SIZE
        overlap, expand = _selection_tables(past_len // CMP_STRIDE, past_len + KEY_TILE)
        ob, win_out_t = _nsa_sample(page_table, qn, gates, zb, full_new, win_new, win_past, cache, pe, w_bd,
                                    overlap, expand, batch, seq, n_pages)
    y = _even_out(x2d, oa, ob, wa, wb).reshape(batch, seq, d)
    w_buf = win_out_t.shape[3]
    return (y, s_fin, full_new.reshape(batch, seq, 4, g, dh),
            win_out_t.reshape(batch, 2, g, dh, w_buf).transpose(0, 4, 1, 2, 3))


def _odd_layer(x, s_re, s_im, gain, w_in_bf, s5p, d_row, w1, w2, wo, tt):
    batch, seq, d = x.shape
    x2d = x.reshape(batch * seq, d)
    u, zs = _odd_in(x2d, gain, w_in_bf)
    e = u.shape[1]
    a_re, a_im, bmat, cmat = s5p
    n_groups, n_state = s_re.shape[1], s_re.shape[2]
    y, f_re, f_im = _s5(u.reshape(batch, seq, e), s_re.reshape(batch, n_groups * n_state),
                        s_im.reshape(batch, n_groups * n_state), a_re, a_im, bmat, cmat, d_row, tt)
    out = _odd_out(x2d, y.reshape(batch * seq, e), zs, w1, w2, wo).reshape(batch, seq, d)
    return out, f_re.reshape(batch, n_groups, n_state), f_im.reshape(batch, n_groups, n_state)


S5_TIME_TILE = 128


def kernel(x_prompt, x_sample, cache_nsa_kv, cache_nsa_win, state_ret, state_ssm_re, state_ssm_im, page_table,
           norm_even, w_in_even, w_out_even, ret_gn_gain, nsa_q_norm, nsa_k_norm, nsa_cmp_pos, nsa_cmp_w,
           norm_odd, w_in_odd, ssm_lambda_re, ssm_lambda_im, ssm_b_re, ssm_b_im, ssm_c_re, ssm_c_im, ssm_d,
           ssm_log_step, glu_w1, glu_w2, w_out_odd):
    bp, seq_p, _ = x_prompt.shape
    db, seq_s, _ = x_sample.shape
    n_pages = page_table.shape[1]
    past_len = n_pages * PAGE_SIZE
    depth = norm_even.shape[0] + norm_odd.shape[0]
    yp, ys = x_prompt, x_sample
    ret_p, ret_s, kv_p, kv_s, win_p, win_s = [], [], [], [], [], []
    sre_p, sim_p, sre_s, sim_s = [], [], [], []
    for layer in range(depth):
        li = layer // 2
        if layer % 2 == 0:
            ew = _even_weights(norm_even[li], w_in_even[li], w_out_even[li], nsa_q_norm[li], nsa_k_norm[li],
                               nsa_cmp_pos[li], nsa_cmp_w[li])
            s0 = jnp.zeros((bp, RET_HEADS, RET_DK, RET_DV), F32)
            yp, sr, kvr, wr = _even_layer(yp, 0, s0, ew, ret_gn_gain[li])
            ret_p.append(sr); kv_p.append(kvr); win_p.append(wr)
            cache_t = cache_nsa_kv[li].transpose(0, 2, 3, 4, 1).reshape(cache_nsa_kv.shape[1], 4, LANES, PAGE_SIZE)
            win_t = cache_nsa_win[li].transpose(0, 2, 3, 4, 1).reshape(db, 2, LANES, cache_nsa_win.shape[2])
            ys, sr2, kvr2, wr2 = _even_layer(ys, past_len, state_ret[li], ew, ret_gn_gain[li],
                                             cache=cache_t, page_table=page_table, win_past=win_t)
            ret_s.append(sr2); kv_s.append(kvr2); win_s.append(wr2)
        else:
            s5p = _s5_params(ssm_lambda_re[li], ssm_lambda_im[li], ssm_b_re[li], ssm_b_im[li],
                             ssm_c_re[li], ssm_c_im[li], ssm_log_step[li])
            gain = norm_odd[li][None, :]
            w_in_bf = w_in_odd[li].astype(BF16)
            w1, w2, wo = glu_w1[li].astype(BF16), glu_w2[li].astype(BF16), w_out_odd[li].astype(BF16)
            d_row = ssm_d[li][None, :]
            n_groups = ssm_lambda_re.shape[1]
            z0 = jnp.zeros((bp, n_groups, S5_STATE), F32)
            yp, fr, fi = _odd_layer(yp, z0, z0, gain, w_in_bf, s5p, d_row, w1, w2, wo, min(S5_TIME_TILE, seq_p))
            sre_p.append(fr); sim_p.append(fi)
            ys, fr2, fi2 = _odd_layer(ys, state_ssm_re[li], state_ssm_im[li], gain, w_in_bf, s5p, d_row,
                                      w1, w2, wo, min(S5_TIME_TILE, seq_s))
            sre_s.append(fr2); sim_s.append(fi2)
    return (yp, ys, jnp.stack(ret_p), jnp.stack(ret_s), jnp.stack(kv_p), jnp.stack(kv_s), jnp.stack(win_p),
            jnp.stack(win_s), jnp.stack(sre_p), jnp.stack(sim_p), jnp.stack(sre_s), jnp.stack(sim_s))
```

```python
import functools
import math

import jax
import jax.numpy as jnp
from jax import lax
from jax.experimental import pallas as pl
from jax.experimental.pallas import tpu as pltpu

F32 = jnp.float32
BF16 = jnp.bfloat16

LANES = 128
SUBLANES = 8
VMEM_LIMIT_BYTES = 48 * 2**20

EPS = 1e-6
ROPE_THETA = 10000.0
NEG_INF = -1e30
FORCE = 1e4

RET_HEADS = 4
RET_DK = 64
RET_DV = 128
RET_CHUNK = 128
A_WIDTH = RET_HEADS * RET_DV

NSA_HEADS = 8
NSA_KV_HEADS = 2
NSA_DH = 64
NSA_HPG = NSA_HEADS // NSA_KV_HEADS
B_WIDTH = NSA_HEADS * NSA_DH
L_CMP = 32
CMP_STRIDE = 16
CMP_RATIO = L_CMP // CMP_STRIDE
L_SEL = 64
N_SEL = 8
WINDOW = 512
PAGE_SIZE = 128
KV_ROW = 4 * NSA_KV_HEADS * NSA_DH
WIN_ROW = 2 * NSA_KV_HEADS * NSA_DH
KEY_TILE = 128
WIN_KEYS = WINDOW + KEY_TILE
SEL_CHUNK = 512
CMP_SLOT = CMP_STRIDE + 4
SAMPLE_REQS_PER_STEP = 2

S5_GROUP = 16
S5_STATE = 64
S5_LANE_GROUPS = LANES // S5_GROUP
S5_BLOCK_STATE = S5_LANE_GROUPS * S5_STATE
S5_ROW_PAD = 4
S5_SCAN_UNROLL = 4

QA0, KA0, VA0, ZA0, QN0, KVB0, ZB0, GL0 = 0, 256, 512, 1024, 1536, 2048, 2816, 3328
EVEN_COLS = GL0 + LANES
N_GATES = 3 * NSA_HEADS

ROW_TILE = 256


def _cparams(*sem):
    return pltpu.CompilerParams(dimension_semantics=sem, vmem_limit_bytes=VMEM_LIMIT_BYTES)


def _lane_iota(shape):
    return lax.broadcasted_iota(jnp.int32, shape, len(shape) - 1)


def _row_iota(shape):
    return lax.broadcasted_iota(jnp.int32, shape, len(shape) - 2)


def _dot(a, b):
    return jnp.dot(a, b, preferred_element_type=F32)


def _dot_nt(a, b):
    return lax.dot_general(a, b, (((1,), (1,)), ((), ())), preferred_element_type=F32)


def _rmsnorm_rows(x, g):
    return x * lax.rsqrt(jnp.mean(x * x, axis=-1, keepdims=True) + EPS) * g


def _swap_halves(x):
    return pltpu.roll(x, NSA_DH, axis=1)


def _rope_block(x, cos, sin_signed):
    half = NSA_DH // 2
    lane = _lane_iota(x.shape)
    first = (lane % NSA_DH) < half
    partner = jnp.where(first, pltpu.roll(x, LANES - half, axis=1), pltpu.roll(x, half, axis=1))
    return x * cos + partner * sin_signed


def _head_rms_block(x, g):
    lane = _lane_iota(x.shape)
    lo = lane < NSA_DH
    sq = x * x
    s_lo = jnp.sum(jnp.where(lo, sq, 0.0), axis=-1, keepdims=True)
    s_hi = jnp.sum(jnp.where(lo, 0.0, sq), axis=-1, keepdims=True)
    ms = jnp.where(lo, s_lo, s_hi) * (1.0 / NSA_DH)
    return x * lax.rsqrt(ms + EPS) * g


def _even_in_kernel(x_ref, g_ref, w_ref, cos_ref, sin_ref, qg_ref, kg_ref, *out_refs, transposed_kv):
    if transposed_kv:
        qa_ref, ka_ref, va_ref, za_ref, qn_ref, full_ref, win_ref, zb_ref, gl_ref, cmp_ref = out_refs
    else:
        qa_ref, ka_ref, va_ref, za_ref, qn_ref, full_ref, win_ref, zb_ref, gl_ref = out_refs
    hb = _rmsnorm_rows(x_ref[...], g_ref[...]).astype(BF16)
    cos = cos_ref[...]
    sin = sin_ref[...]

    def proj(c0):
        return _dot(hb, w_ref[:, c0:c0 + LANES])

    for j in range(RET_HEADS * RET_DK // LANES):
        c = j * LANES
        qa_ref[:, c:c + LANES] = _rope_block(proj(QA0 + c), cos, sin)
        ka_ref[:, c:c + LANES] = _rope_block(proj(KA0 + c), cos, sin) * (RET_DK ** -0.5)
    for j in range(A_WIDTH // LANES):
        c = j * LANES
        va_ref[:, c:c + LANES] = proj(VA0 + c)
        za_ref[:, c:c + LANES] = jax.nn.silu(proj(ZA0 + c))
    for j in range(B_WIDTH // LANES):
        c = j * LANES
        qn_ref[:, c:c + LANES] = _rope_block(_head_rms_block(proj(QN0 + c), qg_ref[...]), cos, sin)
        zb_ref[:, c:c + LANES] = jax.nn.silu(proj(ZB0 + c))
    for j in range(6):
        y = proj(KVB0 + j * LANES)
        if j % 2 == 0:
            y = _rope_block(_head_rms_block(y, kg_ref[j // 2:j // 2 + 1, :]), cos, sin)
        if transposed_kv:
            if j < 4:
                full_ref[0, j * LANES:(j + 1) * LANES, :] = y.T
            else:
                win_ref[0, (j - 4) * LANES:(j - 3) * LANES, :] = y.T
            if j < 2:
                cmp_ref[:, j * LANES:(j + 1) * LANES] = y
        elif j < 4:
            full_ref[:, j * LANES:(j + 1) * LANES] = y
        else:
            win_ref[:, (j - 4) * LANES:(j - 3) * LANES] = y
    gl_ref[...] = jax.nn.sigmoid(proj(GL0))


def _even_in(x2d, gain, w_bf, cos_t, sin_t, qg, kg, n_table_blocks, seq_tiles=None):
    m, d = x2d.shape
    grid = (m // ROW_TILE,)
    row = lambda i: (i, 0)
    fixed = lambda i: (0, 0)
    table = lambda i: (i % n_table_blocks, 0)
    widths = (RET_HEADS * RET_DK, RET_HEADS * RET_DK, A_WIDTH, A_WIDTH, B_WIDTH, KV_ROW, WIN_ROW, B_WIDTH, LANES)
    out_specs = [pl.BlockSpec((ROW_TILE, w), row) for w in widths]
    out_shape = [jax.ShapeDtypeStruct((m, w), F32) for w in widths]
    if seq_tiles is not None:
        batch, seq = grid[0] // seq_tiles, seq_tiles * ROW_TILE
        fmajor = lambda i: (i // seq_tiles, 0, i % seq_tiles)
        for idx, w in ((5, KV_ROW), (6, WIN_ROW)):
            out_specs[idx] = pl.BlockSpec((1, w, ROW_TILE), fmajor)
            out_shape[idx] = jax.ShapeDtypeStruct((batch, w, seq), F32)
        out_specs.append(pl.BlockSpec((ROW_TILE, 2 * LANES), row))
        out_shape.append(jax.ShapeDtypeStruct((m, 2 * LANES), F32))
    return pl.pallas_call(
        functools.partial(_even_in_kernel, transposed_kv=seq_tiles is not None),
        grid=grid,
        in_specs=[
            pl.BlockSpec((ROW_TILE, d), row),
            pl.BlockSpec((1, d), fixed),
            pl.BlockSpec((d, EVEN_COLS), fixed),
            pl.BlockSpec((ROW_TILE, LANES), table),
            pl.BlockSpec((ROW_TILE, LANES), table),
            pl.BlockSpec((1, LANES), fixed),
            pl.BlockSpec((3, LANES), fixed),
        ],
        out_specs=out_specs,
        out_shape=out_shape,
        compiler_params=_cparams("parallel"),
        name="even_in_proj",
    )(x2d, gain, w_bf, cos_t, sin_t, qg, kg)


def _retention_kernel(q_ref, k_ref, v_ref, z_ref, gn_ref, s0_ref, dmat_ref, qdec_ref, kdec_ref, cdec_ref,
                      o_ref, sfin_ref, s_scr, *, rows):
    c = pl.program_id(1)

    @pl.when(c == 0)
    def _():
        s_scr[...] = s0_ref[0]

    def padded(ref):
        x = ref[...]
        if rows == RET_CHUNK:
            return x
        return jnp.concatenate([x, jnp.zeros((RET_CHUNK - rows, x.shape[1]), F32)], axis=0)

    q = padded(q_ref)
    k = padded(k_ref)
    v = padded(v_ref)
    qd = q * qdec_ref[...]
    kd = k * kdec_ref[...]
    lane = _lane_iota((RET_CHUNK, LANES))
    for pair in range(RET_HEADS // 2):
        cols = slice(pair * LANES, (pair + 1) * LANES)
        q2, k2b, qd2, kd2 = q[:, cols], k[:, cols].astype(BF16), qd[:, cols], kd[:, cols]
        kd2_t = kd2.T
        s_pair = jnp.concatenate([s_scr[2 * pair], s_scr[2 * pair + 1]], axis=0).astype(BF16)
        for sub in range(2):
            h = 2 * pair + sub
            mine = (lane >= sub * RET_DK) & (lane < (sub + 1) * RET_DK)
            qm = jnp.where(mine, q2, 0.0).astype(BF16)
            qdm = jnp.where(mine, qd2, 0.0).astype(BF16)
            vh = v[:, h * RET_DV:(h + 1) * RET_DV]
            vhb = vh.astype(BF16)
            scores = _dot_nt(qm, k2b) * dmat_ref[h]
            out = _dot(scores.astype(BF16), vhb) + _dot(qdm, s_pair)
            kt = kd2_t[sub * RET_DK:(sub + 1) * RET_DK, :].astype(BF16)
            s_scr[h] = s_scr[h] * cdec_ref[h] + _dot(kt, vhb)
            mu = jnp.mean(out, axis=-1, keepdims=True)
            cen = out - mu
            var = jnp.mean(cen * cen, axis=-1, keepdims=True)
            y = cen * lax.rsqrt(var + EPS) * gn_ref[:, h * RET_DV:(h + 1) * RET_DV]
            o_ref[:, h * RET_DV:(h + 1) * RET_DV] = y[:rows] * z_ref[:, h * RET_DV:(h + 1) * RET_DV]

    @pl.when(c == pl.num_programs(1) - 1)
    def _():
        sfin_ref[0] = s_scr[...]


def _retention_tables(rows):
    log_g = jnp.log(1.0 - 2.0 ** (-5.0 - jnp.arange(RET_HEADS, dtype=F32)))
    idx = jnp.arange(RET_CHUNK, dtype=F32)
    live = idx < rows
    diff = idx[:, None] - idx[None, :]
    causal = (diff >= 0) & live[:, None] & live[None, :]
    dmat = jnp.exp(jnp.where(causal, diff, 0.0)[None] * log_g[:, None, None]) * causal[None]
    qdec = jnp.exp((idx + 1.0)[:, None] * log_g[None, :]) * live[:, None]
    kdec = jnp.exp((rows - 1.0 - idx)[:, None] * log_g[None, :]) * live[:, None]
    qdec = jnp.repeat(qdec, RET_DK, axis=1)
    kdec = jnp.repeat(kdec, RET_DK, axis=1)
    cdec = jnp.broadcast_to(jnp.exp(rows * log_g)[:, None, None], (RET_HEADS, RET_DK, RET_DV))
    return dmat, qdec, kdec, cdec


def _retention(qa, ka, va, za, gn_gain, s0, batch, seq):
    rows = min(RET_CHUNK, seq)
    n_chunks = seq // rows
    dmat, qdec, kdec, cdec = _retention_tables(rows)
    tok = lambda b, c: (b * n_chunks + c, 0)
    fixed2 = lambda b, c: (0, 0)
    fixed3 = lambda b, c: (0, 0, 0)
    state = lambda b, c: (b, 0, 0, 0)
    qk_w = RET_HEADS * RET_DK
    return pl.pallas_call(
        functools.partial(_retention_kernel, rows=rows),
        grid=(batch, n_chunks),
        in_specs=[
            pl.BlockSpec((rows, qk_w), tok),
            pl.BlockSpec((rows, qk_w), tok),
            pl.BlockSpec((rows, A_WIDTH), tok),
            pl.BlockSpec((rows, A_WIDTH), tok),
            pl.BlockSpec((1, A_WIDTH), fixed2),
            pl.BlockSpec((1, RET_HEADS, RET_DK, RET_DV), state),
            pl.BlockSpec((RET_HEADS, RET_CHUNK, RET_CHUNK), fixed3),
            pl.BlockSpec((RET_CHUNK, qk_w), fixed2),
            pl.BlockSpec((RET_CHUNK, qk_w), fixed2),
            pl.BlockSpec((RET_HEADS, RET_DK, RET_DV), fixed3),
        ],
        out_specs=[
            pl.BlockSpec((rows, A_WIDTH), tok),
            pl.BlockSpec((1, RET_HEADS, RET_DK, RET_DV), state),
        ],
        out_shape=[
            jax.ShapeDtypeStruct((batch * seq, A_WIDTH), F32),
            jax.ShapeDtypeStruct((batch, RET_HEADS, RET_DK, RET_DV), F32),
        ],
        scratch_shapes=[pltpu.VMEM((RET_HEADS, RET_DK, RET_DV), F32)],
        compiler_params=_cparams("parallel", "arbitrary"),
        name="retention",
    )(qa, ka, va, za, gn_gain, s0, dmat, qdec, kdec, cdec)


def _compress_kernel(*refs, n_pref, n_src, src_rows):
    refs = refs[n_pref:]
    src_k = refs[:n_src]
    src_v = refs[n_src:2 * n_src]
    pe_ref, w_ref, o_ref = refs[2 * n_src:]
    per_src = src_rows // CMP_STRIDE
    n_half = n_src * per_src
    acc = [jnp.zeros((n_half, 2 * LANES), F32) for _ in range(CMP_RATIO)]

    def strided_rows(srcs, l):
        rows = [s[0, pl.ds(l, per_src, stride=CMP_STRIDE), :] for s in srcs]
        return rows[0] if n_src == 1 else jnp.concatenate(rows, axis=0)

    for l in range(CMP_STRIDE):
        x = jnp.concatenate([strided_rows(src_k, l), strided_rows(src_v, l)], axis=1)
        for r in range(CMP_RATIO):
            i = r * CMP_STRIDE + l
            acc[r] = acc[r] + _dot((x + pe_ref[i:i + 1, :]).astype(BF16), w_ref[i])
    out = acc[0]
    for r in range(1, CMP_RATIO):
        out = out + pltpu.roll(acc[r], n_half - r, axis=0)
    live = _row_iota(out.shape) < n_half - CMP_RATIO + 1
    o_ref[0] = jnp.where(live, out, 0.0)


def _compress(src_arrays, src_specs, grid, extra_prefetch, pe, w_bd, batch, src_rows, n_half):
    n_src = len(src_specs) // 2
    n_pref = len(extra_prefetch)
    kern = functools.partial(_compress_kernel, n_pref=n_pref, n_src=n_src, src_rows=src_rows)
    fixed2 = lambda *a: (0, 0)
    fixed3 = lambda *a: (0, 0, 0)
    out_map = lambda b, *a: (b, 0, 0)
    return pl.pallas_call(
        kern,
        grid_spec=pltpu.PrefetchScalarGridSpec(
            num_scalar_prefetch=n_pref,
            grid=grid,
            in_specs=list(src_specs) + [
                pl.BlockSpec((L_CMP, 2 * LANES), fixed2),
                pl.BlockSpec((L_CMP, 2 * LANES, 2 * LANES), fixed3),
            ],
            out_specs=pl.BlockSpec((1, n_half, 2 * LANES), out_map),
        ),
        out_shape=jax.ShapeDtypeStruct((batch, n_half, 2 * LANES), F32),
        compiler_params=_cparams("parallel"),
        name="kv_compress",
    )(*extra_prefetch, *src_arrays, pe, w_bd)


def _split3_bf16(x):
    hi = x.astype(BF16)
    r1 = x - hi.astype(F32)
    mid = r1.astype(BF16)
    lo = (r1 - mid.astype(F32)).astype(BF16)
    return hi, mid, lo


def _stack_group_queries(q, g):
    lane = _lane_iota((q.shape[0], LANES))
    in_g = (lane >= g * NSA_DH) & (lane < (g + 1) * NSA_DH)
    parts = []
    for hh in range(NSA_HPG):
        h = g * NSA_HPG + hh
        two = q[:, (h // 2) * LANES:(h // 2 + 1) * LANES]
        if h % 2 != g:
            two = _swap_halves(two)
        parts.append(jnp.where(in_g, two, 0.0))
    return (jnp.concatenate(parts, axis=0) * (NSA_DH ** -0.5)).astype(BF16)


def _masked_softmax(s, ok):
    s = jnp.where(ok, s, NEG_INF)
    e = jnp.exp(s - jnp.max(s, axis=-1, keepdims=True))
    return jnp.where(ok, e * (1.0 / jnp.sum(e, axis=-1, keepdims=True)), 0.0)


def _sum_heads(p1, tq):
    psum = p1[0:tq]
    for hh in range(1, NSA_HPG):
        psum = psum + p1[hh * tq:(hh + 1) * tq]
    return psum


def _choose_blocks(psum, t0, tq, overlap, n_blocks):
    rows = psum.shape[0]
    imp = sum(_dot(t, overlap) for t in _split3_bf16(psum))
    shift = int(math.log2(L_SEL))

    def ranked(score, blk, valid, take):
        rank = jnp.zeros(score.shape, F32)
        for s in range(n_blocks):
            cand = take(s)
            rank = rank + ((cand > score) | ((cand == score) & (blk > s))).astype(F32)
        return ((rank < N_SEL) & valid).astype(F32)

    if rows % LANES:
        tpos = jnp.concatenate([t0 + _row_iota((tq, 1))] * (rows // tq), axis=0)
        blk = _lane_iota((rows, LANES))
        cur = tpos >> shift
        forced = (blk == 0) | (blk == cur) | (blk == cur - 1)
        valid = (blk * L_SEL <= tpos) & (blk < n_blocks)
        score = jnp.where(valid, jnp.where(forced, FORCE, imp), -FORCE)
        return ranked(score, blk, valid, lambda s: score[:, s:s + 1]).astype(BF16)

    nbp = -(-n_blocks // SUBLANES) * SUBLANES
    tiles = rows // LANES
    imp_t = jnp.concatenate([imp[i * LANES:(i + 1) * LANES].T[:nbp] for i in range(tiles)], axis=1)
    tpos = jnp.concatenate([t0 + _lane_iota((1, tq))] * (rows // tq), axis=1)
    blk = _row_iota((nbp, rows))
    cur = tpos >> shift
    forced = (blk == 0) | (blk == cur) | (blk == cur - 1)
    valid = (blk * L_SEL <= tpos) & (blk < n_blocks)
    score = jnp.where(valid, jnp.where(forced, FORCE, imp_t), -FORCE)
    chosen_t = ranked(score, blk, valid, lambda s: score[s:s + 1, :])
    chosen_t = jnp.concatenate([chosen_t, jnp.zeros((LANES - nbp, rows), F32)], axis=0)
    return jnp.concatenate([chosen_t[:, i * LANES:(i + 1) * LANES].T for i in range(tiles)],
                           axis=0).astype(BF16)


def _gate_and_store(gates, zs, branch_outs, o_ref, row0=0):
    tq = gates.shape[0]
    lane = _lane_iota((tq, LANES))
    ext = []
    for g in range(NSA_KV_HEADS):
        o_cmp, o_sel, o_win = branch_outs[g]
        for hh in range(NSA_HPG):
            h = g * NSA_HPG + hh
            rows = slice(hh * tq, (hh + 1) * tq)
            o = (gates[:, 3 * h:3 * h + 1] * o_cmp[rows] + gates[:, 3 * h + 1:3 * h + 2] * o_sel[rows]
                 + gates[:, 3 * h + 2:3 * h + 3] * o_win[rows])
            ext.append(_swap_halves(o) if h % 2 != g else o)
    for j in range(NSA_HEADS // 2):
        both = jnp.where(lane < NSA_DH, ext[2 * j], ext[2 * j + 1])
        o_ref[row0:row0 + tq, j * LANES:(j + 1) * LANES] = both * zs[:, j * LANES:(j + 1) * LANES]


def _nsa_prompt_kernel(q_ref, g_ref, z_ref, cmp_ref, sel_ref, win_ref, ov_ref, ex_ref, o_ref, *, n_blocks):
    i = pl.program_id(1)
    tq = q_ref.shape[0]
    q = q_ref[...]
    t0 = i * tq
    tpos = t0 + _row_iota((tq, 1))
    tpos_all = jnp.concatenate([tpos] * NSA_HEADS, axis=0)
    grp_rows = NSA_HPG * tq
    qs = jnp.concatenate([_stack_group_queries(q, g) for g in range(NSA_KV_HEADS)], axis=0)

    cmp_blk = cmp_ref[0].astype(BF16)
    cmp_ok = (_lane_iota((1, cmp_blk.shape[0])) * CMP_STRIDE + (L_CMP - 1)) <= tpos_all
    p1 = _masked_softmax(_dot_nt(qs, cmp_blk[:, :LANES]), cmp_ok)
    o_cmp = _dot(p1.astype(BF16), cmp_blk[:, LANES:])
    psum = jnp.concatenate([_sum_heads(p1[g * grp_rows:(g + 1) * grp_rows], tq)
                            for g in range(NSA_KV_HEADS)], axis=0)
    chosen = _choose_blocks(psum, t0, tq, ov_ref[...], n_blocks)

    def sel_chunk(c, carry):
        m, l, acc = carry
        k0 = pl.multiple_of(c * SEL_CHUNK, SEL_CHUNK)
        k_t = sel_ref[0, 0:LANES, pl.ds(k0, SEL_CHUNK)].astype(BF16)
        v_t = sel_ref[0, LANES:2 * LANES, pl.ds(k0, SEL_CHUNK)].astype(BF16)
        causal = jnp.concatenate([(k0 + _lane_iota((1, SEL_CHUNK))) <= tpos] * NSA_KV_HEADS, axis=0)
        key_on = (_dot(chosen, ex_ref[:, pl.ds(k0, SEL_CHUNK)]) > 0.5) & causal
        bias = jnp.where(key_on, 0.0, NEG_INF)
        bias = jnp.concatenate([bias[g * tq:(g + 1) * tq] for g in range(NSA_KV_HEADS)
                                for _ in range(NSA_HPG)], axis=0)
        s = _dot(qs, k_t) + bias
        m_new = jnp.maximum(m, jnp.max(s, axis=-1, keepdims=True))
        alpha = jnp.exp(m - m_new)
        e = jnp.exp(s - m_new)
        l_new = alpha * l + jnp.sum(e, axis=-1, keepdims=True)
        return m_new, l_new, alpha * acc + _dot_nt(e.astype(BF16), v_t)

    rows = NSA_HEADS * tq
    init = (jnp.full((rows, 1), NEG_INF, F32), jnp.zeros((rows, 1), F32), jnp.zeros((rows, LANES), F32))
    n_chunks = t0 // SEL_CHUNK + 1
    _, l2, acc2 = lax.fori_loop(0, n_chunks, sel_chunk, init)
    o_sel = acc2 / l2

    w0 = pl.multiple_of(jnp.maximum(t0 - WINDOW, 0), KEY_TILE)
    wk_t = win_ref[0, 0:LANES, pl.ds(w0, WIN_KEYS)].astype(BF16)
    wv_t = win_ref[0, LANES:2 * LANES, pl.ds(w0, WIN_KEYS)].astype(BF16)
    win_kpos = w0 + _lane_iota((1, WIN_KEYS))
    win_ok = (win_kpos <= tpos_all) & (win_kpos > tpos_all - WINDOW)
    s3 = jnp.where(win_ok, _dot(qs, wk_t), NEG_INF)
    e3 = jnp.exp(s3 - jnp.max(s3, axis=-1, keepdims=True))
    o_win = _dot_nt(e3.astype(BF16), wv_t) / jnp.sum(e3, axis=-1, keepdims=True)

    branch_outs = [(o_cmp[g * grp_rows:(g + 1) * grp_rows], o_sel[g * grp_rows:(g + 1) * grp_rows],
                    o_win[g * grp_rows:(g + 1) * grp_rows]) for g in range(NSA_KV_HEADS)]
    _gate_and_store(g_ref[...], z_ref[...], branch_outs, o_ref)


def _nsa_prompt(qn, gates, zb, cmp_kv, full_t, win_t, overlap, expand, batch, seq):
    tq = KEY_TILE
    nq = seq // tq
    tok = lambda b, i: (b * nq + i, 0)
    per_b = lambda b, i: (b, 0, 0)
    sel_half = lambda b, i: (b, 1, 0)
    fixed = lambda b, i: (0, 0)
    n_blocks = -(-seq // L_SEL)
    return pl.pallas_call(
        functools.partial(_nsa_prompt_kernel, n_blocks=n_blocks),
        grid=(batch, nq),
        in_specs=[
            pl.BlockSpec((tq, B_WIDTH), tok),
            pl.BlockSpec((tq, LANES), tok),
            pl.BlockSpec((tq, B_WIDTH), tok),
            pl.BlockSpec((1, cmp_kv.shape[1], 2 * LANES), per_b),
            pl.BlockSpec((1, 2 * LANES, seq), sel_half),
            pl.BlockSpec((1, WIN_ROW, seq), per_b),
            pl.BlockSpec(overlap.shape, fixed),
            pl.BlockSpec(expand.shape, fixed),
        ],
        out_specs=pl.BlockSpec((tq, B_WIDTH), tok),
        out_shape=jax.ShapeDtypeStruct((batch * seq, B_WIDTH), F32),
        compiler_params=_cparams("parallel", "arbitrary"),
        name="nsa_prompt",
    )(qn, gates, zb, cmp_kv, full_t, win_t, overlap, expand)


def _nsa_sample_kernel(*refs, nb, n_pages, past_len, n_blocks):
    q_ref, g_ref, z_ref, newf_ref, neww_ref, winp_ref = refs[1:7]
    pages = refs[7:7 + nb * n_pages]
    pe_ref, w_ref, ov_ref, ex_ref, o_ref, wout_ref, xs_ref = refs[7 + nb * n_pages:]
    tq = q_ref.shape[0] // nb
    w_buf = winp_ref.shape[3]
    halves_per_page = PAGE_SIZE // CMP_STRIDE
    n_half = past_len // CMP_STRIDE
    total_half = nb * n_half

    for j in range(nb * n_pages):
        for c in range(2):
            x = pages[j][0, c].T
            for hb in range(halves_per_page):
                slot = (j * halves_per_page + hb) * CMP_SLOT
                xs_ref[c, slot:slot + CMP_STRIDE, :] = x[hb * CMP_STRIDE:(hb + 1) * CMP_STRIDE]

    acc = [jnp.zeros((total_half, 2 * LANES), F32) for _ in range(CMP_RATIO)]
    for l in range(CMP_STRIDE):
        x = jnp.concatenate([xs_ref[c, pl.ds(l, total_half, stride=CMP_SLOT), :] for c in range(2)], axis=1)
        for r in range(CMP_RATIO):
            i = r * CMP_STRIDE + l
            acc[r] = acc[r] + _dot((x + pe_ref[i:i + 1, :]).astype(BF16), w_ref[i])
    cmp_all = acc[0]
    for r in range(1, CMP_RATIO):
        cmp_all = cmp_all + pltpu.roll(acc[r], total_half - r, axis=0)
    live = _row_iota((n_half, 2 * LANES)) < n_half - CMP_RATIO + 1

    pad = jnp.zeros((KEY_TILE - tq, LANES), F32)
    lane = _lane_iota((LANES, LANES))
    per_req = NSA_HEADS * tq
    tpos = past_len + _row_iota((tq, 1))
    tpos_all = jnp.concatenate([tpos] * (nb * NSA_HEADS), axis=0)
    tpos_grp = jnp.concatenate([tpos] * (nb * NSA_KV_HEADS), axis=0)

    def padded_rows(x):
        return jnp.concatenate([x, pad], axis=0)

    def per_request(fn):
        return jnp.concatenate([fn(j, slice(j * per_req, (j + 1) * per_req)) for j in range(nb)], axis=0)

    tok = [slice(j * tq, (j + 1) * tq) for j in range(nb)]
    qs = jnp.concatenate([_stack_group_queries(q_ref[tok[j], :], g)
                          for j in range(nb) for g in range(NSA_KV_HEADS)], axis=0)
    cmp_blk = [jnp.where(live, cmp_all[j * n_half:(j + 1) * n_half], 0.0).astype(BF16) for j in range(nb)]
    mine = [pages[j * n_pages:(j + 1) * n_pages] for j in range(nb)]
    newf = [newf_ref[tok[j], :] for j in range(nb)]
    neww = [neww_ref[tok[j], :] for j in range(nb)]

    cmp_ok = (_lane_iota((1, n_half)) * CMP_STRIDE + (L_CMP - 1)) <= tpos_all
    p1 = _masked_softmax(per_request(lambda j, r: _dot_nt(qs[r], cmp_blk[j][:, :LANES])), cmp_ok)
    p1b = p1.astype(BF16)
    o_cmp = per_request(lambda j, r: _dot(p1b[r], cmp_blk[j][:, LANES:]))

    grp_rows = NSA_HPG * tq
    psum = jnp.concatenate([_sum_heads(p1[i * grp_rows:(i + 1) * grp_rows], tq)
                            for i in range(nb * NSA_KV_HEADS)], axis=0)
    chosen = _choose_blocks(psum, past_len, tq, ov_ref[...], n_blocks)
    key_on = (_dot(chosen, ex_ref[...]) > 0.5) & (_lane_iota((1, past_len + KEY_TILE)) <= tpos_grp)
    bias = jnp.where(key_on, 0.0, NEG_INF)
    bias = jnp.concatenate([bias[i * tq:(i + 1) * tq] for i in range(nb * NSA_KV_HEADS)
                            for _ in range(NSA_HPG)], axis=0)

    def sel_scores(j, r):
        k_t = jnp.concatenate([p[0, 2].astype(BF16) for p in mine[j]], axis=1)
        k_new = padded_rows(newf[j][:, 2 * LANES:3 * LANES]).astype(BF16)
        return jnp.concatenate([_dot(qs[r], k_t), _dot_nt(qs[r], k_new)], axis=1)

    s2 = per_request(sel_scores) + bias
    e2 = jnp.exp(s2 - jnp.max(s2, axis=-1, keepdims=True))
    l2 = jnp.sum(e2, axis=-1, keepdims=True)
    e2 = e2.astype(BF16)

    def sel_values(j, r):
        v_t = jnp.concatenate([p[0, 3].astype(BF16) for p in mine[j]], axis=1)
        v_new = padded_rows(newf[j][:, 3 * LANES:4 * LANES]).astype(BF16)
        return _dot_nt(e2[r, :past_len], v_t) + _dot(e2[r, past_len:], v_new)

    o_sel = per_request(sel_values) / l2

    win_kpos = (past_len - w_buf) + _lane_iota((1, w_buf + KEY_TILE))
    win_ok = (win_kpos <= tpos_all) & (win_kpos > tpos_all - WINDOW)
    s3 = per_request(lambda j, r: jnp.concatenate(
        [_dot(qs[r], winp_ref[j, 0].astype(BF16)),
         _dot_nt(qs[r], padded_rows(neww[j][:, :LANES]).astype(BF16))], axis=1))
    s3 = jnp.where(win_ok, s3, NEG_INF)
    e3 = jnp.exp(s3 - jnp.max(s3, axis=-1, keepdims=True))
    l3 = jnp.sum(e3, axis=-1, keepdims=True)
    e3 = e3.astype(BF16)
    o_win = per_request(lambda j, r: _dot_nt(e3[r, :w_buf], winp_ref[j, 1].astype(BF16))
                        + _dot(e3[r, w_buf:], padded_rows(neww[j][:, LANES:]).astype(BF16))) / l3

    for j in range(nb):
        branch_outs = []
        for g in range(NSA_KV_HEADS):
            r = slice((j * NSA_KV_HEADS + g) * grp_rows, (j * NSA_KV_HEADS + g + 1) * grp_rows)
            branch_outs.append((o_cmp[r], o_sel[r], o_win[r]))
        _gate_and_store(g_ref[tok[j], :], z_ref[tok[j], :], branch_outs, o_ref, row0=j * tq)

        for c in range(2):
            shifted = pltpu.roll(winp_ref[j, c], w_buf - tq, axis=1)
            new_cols = pltpu.roll(padded_rows(neww[j][:, c * LANES:(c + 1) * LANES]).T, LANES - tq, axis=1)
            wout_ref[j, c, :, 0:w_buf - LANES] = shifted[:, 0:w_buf - LANES]
            wout_ref[j, c, :, w_buf - LANES:w_buf] = jnp.where(lane >= LANES - tq, new_cols,
                                                               shifted[:, w_buf - LANES:w_buf])


def _nsa_sample(page_table, qn, gates, zb, full_new, win_new, win_t, cache_t, pe, w_bd, overlap, expand,
                batch, tq, n_pages):
    nb = SAMPLE_REQS_PER_STEP
    past_len = n_pages * PAGE_SIZE
    w_buf = win_t.shape[3]
    n_blocks = -(-(past_len + tq) // L_SEL)
    tok = lambda b, pt: (b, 0)
    per_b = lambda b, pt: (b, 0, 0, 0)
    fixed2 = lambda b, pt: (0, 0)
    fixed3 = lambda b, pt: (0, 0, 0)

    def page_spec(j, p):
        return pl.BlockSpec((1, 4, LANES, PAGE_SIZE), lambda b, pt: (pt[b * nb + j, p], 0, 0, 0))

    n_slots = nb * n_pages * (PAGE_SIZE // CMP_STRIDE)
    return pl.pallas_call(
        functools.partial(_nsa_sample_kernel, nb=nb, n_pages=n_pages, past_len=past_len, n_blocks=n_blocks),
        grid_spec=pltpu.PrefetchScalarGridSpec(
            num_scalar_prefetch=1,
            grid=(batch // nb,),
            in_specs=[
                pl.BlockSpec((nb * tq, B_WIDTH), tok),
                pl.BlockSpec((nb * tq, LANES), tok),
                pl.BlockSpec((nb * tq, B_WIDTH), tok),
                pl.BlockSpec((nb * tq, KV_ROW), tok),
                pl.BlockSpec((nb * tq, WIN_ROW), tok),
                pl.BlockSpec((nb, 2, LANES, w_buf), per_b),
            ] + [page_spec(j, p) for j in range(nb) for p in range(n_pages)] + [
                pl.BlockSpec(pe.shape, fixed2),
                pl.BlockSpec(w_bd.shape, fixed3),
                pl.BlockSpec(overlap.shape, fixed2),
                pl.BlockSpec(expand.shape, fixed2),
            ],
            out_specs=[
                pl.BlockSpec((nb * tq, B_WIDTH), tok),
                pl.BlockSpec((nb, 2, LANES, w_buf), per_b),
            ],
            scratch_shapes=[pltpu.VMEM((2, n_slots * CMP_SLOT, LANES), F32)],
        ),
        out_shape=[
            jax.ShapeDtypeStruct((batch * tq, B_WIDTH), F32),
            jax.ShapeDtypeStruct((batch, 2, LANES, w_buf), F32),
        ],
        compiler_params=_cparams("parallel"),
        name="nsa_sample",
    )(page_table, qn, gates, zb, full_new, win_new, win_t, *([cache_t] * (nb * n_pages)), pe, w_bd,
      overlap, expand)


def _selection_tables(n_cmp_rows, n_sel_keys):
    n = jnp.arange(n_cmp_rows, dtype=jnp.int32)[:, None]
    s = jnp.arange(LANES, dtype=jnp.int32)[None, :]
    c_start = n * CMP_STRIDE
    s_start = s * L_SEL
    overlap = ((c_start < s_start + L_SEL) & (s_start < c_start + L_CMP)).astype(BF16)
    key = jnp.arange(n_sel_keys, dtype=jnp.int32)[None, :]
    expand = ((key // L_SEL) == jnp.arange(LANES, dtype=jnp.int32)[:, None]).astype(BF16)
    return overlap, expand


def _even_out_kernel(x_ref, oa_ref, ob_ref, wa_ref, wb_ref, y_ref):
    y_ref[...] = (x_ref[...] + _dot(oa_ref[...].astype(BF16), wa_ref[...])
                  + _dot(ob_ref[...].astype(BF16), wb_ref[...]))


def _even_out(x2d, oa, ob, wa, wb):
    m, d = x2d.shape
    row = lambda i: (i, 0)
    fixed = lambda i: (0, 0)
    return pl.pallas_call(
        _even_out_kernel,
        grid=(m // ROW_TILE,),
        in_specs=[
            pl.BlockSpec((ROW_TILE, d), row),
            pl.BlockSpec((ROW_TILE, A_WIDTH), row),
            pl.BlockSpec((ROW_TILE, B_WIDTH), row),
            pl.BlockSpec((A_WIDTH, d), fixed),
            pl.BlockSpec((B_WIDTH, d), fixed),
        ],
        out_specs=pl.BlockSpec((ROW_TILE, d), row),
        out_shape=jax.ShapeDtypeStruct((m, d), F32),
        compiler_params=_cparams("parallel"),
        name="even_out_proj",
    )(x2d, oa, ob, wa, wb)


def _odd_in_kernel(x_ref, g_ref, w_ref, u_ref, z_ref):
    hb = _rmsnorm_rows(x_ref[...], g_ref[...]).astype(BF16)
    e = u_ref.shape[1]
    u_ref[...] = _dot(hb, w_ref[:, :e])
    z_ref[...] = jax.nn.silu(_dot(hb, w_ref[:, e:]))


def _odd_in(x2d, gain, w_bf):
    m, d = x2d.shape
    e = w_bf.shape[1] // 2
    row = lambda i: (i, 0)
    fixed = lambda i: (0, 0)
    return pl.pallas_call(
        _odd_in_kernel,
        grid=(m // ROW_TILE,),
        in_specs=[
            pl.BlockSpec((ROW_TILE, d), row),
            pl.BlockSpec((1, d), fixed),
            pl.BlockSpec((d, 2 * e), fixed),
        ],
        out_specs=[pl.BlockSpec((ROW_TILE, e), row), pl.BlockSpec((ROW_TILE, e), row)],
        out_shape=[jax.ShapeDtypeStruct((m, e), F32), jax.ShapeDtypeStruct((m, e), F32)],
        compiler_params=_cparams("parallel"),
        name="odd_in_proj",
    )(x2d, gain, w_bf)


def _s5_kernel(u_ref, x0r_ref, x0i_ref, ar_ref, ai_ref, bm_ref, cm_ref, d_ref,
               y_ref, fr_ref, fi_ref, st_re, st_im, ubuf, xbuf, *, nb, tt):
    t_idx = pl.program_id(1)
    n_blk = bm_ref.shape[0]
    half = S5_BLOCK_STATE
    n_chunk = half // LANES
    pitch = tt + S5_ROW_PAD

    @pl.when(t_idx == 0)
    def _():
        for kb in range(n_blk):
            st_re[kb] = x0r_ref[:, kb * half:(kb + 1) * half]
            st_im[kb] = x0i_ref[:, kb * half:(kb + 1) * half]

    for b in range(nb):
        ubuf[b * pitch + tt:(b + 1) * pitch, :] = jnp.zeros((S5_ROW_PAD, LANES), F32)

    for kb in range(n_blk):
        lanes = slice(kb * LANES, (kb + 1) * LANES)
        for b in range(nb):
            ubuf[b * pitch:b * pitch + tt, :] = u_ref[b, :, lanes]
        u_blk = ubuf[...]
        bu = _dot(u_blk.astype(BF16), bm_ref[kb])
        for j in range(2 * n_chunk):
            xbuf[j] = bu[:, j * LANES:(j + 1) * LANES]
        a_re = jnp.broadcast_to(ar_ref[kb], (nb, half))
        a_im = jnp.broadcast_to(ai_ref[kb], (nb, half))

        def step(t, carry):
            s_re, s_im = carry
            rows = pl.ds(t, nb, stride=pitch)
            b_re = jnp.concatenate([xbuf[j, rows, :] for j in range(n_chunk)], axis=1)
            b_im = jnp.concatenate([xbuf[n_chunk + j, rows, :] for j in range(n_chunk)], axis=1)
            n_re = a_re * s_re - a_im * s_im + b_re
            n_im = a_re * s_im + a_im * s_re + b_im
            for j in range(n_chunk):
                xbuf[j, rows, :] = n_re[:, j * LANES:(j + 1) * LANES]
                xbuf[n_chunk + j, rows, :] = n_im[:, j * LANES:(j + 1) * LANES]
            return n_re, n_im

        s_re, s_im = lax.fori_loop(0, tt, step, (st_re[kb], st_im[kb]), unroll=S5_SCAN_UNROLL)
        st_re[kb] = s_re
        st_im[kb] = s_im
        states = jnp.concatenate([xbuf[j] for j in range(2 * n_chunk)], axis=1)
        ubuf[...] = _dot(states.astype(BF16), cm_ref[kb]) + d_ref[:, lanes] * u_blk
        for b in range(nb):
            y_ref[b, :, lanes] = ubuf[b * pitch:b * pitch + tt, :]

    @pl.when(t_idx == pl.num_programs(1) - 1)
    def _():
        for kb in range(n_blk):
            fr_ref[:, kb * half:(kb + 1) * half] = st_re[kb]
            fi_ref[:, kb * half:(kb + 1) * half] = st_im[kb]


def _s5(u3, x0_re, x0_im, a_re, a_im, bmat, cmat, d_row, tt):
    batch, seq, e = u3.shape
    nb = SUBLANES
    n_blk = e // LANES
    n_state = n_blk * S5_BLOCK_STATE
    seq_map = lambda b, t: (b, t, 0)
    st_map = lambda b, t: (b, 0)
    fixed2 = lambda b, t: (0, 0)
    fixed3 = lambda b, t: (0, 0, 0)
    return pl.pallas_call(
        functools.partial(_s5_kernel, nb=nb, tt=tt),
        grid=(batch // nb, seq // tt),
        in_specs=[
            pl.BlockSpec((nb, tt, e), seq_map),
            pl.BlockSpec((nb, n_state), st_map),
            pl.BlockSpec((nb, n_state), st_map),
            pl.BlockSpec((n_blk, 1, S5_BLOCK_STATE), fixed3),
            pl.BlockSpec((n_blk, 1, S5_BLOCK_STATE), fixed3),
            pl.BlockSpec((n_blk, LANES, 2 * S5_BLOCK_STATE), fixed3),
            pl.BlockSpec((n_blk, 2 * S5_BLOCK_STATE, LANES), fixed3),
            pl.BlockSpec((1, e), fixed2),
        ],
        out_specs=[
            pl.BlockSpec((nb, tt, e), seq_map),
            pl.BlockSpec((nb, n_state), st_map),
            pl.BlockSpec((nb, n_state), st_map),
        ],
        out_shape=[
            jax.ShapeDtypeStruct((batch, seq, e), F32),
            jax.ShapeDtypeStruct((batch, n_state), F32),
            jax.ShapeDtypeStruct((batch, n_state), F32),
        ],
        scratch_shapes=[
            pltpu.VMEM((n_blk, nb, S5_BLOCK_STATE), F32),
            pltpu.VMEM((n_blk, nb, S5_BLOCK_STATE), F32),
            pltpu.VMEM((nb * (tt + S5_ROW_PAD), LANES), F32),
            pltpu.VMEM((2 * S5_BLOCK_STATE // LANES, nb * (tt + S5_ROW_PAD), LANES), F32),
        ],
        compiler_params=_cparams("parallel", "arbitrary"),
        name="s5_scan",
    )(u3, x0_re, x0_im, a_re, a_im, bmat, cmat, d_row)


def _s5_params(lam_re, lam_im, b_re, b_im, c_re, c_im, log_step):
    n_groups = lam_re.shape[0]
    n_blk = n_groups // S5_LANE_GROUPS
    dt = jnp.exp(log_step)[:, None]
    mag = jnp.exp(lam_re * dt)
    ang = lam_im * dt
    ab_re, ab_im = mag * jnp.cos(ang), mag * jnp.sin(ang)
    den = lam_re * lam_re + lam_im * lam_im
    nr = ab_re - 1.0
    f_re = (nr * lam_re + ab_im * lam_im) / den
    f_im = (ab_im * lam_re - nr * lam_im) / den
    bb_re = f_re[..., None] * b_re - f_im[..., None] * b_im
    bb_im = f_re[..., None] * b_im + f_im[..., None] * b_re
    eye = jnp.eye(S5_LANE_GROUPS, dtype=lam_re.dtype)

    def in_map(bb):
        bb = bb.reshape(n_blk, S5_LANE_GROUPS, S5_STATE, S5_GROUP)
        m = jnp.einsum('kgpc,gh->kgchp', bb, eye)
        return m.reshape(n_blk, LANES, S5_BLOCK_STATE)

    def out_map(cc):
        cc = cc.reshape(n_blk, S5_LANE_GROUPS, S5_GROUP, S5_STATE)
        m = jnp.einsum('kgcp,gh->kgphc', cc, eye)
        return m.reshape(n_blk, S5_BLOCK_STATE, LANES)

    bmat = jnp.concatenate([in_map(bb_re), in_map(bb_im)], axis=2).astype(BF16)
    cmat = jnp.concatenate([out_map(c_re), out_map(-c_im)], axis=1).astype(BF16)
    a_re = ab_re.reshape(n_blk, 1, S5_BLOCK_STATE)
    a_im = ab_im.reshape(n_blk, 1, S5_BLOCK_STATE)
    return a_re, a_im, bmat, cmat


def _odd_out_kernel(x_ref, y_ref, z_ref, w1_ref, w2_ref, wo_ref, o_ref):
    yb = jax.nn.gelu(y_ref[...]).astype(BF16)
    t = _dot(yb, w1_ref[...]) * jax.nn.sigmoid(_dot(yb, w2_ref[...])) * z_ref[...]
    o_ref[...] = x_ref[...] + _dot(t.astype(BF16), wo_ref[...])


def _odd_out(x2d, y2d, z2d, w1, w2, wo):
    m, d = x2d.shape
    e = y2d.shape[1]
    row = lambda i: (i, 0)
    fixed = lambda i: (0, 0)
    return pl.pallas_call(
        _odd_out_kernel,
        grid=(m // ROW_TILE,),
        in_specs=[
            pl.BlockSpec((ROW_TILE, d), row),
            pl.BlockSpec((ROW_TILE, e), row),
            pl.BlockSpec((ROW_TILE, e), row),
            pl.BlockSpec((e, e), fixed),
            pl.BlockSpec((e, e), fixed),
            pl.BlockSpec((e, d), fixed),
        ],
        out_specs=pl.BlockSpec((ROW_TILE, d), row),
        out_shape=jax.ShapeDtypeStruct((m, d), F32),
        compiler_params=_cparams("parallel"),
        name="odd_out_proj",
    )(x2d, y2d, z2d, w1, w2, wo)


def _rope_tables(pos):
    half = NSA_DH // 2
    inv = ROPE_THETA ** (-jnp.arange(half, dtype=F32) / half)
    ang = pos.astype(F32)[:, None] * inv[None, :]
    cos, sin = jnp.cos(ang), jnp.sin(ang)
    reps = LANES // NSA_DH
    return jnp.tile(cos, (1, 2 * reps)), jnp.tile(jnp.concatenate([-sin, sin], axis=1), (1, reps))


def _even_weights(norm_g, w_in, w_out, q_norm, k_norm, cmp_pos, cmp_w):
    d = w_in.shape[0]
    sizes = (RET_HEADS * RET_DK, RET_HEADS * RET_DK, A_WIDTH, A_WIDTH, B_WIDTH, 6 * NSA_KV_HEADS * NSA_DH,
             N_GATES, B_WIDTH)
    parts, o = [], 0
    for s in sizes:
        parts.append(w_in[:, o:o + s])
        o += s
    qa, ka, va, za, qn, kvb, gl, zb = parts
    w_perm = jnp.concatenate([qa, ka, va, za, qn, kvb, zb, gl, jnp.zeros((d, LANES - N_GATES), w_in.dtype)],
                             axis=1).astype(BF16)
    reps = LANES // NSA_DH
    qg = jnp.tile(q_norm[None, :], (1, reps))
    kg = jnp.tile(k_norm, (1, reps))
    eye = jnp.eye(NSA_KV_HEADS, dtype=cmp_w.dtype)
    w_bd = jnp.einsum('clde,gh,ck->lcgdkhe', cmp_w, eye, jnp.eye(2, dtype=cmp_w.dtype))
    w_bd = w_bd.reshape(L_CMP, 2 * LANES, 2 * LANES).astype(BF16)
    pe = jnp.broadcast_to(cmp_pos.transpose(1, 0, 2)[:, :, None, :], (L_CMP, 2, NSA_KV_HEADS, NSA_DH))
    pe = pe.reshape(L_CMP, 2 * LANES)
    wa = w_out[:A_WIDTH].astype(BF16)
    wb = w_out[A_WIDTH:].astype(BF16)
    return norm_g[None, :], w_perm, qg, kg, w_bd, pe, wa, wb


def _even_layer(x, pos0, s_ret, ew, gn_gain, cache=None, page_table=None, win_past=None):
    batch, seq, d = x.shape
    gain, w_perm, qg, kg, w_bd, pe, wa, wb = ew
    x2d = x.reshape(batch * seq, d)
    pos = pos0 + jnp.arange(seq, dtype=jnp.int32)
    cos_t, sin_t = _rope_tables(pos)
    if seq >= ROW_TILE:
        n_table_blocks = seq // ROW_TILE
    else:
        cos_t = jnp.tile(cos_t, (ROW_TILE // seq, 1))
        sin_t = jnp.tile(sin_t, (ROW_TILE // seq, 1))
        n_table_blocks = 1
    g, dh = NSA_KV_HEADS, NSA_DH
    if cache is None:
        qa, ka, va, za, qn, full_t, win_t, zb, gates, cmp_rows = _even_in(
            x2d, gain, w_perm, cos_t, sin_t, qg, kg, n_table_blocks, seq_tiles=seq // ROW_TILE)
        oa, s_fin = _retention(qa, ka, va, za, gn_gain[None, :], s_ret, batch, seq)
        cmp3 = cmp_rows.reshape(batch, seq, 2 * LANES)
        n_half = seq // CMP_STRIDE
        src_specs = [pl.BlockSpec((1, seq, LANES), lambda b: (b, 0, 0)),
                     pl.BlockSpec((1, seq, LANES), lambda b: (b, 0, 1))]
        cmp_kv = _compress([cmp3, cmp3], src_specs, (batch,), (), pe, w_bd, batch, seq, n_half)
        overlap, expand = _selection_tables(n_half, seq)
        ob = _nsa_prompt(qn, gates, zb, cmp_kv, full_t, win_t, overlap, expand, batch, seq)
        keep = min(WINDOW, seq)
        y = _even_out(x2d, oa, ob, wa, wb).reshape(batch, seq, d)
        full_rows = full_t.reshape(batch, 4, g, dh, seq).transpose(0, 4, 1, 2, 3)
        win_rows = win_t[:, :, seq - keep:].reshape(batch, 2, g, dh, keep).transpose(0, 4, 1, 2, 3)
        return y, s_fin, full_rows, win_rows
    else:
        qa, ka, va, za, qn, full_new, win_new, zb, gates = _even_in(x2d, gain, w_perm, cos_t, sin_t, qg, kg,
                                                                     n_table_blocks)
        oa, s_fin = _retention(qa, ka, va, za, gn_gain[None, :], s_ret, batch, seq)
        n_pages = page_table.shape[1]
        past_len = n_pages * PAGE_SIZE
        overlap, expand = _selection_tables(past_len // CMP_STRIDE, past_len + KEY_TILE)
        ob, win_out_t = _nsa_sample(page_table, qn, gates, zb, full_new, win_new, win_past, cache, pe, w_bd,
                                    overlap, expand, batch, seq, n_pages)
    y = _even_out(x2d, oa, ob, wa, wb).reshape(batch, seq, d)
    w_buf = win_out_t.shape[3]
    return (y, s_fin, full_new.reshape(batch, seq, 4, g, dh),
            win_out_t.reshape(batch, 2, g, dh, w_buf).transpose(0, 4, 1, 2, 3))


def _odd_layer(x, s_re, s_im, gain, w_in_bf, s5p, d_row, w1, w2, wo, tt):
    batch, seq, d = x.shape
    x2d = x.reshape(batch * seq, d)
    u, zs = _odd_in(x2d, gain, w_in_bf)
    e = u.shape[1]
    a_re, a_im, bmat, cmat = s5p
    n_groups, n_state = s_re.shape[1], s_re.shape[2]
    y, f_re, f_im = _s5(u.reshape(batch, seq, e), s_re.reshape(batch, n_groups * n_state),
                        s_im.reshape(batch, n_groups * n_state), a_re, a_im, bmat, cmat, d_row, tt)
    out = _odd_out(x2d, y.reshape(batch * seq, e), zs, w1, w2, wo).reshape(batch, seq, d)
    return out, f_re.reshape(batch, n_groups, n_state), f_im.reshape(batch, n_groups, n_state)


S5_TIME_TILE = 128


def kernel(x_prompt, x_sample, cache_nsa_kv, cache_nsa_win, state_ret, state_ssm_re, state_ssm_im, page_table,
           norm_even, w_in_even, w_out_even, ret_gn_gain, nsa_q_norm, nsa_k_norm, nsa_cmp_pos, nsa_cmp_w,
           norm_odd, w_in_odd, ssm_lambda_re, ssm_lambda_im, ssm_b_re, ssm_b_im, ssm_c_re, ssm_c_im, ssm_d,
           ssm_log_step, glu_w1, glu_w2, w_out_odd):
    bp, seq_p, _ = x_prompt.shape
    db, seq_s, _ = x_sample.shape
    n_pages = page_table.shape[1]
    past_len = n_pages * PAGE_SIZE
    depth = norm_even.shape[0] + norm_odd.shape[0]
    yp, ys = x_prompt, x_sample
    ret_p, ret_s, kv_p, kv_s, win_p, win_s = [], [], [], [], [], []
    sre_p, sim_p, sre_s, sim_s = [], [], [], []
    for layer in range(depth):
        li = layer // 2
        if layer % 2 == 0:
            ew = _even_weights(norm_even[li], w_in_even[li], w_out_even[li], nsa_q_norm[li], nsa_k_norm[li],
                               nsa_cmp_pos[li], nsa_cmp_w[li])
            s0 = jnp.zeros((bp, RET_HEADS, RET_DK, RET_DV), F32)
            yp, sr, kvr, wr = _even_layer(yp, 0, s0, ew, ret_gn_gain[li])
            ret_p.append(sr); kv_p.append(kvr); win_p.append(wr)
            cache_t = cache_nsa_kv[li].transpose(0, 2, 3, 4, 1).reshape(cache_nsa_kv.shape[1], 4, LANES, PAGE_SIZE)
            win_t = cache_nsa_win[li].transpose(0, 2, 3, 4, 1).reshape(db, 2, LANES, cache_nsa_win.shape[2])
            ys, sr2, kvr2, wr2 = _even_layer(ys, past_len, state_ret[li], ew, ret_gn_gain[li],
                                             cache=cache_t, page_table=page_table, win_past=win_t)
            ret_s.append(sr2); kv_s.append(kvr2); win_s.append(wr2)
        else:
            s5p = _s5_params(ssm_lambda_re[li], ssm_lambda_im[li], ssm_b_re[li], ssm_b_im[li],
                             ssm_c_re[li], ssm_c_im[li], ssm_log_step[li])
            gain = norm_odd[li][None, :]
            w_in_bf = w_in_odd[li].astype(BF16)
            w1, w2, wo = glu_w1[li].astype(BF16), glu_w2[li].astype(BF16), w_out_odd[li].astype(BF16)
            d_row = ssm_d[li][None, :]
            n_groups = ssm_lambda_re.shape[1]
            z0 = jnp.zeros((bp, n_groups, S5_STATE), F32)
            yp, fr, fi = _odd_layer(yp, z0, z0, gain, w_in_bf, s5p, d_row, w1, w2, wo, min(S5_TIME_TILE, seq_p))
            sre_p.append(fr); sim_p.append(fi)
            ys, fr2, fi2 = _odd_layer(ys, state_ssm_re[li], state_ssm_im[li], gain, w_in_bf, s5p, d_row,
                                      w1, w2, wo, min(S5_TIME_TILE, seq_s))
            sre_s.append(fr2); sim_s.append(fi2)
    return (yp, ys, jnp.stack(ret_p), jnp.stack(ret_s), jnp.stack(kv_p), jnp.stack(kv_s), jnp.stack(win_p),
            jnp.stack(win_s), jnp.stack(sre_p), jnp.stack(sim_p), jnp.stack(sre_s), jnp.stack(sim_s))
```

```python
import functools
import math

import jax
import jax.numpy as jnp
from jax import lax
from jax.experimental import pallas as pl
from jax.experimental.pallas import tpu as pltpu

F32 = jnp.float32
BF16 = jnp.bfloat16

LANES = 128
SUBLANES = 8
VMEM_LIMIT_BYTES = 48 * 2**20

EPS = 1e-6
ROPE_THETA = 10000.0
NEG_INF = -1e30
FORCE = 1e4

RET_HEADS = 4
RET_DK = 64
RET_DV = 128
RET_CHUNK = 128
RET_TILES_PER_STEP = 4
A_WIDTH = RET_HEADS * RET_DV

NSA_HEADS = 8
NSA_KV_HEADS = 2
NSA_DH = 64
NSA_HPG = NSA_HEADS // NSA_KV_HEADS
B_WIDTH = NSA_HEADS * NSA_DH
L_CMP = 32
CMP_STRIDE = 16
CMP_RATIO = L_CMP // CMP_STRIDE
L_SEL = 64
N_SEL = 8
WINDOW = 512
PAGE_SIZE = 128
KV_ROW = 4 * NSA_KV_HEADS * NSA_DH
WIN_ROW = 2 * NSA_KV_HEADS * NSA_DH
KEY_TILE = 128
WIN_KEYS = WINDOW + KEY_TILE
SEL_CHUNK = 512
CMP_SLOT = CMP_STRIDE + 4
SAMPLE_REQS_PER_STEP = 2

S5_GROUP = 16
S5_STATE = 64
S5_LANE_GROUPS = LANES // S5_GROUP
S5_BLOCK_STATE = S5_LANE_GROUPS * S5_STATE
S5_ROW_PAD = 4
S5_SCAN_UNROLL = 4

QA0, KA0, VA0, ZA0, QN0, KVB0, ZB0, GL0 = 0, 256, 512, 1024, 1536, 2048, 2816, 3328
EVEN_COLS = GL0 + LANES
N_GATES = 3 * NSA_HEADS

ROW_TILE = 512
EVEN_IN_TILE = 256


def _cparams(*sem):
    return pltpu.CompilerParams(dimension_semantics=sem, vmem_limit_bytes=VMEM_LIMIT_BYTES)


def _lane_iota(shape):
    return lax.broadcasted_iota(jnp.int32, shape, len(shape) - 1)


def _row_iota(shape):
    return lax.broadcasted_iota(jnp.int32, shape, len(shape) - 2)


def _dot(a, b):
    return jnp.dot(a, b, preferred_element_type=F32)


def _dot_nt(a, b):
    return lax.dot_general(a, b, (((1,), (1,)), ((), ())), preferred_element_type=F32)


def _rmsnorm_rows(x, g):
    return x * lax.rsqrt(jnp.mean(x * x, axis=-1, keepdims=True) + EPS) * g


def _swap_halves(x):
    return pltpu.roll(x, NSA_DH, axis=1)


def _rope_block(x, cos, sin_signed):
    half = NSA_DH // 2
    lane = _lane_iota(x.shape)
    first = (lane % NSA_DH) < half
    partner = jnp.where(first, pltpu.roll(x, LANES - half, axis=1), pltpu.roll(x, half, axis=1))
    return x * cos + partner * sin_signed


def _head_rms_block(x, g):
    lane = _lane_iota(x.shape)
    lo = lane < NSA_DH
    sq = x * x
    s_lo = jnp.sum(jnp.where(lo, sq, 0.0), axis=-1, keepdims=True)
    s_hi = jnp.sum(jnp.where(lo, 0.0, sq), axis=-1, keepdims=True)
    ms = jnp.where(lo, s_lo, s_hi) * (1.0 / NSA_DH)
    return x * lax.rsqrt(ms + EPS) * g


def _even_in_kernel(x_ref, g_ref, w_ref, cos_ref, sin_ref, qg_ref, kg_ref, *out_refs, transposed_kv):
    if transposed_kv:
        qa_ref, ka_ref, va_ref, za_ref, qn_ref, full_ref, win_ref, zb_ref, gl_ref, cmp_ref = out_refs
    else:
        qa_ref, ka_ref, va_ref, za_ref, qn_ref, full_ref, win_ref, zb_ref, gl_ref = out_refs
    hb = _rmsnorm_rows(x_ref[...], g_ref[...]).astype(BF16)
    cos = cos_ref[...]
    sin = sin_ref[...]

    def proj(c0):
        return _dot(hb, w_ref[:, c0:c0 + LANES])

    for j in range(RET_HEADS * RET_DK // LANES):
        c = j * LANES
        qa_ref[:, c:c + LANES] = _rope_block(proj(QA0 + c), cos, sin)
        ka_ref[:, c:c + LANES] = _rope_block(proj(KA0 + c), cos, sin) * (RET_DK ** -0.5)
    for j in range(A_WIDTH // LANES):
        c = j * LANES
        va_ref[:, c:c + LANES] = proj(VA0 + c)
        za_ref[:, c:c + LANES] = jax.nn.silu(proj(ZA0 + c))
    for j in range(B_WIDTH // LANES):
        c = j * LANES
        qn_ref[:, c:c + LANES] = _rope_block(_head_rms_block(proj(QN0 + c), qg_ref[...]), cos, sin)
        zb_ref[:, c:c + LANES] = jax.nn.silu(proj(ZB0 + c))
    for j in range(6):
        y = proj(KVB0 + j * LANES)
        if j % 2 == 0:
            y = _rope_block(_head_rms_block(y, kg_ref[j // 2:j // 2 + 1, :]), cos, sin)
        if transposed_kv:
            if j < 4:
                full_ref[0, j * LANES:(j + 1) * LANES, :] = y.T
            else:
                win_ref[0, (j - 4) * LANES:(j - 3) * LANES, :] = y.T
            if j < 2:
                cmp_ref[:, j * LANES:(j + 1) * LANES] = y
        elif j < 4:
            full_ref[:, j * LANES:(j + 1) * LANES] = y
        else:
            win_ref[:, (j - 4) * LANES:(j - 3) * LANES] = y
    gl_ref[...] = jax.nn.sigmoid(proj(GL0))


def _even_in(x2d, gain, w_bf, cos_t, sin_t, qg, kg, n_table_blocks, seq_tiles=None):
    m, d = x2d.shape
    tile = EVEN_IN_TILE
    grid = (m // tile,)
    row = lambda i: (i, 0)
    fixed = lambda i: (0, 0)
    table = lambda i: (i % n_table_blocks, 0)
    widths = (RET_HEADS * RET_DK, RET_HEADS * RET_DK, A_WIDTH, A_WIDTH, B_WIDTH, KV_ROW, WIN_ROW, B_WIDTH, LANES)
    out_specs = [pl.BlockSpec((tile, w), row) for w in widths]
    out_shape = [jax.ShapeDtypeStruct((m, w), F32) for w in widths]
    if seq_tiles is not None:
        batch, seq = grid[0] // seq_tiles, seq_tiles * tile
        fmajor = lambda i: (i // seq_tiles, 0, i % seq_tiles)
        for idx, w in ((5, KV_ROW), (6, WIN_ROW)):
            out_specs[idx] = pl.BlockSpec((1, w, tile), fmajor)
            out_shape[idx] = jax.ShapeDtypeStruct((batch, w, seq), F32)
        out_specs.append(pl.BlockSpec((tile, 2 * LANES), row))
        out_shape.append(jax.ShapeDtypeStruct((m, 2 * LANES), F32))
    return pl.pallas_call(
        functools.partial(_even_in_kernel, transposed_kv=seq_tiles is not None),
        grid=grid,
        in_specs=[
            pl.BlockSpec((tile, d), row),
            pl.BlockSpec((1, d), fixed),
            pl.BlockSpec((d, EVEN_COLS), fixed),
            pl.BlockSpec((tile, LANES), table),
            pl.BlockSpec((tile, LANES), table),
            pl.BlockSpec((1, LANES), fixed),
            pl.BlockSpec((3, LANES), fixed),
        ],
        out_specs=out_specs,
        out_shape=out_shape,
        compiler_params=_cparams("parallel"),
        name="even_in_proj",
    )(x2d, gain, w_bf, cos_t, sin_t, qg, kg)


def _retention_kernel(q_ref, k_ref, v_ref, z_ref, gn_ref, s0_ref, dmat_ref, qdec_ref, kdec_ref, cdec_ref,
                      o_ref, sfin_ref, s_scr, *, rows, n_tiles):
    n_seq = RET_CHUNK // rows
    c = pl.program_id(1)

    @pl.when(c == 0)
    def _():
        s_scr[...] = s0_ref[...]

    lane = _lane_iota((RET_CHUNK, LANES))
    col_seq = _lane_iota((RET_DK, RET_CHUNK)) >> int(math.log2(rows)) if n_seq > 1 else None
    state = [[s_scr[r, h] for h in range(RET_HEADS)] for r in range(n_seq)]
    for t in range(n_tiles):
        tile = slice(t * RET_CHUNK, (t + 1) * RET_CHUNK)
        q, k, v = q_ref[tile, :], k_ref[tile, :], v_ref[tile, :]
        qd = q * qdec_ref[...]
        kd = k * kdec_ref[...]
        for pair in range(RET_HEADS // 2):
            cols = slice(pair * LANES, (pair + 1) * LANES)
            q2, k2b, qd2 = q[:, cols], k[:, cols].astype(BF16), qd[:, cols]
            kd2_t = kd[:, cols].T
            s_pair = [jnp.concatenate([state[r][2 * pair], state[r][2 * pair + 1]], axis=0).astype(BF16)
                      for r in range(n_seq)]
            for sub in range(2):
                h = 2 * pair + sub
                mine = (lane >= sub * RET_DK) & (lane < (sub + 1) * RET_DK)
                qm = jnp.where(mine, q2, 0.0).astype(BF16)
                qdm = jnp.where(mine, qd2, 0.0).astype(BF16)
                vhb = v[:, h * RET_DV:(h + 1) * RET_DV].astype(BF16)
                intra = _dot((_dot_nt(qm, k2b) * dmat_ref[h]).astype(BF16), vhb)
                cross = jnp.concatenate([_dot(qdm[r * rows:(r + 1) * rows], s_pair[r]) for r in range(n_seq)],
                                        axis=0)
                out = intra + cross
                kt = kd2_t[sub * RET_DK:(sub + 1) * RET_DK, :]
                if n_seq > 1:
                    kt = jnp.concatenate([jnp.where(col_seq == r, kt, 0.0) for r in range(n_seq)], axis=0)
                kv = _dot(kt.astype(BF16), vhb)
                for r in range(n_seq):
                    state[r][h] = state[r][h] * cdec_ref[h] + kv[r * RET_DK:(r + 1) * RET_DK]
                mu = jnp.mean(out, axis=-1, keepdims=True)
                cen = out - mu
                var = jnp.mean(cen * cen, axis=-1, keepdims=True)
                y = cen * lax.rsqrt(var + EPS) * gn_ref[:, h * RET_DV:(h + 1) * RET_DV]
                o_ref[tile, h * RET_DV:(h + 1) * RET_DV] = y * z_ref[tile, h * RET_DV:(h + 1) * RET_DV]
    for r in range(n_seq):
        for h in range(RET_HEADS):
            s_scr[r, h] = state[r][h]

    @pl.when(c == pl.num_programs(1) - 1)
    def _():
        sfin_ref[...] = s_scr[...]


def _retention_tables(rows):
    log_g = jnp.log(1.0 - 2.0 ** (-5.0 - jnp.arange(RET_HEADS, dtype=F32)))
    idx = jnp.arange(RET_CHUNK, dtype=jnp.int32)
    pos = (idx % rows).astype(F32)
    seq = idx // rows
    diff = pos[:, None] - pos[None, :]
    causal = (diff >= 0) & (seq[:, None] == seq[None, :])
    dmat = jnp.exp(jnp.where(causal, diff, 0.0)[None] * log_g[:, None, None]) * causal[None]
    qdec = jnp.exp((pos + 1.0)[:, None] * log_g[None, :])
    kdec = jnp.exp((rows - 1.0 - pos)[:, None] * log_g[None, :])
    qdec = jnp.repeat(qdec, RET_DK, axis=1)
    kdec = jnp.repeat(kdec, RET_DK, axis=1)
    cdec = jnp.broadcast_to(jnp.exp(rows * log_g)[:, None, None], (RET_HEADS, RET_DK, RET_DV))
    return dmat, qdec, kdec, cdec


def _retention(qa, ka, va, za, gn_gain, s0, batch, seq):
    rows = min(RET_CHUNK, seq)
    n_seq = RET_CHUNK // rows
    n_chunks = seq // rows
    n_tiles = min(RET_TILES_PER_STEP, n_chunks)
    steps = n_chunks // n_tiles
    dmat, qdec, kdec, cdec = _retention_tables(rows)
    tok = lambda b, c: (b * steps + c, 0)
    fixed2 = lambda b, c: (0, 0)
    fixed3 = lambda b, c: (0, 0, 0)
    state = lambda b, c: (b, 0, 0, 0)
    qk_w = RET_HEADS * RET_DK
    step_rows = n_tiles * RET_CHUNK
    return pl.pallas_call(
        functools.partial(_retention_kernel, rows=rows, n_tiles=n_tiles),
        grid=(batch // n_seq, steps),
        in_specs=[
            pl.BlockSpec((step_rows, qk_w), tok),
            pl.BlockSpec((step_rows, qk_w), tok),
            pl.BlockSpec((step_rows, A_WIDTH), tok),
            pl.BlockSpec((step_rows, A_WIDTH), tok),
            pl.BlockSpec((1, A_WIDTH), fixed2),
            pl.BlockSpec((n_seq, RET_HEADS, RET_DK, RET_DV), state),
            pl.BlockSpec((RET_HEADS, RET_CHUNK, RET_CHUNK), fixed3),
            pl.BlockSpec((RET_CHUNK, qk_w), fixed2),
            pl.BlockSpec((RET_CHUNK, qk_w), fixed2),
            pl.BlockSpec((RET_HEADS, RET_DK, RET_DV), fixed3),
        ],
        out_specs=[
            pl.BlockSpec((step_rows, A_WIDTH), tok),
            pl.BlockSpec((n_seq, RET_HEADS, RET_DK, RET_DV), state),
        ],
        out_shape=[
            jax.ShapeDtypeStruct((batch * seq, A_WIDTH), F32),
            jax.ShapeDtypeStruct((batch, RET_HEADS, RET_DK, RET_DV), F32),
        ],
        scratch_shapes=[pltpu.VMEM((n_seq, RET_HEADS, RET_DK, RET_DV), F32)],
        compiler_params=_cparams("parallel", "arbitrary"),
        name="retention",
    )(qa, ka, va, za, gn_gain, s0, dmat, qdec, kdec, cdec)


def _compress_kernel(*refs, n_pref, n_src, src_rows):
    refs = refs[n_pref:]
    src_k = refs[:n_src]
    src_v = refs[n_src:2 * n_src]
    pe_ref, w_ref, o_ref = refs[2 * n_src:]
    per_src = src_rows // CMP_STRIDE
    n_half = n_src * per_src
    acc = [jnp.zeros((n_half, 2 * LANES), F32) for _ in range(CMP_RATIO)]

    def strided_rows(srcs, l):
        rows = [s[0, pl.ds(l, per_src, stride=CMP_STRIDE), :] for s in srcs]
        return rows[0] if n_src == 1 else jnp.concatenate(rows, axis=0)

    for l in range(CMP_STRIDE):
        x = jnp.concatenate([strided_rows(src_k, l), strided_rows(src_v, l)], axis=1)
        for r in range(CMP_RATIO):
            i = r * CMP_STRIDE + l
            acc[r] = acc[r] + _dot((x + pe_ref[i:i + 1, :]).astype(BF16), w_ref[i])
    out = acc[0]
    for r in range(1, CMP_RATIO):
        out = out + pltpu.roll(acc[r], n_half - r, axis=0)
    live = _row_iota(out.shape) < n_half - CMP_RATIO + 1
    o_ref[0] = jnp.where(live, out, 0.0)


def _compress(src_arrays, src_specs, grid, extra_prefetch, pe, w_bd, batch, src_rows, n_half):
    n_src = len(src_specs) // 2
    n_pref = len(extra_prefetch)
    kern = functools.partial(_compress_kernel, n_pref=n_pref, n_src=n_src, src_rows=src_rows)
    fixed2 = lambda *a: (0, 0)
    fixed3 = lambda *a: (0, 0, 0)
    out_map = lambda b, *a: (b, 0, 0)
    return pl.pallas_call(
        kern,
        grid_spec=pltpu.PrefetchScalarGridSpec(
            num_scalar_prefetch=n_pref,
            grid=grid,
            in_specs=list(src_specs) + [
                pl.BlockSpec((L_CMP, 2 * LANES), fixed2),
                pl.BlockSpec((L_CMP, 2 * LANES, 2 * LANES), fixed3),
            ],
            out_specs=pl.BlockSpec((1, n_half, 2 * LANES), out_map),
        ),
        out_shape=jax.ShapeDtypeStruct((batch, n_half, 2 * LANES), F32),
        compiler_params=_cparams("parallel"),
        name="kv_compress",
    )(*extra_prefetch, *src_arrays, pe, w_bd)


def _split3_bf16(x):
    hi = x.astype(BF16)
    r1 = x - hi.astype(F32)
    mid = r1.astype(BF16)
    lo = (r1 - mid.astype(F32)).astype(BF16)
    return hi, mid, lo


def _stack_group_queries(q, g):
    lane = _lane_iota((q.shape[0], LANES))
    in_g = (lane >= g * NSA_DH) & (lane < (g + 1) * NSA_DH)
    parts = []
    for hh in range(NSA_HPG):
        h = g * NSA_HPG + hh
        two = q[:, (h // 2) * LANES:(h // 2 + 1) * LANES]
        if h % 2 != g:
            two = _swap_halves(two)
        parts.append(jnp.where(in_g, two, 0.0))
    return (jnp.concatenate(parts, axis=0) * (NSA_DH ** -0.5)).astype(BF16)


def _masked_softmax(s, ok):
    s = jnp.where(ok, s, NEG_INF)
    e = jnp.exp(s - jnp.max(s, axis=-1, keepdims=True))
    return jnp.where(ok, e * (1.0 / jnp.sum(e, axis=-1, keepdims=True)), 0.0)


def _sum_heads(p1, tq):
    psum = p1[0:tq]
    for hh in range(1, NSA_HPG):
        psum = psum + p1[hh * tq:(hh + 1) * tq]
    return psum


def _choose_blocks(psum, t0, tq, overlap, n_blocks):
    rows = psum.shape[0]
    imp = sum(_dot(t, overlap) for t in _split3_bf16(psum))
    shift = int(math.log2(L_SEL))

    def ranked(score, blk, valid, take):
        rank = jnp.zeros(score.shape, F32)
        for s in range(n_blocks):
            cand = take(s)
            rank = rank + ((cand > score) | ((cand == score) & (blk > s))).astype(F32)
        return ((rank < N_SEL) & valid).astype(F32)

    if rows % LANES:
        tpos = jnp.concatenate([t0 + _row_iota((tq, 1))] * (rows // tq), axis=0)
        blk = _lane_iota((rows, LANES))
        cur = tpos >> shift
        forced = (blk == 0) | (blk == cur) | (blk == cur - 1)
        valid = (blk * L_SEL <= tpos) & (blk < n_blocks)
        score = jnp.where(valid, jnp.where(forced, FORCE, imp), -FORCE)
        return ranked(score, blk, valid, lambda s: score[:, s:s + 1]).astype(BF16)

    nbp = -(-n_blocks // SUBLANES) * SUBLANES
    tiles = rows // LANES
    imp_t = jnp.concatenate([imp[i * LANES:(i + 1) * LANES].T[:nbp] for i in range(tiles)], axis=1)
    tpos = jnp.concatenate([t0 + _lane_iota((1, tq))] * (rows // tq), axis=1)
    blk = _row_iota((nbp, rows))
    cur = tpos >> shift
    forced = (blk == 0) | (blk == cur) | (blk == cur - 1)
    valid = (blk * L_SEL <= tpos) & (blk < n_blocks)
    score = jnp.where(valid, jnp.where(forced, FORCE, imp_t), -FORCE)
    chosen_t = ranked(score, blk, valid, lambda s: score[s:s + 1, :])
    chosen_t = jnp.concatenate([chosen_t, jnp.zeros((LANES - nbp, rows), F32)], axis=0)
    return jnp.concatenate([chosen_t[:, i * LANES:(i + 1) * LANES].T for i in range(tiles)],
                           axis=0).astype(BF16)


def _gate_and_store(gates, zs, branch_outs, o_ref, row0=0):
    tq = gates.shape[0]
    lane = _lane_iota((tq, LANES))
    ext = []
    for g in range(NSA_KV_HEADS):
        o_cmp, o_sel, o_win = branch_outs[g]
        for hh in range(NSA_HPG):
            h = g * NSA_HPG + hh
            rows = slice(hh * tq, (hh + 1) * tq)
            o = (gates[:, 3 * h:3 * h + 1] * o_cmp[rows] + gates[:, 3 * h + 1:3 * h + 2] * o_sel[rows]
                 + gates[:, 3 * h + 2:3 * h + 3] * o_win[rows])
            ext.append(_swap_halves(o) if h % 2 != g else o)
    for j in range(NSA_HEADS // 2):
        both = jnp.where(lane < NSA_DH, ext[2 * j], ext[2 * j + 1])
        o_ref[row0:row0 + tq, j * LANES:(j + 1) * LANES] = both * zs[:, j * LANES:(j + 1) * LANES]


def _nsa_prompt_kernel(q_ref, g_ref, z_ref, cmp_ref, sel_ref, win_ref, ov_ref, ex_ref, o_ref, *, n_blocks):
    i = pl.program_id(1)
    tq = q_ref.shape[0]
    q = q_ref[...]
    t0 = i * tq
    tpos = t0 + _row_iota((tq, 1))
    tpos_all = jnp.concatenate([tpos] * NSA_HEADS, axis=0)
    grp_rows = NSA_HPG * tq
    qs = jnp.concatenate([_stack_group_queries(q, g) for g in range(NSA_KV_HEADS)], axis=0)

    cmp_blk = cmp_ref[0].astype(BF16)
    cmp_ok = (_lane_iota((1, cmp_blk.shape[0])) * CMP_STRIDE + (L_CMP - 1)) <= tpos_all
    p1 = _masked_softmax(_dot_nt(qs, cmp_blk[:, :LANES]), cmp_ok)
    o_cmp = _dot(p1.astype(BF16), cmp_blk[:, LANES:])
    psum = jnp.concatenate([_sum_heads(p1[g * grp_rows:(g + 1) * grp_rows], tq)
                            for g in range(NSA_KV_HEADS)], axis=0)
    chosen = _choose_blocks(psum, t0, tq, ov_ref[...], n_blocks)

    def sel_chunk(c, carry):
        m, l, acc = carry
        k0 = pl.multiple_of(c * SEL_CHUNK, SEL_CHUNK)
        k_t = sel_ref[0, 0:LANES, pl.ds(k0, SEL_CHUNK)].astype(BF16)
        v_t = sel_ref[0, LANES:2 * LANES, pl.ds(k0, SEL_CHUNK)].astype(BF16)
        causal = jnp.concatenate([(k0 + _lane_iota((1, SEL_CHUNK))) <= tpos] * NSA_KV_HEADS, axis=0)
        key_on = (_dot(chosen, ex_ref[:, pl.ds(k0, SEL_CHUNK)]) > 0.5) & causal
        bias = jnp.where(key_on, 0.0, NEG_INF)
        bias = jnp.concatenate([bias[g * tq:(g + 1) * tq] for g in range(NSA_KV_HEADS)
                                for _ in range(NSA_HPG)], axis=0)
        s = _dot(qs, k_t) + bias
        m_new = jnp.maximum(m, jnp.max(s, axis=-1, keepdims=True))
        alpha = jnp.exp(m - m_new)
        e = jnp.exp(s - m_new)
        l_new = alpha * l + jnp.sum(e, axis=-1, keepdims=True)
        return m_new, l_new, alpha * acc + _dot_nt(e.astype(BF16), v_t)

    rows = NSA_HEADS * tq
    init = (jnp.full((rows, 1), NEG_INF, F32), jnp.zeros((rows, 1), F32), jnp.zeros((rows, LANES), F32))
    n_chunks = t0 // SEL_CHUNK + 1
    _, l2, acc2 = lax.fori_loop(0, n_chunks, sel_chunk, init)
    o_sel = acc2 / l2

    w0 = pl.multiple_of(jnp.maximum(t0 - WINDOW, 0), KEY_TILE)
    wk_t = win_ref[0, 0:LANES, pl.ds(w0, WIN_KEYS)].astype(BF16)
    wv_t = win_ref[0, LANES:2 * LANES, pl.ds(w0, WIN_KEYS)].astype(BF16)
    win_kpos = w0 + _lane_iota((1, WIN_KEYS))
    win_ok = (win_kpos <= tpos_all) & (win_kpos > tpos_all - WINDOW)
    s3 = jnp.where(win_ok, _dot(qs, wk_t), NEG_INF)
    e3 = jnp.exp(s3 - jnp.max(s3, axis=-1, keepdims=True))
    o_win = _dot_nt(e3.astype(BF16), wv_t) / jnp.sum(e3, axis=-1, keepdims=True)

    branch_outs = [(o_cmp[g * grp_rows:(g + 1) * grp_rows], o_sel[g * grp_rows:(g + 1) * grp_rows],
                    o_win[g * grp_rows:(g + 1) * grp_rows]) for g in range(NSA_KV_HEADS)]
    _gate_and_store(g_ref[...], z_ref[...], branch_outs, o_ref)


def _nsa_prompt(qn, gates, zb, cmp_kv, full_t, win_t, overlap, expand, batch, seq):
    tq = KEY_TILE
    nq = seq // tq
    tok = lambda b, i: (b * nq + i, 0)
    per_b = lambda b, i: (b, 0, 0)
    sel_half = lambda b, i: (b, 1, 0)
    fixed = lambda b, i: (0, 0)
    n_blocks = -(-seq // L_SEL)
    return pl.pallas_call(
        functools.partial(_nsa_prompt_kernel, n_blocks=n_blocks),
        grid=(batch, nq),
        in_specs=[
            pl.BlockSpec((tq, B_WIDTH), tok),
            pl.BlockSpec((tq, LANES), tok),
            pl.BlockSpec((tq, B_WIDTH), tok),
            pl.BlockSpec((1, cmp_kv.shape[1], 2 * LANES), per_b),
            pl.BlockSpec((1, 2 * LANES, seq), sel_half),
            pl.BlockSpec((1, WIN_ROW, seq), per_b),
            pl.BlockSpec(overlap.shape, fixed),
            pl.BlockSpec(expand.shape, fixed),
        ],
        out_specs=pl.BlockSpec((tq, B_WIDTH), tok),
        out_shape=jax.ShapeDtypeStruct((batch * seq, B_WIDTH), F32),
        compiler_params=_cparams("parallel", "arbitrary"),
        name="nsa_prompt",
    )(qn, gates, zb, cmp_kv, full_t, win_t, overlap, expand)


def _nsa_sample_kernel(*refs, nb, n_pages, past_len, n_blocks):
    q_ref, g_ref, z_ref, newf_ref, neww_ref, winp_ref = refs[1:7]
    pages = refs[7:7 + nb * n_pages]
    pe_ref, w_ref, ov_ref, ex_ref, o_ref, wout_ref, xs_ref = refs[7 + nb * n_pages:]
    tq = q_ref.shape[0] // nb
    w_buf = winp_ref.shape[3]
    halves_per_page = PAGE_SIZE // CMP_STRIDE
    n_half = past_len // CMP_STRIDE
    total_half = nb * n_half

    for j in range(nb * n_pages):
        for c in range(2):
            x = pages[j][0, c].T
            for hb in range(halves_per_page):
                slot = (j * halves_per_page + hb) * CMP_SLOT
                xs_ref[c, slot:slot + CMP_STRIDE, :] = x[hb * CMP_STRIDE:(hb + 1) * CMP_STRIDE]

    acc = [jnp.zeros((total_half, 2 * LANES), F32) for _ in range(CMP_RATIO)]
    for l in range(CMP_STRIDE):
        x = jnp.concatenate([xs_ref[c, pl.ds(l, total_half, stride=CMP_SLOT), :] for c in range(2)], axis=1)
        for r in range(CMP_RATIO):
            i = r * CMP_STRIDE + l
            acc[r] = acc[r] + _dot((x + pe_ref[i:i + 1, :]).astype(BF16), w_ref[i])
    cmp_all = acc[0]
    for r in range(1, CMP_RATIO):
        cmp_all = cmp_all + pltpu.roll(acc[r], total_half - r, axis=0)
    live = _row_iota((n_half, 2 * LANES)) < n_half - CMP_RATIO + 1

    pad = jnp.zeros((KEY_TILE - tq, LANES), F32)
    lane = _lane_iota((LANES, LANES))
    per_req = NSA_HEADS * tq
    tpos = past_len + _row_iota((tq, 1))
    tpos_all = jnp.concatenate([tpos] * (nb * NSA_HEADS), axis=0)
    tpos_grp = jnp.concatenate([tpos] * (nb * NSA_KV_HEADS), axis=0)

    def padded_rows(x):
        return jnp.concatenate([x, pad], axis=0)

    def per_request(fn):
        return jnp.concatenate([fn(j, slice(j * per_req, (j + 1) * per_req)) for j in range(nb)], axis=0)

    tok = [slice(j * tq, (j + 1) * tq) for j in range(nb)]
    qs = jnp.concatenate([_stack_group_queries(q_ref[tok[j], :], g)
                          for j in range(nb) for g in range(NSA_KV_HEADS)], axis=0)
    cmp_blk = [jnp.where(live, cmp_all[j * n_half:(j + 1) * n_half], 0.0).astype(BF16) for j in range(nb)]
    mine = [pages[j * n_pages:(j + 1) * n_pages] for j in range(nb)]
    newf = [newf_ref[tok[j], :] for j in range(nb)]
    neww = [neww_ref[tok[j], :] for j in range(nb)]

    cmp_ok = (_lane_iota((1, n_half)) * CMP_STRIDE + (L_CMP - 1)) <= tpos_all
    p1 = _masked_softmax(per_request(lambda j, r: _dot_nt(qs[r], cmp_blk[j][:, :LANES])), cmp_ok)
    p1b = p1.astype(BF16)
    o_cmp = per_request(lambda j, r: _dot(p1b[r], cmp_blk[j][:, LANES:]))

    grp_rows = NSA_HPG * tq
    psum = jnp.concatenate([_sum_heads(p1[i * grp_rows:(i + 1) * grp_rows], tq)
                            for i in range(nb * NSA_KV_HEADS)], axis=0)
    chosen = _choose_blocks(psum, past_len, tq, ov_ref[...], n_blocks)
    key_on = (_dot(chosen, ex_ref[...]) > 0.5) & (_lane_iota((1, past_len + KEY_TILE)) <= tpos_grp)
    bias = jnp.where(key_on, 0.0, NEG_INF)
    bias = jnp.concatenate([bias[i * tq:(i + 1) * tq] for i in range(nb * NSA_KV_HEADS)
                            for _ in range(NSA_HPG)], axis=0)

    def sel_scores(j, r):
        k_t = jnp.concatenate([p[0, 2].astype(BF16) for p in mine[j]], axis=1)
        k_new = padded_rows(newf[j][:, 2 * LANES:3 * LANES]).astype(BF16)
        return jnp.concatenate([_dot(qs[r], k_t), _dot_nt(qs[r], k_new)], axis=1)

    s2 = per_request(sel_scores) + bias
    e2 = jnp.exp(s2 - jnp.max(s2, axis=-1, keepdims=True))
    l2 = jnp.sum(e2, axis=-1, keepdims=True)
    e2 = e2.astype(BF16)

    def sel_values(j, r):
        v_t = jnp.concatenate([p[0, 3].astype(BF16) for p in mine[j]], axis=1)
        v_new = padded_rows(newf[j][:, 3 * LANES:4 * LANES]).astype(BF16)
        return _dot_nt(e2[r, :past_len], v_t) + _dot(e2[r, past_len:], v_new)

    o_sel = per_request(sel_values) / l2

    win_kpos = (past_len - w_buf) + _lane_iota((1, w_buf + KEY_TILE))
    win_ok = (win_kpos <= tpos_all) & (win_kpos > tpos_all - WINDOW)
    s3 = per_request(lambda j, r: jnp.concatenate(
        [_dot(qs[r], winp_ref[j, 0].astype(BF16)),
         _dot_nt(qs[r], padded_rows(neww[j][:, :LANES]).astype(BF16))], axis=1))
    s3 = jnp.where(win_ok, s3, NEG_INF)
    e3 = jnp.exp(s3 - jnp.max(s3, axis=-1, keepdims=True))
    l3 = jnp.sum(e3, axis=-1, keepdims=True)
    e3 = e3.astype(BF16)
    o_win = per_request(lambda j, r: _dot_nt(e3[r, :w_buf], winp_ref[j, 1].astype(BF16))
                        + _dot(e3[r, w_buf:], padded_rows(neww[j][:, LANES:]).astype(BF16))) / l3

    for j in range(nb):
        branch_outs = []
        for g in range(NSA_KV_HEADS):
            r = slice((j * NSA_KV_HEADS + g) * grp_rows, (j * NSA_KV_HEADS + g + 1) * grp_rows)
            branch_outs.append((o_cmp[r], o_sel[r], o_win[r]))
        _gate_and_store(g_ref[tok[j], :], z_ref[tok[j], :], branch_outs, o_ref, row0=j * tq)

        for c in range(2):
            shifted = pltpu.roll(winp_ref[j, c], w_buf - tq, axis=1)
            new_cols = pltpu.roll(padded_rows(neww[j][:, c * LANES:(c + 1) * LANES]).T, LANES - tq, axis=1)
            wout_ref[j, c, :, 0:w_buf - LANES] = shifted[:, 0:w_buf - LANES]
            wout_ref[j, c, :, w_buf - LANES:w_buf] = jnp.where(lane >= LANES - tq, new_cols,
                                                               shifted[:, w_buf - LANES:w_buf])


def _nsa_sample(page_table, qn, gates, zb, full_new, win_new, win_t, cache_t, pe, w_bd, overlap, expand,
                batch, tq, n_pages):
    nb = SAMPLE_REQS_PER_STEP
    past_len = n_pages * PAGE_SIZE
    w_buf = win_t.shape[3]
    n_blocks = -(-(past_len + tq) // L_SEL)
    tok = lambda b, pt: (b, 0)
    per_b = lambda b, pt: (b, 0, 0, 0)
    fixed2 = lambda b, pt: (0, 0)
    fixed3 = lambda b, pt: (0, 0, 0)

    def page_spec(j, p):
        return pl.BlockSpec((1, 4, LANES, PAGE_SIZE), lambda b, pt: (pt[b * nb + j, p], 0, 0, 0))

    n_slots = nb * n_pages * (PAGE_SIZE // CMP_STRIDE)
    return pl.pallas_call(
        functools.partial(_nsa_sample_kernel, nb=nb, n_pages=n_pages, past_len=past_len, n_blocks=n_blocks),
        grid_spec=pltpu.PrefetchScalarGridSpec(
            num_scalar_prefetch=1,
            grid=(batch // nb,),
            in_specs=[
                pl.BlockSpec((nb * tq, B_WIDTH), tok),
                pl.BlockSpec((nb * tq, LANES), tok),
                pl.BlockSpec((nb * tq, B_WIDTH), tok),
                pl.BlockSpec((nb * tq, KV_ROW), tok),
                pl.BlockSpec((nb * tq, WIN_ROW), tok),
                pl.BlockSpec((nb, 2, LANES, w_buf), per_b),
            ] + [page_spec(j, p) for j in range(nb) for p in range(n_pages)] + [
                pl.BlockSpec(pe.shape, fixed2),
                pl.BlockSpec(w_bd.shape, fixed3),
                pl.BlockSpec(overlap.shape, fixed2),
                pl.BlockSpec(expand.shape, fixed2),
            ],
            out_specs=[
                pl.BlockSpec((nb * tq, B_WIDTH), tok),
                pl.BlockSpec((nb, 2, LANES, w_buf), per_b),
            ],
            scratch_shapes=[pltpu.VMEM((2, n_slots * CMP_SLOT, LANES), F32)],
        ),
        out_shape=[
            jax.ShapeDtypeStruct((batch * tq, B_WIDTH), F32),
            jax.ShapeDtypeStruct((batch, 2, LANES, w_buf), F32),
        ],
        compiler_params=_cparams("parallel"),
        name="nsa_sample",
    )(page_table, qn, gates, zb, full_new, win_new, win_t, *([cache_t] * (nb * n_pages)), pe, w_bd,
      overlap, expand)


def _selection_tables(n_cmp_rows, n_sel_keys):
    n = jnp.arange(n_cmp_rows, dtype=jnp.int32)[:, None]
    s = jnp.arange(LANES, dtype=jnp.int32)[None, :]
    c_start = n * CMP_STRIDE
    s_start = s * L_SEL
    overlap = ((c_start < s_start + L_SEL) & (s_start < c_start + L_CMP)).astype(BF16)
    key = jnp.arange(n_sel_keys, dtype=jnp.int32)[None, :]
    expand = ((key // L_SEL) == jnp.arange(LANES, dtype=jnp.int32)[:, None]).astype(BF16)
    return overlap, expand


def _even_out_kernel(x_ref, oa_ref, ob_ref, wa_ref, wb_ref, y_ref):
    y_ref[...] = (x_ref[...] + _dot(oa_ref[...].astype(BF16), wa_ref[...])
                  + _dot(ob_ref[...].astype(BF16), wb_ref[...]))


def _even_out(x2d, oa, ob, wa, wb):
    m, d = x2d.shape
    row = lambda i: (i, 0)
    fixed = lambda i: (0, 0)
    return pl.pallas_call(
        _even_out_kernel,
        grid=(m // ROW_TILE,),
        in_specs=[
            pl.BlockSpec((ROW_TILE, d), row),
            pl.BlockSpec((ROW_TILE, A_WIDTH), row),
            pl.BlockSpec((ROW_TILE, B_WIDTH), row),
            pl.BlockSpec((A_WIDTH, d), fixed),
            pl.BlockSpec((B_WIDTH, d), fixed),
        ],
        out_specs=pl.BlockSpec((ROW_TILE, d), row),
        out_shape=jax.ShapeDtypeStruct((m, d), F32),
        compiler_params=_cparams("parallel"),
        name="even_out_proj",
    )(x2d, oa, ob, wa, wb)


def _odd_in_kernel(x_ref, g_ref, w_ref, u_ref, z_ref):
    hb = _rmsnorm_rows(x_ref[...], g_ref[...]).astype(BF16)
    e = u_ref.shape[1]
    u_ref[...] = _dot(hb, w_ref[:, :e])
    z_ref[...] = jax.nn.silu(_dot(hb, w_ref[:, e:]))


def _odd_in(x2d, gain, w_bf):
    m, d = x2d.shape
    e = w_bf.shape[1] // 2
    row = lambda i: (i, 0)
    fixed = lambda i: (0, 0)
    return pl.pallas_call(
        _odd_in_kernel,
        grid=(m // ROW_TILE,),
        in_specs=[
            pl.BlockSpec((ROW_TILE, d), row),
            pl.BlockSpec((1, d), fixed),
            pl.BlockSpec((d, 2 * e), fixed),
        ],
        out_specs=[pl.BlockSpec((ROW_TILE, e), row), pl.BlockSpec((ROW_TILE, e), row)],
        out_shape=[jax.ShapeDtypeStruct((m, e), F32), jax.ShapeDtypeStruct((m, e), F32)],
        compiler_params=_cparams("parallel"),
        name="odd_in_proj",
    )(x2d, gain, w_bf)


def _s5_kernel(u_ref, x0r_ref, x0i_ref, ar_ref, ai_ref, bm_ref, cm_ref, d_ref,
               y_ref, fr_ref, fi_ref, st_re, st_im, ubuf, xbuf, *, nb, tt):
    t_idx = pl.program_id(1)
    n_blk = bm_ref.shape[0]
    half = S5_BLOCK_STATE
    n_chunk = half // LANES
    pitch = tt + S5_ROW_PAD

    @pl.when(t_idx == 0)
    def _():
        for kb in range(n_blk):
            st_re[kb] = x0r_ref[:, kb * half:(kb + 1) * half]
            st_im[kb] = x0i_ref[:, kb * half:(kb + 1) * half]

    for b in range(nb):
        ubuf[b * pitch + tt:(b + 1) * pitch, :] = jnp.zeros((S5_ROW_PAD, LANES), F32)

    for kb in range(n_blk):
        lanes = slice(kb * LANES, (kb + 1) * LANES)
        for b in range(nb):
            ubuf[b * pitch:b * pitch + tt, :] = u_ref[b, :, lanes]
        u_blk = ubuf[...]
        bu = _dot(u_blk.astype(BF16), bm_ref[kb])
        for j in range(2 * n_chunk):
            xbuf[j] = bu[:, j * LANES:(j + 1) * LANES]
        a_re = jnp.broadcast_to(ar_ref[kb], (nb, half))
        a_im = jnp.broadcast_to(ai_ref[kb], (nb, half))

        def step(t, carry):
            s_re, s_im = carry
            rows = pl.ds(t, nb, stride=pitch)
            b_re = jnp.concatenate([xbuf[j, rows, :] for j in range(n_chunk)], axis=1)
            b_im = jnp.concatenate([xbuf[n_chunk + j, rows, :] for j in range(n_chunk)], axis=1)
            n_re = a_re * s_re - a_im * s_im + b_re
            n_im = a_re * s_im + a_im * s_re + b_im
            for j in range(n_chunk):
                xbuf[j, rows, :] = n_re[:, j * LANES:(j + 1) * LANES]
                xbuf[n_chunk + j, rows, :] = n_im[:, j * LANES:(j + 1) * LANES]
            return n_re, n_im

        s_re, s_im = lax.fori_loop(0, tt, step, (st_re[kb], st_im[kb]), unroll=S5_SCAN_UNROLL)
        st_re[kb] = s_re
        st_im[kb] = s_im
        states = jnp.concatenate([xbuf[j] for j in range(2 * n_chunk)], axis=1)
        ubuf[...] = _dot(states.astype(BF16), cm_ref[kb]) + d_ref[:, lanes] * u_blk
        for b in range(nb):
            y_ref[b, :, lanes] = ubuf[b * pitch:b * pitch + tt, :]

    @pl.when(t_idx == pl.num_programs(1) - 1)
    def _():
        for kb in range(n_blk):
            fr_ref[:, kb * half:(kb + 1) * half] = st_re[kb]
            fi_ref[:, kb * half:(kb + 1) * half] = st_im[kb]


def _s5(u3, x0_re, x0_im, a_re, a_im, bmat, cmat, d_row, tt):
    batch, seq, e = u3.shape
    nb = SUBLANES
    n_blk = e // LANES
    n_state = n_blk * S5_BLOCK_STATE
    seq_map = lambda b, t: (b, t, 0)
    st_map = lambda b, t: (b, 0)
    fixed2 = lambda b, t: (0, 0)
    fixed3 = lambda b, t: (0, 0, 0)
    return pl.pallas_call(
        functools.partial(_s5_kernel, nb=nb, tt=tt),
        grid=(batch // nb, seq // tt),
        in_specs=[
            pl.BlockSpec((nb, tt, e), seq_map),
            pl.BlockSpec((nb, n_state), st_map),
            pl.BlockSpec((nb, n_state), st_map),
            pl.BlockSpec((n_blk, 1, S5_BLOCK_STATE), fixed3),
            pl.BlockSpec((n_blk, 1, S5_BLOCK_STATE), fixed3),
            pl.BlockSpec((n_blk, LANES, 2 * S5_BLOCK_STATE), fixed3),
            pl.BlockSpec((n_blk, 2 * S5_BLOCK_STATE, LANES), fixed3),
            pl.BlockSpec((1, e), fixed2),
        ],
        out_specs=[
            pl.BlockSpec((nb, tt, e), seq_map),
            pl.BlockSpec((nb, n_state), st_map),
            pl.BlockSpec((nb, n_state), st_map),
        ],
        out_shape=[
            jax.ShapeDtypeStruct((batch, seq, e), F32),
            jax.ShapeDtypeStruct((batch, n_state), F32),
            jax.ShapeDtypeStruct((batch, n_state), F32),
        ],
        scratch_shapes=[
            pltpu.VMEM((n_blk, nb, S5_BLOCK_STATE), F32),
            pltpu.VMEM((n_blk, nb, S5_BLOCK_STATE), F32),
            pltpu.VMEM((nb * (tt + S5_ROW_PAD), LANES), F32),
            pltpu.VMEM((2 * S5_BLOCK_STATE // LANES, nb * (tt + S5_ROW_PAD), LANES), F32),
        ],
        compiler_params=_cparams("parallel", "arbitrary"),
        name="s5_scan",
    )(u3, x0_re, x0_im, a_re, a_im, bmat, cmat, d_row)


def _s5_params(lam_re, lam_im, b_re, b_im, c_re, c_im, log_step):
    n_groups = lam_re.shape[0]
    n_blk = n_groups // S5_LANE_GROUPS
    dt = jnp.exp(log_step)[:, None]
    mag = jnp.exp(lam_re * dt)
    ang = lam_im * dt
    ab_re, ab_im = mag * jnp.cos(ang), mag * jnp.sin(ang)
    den = lam_re * lam_re + lam_im * lam_im
    nr = ab_re - 1.0
    f_re = (nr * lam_re + ab_im * lam_im) / den
    f_im = (ab_im * lam_re - nr * lam_im) / den
    bb_re = f_re[..., None] * b_re - f_im[..., None] * b_im
    bb_im = f_re[..., None] * b_im + f_im[..., None] * b_re
    eye = jnp.eye(S5_LANE_GROUPS, dtype=lam_re.dtype)

    def in_map(bb):
        bb = bb.reshape(n_blk, S5_LANE_GROUPS, S5_STATE, S5_GROUP)
        m = jnp.einsum('kgpc,gh->kgchp', bb, eye)
        return m.reshape(n_blk, LANES, S5_BLOCK_STATE)

    def out_map(cc):
        cc = cc.reshape(n_blk, S5_LANE_GROUPS, S5_GROUP, S5_STATE)
        m = jnp.einsum('kgcp,gh->kgphc', cc, eye)
        return m.reshape(n_blk, S5_BLOCK_STATE, LANES)

    bmat = jnp.concatenate([in_map(bb_re), in_map(bb_im)], axis=2).astype(BF16)
    cmat = jnp.concatenate([out_map(c_re), out_map(-c_im)], axis=1).astype(BF16)
    a_re = ab_re.reshape(n_blk, 1, S5_BLOCK_STATE)
    a_im = ab_im.reshape(n_blk, 1, S5_BLOCK_STATE)
    return a_re, a_im, bmat, cmat


def _odd_out_kernel(x_ref, y_ref, z_ref, w1_ref, w2_ref, wo_ref, o_ref):
    yb = jax.nn.gelu(y_ref[...]).astype(BF16)
    t = _dot(yb, w1_ref[...]) * jax.nn.sigmoid(_dot(yb, w2_ref[...])) * z_ref[...]
    o_ref[...] = x_ref[...] + _dot(t.astype(BF16), wo_ref[...])


def _odd_out(x2d, y2d, z2d, w1, w2, wo):
    m, d = x2d.shape
    e = y2d.shape[1]
    row = lambda i: (i, 0)
    fixed = lambda i: (0, 0)
    return pl.pallas_call(
        _odd_out_kernel,
        grid=(m // ROW_TILE,),
        in_specs=[
            pl.BlockSpec((ROW_TILE, d), row),
            pl.BlockSpec((ROW_TILE, e), row),
            pl.BlockSpec((ROW_TILE, e), row),
            pl.BlockSpec((e, e), fixed),
            pl.BlockSpec((e, e), fixed),
            pl.BlockSpec((e, d), fixed),
        ],
        out_specs=pl.BlockSpec((ROW_TILE, d), row),
        out_shape=jax.ShapeDtypeStruct((m, d), F32),
        compiler_params=_cparams("parallel"),
        name="odd_out_proj",
    )(x2d, y2d, z2d, w1, w2, wo)


def _rope_tables(pos):
    half = NSA_DH // 2
    inv = ROPE_THETA ** (-jnp.arange(half, dtype=F32) / half)
    ang = pos.astype(F32)[:, None] * inv[None, :]
    cos, sin = jnp.cos(ang), jnp.sin(ang)
    reps = LANES // NSA_DH
    return jnp.tile(cos, (1, 2 * reps)), jnp.tile(jnp.concatenate([-sin, sin], axis=1), (1, reps))


def _even_weights(norm_g, w_in, w_out, q_norm, k_norm, cmp_pos, cmp_w):
    d = w_in.shape[0]
    sizes = (RET_HEADS * RET_DK, RET_HEADS * RET_DK, A_WIDTH, A_WIDTH, B_WIDTH, 6 * NSA_KV_HEADS * NSA_DH,
             N_GATES, B_WIDTH)
    parts, o = [], 0
    for s in sizes:
        parts.append(w_in[:, o:o + s])
        o += s
    qa, ka, va, za, qn, kvb, gl, zb = parts
    w_perm = jnp.concatenate([qa, ka, va, za, qn, kvb, zb, gl, jnp.zeros((d, LANES - N_GATES), w_in.dtype)],
                             axis=1).astype(BF16)
    reps = LANES // NSA_DH
    qg = jnp.tile(q_norm[None, :], (1, reps))
    kg = jnp.tile(k_norm, (1, reps))
    eye = jnp.eye(NSA_KV_HEADS, dtype=cmp_w.dtype)
    w_bd = jnp.einsum('clde,gh,ck->lcgdkhe', cmp_w, eye, jnp.eye(2, dtype=cmp_w.dtype))
    w_bd = w_bd.reshape(L_CMP, 2 * LANES, 2 * LANES).astype(BF16)
    pe = jnp.broadcast_to(cmp_pos.transpose(1, 0, 2)[:, :, None, :], (L_CMP, 2, NSA_KV_HEADS, NSA_DH))
    pe = pe.reshape(L_CMP, 2 * LANES)
    wa = w_out[:A_WIDTH].astype(BF16)
    wb = w_out[A_WIDTH:].astype(BF16)
    return norm_g[None, :], w_perm, qg, kg, w_bd, pe, wa, wb


def _even_layer(x, pos0, s_ret, ew, gn_gain, cache=None, page_table=None, win_past=None):
    batch, seq, d = x.shape
    gain, w_perm, qg, kg, w_bd, pe, wa, wb = ew
    x2d = x.reshape(batch * seq, d)
    pos = pos0 + jnp.arange(seq, dtype=jnp.int32)
    cos_t, sin_t = _rope_tables(pos)
    if seq >= EVEN_IN_TILE:
        n_table_blocks = seq // EVEN_IN_TILE
    else:
        cos_t = jnp.tile(cos_t, (EVEN_IN_TILE // seq, 1))
        sin_t = jnp.tile(sin_t, (EVEN_IN_TILE // seq, 1))
        n_table_blocks = 1
    g, dh = NSA_KV_HEADS, NSA_DH
    if cache is None:
        qa, ka, va, za, qn, full_t, win_t, zb, gates, cmp_rows = _even_in(
            x2d, gain, w_perm, cos_t, sin_t, qg, kg, n_table_blocks, seq_tiles=seq // EVEN_IN_TILE)
        oa, s_fin = _retention(qa, ka, va, za, gn_gain[None, :], s_ret, batch, seq)
        cmp3 = cmp_rows.reshape(batch, seq, 2 * LANES)
        n_half = seq // CMP_STRIDE
        src_specs = [pl.BlockSpec((1, seq, LANES), lambda b: (b, 0, 0)),
                     pl.BlockSpec((1, seq, LANES), lambda b: (b, 0, 1))]
        cmp_kv = _compress([cmp3, cmp3], src_specs, (batch,), (), pe, w_bd, batch, seq, n_half)
        overlap, expand = _selection_tables(n_half, seq)
        ob = _nsa_prompt(qn, gates, zb, cmp_kv, full_t, win_t, overlap, expand, batch, seq)
        keep = min(WINDOW, seq)
        y = _even_out(x2d, oa, ob, wa, wb).reshape(batch, seq, d)
        full_rows = full_t.reshape(batch, 4, g, dh, seq).transpose(0, 4, 1, 2, 3)
        win_rows = win_t[:, :, seq - keep:].reshape(batch, 2, g, dh, keep).transpose(0, 4, 1, 2, 3)
        return y, s_fin, full_rows, win_rows
    else:
        qa, ka, va, za, qn, full_new, win_new, zb, gates = _even_in(x2d, gain, w_perm, cos_t, sin_t, qg, kg,
                                                                     n_table_blocks)
        oa, s_fin = _retention(qa, ka, va, za, gn_gain[None, :], s_ret, batch, seq)
        n_pages = page_table.shape[1]
        past_len = n_pages * PAGE_SIZE
        overlap, expand = _selection_tables(past_len // CMP_STRIDE, past_len + KEY_TILE)
        ob, win_out_t = _nsa_sample(page_table, qn, gates, zb, full_new, win_new, win_past, cache, pe, w_bd,
                                    overlap, expand, batch, seq, n_pages)
    y = _even_out(x2d, oa, ob, wa, wb).reshape(batch, seq, d)
    w_buf = win_out_t.shape[3]
    return (y, s_fin, full_new.reshape(batch, seq, 4, g, dh),
            win_out_t.reshape(batch, 2, g, dh, w_buf).transpose(0, 4, 1, 2, 3))


def _odd_layer(x, s_re, s_im, gain, w_in_bf, s5p, d_row, w1, w2, wo, tt):
    batch, seq, d = x.shape
    x2d = x.reshape(batch * seq, d)
    u, zs = _odd_in(x2d, gain, w_in_bf)
    e = u.shape[1]
    a_re, a_im, bmat, cmat = s5p
    n_groups, n_state = s_re.shape[1], s_re.shape[2]
    y, f_re, f_im = _s5(u.reshape(batch, seq, e), s_re.reshape(batch, n_groups * n_state),
                        s_im.reshape(batch, n_groups * n_state), a_re, a_im, bmat, cmat, d_row, tt)
    out = _odd_out(x2d, y.reshape(batch * seq, e), zs, w1, w2, wo).reshape(batch, seq, d)
    return out, f_re.reshape(batch, n_groups, n_state), f_im.reshape(batch, n_groups, n_state)


S5_TIME_TILE = 128


def kernel(x_prompt, x_sample, cache_nsa_kv, cache_nsa_win, state_ret, state_ssm_re, state_ssm_im, page_table,
           norm_even, w_in_even, w_out_even, ret_gn_gain, nsa_q_norm, nsa_k_norm, nsa_cmp_pos, nsa_cmp_w,
           norm_odd, w_in_odd, ssm_lambda_re, ssm_lambda_im, ssm_b_re, ssm_b_im, ssm_c_re, ssm_c_im, ssm_d,
           ssm_log_step, glu_w1, glu_w2, w_out_odd):
    bp, seq_p, _ = x_prompt.shape
    db, seq_s, _ = x_sample.shape
    n_pages = page_table.shape[1]
    past_len = n_pages * PAGE_SIZE
    depth = norm_even.shape[0] + norm_odd.shape[0]
    yp, ys = x_prompt, x_sample
    ret_p, ret_s, kv_p, kv_s, win_p, win_s = [], [], [], [], [], []
    sre_p, sim_p, sre_s, sim_s = [], [], [], []
    for layer in range(depth):
        li = layer // 2
        if layer % 2 == 0:
            ew = _even_weights(norm_even[li], w_in_even[li], w_out_even[li], nsa_q_norm[li], nsa_k_norm[li],
                               nsa_cmp_pos[li], nsa_cmp_w[li])
            s0 = jnp.zeros((bp, RET_HEADS, RET_DK, RET_DV), F32)
            yp, sr, kvr, wr = _even_layer(yp, 0, s0, ew, ret_gn_gain[li])
            ret_p.append(sr); kv_p.append(kvr); win_p.append(wr)
            cache_t = cache_nsa_kv[li].transpose(0, 2, 3, 4, 1).reshape(cache_nsa_kv.shape[1], 4, LANES, PAGE_SIZE)
            win_t = cache_nsa_win[li].transpose(0, 2, 3, 4, 1).reshape(db, 2, LANES, cache_nsa_win.shape[2])
            ys, sr2, kvr2, wr2 = _even_layer(ys, past_len, state_ret[li], ew, ret_gn_gain[li],
                                             cache=cache_t, page_table=page_table, win_past=win_t)
            ret_s.append(sr2); kv_s.append(kvr2); win_s.append(wr2)
        else:
            s5p = _s5_params(ssm_lambda_re[li], ssm_lambda_im[li], ssm_b_re[li], ssm_b_im[li],
                             ssm_c_re[li], ssm_c_im[li], ssm_log_step[li])
            gain = norm_odd[li][None, :]
            w_in_bf = w_in_odd[li].astype(BF16)
            w1, w2, wo = glu_w1[li].astype(BF16), glu_w2[li].astype(BF16), w_out_odd[li].astype(BF16)
            d_row = ssm_d[li][None, :]
            n_groups = ssm_lambda_re.shape[1]
            z0 = jnp.zeros((bp, n_groups, S5_STATE), F32)
            yp, fr, fi = _odd_layer(yp, z0, z0, gain, w_in_bf, s5p, d_row, w1, w2, wo, min(S5_TIME_TILE, seq_p))
            sre_p.append(fr); sim_p.append(fi)
            ys, fr2, fi2 = _odd_layer(ys, state_ssm_re[li], state_ssm_im[li], gain, w_in_bf, s5p, d_row,
                                      w1, w2, wo, min(S5_TIME_TILE, seq_s))
            sre_s.append(fr2); sim_s.append(fi2)
    return (yp, ys, jnp.stack(ret_p), jnp.stack(ret_s), jnp.stack(kv_p), jnp.stack(kv_s), jnp.stack(win_p),
            jnp.stack(win_s), jnp.stack(sre_p), jnp.stack(sim_p), jnp.stack(sre_s), jnp.stack(sim_s))
```

```python
import functools
import math

import jax
import jax.numpy as jnp
from jax import lax
from jax.experimental import pallas as pl
from jax.experimental.pallas import tpu as pltpu

F32 = jnp.float32
BF16 = jnp.bfloat16

LANES = 128
SUBLANES = 8
VMEM_LIMIT_BYTES = 48 * 2**20

EPS = 1e-6
ROPE_THETA = 10000.0
NEG_INF = -1e30
FORCE = 1e4

RET_HEADS = 4
RET_DK = 64
RET_DV = 128
RET_CHUNK = 128
RET_TILES_PER_STEP = 4
A_WIDTH = RET_HEADS * RET_DV

NSA_HEADS = 8
NSA_KV_HEADS = 2
NSA_DH = 64
NSA_HPG = NSA_HEADS // NSA_KV_HEADS
B_WIDTH = NSA_HEADS * NSA_DH
L_CMP = 32
CMP_STRIDE = 16
CMP_RATIO = L_CMP // CMP_STRIDE
L_SEL = 64
N_SEL = 8
WINDOW = 512
PAGE_SIZE = 128
KV_ROW = 4 * NSA_KV_HEADS * NSA_DH
WIN_ROW = 2 * NSA_KV_HEADS * NSA_DH
KEY_TILE = 128
WIN_KEYS = WINDOW + KEY_TILE
SEL_CHUNK = 512
CMP_SLOT = CMP_STRIDE + 4
SAMPLE_REQS_PER_STEP = 2

S5_GROUP = 16
S5_STATE = 64
S5_LANE_GROUPS = LANES // S5_GROUP
S5_BLOCK_STATE = S5_LANE_GROUPS * S5_STATE
S5_ROW_PAD = 4
S5_SCAN_UNROLL = 4

QA0, KA0, VA0, ZA0, QN0, KVB0, ZB0, GL0 = 0, 256, 512, 1024, 1536, 2048, 2816, 3328
EVEN_COLS = GL0 + LANES
N_GATES = 3 * NSA_HEADS

ROW_TILE = 512
EVEN_IN_TILE = 256


def _cparams(*sem):
    return pltpu.CompilerParams(dimension_semantics=sem, vmem_limit_bytes=VMEM_LIMIT_BYTES)


def _lane_iota(shape):
    return lax.broadcasted_iota(jnp.int32, shape, len(shape) - 1)


def _row_iota(shape):
    return lax.broadcasted_iota(jnp.int32, shape, len(shape) - 2)


def _dot(a, b):
    return jnp.dot(a, b, preferred_element_type=F32)


def _dot_nt(a, b):
    return lax.dot_general(a, b, (((1,), (1,)), ((), ())), preferred_element_type=F32)


def _rmsnorm_rows(x, g):
    return x * lax.rsqrt(jnp.mean(x * x, axis=-1, keepdims=True) + EPS) * g


def _swap_halves(x):
    return pltpu.roll(x, NSA_DH, axis=1)


def _rope_block(x, cos, sin_signed):
    half = NSA_DH // 2
    lane = _lane_iota(x.shape)
    first = (lane % NSA_DH) < half
    partner = jnp.where(first, pltpu.roll(x, LANES - half, axis=1), pltpu.roll(x, half, axis=1))
    return x * cos + partner * sin_signed


def _head_rms_block(x, g):
    lane = _lane_iota(x.shape)
    lo = lane < NSA_DH
    sq = x * x
    s_lo = jnp.sum(jnp.where(lo, sq, 0.0), axis=-1, keepdims=True)
    s_hi = jnp.sum(jnp.where(lo, 0.0, sq), axis=-1, keepdims=True)
    ms = jnp.where(lo, s_lo, s_hi) * (1.0 / NSA_DH)
    return x * lax.rsqrt(ms + EPS) * g


def _even_in_kernel(x_ref, g_ref, w_ref, cos_ref, sin_ref, qg_ref, kg_ref, *out_refs, transposed_kv):
    if transposed_kv:
        qa_ref, ka_ref, va_ref, za_ref, qn_ref, full_ref, win_ref, zb_ref, gl_ref, cmp_ref = out_refs
    else:
        qa_ref, ka_ref, va_ref, za_ref, qn_ref, full_ref, win_ref, zb_ref, gl_ref = out_refs
    hb = _rmsnorm_rows(x_ref[...], g_ref[...]).astype(BF16)
    cos = cos_ref[...]
    sin = sin_ref[...]

    def proj(c0):
        return _dot(hb, w_ref[:, c0:c0 + LANES])

    for j in range(RET_HEADS * RET_DK // LANES):
        c = j * LANES
        qa_ref[:, c:c + LANES] = _rope_block(proj(QA0 + c), cos, sin)
        ka_ref[:, c:c + LANES] = _rope_block(proj(KA0 + c), cos, sin) * (RET_DK ** -0.5)
    for j in range(A_WIDTH // LANES):
        c = j * LANES
        va_ref[:, c:c + LANES] = proj(VA0 + c)
        za_ref[:, c:c + LANES] = jax.nn.silu(proj(ZA0 + c))
    for j in range(B_WIDTH // LANES):
        c = j * LANES
        qn_ref[:, c:c + LANES] = _rope_block(_head_rms_block(proj(QN0 + c), qg_ref[...]), cos, sin)
        zb_ref[:, c:c + LANES] = jax.nn.silu(proj(ZB0 + c))
    for j in range(6):
        y = proj(KVB0 + j * LANES)
        if j % 2 == 0:
            y = _rope_block(_head_rms_block(y, kg_ref[j // 2:j // 2 + 1, :]), cos, sin)
        if transposed_kv:
            if j < 4:
                full_ref[0, j * LANES:(j + 1) * LANES, :] = y.T
            else:
                win_ref[0, (j - 4) * LANES:(j - 3) * LANES, :] = y.T
            if j < 2:
                cmp_ref[:, j * LANES:(j + 1) * LANES] = y
        elif j < 4:
            full_ref[:, j * LANES:(j + 1) * LANES] = y
        else:
            win_ref[:, (j - 4) * LANES:(j - 3) * LANES] = y
    gl_ref[...] = jax.nn.sigmoid(proj(GL0))


def _even_in(x2d, gain, w_bf, cos_t, sin_t, qg, kg, n_table_blocks, seq_tiles=None):
    m, d = x2d.shape
    tile = EVEN_IN_TILE
    grid = (m // tile,)
    row = lambda i: (i, 0)
    fixed = lambda i: (0, 0)
    table = lambda i: (i % n_table_blocks, 0)
    widths = (RET_HEADS * RET_DK, RET_HEADS * RET_DK, A_WIDTH, A_WIDTH, B_WIDTH, KV_ROW, WIN_ROW, B_WIDTH, LANES)
    out_specs = [pl.BlockSpec((tile, w), row) for w in widths]
    out_shape = [jax.ShapeDtypeStruct((m, w), F32) for w in widths]
    if seq_tiles is not None:
        batch, seq = grid[0] // seq_tiles, seq_tiles * tile
        fmajor = lambda i: (i // seq_tiles, 0, i % seq_tiles)
        for idx, w in ((5, KV_ROW), (6, WIN_ROW)):
            out_specs[idx] = pl.BlockSpec((1, w, tile), fmajor)
            out_shape[idx] = jax.ShapeDtypeStruct((batch, w, seq), F32)
        out_specs.append(pl.BlockSpec((tile, 2 * LANES), row))
        out_shape.append(jax.ShapeDtypeStruct((m, 2 * LANES), F32))
    return pl.pallas_call(
        functools.partial(_even_in_kernel, transposed_kv=seq_tiles is not None),
        grid=grid,
        in_specs=[
            pl.BlockSpec((tile, d), row),
            pl.BlockSpec((1, d), fixed),
            pl.BlockSpec((d, EVEN_COLS), fixed),
            pl.BlockSpec((tile, LANES), table),
            pl.BlockSpec((tile, LANES), table),
            pl.BlockSpec((1, LANES), fixed),
            pl.BlockSpec((3, LANES), fixed),
        ],
        out_specs=out_specs,
        out_shape=out_shape,
        compiler_params=_cparams("parallel"),
        name="even_in_proj",
    )(x2d, gain, w_bf, cos_t, sin_t, qg, kg)


def _retention_kernel(q_ref, k_ref, v_ref, z_ref, gn_ref, s0_ref, dmat_ref, qdec_ref, kdec_ref, cdec_ref,
                      o_ref, sfin_ref, s_scr, *, rows, n_tiles):
    n_seq = RET_CHUNK // rows
    c = pl.program_id(1)

    @pl.when(c == 0)
    def _():
        s_scr[...] = s0_ref[...]

    lane = _lane_iota((RET_CHUNK, LANES))
    col_seq = _lane_iota((RET_DK, RET_CHUNK)) >> int(math.log2(rows)) if n_seq > 1 else None
    state = [[s_scr[r, h] for h in range(RET_HEADS)] for r in range(n_seq)]
    for t in range(n_tiles):
        tile = slice(t * RET_CHUNK, (t + 1) * RET_CHUNK)
        q, k, v = q_ref[tile, :], k_ref[tile, :], v_ref[tile, :]
        qd = q * qdec_ref[...]
        kd = k * kdec_ref[...]
        for pair in range(RET_HEADS // 2):
            cols = slice(pair * LANES, (pair + 1) * LANES)
            q2, k2b, qd2 = q[:, cols], k[:, cols].astype(BF16), qd[:, cols]
            kd2_t = kd[:, cols].T
            s_pair = [jnp.concatenate([state[r][2 * pair], state[r][2 * pair + 1]], axis=0).astype(BF16)
                      for r in range(n_seq)]
            for sub in range(2):
                h = 2 * pair + sub
                mine = (lane >= sub * RET_DK) & (lane < (sub + 1) * RET_DK)
                qm = jnp.where(mine, q2, 0.0).astype(BF16)
                qdm = jnp.where(mine, qd2, 0.0).astype(BF16)
                vhb = v[:, h * RET_DV:(h + 1) * RET_DV].astype(BF16)
                intra = _dot((_dot_nt(qm, k2b) * dmat_ref[h]).astype(BF16), vhb)
                cross = jnp.concatenate([_dot(qdm[r * rows:(r + 1) * rows], s_pair[r]) for r in range(n_seq)],
                                        axis=0)
                out = intra + cross
                kt = kd2_t[sub * RET_DK:(sub + 1) * RET_DK, :]
                if n_seq > 1:
                    kt = jnp.concatenate([jnp.where(col_seq == r, kt, 0.0) for r in range(n_seq)], axis=0)
                kv = _dot(kt.astype(BF16), vhb)
                for r in range(n_seq):
                    state[r][h] = state[r][h] * cdec_ref[h] + kv[r * RET_DK:(r + 1) * RET_DK]
                mu = jnp.mean(out, axis=-1, keepdims=True)
                cen = out - mu
                var = jnp.mean(cen * cen, axis=-1, keepdims=True)
                y = cen * lax.rsqrt(var + EPS) * gn_ref[:, h * RET_DV:(h + 1) * RET_DV]
                o_ref[tile, h * RET_DV:(h + 1) * RET_DV] = y * z_ref[tile, h * RET_DV:(h + 1) * RET_DV]
    for r in range(n_seq):
        for h in range(RET_HEADS):
            s_scr[r, h] = state[r][h]

    @pl.when(c == pl.num_programs(1) - 1)
    def _():
        sfin_ref[...] = s_scr[...]


def _retention_tables(rows):
    log_g = jnp.log(1.0 - 2.0 ** (-5.0 - jnp.arange(RET_HEADS, dtype=F32)))
    idx = jnp.arange(RET_CHUNK, dtype=jnp.int32)
    pos = (idx % rows).astype(F32)
    seq = idx // rows
    diff = pos[:, None] - pos[None, :]
    causal = (diff >= 0) & (seq[:, None] == seq[None, :])
    dmat = jnp.exp(jnp.where(causal, diff, 0.0)[None] * log_g[:, None, None]) * causal[None]
    qdec = jnp.exp((pos + 1.0)[:, None] * log_g[None, :])
    kdec = jnp.exp((rows - 1.0 - pos)[:, None] * log_g[None, :])
    qdec = jnp.repeat(qdec, RET_DK, axis=1)
    kdec = jnp.repeat(kdec, RET_DK, axis=1)
    cdec = jnp.broadcast_to(jnp.exp(rows * log_g)[:, None, None], (RET_HEADS, RET_DK, RET_DV))
    return dmat, qdec, kdec, cdec


def _retention(qa, ka, va, za, gn_gain, s0, batch, seq):
    rows = min(RET_CHUNK, seq)
    n_seq = RET_CHUNK // rows
    n_chunks = seq // rows
    n_tiles = min(RET_TILES_PER_STEP, n_chunks)
    steps = n_chunks // n_tiles
    dmat, qdec, kdec, cdec = _retention_tables(rows)
    tok = lambda b, c: (b * steps + c, 0)
    fixed2 = lambda b, c: (0, 0)
    fixed3 = lambda b, c: (0, 0, 0)
    state = lambda b, c: (b, 0, 0, 0)
    qk_w = RET_HEADS * RET_DK
    step_rows = n_tiles * RET_CHUNK
    return pl.pallas_call(
        functools.partial(_retention_kernel, rows=rows, n_tiles=n_tiles),
        grid=(batch // n_seq, steps),
        in_specs=[
            pl.BlockSpec((step_rows, qk_w), tok),
            pl.BlockSpec((step_rows, qk_w), tok),
            pl.BlockSpec((step_rows, A_WIDTH), tok),
            pl.BlockSpec((step_rows, A_WIDTH), tok),
            pl.BlockSpec((1, A_WIDTH), fixed2),
            pl.BlockSpec((n_seq, RET_HEADS, RET_DK, RET_DV), state),
            pl.BlockSpec((RET_HEADS, RET_CHUNK, RET_CHUNK), fixed3),
            pl.BlockSpec((RET_CHUNK, qk_w), fixed2),
            pl.BlockSpec((RET_CHUNK, qk_w), fixed2),
            pl.BlockSpec((RET_HEADS, RET_DK, RET_DV), fixed3),
        ],
        out_specs=[
            pl.BlockSpec((step_rows, A_WIDTH), tok),
            pl.BlockSpec((n_seq, RET_HEADS, RET_DK, RET_DV), state),
        ],
        out_shape=[
            jax.ShapeDtypeStruct((batch * seq, A_WIDTH), F32),
            jax.ShapeDtypeStruct((batch, RET_HEADS, RET_DK, RET_DV), F32),
        ],
        scratch_shapes=[pltpu.VMEM((n_seq, RET_HEADS, RET_DK, RET_DV), F32)],
        compiler_params=_cparams("parallel", "arbitrary"),
        name="retention",
    )(qa, ka, va, za, gn_gain, s0, dmat, qdec, kdec, cdec)


def _compress_kernel(*refs, n_pref, n_src, src_rows):
    refs = refs[n_pref:]
    src_k = refs[:n_src]
    src_v = refs[n_src:2 * n_src]
    pe_ref, w_ref, o_ref = refs[2 * n_src:]
    per_src = src_rows // CMP_STRIDE
    n_half = n_src * per_src
    acc = [jnp.zeros((n_half, 2 * LANES), F32) for _ in range(CMP_RATIO)]

    def strided_rows(srcs, l):
        rows = [s[0, pl.ds(l, per_src, stride=CMP_STRIDE), :] for s in srcs]
        return rows[0] if n_src == 1 else jnp.concatenate(rows, axis=0)

    for l in range(CMP_STRIDE):
        x = jnp.concatenate([strided_rows(src_k, l), strided_rows(src_v, l)], axis=1)
        for r in range(CMP_RATIO):
            i = r * CMP_STRIDE + l
            acc[r] = acc[r] + _dot((x + pe_ref[i:i + 1, :]).astype(BF16), w_ref[i])
    out = acc[0]
    for r in range(1, CMP_RATIO):
        out = out + pltpu.roll(acc[r], n_half - r, axis=0)
    live = _row_iota(out.shape) < n_half - CMP_RATIO + 1
    o_ref[0] = jnp.where(live, out, 0.0)


def _compress(src_arrays, src_specs, grid, extra_prefetch, pe, w_bd, batch, src_rows, n_half):
    n_src = len(src_specs) // 2
    n_pref = len(extra_prefetch)
    kern = functools.partial(_compress_kernel, n_pref=n_pref, n_src=n_src, src_rows=src_rows)
    fixed2 = lambda *a: (0, 0)
    fixed3 = lambda *a: (0, 0, 0)
    out_map = lambda b, *a: (b, 0, 0)
    return pl.pallas_call(
        kern,
        grid_spec=pltpu.PrefetchScalarGridSpec(
            num_scalar_prefetch=n_pref,
            grid=grid,
            in_specs=list(src_specs) + [
                pl.BlockSpec((L_CMP, 2 * LANES), fixed2),
                pl.BlockSpec((L_CMP, 2 * LANES, 2 * LANES), fixed3),
            ],
            out_specs=pl.BlockSpec((1, n_half, 2 * LANES), out_map),
        ),
        out_shape=jax.ShapeDtypeStruct((batch, n_half, 2 * LANES), F32),
        compiler_params=_cparams("parallel"),
        name="kv_compress",
    )(*extra_prefetch, *src_arrays, pe, w_bd)


def _split3_bf16(x):
    hi = x.astype(BF16)
    r1 = x - hi.astype(F32)
    mid = r1.astype(BF16)
    lo = (r1 - mid.astype(F32)).astype(BF16)
    return hi, mid, lo


def _stack_group_queries(q, g):
    lane = _lane_iota((q.shape[0], LANES))
    in_g = (lane >= g * NSA_DH) & (lane < (g + 1) * NSA_DH)
    parts = []
    for hh in range(NSA_HPG):
        h = g * NSA_HPG + hh
        two = q[:, (h // 2) * LANES:(h // 2 + 1) * LANES]
        if h % 2 != g:
            two = _swap_halves(two)
        parts.append(jnp.where(in_g, two, 0.0))
    return (jnp.concatenate(parts, axis=0) * (NSA_DH ** -0.5)).astype(BF16)


def _masked_softmax(s, ok):
    s = jnp.where(ok, s, NEG_INF)
    e = jnp.exp(s - jnp.max(s, axis=-1, keepdims=True))
    return jnp.where(ok, e * (1.0 / jnp.sum(e, axis=-1, keepdims=True)), 0.0)


def _sum_heads(p1, tq):
    psum = p1[0:tq]
    for hh in range(1, NSA_HPG):
        psum = psum + p1[hh * tq:(hh + 1) * tq]
    return psum


def _choose_blocks(psum, t0, tq, overlap, n_blocks):
    rows = psum.shape[0]
    imp = sum(_dot(t, overlap) for t in _split3_bf16(psum))
    shift = int(math.log2(L_SEL))

    def ranked(score, blk, valid, take):
        rank = jnp.zeros(score.shape, F32)
        for s in range(n_blocks):
            cand = take(s)
            rank = rank + ((cand > score) | ((cand == score) & (blk > s))).astype(F32)
        return ((rank < N_SEL) & valid).astype(F32)

    if rows % LANES:
        tpos = jnp.concatenate([t0 + _row_iota((tq, 1))] * (rows // tq), axis=0)
        blk = _lane_iota((rows, LANES))
        cur = tpos >> shift
        forced = (blk == 0) | (blk == cur) | (blk == cur - 1)
        valid = (blk * L_SEL <= tpos) & (blk < n_blocks)
        score = jnp.where(valid, jnp.where(forced, FORCE, imp), -FORCE)
        return ranked(score, blk, valid, lambda s: score[:, s:s + 1]).astype(BF16)

    nbp = -(-n_blocks // SUBLANES) * SUBLANES
    tiles = rows // LANES
    imp_t = jnp.concatenate([imp[i * LANES:(i + 1) * LANES].T[:nbp] for i in range(tiles)], axis=1)
    tpos = jnp.concatenate([t0 + _lane_iota((1, tq))] * (rows // tq), axis=1)
    blk = _row_iota((nbp, rows))
    cur = tpos >> shift
    forced = (blk == 0) | (blk == cur) | (blk == cur - 1)
    valid = (blk * L_SEL <= tpos) & (blk < n_blocks)
    score = jnp.where(valid, jnp.where(forced, FORCE, imp_t), -FORCE)
    chosen_t = ranked(score, blk, valid, lambda s: score[s:s + 1, :])
    chosen_t = jnp.concatenate([chosen_t, jnp.zeros((LANES - nbp, rows), F32)], axis=0)
    return jnp.concatenate([chosen_t[:, i * LANES:(i + 1) * LANES].T for i in range(tiles)],
                           axis=0).astype(BF16)


def _gate_and_store(gates, zs, branch_outs, o_ref, row0=0):
    tq = gates.shape[0]
    lane = _lane_iota((tq, LANES))
    ext = []
    for g in range(NSA_KV_HEADS):
        o_cmp, o_sel, o_win = branch_outs[g]
        for hh in range(NSA_HPG):
            h = g * NSA_HPG + hh
            rows = slice(hh * tq, (hh + 1) * tq)
            o = (gates[:, 3 * h:3 * h + 1] * o_cmp[rows] + gates[:, 3 * h + 1:3 * h + 2] * o_sel[rows]
                 + gates[:, 3 * h + 2:3 * h + 3] * o_win[rows])
            ext.append(_swap_halves(o) if h % 2 != g else o)
    for j in range(NSA_HEADS // 2):
        both = jnp.where(lane < NSA_DH, ext[2 * j], ext[2 * j + 1])
        o_ref[row0:row0 + tq, j * LANES:(j + 1) * LANES] = both * zs[:, j * LANES:(j + 1) * LANES]


def _nsa_prompt_kernel(q_ref, g_ref, z_ref, cmp_ref, sel_ref, win_ref, ov_ref, ex_ref, o_ref, *, n_blocks):
    i = pl.program_id(1)
    tq = q_ref.shape[0]
    q = q_ref[...]
    t0 = i * tq
    tpos = t0 + _row_iota((tq, 1))
    tpos_all = jnp.concatenate([tpos] * NSA_HEADS, axis=0)
    grp_rows = NSA_HPG * tq
    qs = jnp.concatenate([_stack_group_queries(q, g) for g in range(NSA_KV_HEADS)], axis=0)

    cmp_blk = cmp_ref[0].astype(BF16)
    cmp_ok = (_lane_iota((1, cmp_blk.shape[0])) * CMP_STRIDE + (L_CMP - 1)) <= tpos_all
    p1 = _masked_softmax(_dot_nt(qs, cmp_blk[:, :LANES]), cmp_ok)
    o_cmp = _dot(p1.astype(BF16), cmp_blk[:, LANES:])
    psum = jnp.concatenate([_sum_heads(p1[g * grp_rows:(g + 1) * grp_rows], tq)
                            for g in range(NSA_KV_HEADS)], axis=0)
    chosen = _choose_blocks(psum, t0, tq, ov_ref[...], n_blocks)

    def sel_chunk(c, carry):
        m, l, acc = carry
        k0 = pl.multiple_of(c * SEL_CHUNK, SEL_CHUNK)
        k_t = sel_ref[0, 0:LANES, pl.ds(k0, SEL_CHUNK)].astype(BF16)
        v_t = sel_ref[0, LANES:2 * LANES, pl.ds(k0, SEL_CHUNK)].astype(BF16)
        causal = jnp.concatenate([(k0 + _lane_iota((1, SEL_CHUNK))) <= tpos] * NSA_KV_HEADS, axis=0)
        key_on = (_dot(chosen, ex_ref[:, pl.ds(k0, SEL_CHUNK)]) > 0.5) & causal
        bias = jnp.where(key_on, 0.0, NEG_INF)
        bias = jnp.concatenate([bias[g * tq:(g + 1) * tq] for g in range(NSA_KV_HEADS)
                                for _ in range(NSA_HPG)], axis=0)
        s = _dot(qs, k_t) + bias
        m_new = jnp.maximum(m, jnp.max(s, axis=-1, keepdims=True))
        alpha = jnp.exp(m - m_new)
        e = jnp.exp(s - m_new)
        l_new = alpha * l + jnp.sum(e, axis=-1, keepdims=True)
        return m_new, l_new, alpha * acc + _dot_nt(e.astype(BF16), v_t)

    rows = NSA_HEADS * tq
    init = (jnp.full((rows, 1), NEG_INF, F32), jnp.zeros((rows, 1), F32), jnp.zeros((rows, LANES), F32))
    n_chunks = t0 // SEL_CHUNK + 1
    _, l2, acc2 = lax.fori_loop(0, n_chunks, sel_chunk, init)
    o_sel = acc2 / l2

    w0 = pl.multiple_of(jnp.maximum(t0 - WINDOW, 0), KEY_TILE)
    wk_t = win_ref[0, 0:LANES, pl.ds(w0, WIN_KEYS)].astype(BF16)
    wv_t = win_ref[0, LANES:2 * LANES, pl.ds(w0, WIN_KEYS)].astype(BF16)
    win_kpos = w0 + _lane_iota((1, WIN_KEYS))
    win_ok = (win_kpos <= tpos_all) & (win_kpos > tpos_all - WINDOW)
    s3 = jnp.where(win_ok, _dot(qs, wk_t), NEG_INF)
    e3 = jnp.exp(s3 - jnp.max(s3, axis=-1, keepdims=True))
    o_win = _dot_nt(e3.astype(BF16), wv_t) / jnp.sum(e3, axis=-1, keepdims=True)

    branch_outs = [(o_cmp[g * grp_rows:(g + 1) * grp_rows], o_sel[g * grp_rows:(g + 1) * grp_rows],
                    o_win[g * grp_rows:(g + 1) * grp_rows]) for g in range(NSA_KV_HEADS)]
    _gate_and_store(g_ref[...], z_ref[...], branch_outs, o_ref)


def _nsa_prompt(qn, gates, zb, cmp_kv, full_t, win_t, overlap, expand, batch, seq):
    tq = KEY_TILE
    nq = seq // tq
    tok = lambda b, i: (b * nq + i, 0)
    per_b = lambda b, i: (b, 0, 0)
    sel_half = lambda b, i: (b, 1, 0)
    fixed = lambda b, i: (0, 0)
    n_blocks = -(-seq // L_SEL)
    return pl.pallas_call(
        functools.partial(_nsa_prompt_kernel, n_blocks=n_blocks),
        grid=(batch, nq),
        in_specs=[
            pl.BlockSpec((tq, B_WIDTH), tok),
            pl.BlockSpec((tq, LANES), tok),
            pl.BlockSpec((tq, B_WIDTH), tok),
            pl.BlockSpec((1, cmp_kv.shape[1], 2 * LANES), per_b),
            pl.BlockSpec((1, 2 * LANES, seq), sel_half),
            pl.BlockSpec((1, WIN_ROW, seq), per_b),
            pl.BlockSpec(overlap.shape, fixed),
            pl.BlockSpec(expand.shape, fixed),
        ],
        out_specs=pl.BlockSpec((tq, B_WIDTH), tok),
        out_shape=jax.ShapeDtypeStruct((batch * seq, B_WIDTH), F32),
        compiler_params=_cparams("parallel", "arbitrary"),
        name="nsa_prompt",
    )(qn, gates, zb, cmp_kv, full_t, win_t, overlap, expand)


def _nsa_sample_kernel(*refs, nb, n_pages, past_len, n_blocks):
    q_ref, g_ref, z_ref, newf_ref, neww_ref, winp_ref = refs[1:7]
    pages = refs[7:7 + nb * n_pages]
    pe_ref, w_ref, ov_ref, ex_ref, o_ref, wout_ref, xs_ref = refs[7 + nb * n_pages:]
    tq = q_ref.shape[0] // nb
    w_buf = winp_ref.shape[3]
    halves_per_page = PAGE_SIZE // CMP_STRIDE
    n_half = past_len // CMP_STRIDE
    total_half = nb * n_half

    for j in range(nb * n_pages):
        for c in range(2):
            x = pages[j][0, c].T
            for hb in range(halves_per_page):
                slot = (j * halves_per_page + hb) * CMP_SLOT
                xs_ref[c, slot:slot + CMP_STRIDE, :] = x[hb * CMP_STRIDE:(hb + 1) * CMP_STRIDE]

    acc = [jnp.zeros((total_half, 2 * LANES), F32) for _ in range(CMP_RATIO)]
    for l in range(CMP_STRIDE):
        x = jnp.concatenate([xs_ref[c, pl.ds(l, total_half, stride=CMP_SLOT), :] for c in range(2)], axis=1)
        for r in range(CMP_RATIO):
            i = r * CMP_STRIDE + l
            acc[r] = acc[r] + _dot((x + pe_ref[i:i + 1, :]).astype(BF16), w_ref[i])
    cmp_all = acc[0]
    for r in range(1, CMP_RATIO):
        cmp_all = cmp_all + pltpu.roll(acc[r], total_half - r, axis=0)
    live = _row_iota((n_half, 2 * LANES)) < n_half - CMP_RATIO + 1

    pad = jnp.zeros((KEY_TILE - tq, LANES), F32)
    lane = _lane_iota((LANES, LANES))
    per_req = NSA_HEADS * tq
    tpos = past_len + _row_iota((tq, 1))
    tpos_all = jnp.concatenate([tpos] * (nb * NSA_HEADS), axis=0)
    tpos_grp = jnp.concatenate([tpos] * (nb * NSA_KV_HEADS), axis=0)

    def padded_rows(x):
        return jnp.concatenate([x, pad], axis=0)

    def per_request(fn):
        return jnp.concatenate([fn(j, slice(j * per_req, (j + 1) * per_req)) for j in range(nb)], axis=0)

    tok = [slice(j * tq, (j + 1) * tq) for j in range(nb)]
    qs = jnp.concatenate([_stack_group_queries(q_ref[tok[j], :], g)
                          for j in range(nb) for g in range(NSA_KV_HEADS)], axis=0)
    cmp_blk = [jnp.where(live, cmp_all[j * n_half:(j + 1) * n_half], 0.0).astype(BF16) for j in range(nb)]
    mine = [pages[j * n_pages:(j + 1) * n_pages] for j in range(nb)]
    newf = [newf_ref[tok[j], :] for j in range(nb)]
    neww = [neww_ref[tok[j], :] for j in range(nb)]

    cmp_ok = (_lane_iota((1, n_half)) * CMP_STRIDE + (L_CMP - 1)) <= tpos_all
    p1 = _masked_softmax(per_request(lambda j, r: _dot_nt(qs[r], cmp_blk[j][:, :LANES])), cmp_ok)
    p1b = p1.astype(BF16)
    o_cmp = per_request(lambda j, r: _dot(p1b[r], cmp_blk[j][:, LANES:]))

    grp_rows = NSA_HPG * tq
    psum = jnp.concatenate([_sum_heads(p1[i * grp_rows:(i + 1) * grp_rows], tq)
                            for i in range(nb * NSA_KV_HEADS)], axis=0)
    chosen = _choose_blocks(psum, past_len, tq, ov_ref[...], n_blocks)
    key_on = (_dot(chosen, ex_ref[...]) > 0.5) & (_lane_iota((1, past_len + KEY_TILE)) <= tpos_grp)
    bias = jnp.where(key_on, 0.0, NEG_INF)
    bias = jnp.concatenate([bias[i * tq:(i + 1) * tq] for i in range(nb * NSA_KV_HEADS)
                            for _ in range(NSA_HPG)], axis=0)

    def sel_scores(j, r):
        k_t = jnp.concatenate([p[0, 2].astype(BF16) for p in mine[j]], axis=1)
        k_new = padded_rows(newf[j][:, 2 * LANES:3 * LANES]).astype(BF16)
        return jnp.concatenate([_dot(qs[r], k_t), _dot_nt(qs[r], k_new)], axis=1)

    s2 = per_request(sel_scores) + bias
    e2 = jnp.exp(s2 - jnp.max(s2, axis=-1, keepdims=True))
    l2 = jnp.sum(e2, axis=-1, keepdims=True)
    e2 = e2.astype(BF16)

    def sel_values(j, r):
        v_t = jnp.concatenate([p[0, 3].astype(BF16) for p in mine[j]], axis=1)
        v_new = padded_rows(newf[j][:, 3 * LANES:4 * LANES]).astype(BF16)
        return _dot_nt(e2[r, :past_len], v_t) + _dot(e2[r, past_len:], v_new)

    o_sel = per_request(sel_values) / l2

    win_kpos = (past_len - w_buf) + _lane_iota((1, w_buf + KEY_TILE))
    win_ok = (win_kpos <= tpos_all) & (win_kpos > tpos_all - WINDOW)
    s3 = per_request(lambda j, r: jnp.concatenate(
        [_dot(qs[r], winp_ref[j, 0].astype(BF16)),
         _dot_nt(qs[r], padded_rows(neww[j][:, :LANES]).astype(BF16))], axis=1))
    s3 = jnp.where(win_ok, s3, NEG_INF)
    e3 = jnp.exp(s3 - jnp.max(s3, axis=-1, keepdims=True))
    l3 = jnp.sum(e3, axis=-1, keepdims=True)
    e3 = e3.astype(BF16)
    o_win = per_request(lambda j, r: _dot_nt(e3[r, :w_buf], winp_ref[j, 1].astype(BF16))
                        + _dot(e3[r, w_buf:], padded_rows(neww[j][:, LANES:]).astype(BF16))) / l3

    for j in range(nb):
        branch_outs = []
        for g in range(NSA_KV_HEADS):
            r = slice((j * NSA_KV_HEADS + g) * grp_rows, (j * NSA_KV_HEADS + g + 1) * grp_rows)
            branch_outs.append((o_cmp[r], o_sel[r], o_win[r]))
        _gate_and_store(g_ref[tok[j], :], z_ref[tok[j], :], branch_outs, o_ref, row0=j * tq)

        for c in range(2):
            shifted = pltpu.roll(winp_ref[j, c], w_buf - tq, axis=1)
            new_cols = pltpu.roll(padded_rows(neww[j][:, c * LANES:(c + 1) * LANES]).T, LANES - tq, axis=1)
            wout_ref[j, c, :, 0:w_buf - LANES] = shifted[:, 0:w_buf - LANES]
            wout_ref[j, c, :, w_buf - LANES:w_buf] = jnp.where(lane >= LANES - tq, new_cols,
                                                               shifted[:, w_buf - LANES:w_buf])


def _nsa_sample(page_table, qn, gates, zb, full_new, win_new, win_t, cache_t, pe, w_bd, overlap, expand,
                batch, tq, n_pages):
    nb = SAMPLE_REQS_PER_STEP
    past_len = n_pages * PAGE_SIZE
    w_buf = win_t.shape[3]
    n_blocks = -(-(past_len + tq) // L_SEL)
    tok = lambda b, pt: (b, 0)
    per_b = lambda b, pt: (b, 0, 0, 0)
    fixed2 = lambda b, pt: (0, 0)
    fixed3 = lambda b, pt: (0, 0, 0)

    def page_spec(j, p):
        return pl.BlockSpec((1, 4, LANES, PAGE_SIZE), lambda b, pt: (pt[b * nb + j, p], 0, 0, 0))

    n_slots = nb * n_pages * (PAGE_SIZE // CMP_STRIDE)
    return pl.pallas_call(
        functools.partial(_nsa_sample_kernel, nb=nb, n_pages=n_pages, past_len=past_len, n_blocks=n_blocks),
        grid_spec=pltpu.PrefetchScalarGridSpec(
            num_scalar_prefetch=1,
            grid=(batch // nb,),
            in_specs=[
                pl.BlockSpec((nb * tq, B_WIDTH), tok),
                pl.BlockSpec((nb * tq, LANES), tok),
                pl.BlockSpec((nb * tq, B_WIDTH), tok),
                pl.BlockSpec((nb * tq, KV_ROW), tok),
                pl.BlockSpec((nb * tq, WIN_ROW), tok),
                pl.BlockSpec((nb, 2, LANES, w_buf), per_b),
            ] + [page_spec(j, p) for j in range(nb) for p in range(n_pages)] + [
                pl.BlockSpec(pe.shape, fixed2),
                pl.BlockSpec(w_bd.shape, fixed3),
                pl.BlockSpec(overlap.shape, fixed2),
                pl.BlockSpec(expand.shape, fixed2),
            ],
            out_specs=[
                pl.BlockSpec((nb * tq, B_WIDTH), tok),
                pl.BlockSpec((nb, 2, LANES, w_buf), per_b),
            ],
            scratch_shapes=[pltpu.VMEM((2, n_slots * CMP_SLOT, LANES), F32)],
        ),
        out_shape=[
            jax.ShapeDtypeStruct((batch * tq, B_WIDTH), F32),
            jax.ShapeDtypeStruct((batch, 2, LANES, w_buf), F32),
        ],
        compiler_params=_cparams("parallel"),
        name="nsa_sample",
    )(page_table, qn, gates, zb, full_new, win_new, win_t, *([cache_t] * (nb * n_pages)), pe, w_bd,
      overlap, expand)


def _selection_tables(n_cmp_rows, n_sel_keys):
    n = jnp.arange(n_cmp_rows, dtype=jnp.int32)[:, None]
    s = jnp.arange(LANES, dtype=jnp.int32)[None, :]
    c_start = n * CMP_STRIDE
    s_start = s * L_SEL
    overlap = ((c_start < s_start + L_SEL) & (s_start < c_start + L_CMP)).astype(BF16)
    key = jnp.arange(n_sel_keys, dtype=jnp.int32)[None, :]
    expand = ((key // L_SEL) == jnp.arange(LANES, dtype=jnp.int32)[:, None]).astype(BF16)
    return overlap, expand


def _even_out_kernel(x_ref, oa_ref, ob_ref, wa_ref, wb_ref, *rest, with_next):
    y = (x_ref[...] + _dot(oa_ref[...].astype(BF16), wa_ref[...])
         + _dot(ob_ref[...].astype(BF16), wb_ref[...]))
    if not with_next:
        (y_ref,) = rest
        y_ref[...] = y
        return
    g_ref, w_ref, y_ref, u_ref, z_ref = rest
    y_ref[...] = y
    hb = _rmsnorm_rows(y, g_ref[...]).astype(BF16)
    e = u_ref.shape[1]
    u_ref[...] = _dot(hb, w_ref[:, :e])
    z_ref[...] = jax.nn.silu(_dot(hb, w_ref[:, e:]))


def _even_out(x2d, oa, ob, wa, wb, next_odd=None):
    m, d = x2d.shape
    row = lambda i: (i, 0)
    fixed = lambda i: (0, 0)
    in_specs = [
        pl.BlockSpec((ROW_TILE, d), row),
        pl.BlockSpec((ROW_TILE, A_WIDTH), row),
        pl.BlockSpec((ROW_TILE, B_WIDTH), row),
        pl.BlockSpec((A_WIDTH, d), fixed),
        pl.BlockSpec((B_WIDTH, d), fixed),
    ]
    out_specs = [pl.BlockSpec((ROW_TILE, d), row)]
    out_shape = [jax.ShapeDtypeStruct((m, d), F32)]
    args = [x2d, oa, ob, wa, wb]
    if next_odd is not None:
        gain, w_bf = next_odd
        e = w_bf.shape[1] // 2
        in_specs += [pl.BlockSpec((1, d), fixed), pl.BlockSpec((d, 2 * e), fixed)]
        out_specs += [pl.BlockSpec((ROW_TILE, e), row)] * 2
        out_shape += [jax.ShapeDtypeStruct((m, e), F32)] * 2
        args += [gain, w_bf]
    return pl.pallas_call(
        functools.partial(_even_out_kernel, with_next=next_odd is not None),
        grid=(m // ROW_TILE,),
        in_specs=in_specs,
        out_specs=out_specs,
        out_shape=out_shape,
        compiler_params=_cparams("parallel"),
        name="even_out_proj",
    )(*args)


def _s5_kernel(u_ref, x0r_ref, x0i_ref, ar_ref, ai_ref, bm_ref, cm_ref, d_ref,
               y_ref, fr_ref, fi_ref, st_re, st_im, ubuf, xbuf, *, nb, tt):
    t_idx = pl.program_id(1)
    n_blk = bm_ref.shape[0]
    half = S5_BLOCK_STATE
    n_chunk = half // LANES
    pitch = tt + S5_ROW_PAD

    @pl.when(t_idx == 0)
    def _():
        for kb in range(n_blk):
            st_re[kb] = x0r_ref[:, kb * half:(kb + 1) * half]
            st_im[kb] = x0i_ref[:, kb * half:(kb + 1) * half]

    for b in range(nb):
        ubuf[b * pitch + tt:(b + 1) * pitch, :] = jnp.zeros((S5_ROW_PAD, LANES), F32)

    for kb in range(n_blk):
        lanes = slice(kb * LANES, (kb + 1) * LANES)
        for b in range(nb):
            ubuf[b * pitch:b * pitch + tt, :] = u_ref[b, :, lanes]
        u_blk = ubuf[...]
        bu = _dot(u_blk.astype(BF16), bm_ref[kb])
        for j in range(2 * n_chunk):
            xbuf[j] = bu[:, j * LANES:(j + 1) * LANES]
        a_re = jnp.broadcast_to(ar_ref[kb], (nb, half))
        a_im = jnp.broadcast_to(ai_ref[kb], (nb, half))

        def step(t, carry):
            s_re, s_im = carry
            rows = pl.ds(t, nb, stride=pitch)
            b_re = jnp.concatenate([xbuf[j, rows, :] for j in range(n_chunk)], axis=1)
            b_im = jnp.concatenate([xbuf[n_chunk + j, rows, :] for j in range(n_chunk)], axis=1)
            n_re = a_re * s_re - a_im * s_im + b_re
            n_im = a_re * s_im + a_im * s_re + b_im
            for j in range(n_chunk):
                xbuf[j, rows, :] = n_re[:, j * LANES:(j + 1) * LANES]
                xbuf[n_chunk + j, rows, :] = n_im[:, j * LANES:(j + 1) * LANES]
            return n_re, n_im

        s_re, s_im = lax.fori_loop(0, tt, step, (st_re[kb], st_im[kb]), unroll=S5_SCAN_UNROLL)
        st_re[kb] = s_re
        st_im[kb] = s_im
        states = jnp.concatenate([xbuf[j] for j in range(2 * n_chunk)], axis=1)
        ubuf[...] = _dot(states.astype(BF16), cm_ref[kb]) + d_ref[:, lanes] * u_blk
        for b in range(nb):
            y_ref[b, :, lanes] = ubuf[b * pitch:b * pitch + tt, :]

    @pl.when(t_idx == pl.num_programs(1) - 1)
    def _():
        for kb in range(n_blk):
            fr_ref[:, kb * half:(kb + 1) * half] = st_re[kb]
            fi_ref[:, kb * half:(kb + 1) * half] = st_im[kb]


def _s5(u3, x0_re, x0_im, a_re, a_im, bmat, cmat, d_row, tt):
    batch, seq, e = u3.shape
    nb = SUBLANES
    n_blk = e // LANES
    n_state = n_blk * S5_BLOCK_STATE
    seq_map = lambda b, t: (b, t, 0)
    st_map = lambda b, t: (b, 0)
    fixed2 = lambda b, t: (0, 0)
    fixed3 = lambda b, t: (0, 0, 0)
    return pl.pallas_call(
        functools.partial(_s5_kernel, nb=nb, tt=tt),
        grid=(batch // nb, seq // tt),
        in_specs=[
            pl.BlockSpec((nb, tt, e), seq_map),
            pl.BlockSpec((nb, n_state), st_map),
            pl.BlockSpec((nb, n_state), st_map),
            pl.BlockSpec((n_blk, 1, S5_BLOCK_STATE), fixed3),
            pl.BlockSpec((n_blk, 1, S5_BLOCK_STATE), fixed3),
            pl.BlockSpec((n_blk, LANES, 2 * S5_BLOCK_STATE), fixed3),
            pl.BlockSpec((n_blk, 2 * S5_BLOCK_STATE, LANES), fixed3),
            pl.BlockSpec((1, e), fixed2),
        ],
        out_specs=[
            pl.BlockSpec((nb, tt, e), seq_map),
            pl.BlockSpec((nb, n_state), st_map),
            pl.BlockSpec((nb, n_state), st_map),
        ],
        out_shape=[
            jax.ShapeDtypeStruct((batch, seq, e), F32),
            jax.ShapeDtypeStruct((batch, n_state), F32),
            jax.ShapeDtypeStruct((batch, n_state), F32),
        ],
        scratch_shapes=[
            pltpu.VMEM((n_blk, nb, S5_BLOCK_STATE), F32),
            pltpu.VMEM((n_blk, nb, S5_BLOCK_STATE), F32),
            pltpu.VMEM((nb * (tt + S5_ROW_PAD), LANES), F32),
            pltpu.VMEM((2 * S5_BLOCK_STATE // LANES, nb * (tt + S5_ROW_PAD), LANES), F32),
        ],
        compiler_params=_cparams("parallel", "arbitrary"),
        name="s5_scan",
    )(u3, x0_re, x0_im, a_re, a_im, bmat, cmat, d_row)


def _s5_params(lam_re, lam_im, b_re, b_im, c_re, c_im, log_step):
    n_groups = lam_re.shape[0]
    n_blk = n_groups // S5_LANE_GROUPS
    dt = jnp.exp(log_step)[:, None]
    mag = jnp.exp(lam_re * dt)
    ang = lam_im * dt
    ab_re, ab_im = mag * jnp.cos(ang), mag * jnp.sin(ang)
    den = lam_re * lam_re + lam_im * lam_im
    nr = ab_re - 1.0
    f_re = (nr * lam_re + ab_im * lam_im) / den
    f_im = (ab_im * lam_re - nr * lam_im) / den
    bb_re = f_re[..., None] * b_re - f_im[..., None] * b_im
    bb_im = f_re[..., None] * b_im + f_im[..., None] * b_re
    eye = jnp.eye(S5_LANE_GROUPS, dtype=lam_re.dtype)

    def in_map(bb):
        bb = bb.reshape(n_blk, S5_LANE_GROUPS, S5_STATE, S5_GROUP)
        m = jnp.einsum('kgpc,gh->kgchp', bb, eye)
        return m.reshape(n_blk, LANES, S5_BLOCK_STATE)

    def out_map(cc):
        cc = cc.reshape(n_blk, S5_LANE_GROUPS, S5_GROUP, S5_STATE)
        m = jnp.einsum('kgcp,gh->kgphc', cc, eye)
        return m.reshape(n_blk, S5_BLOCK_STATE, LANES)

    bmat = jnp.concatenate([in_map(bb_re), in_map(bb_im)], axis=2).astype(BF16)
    cmat = jnp.concatenate([out_map(c_re), out_map(-c_im)], axis=1).astype(BF16)
    a_re = ab_re.reshape(n_blk, 1, S5_BLOCK_STATE)
    a_im = ab_im.reshape(n_blk, 1, S5_BLOCK_STATE)
    return a_re, a_im, bmat, cmat


def _odd_out_kernel(x_ref, y_ref, z_ref, w1_ref, w2_ref, wo_ref, o_ref):
    yb = jax.nn.gelu(y_ref[...]).astype(BF16)
    t = _dot(yb, w1_ref[...]) * jax.nn.sigmoid(_dot(yb, w2_ref[...])) * z_ref[...]
    o_ref[...] = x_ref[...] + _dot(t.astype(BF16), wo_ref[...])


def _odd_out(x2d, y2d, z2d, w1, w2, wo):
    m, d = x2d.shape
    e = y2d.shape[1]
    row = lambda i: (i, 0)
    fixed = lambda i: (0, 0)
    return pl.pallas_call(
        _odd_out_kernel,
        grid=(m // ROW_TILE,),
        in_specs=[
            pl.BlockSpec((ROW_TILE, d), row),
            pl.BlockSpec((ROW_TILE, e), row),
            pl.BlockSpec((ROW_TILE, e), row),
            pl.BlockSpec((e, e), fixed),
            pl.BlockSpec((e, e), fixed),
            pl.BlockSpec((e, d), fixed),
        ],
        out_specs=pl.BlockSpec((ROW_TILE, d), row),
        out_shape=jax.ShapeDtypeStruct((m, d), F32),
        compiler_params=_cparams("parallel"),
        name="odd_out_proj",
    )(x2d, y2d, z2d, w1, w2, wo)


def _rope_tables(pos):
    half = NSA_DH // 2
    inv = ROPE_THETA ** (-jnp.arange(half, dtype=F32) / half)
    ang = pos.astype(F32)[:, None] * inv[None, :]
    cos, sin = jnp.cos(ang), jnp.sin(ang)
    reps = LANES // NSA_DH
    return jnp.tile(cos, (1, 2 * reps)), jnp.tile(jnp.concatenate([-sin, sin], axis=1), (1, reps))


def _even_weights(norm_g, w_in, w_out, q_norm, k_norm, cmp_pos, cmp_w):
    d = w_in.shape[0]
    sizes = (RET_HEADS * RET_DK, RET_HEADS * RET_DK, A_WIDTH, A_WIDTH, B_WIDTH, 6 * NSA_KV_HEADS * NSA_DH,
             N_GATES, B_WIDTH)
    parts, o = [], 0
    for s in sizes:
        parts.append(w_in[:, o:o + s])
        o += s
    qa, ka, va, za, qn, kvb, gl, zb = parts
    w_perm = jnp.concatenate([qa, ka, va, za, qn, kvb, zb, gl, jnp.zeros((d, LANES - N_GATES), w_in.dtype)],
                             axis=1).astype(BF16)
    reps = LANES // NSA_DH
    qg = jnp.tile(q_norm[None, :], (1, reps))
    kg = jnp.tile(k_norm, (1, reps))
    w_bd = jnp.zeros((L_CMP, 2 * LANES, 2 * LANES), BF16)
    for c in range(2):
        for g in range(NSA_KV_HEADS):
            r0 = (c * NSA_KV_HEADS + g) * NSA_DH
            w_bd = w_bd.at[:, r0:r0 + NSA_DH, r0:r0 + NSA_DH].set(cmp_w[c].astype(BF16))
    pe = jnp.broadcast_to(cmp_pos.transpose(1, 0, 2)[:, :, None, :], (L_CMP, 2, NSA_KV_HEADS, NSA_DH))
    pe = pe.reshape(L_CMP, 2 * LANES)
    wa = w_out[:A_WIDTH].astype(BF16)
    wb = w_out[A_WIDTH:].astype(BF16)
    return norm_g[None, :], w_perm, qg, kg, w_bd, pe, wa, wb


def _even_layer(x, pos0, s_ret, ew, gn_gain, next_odd, cache=None, page_table=None, win_past=None):
    batch, seq, d = x.shape
    gain, w_perm, qg, kg, w_bd, pe, wa, wb = ew
    x2d = x.reshape(batch * seq, d)
    pos = pos0 + jnp.arange(seq, dtype=jnp.int32)
    cos_t, sin_t = _rope_tables(pos)
    if seq >= EVEN_IN_TILE:
        n_table_blocks = seq // EVEN_IN_TILE
    else:
        cos_t = jnp.tile(cos_t, (EVEN_IN_TILE // seq, 1))
        sin_t = jnp.tile(sin_t, (EVEN_IN_TILE // seq, 1))
        n_table_blocks = 1
    g, dh = NSA_KV_HEADS, NSA_DH
    if cache is None:
        qa, ka, va, za, qn, full_t, win_t, zb, gates, cmp_rows = _even_in(
            x2d, gain, w_perm, cos_t, sin_t, qg, kg, n_table_blocks, seq_tiles=seq // EVEN_IN_TILE)
        oa, s_fin = _retention(qa, ka, va, za, gn_gain[None, :], s_ret, batch, seq)
        cmp3 = cmp_rows.reshape(batch, seq, 2 * LANES)
        n_half = seq // CMP_STRIDE
        src_specs = [pl.BlockSpec((1, seq, LANES), lambda b: (b, 0, 0)),
                     pl.BlockSpec((1, seq, LANES), lambda b: (b, 0, 1))]
        cmp_kv = _compress([cmp3, cmp3], src_specs, (batch,), (), pe, w_bd, batch, seq, n_half)
        overlap, expand = _selection_tables(n_half, seq)
        ob = _nsa_prompt(qn, gates, zb, cmp_kv, full_t, win_t, overlap, expand, batch, seq)
        keep = min(WINDOW, seq)
        y, *uz = _even_out(x2d, oa, ob, wa, wb, next_odd)
        full_rows = full_t.reshape(batch, 4, g, dh, seq).transpose(0, 4, 1, 2, 3)
        win_rows = win_t[:, :, seq - keep:].reshape(batch, 2, g, dh, keep).transpose(0, 4, 1, 2, 3)
        return y.reshape(batch, seq, d), s_fin, full_rows, win_rows, uz
    else:
        qa, ka, va, za, qn, full_new, win_new, zb, gates = _even_in(x2d, gain, w_perm, cos_t, sin_t, qg, kg,
                                                                     n_table_blocks)
        oa, s_fin = _retention(qa, ka, va, za, gn_gain[None, :], s_ret, batch, seq)
        n_pages = page_table.shape[1]
        past_len = n_pages * PAGE_SIZE
        overlap, expand = _selection_tables(past_len // CMP_STRIDE, past_len + KEY_TILE)
        ob, win_out_t = _nsa_sample(page_table, qn, gates, zb, full_new, win_new, win_past, cache, pe, w_bd,
                                    overlap, expand, batch, seq, n_pages)
    y, *uz = _even_out(x2d, oa, ob, wa, wb, next_odd)
    w_buf = win_out_t.shape[3]
    return (y.reshape(batch, seq, d), s_fin, full_new.reshape(batch, seq, 4, g, dh),
            win_out_t.reshape(batch, 2, g, dh, w_buf).transpose(0, 4, 1, 2, 3), uz)


def _odd_layer(x, uz, s_re, s_im, s5p, d_row, w1, w2, wo, tt):
    batch, seq, d = x.shape
    x2d = x.reshape(batch * seq, d)
    u, zs = uz
    e = u.shape[1]
    a_re, a_im, bmat, cmat = s5p
    n_groups, n_state = s_re.shape[1], s_re.shape[2]
    y, f_re, f_im = _s5(u.reshape(batch, seq, e), s_re.reshape(batch, n_groups * n_state),
                        s_im.reshape(batch, n_groups * n_state), a_re, a_im, bmat, cmat, d_row, tt)
    out = _odd_out(x2d, y.reshape(batch * seq, e), zs, w1, w2, wo).reshape(batch, seq, d)
    return out, f_re.reshape(batch, n_groups, n_state), f_im.reshape(batch, n_groups, n_state)


S5_TIME_TILE = 128


def kernel(x_prompt, x_sample, cache_nsa_kv, cache_nsa_win, state_ret, state_ssm_re, state_ssm_im, page_table,
           norm_even, w_in_even, w_out_even, ret_gn_gain, nsa_q_norm, nsa_k_norm, nsa_cmp_pos, nsa_cmp_w,
           norm_odd, w_in_odd, ssm_lambda_re, ssm_lambda_im, ssm_b_re, ssm_b_im, ssm_c_re, ssm_c_im, ssm_d,
           ssm_log_step, glu_w1, glu_w2, w_out_odd):
    bp, seq_p, _ = x_prompt.shape
    db, seq_s, _ = x_sample.shape
    n_pages = page_table.shape[1]
    past_len = n_pages * PAGE_SIZE
    depth = norm_even.shape[0] + norm_odd.shape[0]
    yp, ys = x_prompt, x_sample
    ret_p, ret_s, kv_p, kv_s, win_p, win_s = [], [], [], [], [], []
    sre_p, sim_p, sre_s, sim_s = [], [], [], []
    for layer in range(depth):
        li = layer // 2
        if layer % 2 == 0:
            ew = _even_weights(norm_even[li], w_in_even[li], w_out_even[li], nsa_q_norm[li], nsa_k_norm[li],
                               nsa_cmp_pos[li], nsa_cmp_w[li])
            next_odd = None
            if layer + 1 < depth:
                next_odd = (norm_odd[li][None, :], w_in_odd[li].astype(BF16))
            s0 = jnp.zeros((bp, RET_HEADS, RET_DK, RET_DV), F32)
            yp, sr, kvr, wr, uz_p = _even_layer(yp, 0, s0, ew, ret_gn_gain[li], next_odd)
            ret_p.append(sr); kv_p.append(kvr); win_p.append(wr)
            cache_t = cache_nsa_kv[li].transpose(0, 2, 3, 4, 1).reshape(cache_nsa_kv.shape[1], 4, LANES, PAGE_SIZE)
            win_t = cache_nsa_win[li].transpose(0, 2, 3, 4, 1).reshape(db, 2, LANES, cache_nsa_win.shape[2])
            ys, sr2, kvr2, wr2, uz_s = _even_layer(ys, past_len, state_ret[li], ew, ret_gn_gain[li], next_odd,
                                                   cache=cache_t, page_table=page_table, win_past=win_t)
            ret_s.append(sr2); kv_s.append(kvr2); win_s.append(wr2)
        else:
            s5p = _s5_params(ssm_lambda_re[li], ssm_lambda_im[li], ssm_b_re[li], ssm_b_im[li],
                             ssm_c_re[li], ssm_c_im[li], ssm_log_step[li])
            w1, w2, wo = glu_w1[li].astype(BF16), glu_w2[li].astype(BF16), w_out_odd[li].astype(BF16)
            d_row = ssm_d[li][None, :]
            n_groups = ssm_lambda_re.shape[1]
            z0 = jnp.zeros((bp, n_groups, S5_STATE), F32)
            yp, fr, fi = _odd_layer(yp, uz_p, z0, z0, s5p, d_row, w1, w2, wo, min(S5_TIME_TILE, seq_p))
            sre_p.append(fr); sim_p.append(fi)
            ys, fr2, fi2 = _odd_layer(ys, uz_s, state_ssm_re[li], state_ssm_im[li], s5p, d_row,
                                      w1, w2, wo, min(S5_TIME_TILE, seq_s))
            sre_s.append(fr2); sim_s.append(fi2)
    return (yp, ys, jnp.stack(ret_p), jnp.stack(ret_s), jnp.stack(kv_p), jnp.stack(kv_s), jnp.stack(win_p),
            jnp.stack(win_s), jnp.stack(sre_p), jnp.stack(sim_p), jnp.stack(sre_s), jnp.stack(sim_s))
```

```python
import functools
import math

import jax
import jax.numpy as jnp
from jax import lax
from jax.experimental import pallas as pl
from jax.experimental.pallas import tpu as pltpu

F32 = jnp.float32
BF16 = jnp.bfloat16

LANES = 128
SUBLANES = 8
VMEM_LIMIT_BYTES = 48 * 2**20

EPS = 1e-6
ROPE_THETA = 10000.0
NEG_INF = -1e30
FORCE = 1e4

RET_HEADS = 4
RET_DK = 64
RET_DV = 128
RET_CHUNK = 128
RET_TILES_PER_STEP = 4
A_WIDTH = RET_HEADS * RET_DV

NSA_HEADS = 8
NSA_KV_HEADS = 2
NSA_DH = 64
NSA_HPG = NSA_HEADS // NSA_KV_HEADS
B_WIDTH = NSA_HEADS * NSA_DH
L_CMP = 32
CMP_STRIDE = 16
CMP_RATIO = L_CMP // CMP_STRIDE
L_SEL = 64
N_SEL = 8
WINDOW = 512
PAGE_SIZE = 128
KV_ROW = 4 * NSA_KV_HEADS * NSA_DH
WIN_ROW = 2 * NSA_KV_HEADS * NSA_DH
KEY_TILE = 128
WIN_KEYS = WINDOW + KEY_TILE
SEL_CHUNK = 512
CMP_SLOT = CMP_STRIDE + 4
SAMPLE_REQS_PER_STEP = 2

S5_GROUP = 16
S5_STATE = 64
S5_LANE_GROUPS = LANES // S5_GROUP
S5_BLOCK_STATE = S5_LANE_GROUPS * S5_STATE
S5_ROW_PAD = 4
S5_MXU_COLS = 256

QA0, KA0, VA0, ZA0, QN0, KVB0, ZB0, GL0 = 0, 256, 512, 1024, 1536, 2048, 2816, 3328
EVEN_COLS = GL0 + LANES
N_GATES = 3 * NSA_HEADS

ROW_TILE = 512
EVEN_IN_TILE = 256


def _cparams(*sem):
    return pltpu.CompilerParams(dimension_semantics=sem, vmem_limit_bytes=VMEM_LIMIT_BYTES)


def _lane_iota(shape):
    return lax.broadcasted_iota(jnp.int32, shape, len(shape) - 1)


def _row_iota(shape):
    return lax.broadcasted_iota(jnp.int32, shape, len(shape) - 2)


def _dot(a, b):
    return jnp.dot(a, b, preferred_element_type=F32)


def _dot_nt(a, b):
    return lax.dot_general(a, b, (((1,), (1,)), ((), ())), preferred_element_type=F32)


def _rmsnorm_rows(x, g):
    return x * lax.rsqrt(jnp.mean(x * x, axis=-1, keepdims=True) + EPS) * g


def _swap_halves(x):
    return pltpu.roll(x, NSA_DH, axis=1)


def _rope_block(x, cos, sin_signed):
    half = NSA_DH // 2
    lane = _lane_iota(x.shape)
    first = (lane % NSA_DH) < half
    partner = jnp.where(first, pltpu.roll(x, LANES - half, axis=1), pltpu.roll(x, half, axis=1))
    return x * cos + partner * sin_signed


def _head_rms_block(x, g):
    lane = _lane_iota(x.shape)
    lo = lane < NSA_DH
    sq = x * x
    s_lo = jnp.sum(jnp.where(lo, sq, 0.0), axis=-1, keepdims=True)
    s_hi = jnp.sum(jnp.where(lo, 0.0, sq), axis=-1, keepdims=True)
    ms = jnp.where(lo, s_lo, s_hi) * (1.0 / NSA_DH)
    return x * lax.rsqrt(ms + EPS) * g


def _even_in_kernel(x_ref, g_ref, w_ref, cos_ref, sin_ref, qg_ref, kg_ref, *out_refs, transposed_kv):
    if transposed_kv:
        qa_ref, ka_ref, va_ref, za_ref, qn_ref, full_ref, win_ref, zb_ref, gl_ref, cmp_ref = out_refs
    else:
        qa_ref, ka_ref, va_ref, za_ref, qn_ref, full_ref, win_ref, zb_ref, gl_ref = out_refs
    hb = _rmsnorm_rows(x_ref[...], g_ref[...]).astype(BF16)
    cos = cos_ref[...]
    sin = sin_ref[...]

    def proj(c0):
        return _dot(hb, w_ref[:, c0:c0 + LANES])

    for j in range(RET_HEADS * RET_DK // LANES):
        c = j * LANES
        qa_ref[:, c:c + LANES] = _rope_block(proj(QA0 + c), cos, sin)
        ka_ref[:, c:c + LANES] = _rope_block(proj(KA0 + c), cos, sin) * (RET_DK ** -0.5)
    for j in range(A_WIDTH // LANES):
        c = j * LANES
        va_ref[:, c:c + LANES] = proj(VA0 + c)
        za_ref[:, c:c + LANES] = jax.nn.silu(proj(ZA0 + c))
    for j in range(B_WIDTH // LANES):
        c = j * LANES
        qn_ref[:, c:c + LANES] = _rope_block(_head_rms_block(proj(QN0 + c), qg_ref[...]), cos, sin)
        zb_ref[:, c:c + LANES] = jax.nn.silu(proj(ZB0 + c))
    for j in range(6):
        y = proj(KVB0 + j * LANES)
        if j % 2 == 0:
            y = _rope_block(_head_rms_block(y, kg_ref[j // 2:j // 2 + 1, :]), cos, sin)
        if transposed_kv:
            if j < 4:
                full_ref[0, j * LANES:(j + 1) * LANES, :] = y.T
            else:
                win_ref[0, (j - 4) * LANES:(j - 3) * LANES, :] = y.T
            if j < 2:
                cmp_ref[:, j * LANES:(j + 1) * LANES] = y
        elif j < 4:
            full_ref[:, j * LANES:(j + 1) * LANES] = y
        else:
            win_ref[:, (j - 4) * LANES:(j - 3) * LANES] = y
    gl_ref[...] = jax.nn.sigmoid(proj(GL0))


def _even_in(x2d, gain, w_bf, cos_t, sin_t, qg, kg, n_table_blocks, seq_tiles=None):
    m, d = x2d.shape
    tile = EVEN_IN_TILE
    grid = (m // tile,)
    row = lambda i: (i, 0)
    fixed = lambda i: (0, 0)
    table = lambda i: (i % n_table_blocks, 0)
    widths = (RET_HEADS * RET_DK, RET_HEADS * RET_DK, A_WIDTH, A_WIDTH, B_WIDTH, KV_ROW, WIN_ROW, B_WIDTH, LANES)
    out_specs = [pl.BlockSpec((tile, w), row) for w in widths]
    out_shape = [jax.ShapeDtypeStruct((m, w), F32) for w in widths]
    if seq_tiles is not None:
        batch, seq = grid[0] // seq_tiles, seq_tiles * tile
        fmajor = lambda i: (i // seq_tiles, 0, i % seq_tiles)
        for idx, w in ((5, KV_ROW), (6, WIN_ROW)):
            out_specs[idx] = pl.BlockSpec((1, w, tile), fmajor)
            out_shape[idx] = jax.ShapeDtypeStruct((batch, w, seq), F32)
        out_specs.append(pl.BlockSpec((tile, 2 * LANES), row))
        out_shape.append(jax.ShapeDtypeStruct((m, 2 * LANES), F32))
    return pl.pallas_call(
        functools.partial(_even_in_kernel, transposed_kv=seq_tiles is not None),
        grid=grid,
        in_specs=[
            pl.BlockSpec((tile, d), row),
            pl.BlockSpec((1, d), fixed),
            pl.BlockSpec((d, EVEN_COLS), fixed),
            pl.BlockSpec((tile, LANES), table),
            pl.BlockSpec((tile, LANES), table),
            pl.BlockSpec((1, LANES), fixed),
            pl.BlockSpec((3, LANES), fixed),
        ],
        out_specs=out_specs,
        out_shape=out_shape,
        compiler_params=_cparams("parallel"),
        name="even_in_proj",
    )(x2d, gain, w_bf, cos_t, sin_t, qg, kg)


def _retention_kernel(q_ref, k_ref, v_ref, z_ref, gn_ref, s0_ref, dmat_ref, qdec_ref, kdec_ref, cdec_ref,
                      o_ref, sfin_ref, s_scr, *, rows, n_tiles):
    n_seq = RET_CHUNK // rows
    c = pl.program_id(1)

    @pl.when(c == 0)
    def _():
        s_scr[...] = s0_ref[...]

    lane = _lane_iota((RET_CHUNK, LANES))
    col_seq = _lane_iota((RET_DK, RET_CHUNK)) >> int(math.log2(rows)) if n_seq > 1 else None
    state = [[s_scr[r, h] for h in range(RET_HEADS)] for r in range(n_seq)]
    for t in range(n_tiles):
        tile = slice(t * RET_CHUNK, (t + 1) * RET_CHUNK)
        q, k, v = q_ref[tile, :], k_ref[tile, :], v_ref[tile, :]
        qd = q * qdec_ref[...]
        kd = k * kdec_ref[...]
        for pair in range(RET_HEADS // 2):
            cols = slice(pair * LANES, (pair + 1) * LANES)
            q2, k2b, qd2 = q[:, cols], k[:, cols].astype(BF16), qd[:, cols]
            kd2_t = kd[:, cols].T
            s_pair = [jnp.concatenate([state[r][2 * pair], state[r][2 * pair + 1]], axis=0).astype(BF16)
                      for r in range(n_seq)]
            for sub in range(2):
                h = 2 * pair + sub
                mine = (lane >= sub * RET_DK) & (lane < (sub + 1) * RET_DK)
                qm = jnp.where(mine, q2, 0.0).astype(BF16)
                qdm = jnp.where(mine, qd2, 0.0).astype(BF16)
                vhb = v[:, h * RET_DV:(h + 1) * RET_DV].astype(BF16)
                intra = _dot((_dot_nt(qm, k2b) * dmat_ref[h]).astype(BF16), vhb)
                cross = jnp.concatenate([_dot(qdm[r * rows:(r + 1) * rows], s_pair[r]) for r in range(n_seq)],
                                        axis=0)
                out = intra + cross
                kt = kd2_t[sub * RET_DK:(sub + 1) * RET_DK, :]
                if n_seq > 1:
                    kt = jnp.concatenate([jnp.where(col_seq == r, kt, 0.0) for r in range(n_seq)], axis=0)
                kv = _dot(kt.astype(BF16), vhb)
                for r in range(n_seq):
                    state[r][h] = state[r][h] * cdec_ref[h] + kv[r * RET_DK:(r + 1) * RET_DK]
                mu = jnp.mean(out, axis=-1, keepdims=True)
                cen = out - mu
                var = jnp.mean(cen * cen, axis=-1, keepdims=True)
                y = cen * lax.rsqrt(var + EPS) * gn_ref[:, h * RET_DV:(h + 1) * RET_DV]
                o_ref[tile, h * RET_DV:(h + 1) * RET_DV] = y * z_ref[tile, h * RET_DV:(h + 1) * RET_DV]
    for r in range(n_seq):
        for h in range(RET_HEADS):
            s_scr[r, h] = state[r][h]

    @pl.when(c == pl.num_programs(1) - 1)
    def _():
        sfin_ref[...] = s_scr[...]


def _retention_tables(rows):
    log_g = jnp.log(1.0 - 2.0 ** (-5.0 - jnp.arange(RET_HEADS, dtype=F32)))
    idx = jnp.arange(RET_CHUNK, dtype=jnp.int32)
    pos = (idx % rows).astype(F32)
    seq = idx // rows
    diff = pos[:, None] - pos[None, :]
    causal = (diff >= 0) & (seq[:, None] == seq[None, :])
    dmat = jnp.exp(jnp.where(causal, diff, 0.0)[None] * log_g[:, None, None]) * causal[None]
    qdec = jnp.exp((pos + 1.0)[:, None] * log_g[None, :])
    kdec = jnp.exp((rows - 1.0 - pos)[:, None] * log_g[None, :])
    qdec = jnp.repeat(qdec, RET_DK, axis=1)
    kdec = jnp.repeat(kdec, RET_DK, axis=1)
    cdec = jnp.broadcast_to(jnp.exp(rows * log_g)[:, None, None], (RET_HEADS, RET_DK, RET_DV))
    return dmat, qdec, kdec, cdec


def _retention(qa, ka, va, za, gn_gain, s0, batch, seq):
    rows = min(RET_CHUNK, seq)
    n_seq = RET_CHUNK // rows
    n_chunks = seq // rows
    n_tiles = min(RET_TILES_PER_STEP, n_chunks)
    steps = n_chunks // n_tiles
    dmat, qdec, kdec, cdec = _retention_tables(rows)
    tok = lambda b, c: (b * steps + c, 0)
    fixed2 = lambda b, c: (0, 0)
    fixed3 = lambda b, c: (0, 0, 0)
    state = lambda b, c: (b, 0, 0, 0)
    qk_w = RET_HEADS * RET_DK
    step_rows = n_tiles * RET_CHUNK
    return pl.pallas_call(
        functools.partial(_retention_kernel, rows=rows, n_tiles=n_tiles),
        grid=(batch // n_seq, steps),
        in_specs=[
            pl.BlockSpec((step_rows, qk_w), tok),
            pl.BlockSpec((step_rows, qk_w), tok),
            pl.BlockSpec((step_rows, A_WIDTH), tok),
            pl.BlockSpec((step_rows, A_WIDTH), tok),
            pl.BlockSpec((1, A_WIDTH), fixed2),
            pl.BlockSpec((n_seq, RET_HEADS, RET_DK, RET_DV), state),
            pl.BlockSpec((RET_HEADS, RET_CHUNK, RET_CHUNK), fixed3),
            pl.BlockSpec((RET_CHUNK, qk_w), fixed2),
            pl.BlockSpec((RET_CHUNK, qk_w), fixed2),
            pl.BlockSpec((RET_HEADS, RET_DK, RET_DV), fixed3),
        ],
        out_specs=[
            pl.BlockSpec((step_rows, A_WIDTH), tok),
            pl.BlockSpec((n_seq, RET_HEADS, RET_DK, RET_DV), state),
        ],
        out_shape=[
            jax.ShapeDtypeStruct((batch * seq, A_WIDTH), F32),
            jax.ShapeDtypeStruct((batch, RET_HEADS, RET_DK, RET_DV), F32),
        ],
        scratch_shapes=[pltpu.VMEM((n_seq, RET_HEADS, RET_DK, RET_DV), F32)],
        compiler_params=_cparams("parallel", "arbitrary"),
        name="retention",
    )(qa, ka, va, za, gn_gain, s0, dmat, qdec, kdec, cdec)


def _compress_kernel(*refs, n_pref, n_src, src_rows):
    refs = refs[n_pref:]
    src_k = refs[:n_src]
    src_v = refs[n_src:2 * n_src]
    pe_ref, w_ref, o_ref = refs[2 * n_src:]
    per_src = src_rows // CMP_STRIDE
    n_half = n_src * per_src
    acc = [jnp.zeros((n_half, 2 * LANES), F32) for _ in range(CMP_RATIO)]

    def strided_rows(srcs, l):
        rows = [s[0, pl.ds(l, per_src, stride=CMP_STRIDE), :] for s in srcs]
        return rows[0] if n_src == 1 else jnp.concatenate(rows, axis=0)

    for l in range(CMP_STRIDE):
        x = jnp.concatenate([strided_rows(src_k, l), strided_rows(src_v, l)], axis=1)
        for r in range(CMP_RATIO):
            i = r * CMP_STRIDE + l
            acc[r] = acc[r] + _dot((x + pe_ref[i:i + 1, :]).astype(BF16), w_ref[i])
    out = acc[0]
    for r in range(1, CMP_RATIO):
        out = out + pltpu.roll(acc[r], n_half - r, axis=0)
    live = _row_iota(out.shape) < n_half - CMP_RATIO + 1
    o_ref[0] = jnp.where(live, out, 0.0)


def _compress(src_arrays, src_specs, grid, extra_prefetch, pe, w_bd, batch, src_rows, n_half):
    n_src = len(src_specs) // 2
    n_pref = len(extra_prefetch)
    kern = functools.partial(_compress_kernel, n_pref=n_pref, n_src=n_src, src_rows=src_rows)
    fixed2 = lambda *a: (0, 0)
    fixed3 = lambda *a: (0, 0, 0)
    out_map = lambda b, *a: (b, 0, 0)
    return pl.pallas_call(
        kern,
        grid_spec=pltpu.PrefetchScalarGridSpec(
            num_scalar_prefetch=n_pref,
            grid=grid,
            in_specs=list(src_specs) + [
                pl.BlockSpec((L_CMP, 2 * LANES), fixed2),
                pl.BlockSpec((L_CMP, 2 * LANES, 2 * LANES), fixed3),
            ],
            out_specs=pl.BlockSpec((1, n_half, 2 * LANES), out_map),
        ),
        out_shape=jax.ShapeDtypeStruct((batch, n_half, 2 * LANES), F32),
        compiler_params=_cparams("parallel"),
        name="kv_compress",
    )(*extra_prefetch, *src_arrays, pe, w_bd)


def _split3_bf16(x):
    hi = x.astype(BF16)
    r1 = x - hi.astype(F32)
    mid = r1.astype(BF16)
    lo = (r1 - mid.astype(F32)).astype(BF16)
    return hi, mid, lo


def _stack_group_queries(q, g):
    lane = _lane_iota((q.shape[0], LANES))
    in_g = (lane >= g * NSA_DH) & (lane < (g + 1) * NSA_DH)
    parts = []
    for hh in range(NSA_HPG):
        h = g * NSA_HPG + hh
        two = q[:, (h // 2) * LANES:(h // 2 + 1) * LANES]
        if h % 2 != g:
            two = _swap_halves(two)
        parts.append(jnp.where(in_g, two, 0.0))
    return (jnp.concatenate(parts, axis=0) * (NSA_DH ** -0.5)).astype(BF16)


def _masked_softmax(s, ok):
    s = jnp.where(ok, s, NEG_INF)
    e = jnp.exp(s - jnp.max(s, axis=-1, keepdims=True))
    return jnp.where(ok, e * (1.0 / jnp.sum(e, axis=-1, keepdims=True)), 0.0)


def _sum_heads(p1, tq):
    psum = p1[0:tq]
    for hh in range(1, NSA_HPG):
        psum = psum + p1[hh * tq:(hh + 1) * tq]
    return psum


def _choose_blocks(psum, t0, tq, overlap, n_blocks):
    rows = psum.shape[0]
    imp = sum(_dot(t, overlap) for t in _split3_bf16(psum))
    shift = int(math.log2(L_SEL))

    def ranked(score, blk, valid, take):
        rank = jnp.zeros(score.shape, F32)
        for s in range(n_blocks):
            cand = take(s)
            rank = rank + ((cand > score) | ((cand == score) & (blk > s))).astype(F32)
        return ((rank < N_SEL) & valid).astype(F32)

    if rows % LANES:
        tpos = jnp.concatenate([t0 + _row_iota((tq, 1))] * (rows // tq), axis=0)
        blk = _lane_iota((rows, LANES))
        cur = tpos >> shift
        forced = (blk == 0) | (blk == cur) | (blk == cur - 1)
        valid = (blk * L_SEL <= tpos) & (blk < n_blocks)
        score = jnp.where(valid, jnp.where(forced, FORCE, imp), -FORCE)
        return ranked(score, blk, valid, lambda s: score[:, s:s + 1]).astype(BF16)

    nbp = -(-n_blocks // SUBLANES) * SUBLANES
    tiles = rows // LANES
    imp_t = jnp.concatenate([imp[i * LANES:(i + 1) * LANES].T[:nbp] for i in range(tiles)], axis=1)
    tpos = jnp.concatenate([t0 + _lane_iota((1, tq))] * (rows // tq), axis=1)
    blk = _row_iota((nbp, rows))
    cur = tpos >> shift
    forced = (blk == 0) | (blk == cur) | (blk == cur - 1)
    valid = (blk * L_SEL <= tpos) & (blk < n_blocks)
    score = jnp.where(valid, jnp.where(forced, FORCE, imp_t), -FORCE)
    chosen_t = ranked(score, blk, valid, lambda s: score[s:s + 1, :])
    chosen_t = jnp.concatenate([chosen_t, jnp.zeros((LANES - nbp, rows), F32)], axis=0)
    return jnp.concatenate([chosen_t[:, i * LANES:(i + 1) * LANES].T for i in range(tiles)],
                           axis=0).astype(BF16)


def _gate_and_store(gates, zs, branch_outs, o_ref, row0=0):
    tq = gates.shape[0]
    lane = _lane_iota((tq, LANES))
    ext = []
    for g in range(NSA_KV_HEADS):
        o_cmp, o_sel, o_win = branch_outs[g]
        for hh in range(NSA_HPG):
            h = g * NSA_HPG + hh
            rows = slice(hh * tq, (hh + 1) * tq)
            o = (gates[:, 3 * h:3 * h + 1] * o_cmp[rows] + gates[:, 3 * h + 1:3 * h + 2] * o_sel[rows]
                 + gates[:, 3 * h + 2:3 * h + 3] * o_win[rows])
            ext.append(_swap_halves(o) if h % 2 != g else o)
    for j in range(NSA_HEADS // 2):
        both = jnp.where(lane < NSA_DH, ext[2 * j], ext[2 * j + 1])
        o_ref[row0:row0 + tq, j * LANES:(j + 1) * LANES] = both * zs[:, j * LANES:(j + 1) * LANES]


def _nsa_prompt_kernel(q_ref, g_ref, z_ref, cmp_ref, sel_ref, win_ref, ov_ref, ex_ref, o_ref, *, n_blocks):
    i = pl.program_id(1)
    tq = q_ref.shape[0]
    q = q_ref[...]
    t0 = i * tq
    tpos = t0 + _row_iota((tq, 1))
    tpos_all = jnp.concatenate([tpos] * NSA_HEADS, axis=0)
    grp_rows = NSA_HPG * tq
    qs = jnp.concatenate([_stack_group_queries(q, g) for g in range(NSA_KV_HEADS)], axis=0)

    cmp_blk = cmp_ref[0].astype(BF16)
    cmp_ok = (_lane_iota((1, cmp_blk.shape[0])) * CMP_STRIDE + (L_CMP - 1)) <= tpos_all
    p1 = _masked_softmax(_dot_nt(qs, cmp_blk[:, :LANES]), cmp_ok)
    o_cmp = _dot(p1.astype(BF16), cmp_blk[:, LANES:])
    psum = jnp.concatenate([_sum_heads(p1[g * grp_rows:(g + 1) * grp_rows], tq)
                            for g in range(NSA_KV_HEADS)], axis=0)
    chosen = _choose_blocks(psum, t0, tq, ov_ref[...], n_blocks)

    def sel_chunk(c, carry):
        m, l, acc = carry
        k0 = pl.multiple_of(c * SEL_CHUNK, SEL_CHUNK)
        k_t = sel_ref[0, 0:LANES, pl.ds(k0, SEL_CHUNK)].astype(BF16)
        v_t = sel_ref[0, LANES:2 * LANES, pl.ds(k0, SEL_CHUNK)].astype(BF16)
        causal = jnp.concatenate([(k0 + _lane_iota((1, SEL_CHUNK))) <= tpos] * NSA_KV_HEADS, axis=0)
        key_on = (_dot(chosen, ex_ref[:, pl.ds(k0, SEL_CHUNK)]) > 0.5) & causal
        bias = jnp.where(key_on, 0.0, NEG_INF)
        bias = jnp.concatenate([bias[g * tq:(g + 1) * tq] for g in range(NSA_KV_HEADS)
                                for _ in range(NSA_HPG)], axis=0)
        s = _dot(qs, k_t) + bias
        m_new = jnp.maximum(m, jnp.max(s, axis=-1, keepdims=True))
        alpha = jnp.exp(m - m_new)
        e = jnp.exp(s - m_new)
        l_new = alpha * l + jnp.sum(e, axis=-1, keepdims=True)
        return m_new, l_new, alpha * acc + _dot_nt(e.astype(BF16), v_t)

    rows = NSA_HEADS * tq
    init = (jnp.full((rows, 1), NEG_INF, F32), jnp.zeros((rows, 1), F32), jnp.zeros((rows, LANES), F32))
    n_chunks = t0 // SEL_CHUNK + 1
    _, l2, acc2 = lax.fori_loop(0, n_chunks, sel_chunk, init)
    o_sel = acc2 / l2

    w0 = pl.multiple_of(jnp.maximum(t0 - WINDOW, 0), KEY_TILE)
    wk_t = win_ref[0, 0:LANES, pl.ds(w0, WIN_KEYS)].astype(BF16)
    wv_t = win_ref[0, LANES:2 * LANES, pl.ds(w0, WIN_KEYS)].astype(BF16)
    win_kpos = w0 + _lane_iota((1, WIN_KEYS))
    win_ok = (win_kpos <= tpos_all) & (win_kpos > tpos_all - WINDOW)
    s3 = jnp.where(win_ok, _dot(qs, wk_t), NEG_INF)
    e3 = jnp.exp(s3 - jnp.max(s3, axis=-1, keepdims=True))
    o_win = _dot_nt(e3.astype(BF16), wv_t) / jnp.sum(e3, axis=-1, keepdims=True)

    branch_outs = [(o_cmp[g * grp_rows:(g + 1) * grp_rows], o_sel[g * grp_rows:(g + 1) * grp_rows],
                    o_win[g * grp_rows:(g + 1) * grp_rows]) for g in range(NSA_KV_HEADS)]
    _gate_and_store(g_ref[...], z_ref[...], branch_outs, o_ref)


def _nsa_prompt(qn, gates, zb, cmp_kv, full_t, win_t, overlap, expand, batch, seq):
    tq = KEY_TILE
    nq = seq // tq
    tok = lambda b, i: (b * nq + i, 0)
    per_b = lambda b, i: (b, 0, 0)
    sel_half = lambda b, i: (b, 1, 0)
    fixed = lambda b, i: (0, 0)
    n_blocks = -(-seq // L_SEL)
    return pl.pallas_call(
        functools.partial(_nsa_prompt_kernel, n_blocks=n_blocks),
        grid=(batch, nq),
        in_specs=[
            pl.BlockSpec((tq, B_WIDTH), tok),
            pl.BlockSpec((tq, LANES), tok),
            pl.BlockSpec((tq, B_WIDTH), tok),
            pl.BlockSpec((1, cmp_kv.shape[1], 2 * LANES), per_b),
            pl.BlockSpec((1, 2 * LANES, seq), sel_half),
            pl.BlockSpec((1, WIN_ROW, seq), per_b),
            pl.BlockSpec(overlap.shape, fixed),
            pl.BlockSpec(expand.shape, fixed),
        ],
        out_specs=pl.BlockSpec((tq, B_WIDTH), tok),
        out_shape=jax.ShapeDtypeStruct((batch * seq, B_WIDTH), F32),
        compiler_params=_cparams("parallel", "arbitrary"),
        name="nsa_prompt",
    )(qn, gates, zb, cmp_kv, full_t, win_t, overlap, expand)


def _nsa_sample_kernel(*refs, nb, n_pages, past_len, n_blocks):
    q_ref, g_ref, z_ref, newf_ref, neww_ref, winp_ref = refs[1:7]
    pages = refs[7:7 + nb * n_pages]
    pe_ref, w_ref, ov_ref, ex_ref, o_ref, wout_ref, xs_ref = refs[7 + nb * n_pages:]
    tq = q_ref.shape[0] // nb
    w_buf = winp_ref.shape[3]
    halves_per_page = PAGE_SIZE // CMP_STRIDE
    n_half = past_len // CMP_STRIDE
    total_half = nb * n_half

    for j in range(nb * n_pages):
        for c in range(2):
            x = pages[j][0, c].T
            for hb in range(halves_per_page):
                slot = (j * halves_per_page + hb) * CMP_SLOT
                xs_ref[c, slot:slot + CMP_STRIDE, :] = x[hb * CMP_STRIDE:(hb + 1) * CMP_STRIDE]

    acc = [jnp.zeros((total_half, 2 * LANES), F32) for _ in range(CMP_RATIO)]
    for l in range(CMP_STRIDE):
        x = jnp.concatenate([xs_ref[c, pl.ds(l, total_half, stride=CMP_SLOT), :] for c in range(2)], axis=1)
        for r in range(CMP_RATIO):
            i = r * CMP_STRIDE + l
            acc[r] = acc[r] + _dot((x + pe_ref[i:i + 1, :]).astype(BF16), w_ref[i])
    cmp_all = acc[0]
    for r in range(1, CMP_RATIO):
        cmp_all = cmp_all + pltpu.roll(acc[r], total_half - r, axis=0)
    live = _row_iota((n_half, 2 * LANES)) < n_half - CMP_RATIO + 1

    pad = jnp.zeros((KEY_TILE - tq, LANES), F32)
    lane = _lane_iota((LANES, LANES))
    per_req = NSA_HEADS * tq
    tpos = past_len + _row_iota((tq, 1))
    tpos_all = jnp.concatenate([tpos] * (nb * NSA_HEADS), axis=0)
    tpos_grp = jnp.concatenate([tpos] * (nb * NSA_KV_HEADS), axis=0)

    def padded_rows(x):
        return jnp.concatenate([x, pad], axis=0)

    def per_request(fn):
        return jnp.concatenate([fn(j, slice(j * per_req, (j + 1) * per_req)) for j in range(nb)], axis=0)

    tok = [slice(j * tq, (j + 1) * tq) for j in range(nb)]
    qs = jnp.concatenate([_stack_group_queries(q_ref[tok[j], :], g)
                          for j in range(nb) for g in range(NSA_KV_HEADS)], axis=0)
    cmp_blk = [jnp.where(live, cmp_all[j * n_half:(j + 1) * n_half], 0.0).astype(BF16) for j in range(nb)]
    mine = [pages[j * n_pages:(j + 1) * n_pages] for j in range(nb)]
    newf = [newf_ref[tok[j], :] for j in range(nb)]
    neww = [neww_ref[tok[j], :] for j in range(nb)]

    cmp_ok = (_lane_iota((1, n_half)) * CMP_STRIDE + (L_CMP - 1)) <= tpos_all
    p1 = _masked_softmax(per_request(lambda j, r: _dot_nt(qs[r], cmp_blk[j][:, :LANES])), cmp_ok)
    p1b = p1.astype(BF16)
    o_cmp = per_request(lambda j, r: _dot(p1b[r], cmp_blk[j][:, LANES:]))

    grp_rows = NSA_HPG * tq
    psum = jnp.concatenate([_sum_heads(p1[i * grp_rows:(i + 1) * grp_rows], tq)
                            for i in range(nb * NSA_KV_HEADS)], axis=0)
    chosen = _choose_blocks(psum, past_len, tq, ov_ref[...], n_blocks)
    key_on = (_dot(chosen, ex_ref[...]) > 0.5) & (_lane_iota((1, past_len + KEY_TILE)) <= tpos_grp)
    bias = jnp.where(key_on, 0.0, NEG_INF)
    bias = jnp.concatenate([bias[i * tq:(i + 1) * tq] for i in range(nb * NSA_KV_HEADS)
                            for _ in range(NSA_HPG)], axis=0)

    def sel_scores(j, r):
        k_t = jnp.concatenate([p[0, 2].astype(BF16) for p in mine[j]], axis=1)
        k_new = padded_rows(newf[j][:, 2 * LANES:3 * LANES]).astype(BF16)
        return jnp.concatenate([_dot(qs[r], k_t), _dot_nt(qs[r], k_new)], axis=1)

    s2 = per_request(sel_scores) + bias
    e2 = jnp.exp(s2 - jnp.max(s2, axis=-1, keepdims=True))
    l2 = jnp.sum(e2, axis=-1, keepdims=True)
    e2 = e2.astype(BF16)

    def sel_values(j, r):
        v_t = jnp.concatenate([p[0, 3].astype(BF16) for p in mine[j]], axis=1)
        v_new = padded_rows(newf[j][:, 3 * LANES:4 * LANES]).astype(BF16)
        return _dot_nt(e2[r, :past_len], v_t) + _dot(e2[r, past_len:], v_new)

    o_sel = per_request(sel_values) / l2

    win_kpos = (past_len - w_buf) + _lane_iota((1, w_buf + KEY_TILE))
    win_ok = (win_kpos <= tpos_all) & (win_kpos > tpos_all - WINDOW)
    s3 = per_request(lambda j, r: jnp.concatenate(
        [_dot(qs[r], winp_ref[j, 0].astype(BF16)),
         _dot_nt(qs[r], padded_rows(neww[j][:, :LANES]).astype(BF16))], axis=1))
    s3 = jnp.where(win_ok, s3, NEG_INF)
    e3 = jnp.exp(s3 - jnp.max(s3, axis=-1, keepdims=True))
    l3 = jnp.sum(e3, axis=-1, keepdims=True)
    e3 = e3.astype(BF16)
    o_win = per_request(lambda j, r: _dot_nt(e3[r, :w_buf], winp_ref[j, 1].astype(BF16))
                        + _dot(e3[r, w_buf:], padded_rows(neww[j][:, LANES:]).astype(BF16))) / l3

    for j in range(nb):
        branch_outs = []
        for g in range(NSA_KV_HEADS):
            r = slice((j * NSA_KV_HEADS + g) * grp_rows, (j * NSA_KV_HEADS + g + 1) * grp_rows)
            branch_outs.append((o_cmp[r], o_sel[r], o_win[r]))
        _gate_and_store(g_ref[tok[j], :], z_ref[tok[j], :], branch_outs, o_ref, row0=j * tq)

        for c in range(2):
            shifted = pltpu.roll(winp_ref[j, c], w_buf - tq, axis=1)
            new_cols = pltpu.roll(padded_rows(neww[j][:, c * LANES:(c + 1) * LANES]).T, LANES - tq, axis=1)
            wout_ref[j, c, :, 0:w_buf - LANES] = shifted[:, 0:w_buf - LANES]
            wout_ref[j, c, :, w_buf - LANES:w_buf] = jnp.where(lane >= LANES - tq, new_cols,
                                                               shifted[:, w_buf - LANES:w_buf])


def _nsa_sample(page_table, qn, gates, zb, full_new, win_new, win_t, cache_t, pe, w_bd, overlap, expand,
                batch, tq, n_pages):
    nb = SAMPLE_REQS_PER_STEP
    past_len = n_pages * PAGE_SIZE
    w_buf = win_t.shape[3]
    n_blocks = -(-(past_len + tq) // L_SEL)
    tok = lambda b, pt: (b, 0)
    per_b = lambda b, pt: (b, 0, 0, 0)
    fixed2 = lambda b, pt: (0, 0)
    fixed3 = lambda b, pt: (0, 0, 0)

    def page_spec(j, p):
        return pl.BlockSpec((1, 4, LANES, PAGE_SIZE), lambda b, pt: (pt[b * nb + j, p], 0, 0, 0))

    n_slots = nb * n_pages * (PAGE_SIZE // CMP_STRIDE)
    return pl.pallas_call(
        functools.partial(_nsa_sample_kernel, nb=nb, n_pages=n_pages, past_len=past_len, n_blocks=n_blocks),
        grid_spec=pltpu.PrefetchScalarGridSpec(
            num_scalar_prefetch=1,
            grid=(batch // nb,),
            in_specs=[
                pl.BlockSpec((nb * tq, B_WIDTH), tok),
                pl.BlockSpec((nb * tq, LANES), tok),
                pl.BlockSpec((nb * tq, B_WIDTH), tok),
                pl.BlockSpec((nb * tq, KV_ROW), tok),
                pl.BlockSpec((nb * tq, WIN_ROW), tok),
                pl.BlockSpec((nb, 2, LANES, w_buf), per_b),
            ] + [page_spec(j, p) for j in range(nb) for p in range(n_pages)] + [
                pl.BlockSpec(pe.shape, fixed2),
                pl.BlockSpec(w_bd.shape, fixed3),
                pl.BlockSpec(overlap.shape, fixed2),
                pl.BlockSpec(expand.shape, fixed2),
            ],
            out_specs=[
                pl.BlockSpec((nb * tq, B_WIDTH), tok),
                pl.BlockSpec((nb, 2, LANES, w_buf), per_b),
            ],
            scratch_shapes=[pltpu.VMEM((2, n_slots * CMP_SLOT, LANES), F32)],
        ),
        out_shape=[
            jax.ShapeDtypeStruct((batch * tq, B_WIDTH), F32),
            jax.ShapeDtypeStruct((batch, 2, LANES, w_buf), F32),
        ],
        compiler_params=_cparams("parallel"),
        name="nsa_sample",
    )(page_table, qn, gates, zb, full_new, win_new, win_t, *([cache_t] * (nb * n_pages)), pe, w_bd,
      overlap, expand)


def _selection_tables(n_cmp_rows, n_sel_keys):
    n = jnp.arange(n_cmp_rows, dtype=jnp.int32)[:, None]
    s = jnp.arange(LANES, dtype=jnp.int32)[None, :]
    c_start = n * CMP_STRIDE
    s_start = s * L_SEL
    overlap = ((c_start < s_start + L_SEL) & (s_start < c_start + L_CMP)).astype(BF16)
    key = jnp.arange(n_sel_keys, dtype=jnp.int32)[None, :]
    expand = ((key // L_SEL) == jnp.arange(LANES, dtype=jnp.int32)[:, None]).astype(BF16)
    return overlap, expand


def _even_out_kernel(x_ref, oa_ref, ob_ref, wa_ref, wb_ref, *rest, with_next):
    y = (x_ref[...] + _dot(oa_ref[...].astype(BF16), wa_ref[...])
         + _dot(ob_ref[...].astype(BF16), wb_ref[...]))
    if not with_next:
        (y_ref,) = rest
        y_ref[...] = y
        return
    g_ref, w_ref, y_ref, u_ref, z_ref = rest
    y_ref[...] = y
    hb = _rmsnorm_rows(y, g_ref[...]).astype(BF16)
    e = u_ref.shape[1]
    u_ref[...] = _dot(hb, w_ref[:, :e])
    z_ref[...] = jax.nn.silu(_dot(hb, w_ref[:, e:]))


def _even_out(x2d, oa, ob, wa, wb, next_odd=None):
    m, d = x2d.shape
    row = lambda i: (i, 0)
    fixed = lambda i: (0, 0)
    in_specs = [
        pl.BlockSpec((ROW_TILE, d), row),
        pl.BlockSpec((ROW_TILE, A_WIDTH), row),
        pl.BlockSpec((ROW_TILE, B_WIDTH), row),
        pl.BlockSpec((A_WIDTH, d), fixed),
        pl.BlockSpec((B_WIDTH, d), fixed),
    ]
    out_specs = [pl.BlockSpec((ROW_TILE, d), row)]
    out_shape = [jax.ShapeDtypeStruct((m, d), F32)]
    args = [x2d, oa, ob, wa, wb]
    if next_odd is not None:
        gain, w_bf = next_odd
        e = w_bf.shape[1] // 2
        in_specs += [pl.BlockSpec((1, d), fixed), pl.BlockSpec((d, 2 * e), fixed)]
        out_specs += [pl.BlockSpec((ROW_TILE, e), row)] * 2
        out_shape += [jax.ShapeDtypeStruct((m, e), F32)] * 2
        args += [gain, w_bf]
    return pl.pallas_call(
        functools.partial(_even_out_kernel, with_next=next_odd is not None),
        grid=(m // ROW_TILE,),
        in_specs=in_specs,
        out_specs=out_specs,
        out_shape=out_shape,
        compiler_params=_cparams("parallel"),
        name="even_out_proj",
    )(*args)


def _s5_kernel(u_ref, x0r_ref, x0i_ref, ar_ref, ai_ref, bm_ref, cm_ref, d_ref,
               y_ref, fr_ref, fi_ref, st_re, st_im, ubuf_a, ubuf_b, xbuf_a, xbuf_b, ybuf, *, nb, tt):
    t_idx = pl.program_id(1)
    n_blk = bm_ref.shape[0]
    half = S5_BLOCK_STATE
    n_chunk = half // LANES
    pitch = tt + S5_ROW_PAD
    ubufs, xbufs = (ubuf_a, ubuf_b), (xbuf_a, xbuf_b)
    wide = S5_MXU_COLS // LANES
    n_slices = 2 * n_chunk // wide

    @pl.when(t_idx == 0)
    def _():
        for kb in range(n_blk):
            st_re[kb] = x0r_ref[:, kb * half:(kb + 1) * half]
            st_im[kb] = x0i_ref[:, kb * half:(kb + 1) * half]

    for ubuf in ubufs:
        for b in range(nb):
            ubuf[b * pitch + tt:(b + 1) * pitch, :] = jnp.zeros((S5_ROW_PAD, LANES), F32)

    def lanes_of(kb):
        return slice(kb * LANES, (kb + 1) * LANES)

    def stage_u(kb):
        for b in range(nb):
            ubufs[kb % 2][b * pitch:b * pitch + tt, :] = u_ref[b, :, lanes_of(kb)]

    def input_slice(kb, j):
        cols = slice(j * S5_MXU_COLS, (j + 1) * S5_MXU_COLS)
        bu = _dot(ubufs[kb % 2][...].astype(BF16), bm_ref[kb, :, cols])
        for i in range(wide):
            xbufs[kb % 2][j * wide + i] = bu[:, i * LANES:(i + 1) * LANES]

    def output_slice(kb, j, acc):
        x = jnp.concatenate([xbufs[kb % 2][j * wide + i] for i in range(wide)], axis=1).astype(BF16)
        part = _dot(x, cm_ref[kb, j * S5_MXU_COLS:(j + 1) * S5_MXU_COLS, :])
        return part if acc is None else acc + part

    def finish_y(kb, acc):
        ybuf[...] = acc + d_ref[:, lanes_of(kb)] * ubufs[kb % 2][...]
        for b in range(nb):
            y_ref[b, :, lanes_of(kb)] = ybuf[b * pitch:b * pitch + tt, :]

    stage_u(0)
    for j in range(n_slices):
        input_slice(0, j)
    for kb in range(n_blk + 1):
        side, acc = [], [None]
        if kb >= 1:
            def out_task(j, kb=kb):
                acc[0] = output_slice(kb - 1, j, acc[0])
            side += [functools.partial(out_task, j) for j in range(n_slices)]
            side.append(lambda kb=kb: finish_y(kb - 1, acc[0]))
        if kb + 1 < n_blk:
            side.append(functools.partial(stage_u, kb + 1))
            side += [functools.partial(input_slice, kb + 1, j) for j in range(n_slices)]
        if kb == n_blk:
            for task in side:
                task()
            break
        xbuf = xbufs[kb % 2]
        a_re = jnp.broadcast_to(ar_ref[kb], (nb, half))
        a_im = jnp.broadcast_to(ai_ref[kb], (nb, half))
        s_re, s_im = st_re[kb], st_im[kb]
        every = -(-tt // (len(side) + 1))
        for t in range(tt):
            if t % every == every - 1 and side:
                side.pop(0)()
            rows = pl.ds(t, nb, stride=pitch)
            b_re = jnp.concatenate([xbuf[j, rows, :] for j in range(n_chunk)], axis=1)
            b_im = jnp.concatenate([xbuf[n_chunk + j, rows, :] for j in range(n_chunk)], axis=1)
            s_re, s_im = a_re * s_re - a_im * s_im + b_re, a_re * s_im + a_im * s_re + b_im
            for j in range(n_chunk):
                xbuf[j, rows, :] = s_re[:, j * LANES:(j + 1) * LANES]
                xbuf[n_chunk + j, rows, :] = s_im[:, j * LANES:(j + 1) * LANES]
        for task in side:
            task()
        st_re[kb] = s_re
        st_im[kb] = s_im

    @pl.when(t_idx == pl.num_programs(1) - 1)
    def _():
        for kb in range(n_blk):
            fr_ref[:, kb * half:(kb + 1) * half] = st_re[kb]
            fi_ref[:, kb * half:(kb + 1) * half] = st_im[kb]


def _s5(u3, x0_re, x0_im, a_re, a_im, bmat, cmat, d_row, tt):
    batch, seq, e = u3.shape
    nb = SUBLANES
    n_blk = e // LANES
    n_state = n_blk * S5_BLOCK_STATE
    seq_map = lambda b, t: (b, t, 0)
    st_map = lambda b, t: (b, 0)
    fixed2 = lambda b, t: (0, 0)
    fixed3 = lambda b, t: (0, 0, 0)
    return pl.pallas_call(
        functools.partial(_s5_kernel, nb=nb, tt=tt),
        grid=(batch // nb, seq // tt),
        in_specs=[
            pl.BlockSpec((nb, tt, e), seq_map),
            pl.BlockSpec((nb, n_state), st_map),
            pl.BlockSpec((nb, n_state), st_map),
            pl.BlockSpec((n_blk, 1, S5_BLOCK_STATE), fixed3),
            pl.BlockSpec((n_blk, 1, S5_BLOCK_STATE), fixed3),
            pl.BlockSpec((n_blk, LANES, 2 * S5_BLOCK_STATE), fixed3),
            pl.BlockSpec((n_blk, 2 * S5_BLOCK_STATE, LANES), fixed3),
            pl.BlockSpec((1, e), fixed2),
        ],
        out_specs=[
            pl.BlockSpec((nb, tt, e), seq_map),
            pl.BlockSpec((nb, n_state), st_map),
            pl.BlockSpec((nb, n_state), st_map),
        ],
        out_shape=[
            jax.ShapeDtypeStruct((batch, seq, e), F32),
            jax.ShapeDtypeStruct((batch, n_state), F32),
            jax.ShapeDtypeStruct((batch, n_state), F32),
        ],
        scratch_shapes=[
            pltpu.VMEM((n_blk, nb, S5_BLOCK_STATE), F32),
            pltpu.VMEM((n_blk, nb, S5_BLOCK_STATE), F32),
            pltpu.VMEM((nb * (tt + S5_ROW_PAD), LANES), F32),
            pltpu.VMEM((nb * (tt + S5_ROW_PAD), LANES), F32),
            pltpu.VMEM((2 * S5_BLOCK_STATE // LANES, nb * (tt + S5_ROW_PAD), LANES), F32),
            pltpu.VMEM((2 * S5_BLOCK_STATE // LANES, nb * (tt + S5_ROW_PAD), LANES), F32),
            pltpu.VMEM((nb * (tt + S5_ROW_PAD), LANES), F32),
        ],
        compiler_params=_cparams("parallel", "arbitrary"),
        name="s5_scan",
    )(u3, x0_re, x0_im, a_re, a_im, bmat, cmat, d_row)


def _s5_params(lam_re, lam_im, b_re, b_im, c_re, c_im, log_step):
    n_groups = lam_re.shape[0]
    n_blk = n_groups // S5_LANE_GROUPS
    dt = jnp.exp(log_step)[:, None]
    mag = jnp.exp(lam_re * dt)
    ang = lam_im * dt
    ab_re, ab_im = mag * jnp.cos(ang), mag * jnp.sin(ang)
    den = lam_re * lam_re + lam_im * lam_im
    nr = ab_re - 1.0
    f_re = (nr * lam_re + ab_im * lam_im) / den
    f_im = (ab_im * lam_re - nr * lam_im) / den
    bb_re = f_re[..., None] * b_re - f_im[..., None] * b_im
    bb_im = f_re[..., None] * b_im + f_im[..., None] * b_re
    eye = jnp.eye(S5_LANE_GROUPS, dtype=lam_re.dtype)

    def in_map(bb):
        bb = bb.reshape(n_blk, S5_LANE_GROUPS, S5_STATE, S5_GROUP)
        m = jnp.einsum('kgpc,gh->kgchp', bb, eye)
        return m.reshape(n_blk, LANES, S5_BLOCK_STATE)

    def out_map(cc):
        cc = cc.reshape(n_blk, S5_LANE_GROUPS, S5_GROUP, S5_STATE)
        m = jnp.einsum('kgcp,gh->kgphc', cc, eye)
        return m.reshape(n_blk, S5_BLOCK_STATE, LANES)

    bmat = jnp.concatenate([in_map(bb_re), in_map(bb_im)], axis=2).astype(BF16)
    cmat = jnp.concatenate([out_map(c_re), out_map(-c_im)], axis=1).astype(BF16)
    a_re = ab_re.reshape(n_blk, 1, S5_BLOCK_STATE)
    a_im = ab_im.reshape(n_blk, 1, S5_BLOCK_STATE)
    return a_re, a_im, bmat, cmat


def _odd_out_kernel(x_ref, y_ref, z_ref, w1_ref, w2_ref, wo_ref, o_ref):
    yb = jax.nn.gelu(y_ref[...]).astype(BF16)
    t = _dot(yb, w1_ref[...]) * jax.nn.sigmoid(_dot(yb, w2_ref[...])) * z_ref[...]
    o_ref[...] = x_ref[...] + _dot(t.astype(BF16), wo_ref[...])


def _odd_out(x2d, y2d, z2d, w1, w2, wo):
    m, d = x2d.shape
    e = y2d.shape[1]
    row = lambda i: (i, 0)
    fixed = lambda i: (0, 0)
    return pl.pallas_call(
        _odd_out_kernel,
        grid=(m // ROW_TILE,),
        in_specs=[
            pl.BlockSpec((ROW_TILE, d), row),
            pl.BlockSpec((ROW_TILE, e), row),
            pl.BlockSpec((ROW_TILE, e), row),
            pl.BlockSpec((e, e), fixed),
            pl.BlockSpec((e, e), fixed),
            pl.BlockSpec((e, d), fixed),
        ],
        out_specs=pl.BlockSpec((ROW_TILE, d), row),
        out_shape=jax.ShapeDtypeStruct((m, d), F32),
        compiler_params=_cparams("parallel"),
        name="odd_out_proj",
    )(x2d, y2d, z2d, w1, w2, wo)


def _rope_tables(pos):
    half = NSA_DH // 2
    inv = ROPE_THETA ** (-jnp.arange(half, dtype=F32) / half)
    ang = pos.astype(F32)[:, None] * inv[None, :]
    cos, sin = jnp.cos(ang), jnp.sin(ang)
    reps = LANES // NSA_DH
    return jnp.tile(cos, (1, 2 * reps)), jnp.tile(jnp.concatenate([-sin, sin], axis=1), (1, reps))


def _even_weights(norm_g, w_in, w_out, q_norm, k_norm, cmp_pos, cmp_w):
    d = w_in.shape[0]
    sizes = (RET_HEADS * RET_DK, RET_HEADS * RET_DK, A_WIDTH, A_WIDTH, B_WIDTH, 6 * NSA_KV_HEADS * NSA_DH,
             N_GATES, B_WIDTH)
    parts, o = [], 0
    for s in sizes:
        parts.append(w_in[:, o:o + s])
        o += s
    qa, ka, va, za, qn, kvb, gl, zb = parts
    w_perm = jnp.concatenate([qa, ka, va, za, qn, kvb, zb, gl, jnp.zeros((d, LANES - N_GATES), w_in.dtype)],
                             axis=1).astype(BF16)
    reps = LANES // NSA_DH
    qg = jnp.tile(q_norm[None, :], (1, reps))
    kg = jnp.tile(k_norm, (1, reps))
    cw = cmp_w.astype(BF16)
    row_blocks = []
    for c in range(2):
        for g in range(NSA_KV_HEADS):
            r0 = (c * NSA_KV_HEADS + g) * NSA_DH
            row_blocks.append(jnp.pad(cw[c], ((0, 0), (0, 0), (r0, 2 * LANES - NSA_DH - r0))))
    w_bd = jnp.concatenate(row_blocks, axis=1)
    pe = jnp.broadcast_to(cmp_pos.transpose(1, 0, 2)[:, :, None, :], (L_CMP, 2, NSA_KV_HEADS, NSA_DH))
    pe = pe.reshape(L_CMP, 2 * LANES)
    wa = w_out[:A_WIDTH].astype(BF16)
    wb = w_out[A_WIDTH:].astype(BF16)
    return norm_g[None, :], w_perm, qg, kg, w_bd, pe, wa, wb


def _even_layer(x, pos0, s_ret, ew, gn_gain, next_odd, cache=None, page_table=None, win_past=None):
    batch, seq, d = x.shape
    gain, w_perm, qg, kg, w_bd, pe, wa, wb = ew
    x2d = x.reshape(batch * seq, d)
    pos = pos0 + jnp.arange(seq, dtype=jnp.int32)
    cos_t, sin_t = _rope_tables(pos)
    if seq >= EVEN_IN_TILE:
        n_table_blocks = seq // EVEN_IN_TILE
    else:
        cos_t = jnp.tile(cos_t, (EVEN_IN_TILE // seq, 1))
        sin_t = jnp.tile(sin_t, (EVEN_IN_TILE // seq, 1))
        n_table_blocks = 1
    g, dh = NSA_KV_HEADS, NSA_DH
    if cache is None:
        qa, ka, va, za, qn, full_t, win_t, zb, gates, cmp_rows = _even_in(
            x2d, gain, w_perm, cos_t, sin_t, qg, kg, n_table_blocks, seq_tiles=seq // EVEN_IN_TILE)
        oa, s_fin = _retention(qa, ka, va, za, gn_gain[None, :], s_ret, batch, seq)
        cmp3 = cmp_rows.reshape(batch, seq, 2 * LANES)
        n_half = seq // CMP_STRIDE
        src_specs = [pl.BlockSpec((1, seq, LANES), lambda b: (b, 0, 0)),
                     pl.BlockSpec((1, seq, LANES), lambda b: (b, 0, 1))]
        cmp_kv = _compress([cmp3, cmp3], src_specs, (batch,), (), pe, w_bd, batch, seq, n_half)
        overlap, expand = _selection_tables(n_half, seq)
        ob = _nsa_prompt(qn, gates, zb, cmp_kv, full_t, win_t, overlap, expand, batch, seq)
        keep = min(WINDOW, seq)
        y, *uz = _even_out(x2d, oa, ob, wa, wb, next_odd)
        full_rows = full_t.reshape(batch, 4, g, dh, seq).transpose(0, 4, 1, 2, 3)
        win_rows = win_t[:, :, seq - keep:].reshape(batch, 2, g, dh, keep).transpose(0, 4, 1, 2, 3)
        return y.reshape(batch, seq, d), s_fin, full_rows, win_rows, uz
    else:
        qa, ka, va, za, qn, full_new, win_new, zb, gates = _even_in(x2d, gain, w_perm, cos_t, sin_t, qg, kg,
                                                                     n_table_blocks)
        oa, s_fin = _retention(qa, ka, va, za, gn_gain[None, :], s_ret, batch, seq)
        n_pages = page_table.shape[1]
        past_len = n_pages * PAGE_SIZE
        overlap, expand = _selection_tables(past_len // CMP_STRIDE, past_len + KEY_TILE)
        ob, win_out_t = _nsa_sample(page_table, qn, gates, zb, full_new, win_new, win_past, cache, pe, w_bd,
                                    overlap, expand, batch, seq, n_pages)
    y, *uz = _even_out(x2d, oa, ob, wa, wb, next_odd)
    w_buf = win_out_t.shape[3]
    return (y.reshape(batch, seq, d), s_fin, full_new.reshape(batch, seq, 4, g, dh),
            win_out_t.reshape(batch, 2, g, dh, w_buf).transpose(0, 4, 1, 2, 3), uz)


def _odd_layer(x, uz, s_re, s_im, s5p, d_row, w1, w2, wo, tt):
    batch, seq, d = x.shape
    x2d = x.reshape(batch * seq, d)
    u, zs = uz
    e = u.shape[1]
    a_re, a_im, bmat, cmat = s5p
    n_groups, n_state = s_re.shape[1], s_re.shape[2]
    y, f_re, f_im = _s5(u.reshape(batch, seq, e), s_re.reshape(batch, n_groups * n_state),
                        s_im.reshape(batch, n_groups * n_state), a_re, a_im, bmat, cmat, d_row, tt)
    out = _odd_out(x2d, y.reshape(batch * seq, e), zs, w1, w2, wo).reshape(batch, seq, d)
    return out, f_re.reshape(batch, n_groups, n_state), f_im.reshape(batch, n_groups, n_state)


S5_TIME_TILE = 128


def kernel(x_prompt, x_sample, cache_nsa_kv, cache_nsa_win, state_ret, state_ssm_re, state_ssm_im, page_table,
           norm_even, w_in_even, w_out_even, ret_gn_gain, nsa_q_norm, nsa_k_norm, nsa_cmp_pos, nsa_cmp_w,
           norm_odd, w_in_odd, ssm_lambda_re, ssm_lambda_im, ssm_b_re, ssm_b_im, ssm_c_re, ssm_c_im, ssm_d,
           ssm_log_step, glu_w1, glu_w2, w_out_odd):
    bp, seq_p, _ = x_prompt.shape
    db, seq_s, _ = x_sample.shape
    n_pages = page_table.shape[1]
    past_len = n_pages * PAGE_SIZE
    depth = norm_even.shape[0] + norm_odd.shape[0]
    yp, ys = x_prompt, x_sample
    ret_p, ret_s, kv_p, kv_s, win_p, win_s = [], [], [], [], [], []
    sre_p, sim_p, sre_s, sim_s = [], [], [], []
    for layer in range(depth):
        li = layer // 2
        if layer % 2 == 0:
            ew = _even_weights(norm_even[li], w_in_even[li], w_out_even[li], nsa_q_norm[li], nsa_k_norm[li],
                               nsa_cmp_pos[li], nsa_cmp_w[li])
            next_odd = None
            if layer + 1 < depth:
                next_odd = (norm_odd[li][None, :], w_in_odd[li].astype(BF16))
            s0 = jnp.zeros((bp, RET_HEADS, RET_DK, RET_DV), F32)
            yp, sr, kvr, wr, uz_p = _even_layer(yp, 0, s0, ew, ret_gn_gain[li], next_odd)
            ret_p.append(sr); kv_p.append(kvr); win_p.append(wr)
            cache_t = cache_nsa_kv[li].transpose(0, 2, 3, 4, 1).reshape(cache_nsa_kv.shape[1], 4, LANES, PAGE_SIZE)
            win_t = cache_nsa_win[li].transpose(0, 2, 3, 4, 1).reshape(db, 2, LANES, cache_nsa_win.shape[2])
            ys, sr2, kvr2, wr2, uz_s = _even_layer(ys, past_len, state_ret[li], ew, ret_gn_gain[li], next_odd,
                                                   cache=cache_t, page_table=page_table, win_past=win_t)
            ret_s.append(sr2); kv_s.append(kvr2); win_s.append(wr2)
        else:
            s5p = _s5_params(ssm_lambda_re[li], ssm_lambda_im[li], ssm_b_re[li], ssm_b_im[li],
                             ssm_c_re[li], ssm_c_im[li], ssm_log_step[li])
            w1, w2, wo = glu_w1[li].astype(BF16), glu_w2[li].astype(BF16), w_out_odd[li].astype(BF16)
            d_row = ssm_d[li][None, :]
            n_groups = ssm_lambda_re.shape[1]
            z0 = jnp.zeros((bp, n_groups, S5_STATE), F32)
            yp, fr, fi = _odd_layer(yp, uz_p, z0, z0, s5p, d_row, w1, w2, wo, min(S5_TIME_TILE, seq_p))
            sre_p.append(fr); sim_p.append(fi)
            ys, fr2, fi2 = _odd_layer(ys, uz_s, state_ssm_re[li], state_ssm_im[li], s5p, d_row,
                                      w1, w2, wo, min(S5_TIME_TILE, seq_s))
            sre_s.append(fr2); sim_s.append(fi2)
    return (yp, ys, jnp.stack(ret_p), jnp.stack(ret_s), jnp.stack(kv_p), jnp.stack(kv_s), jnp.stack(win_p),
            jnp.stack(win_s), jnp.stack(sre_p), jnp.stack(sim_p), jnp.stack(sre_s), jnp.stack(sim_s))
```

```python
import functools
import math

import jax
import jax.numpy as jnp
from jax import lax
from jax.experimental import pallas as pl
from jax.experimental.pallas import tpu as pltpu

F32 = jnp.float32
BF16 = jnp.bfloat16

LANES = 128
SUBLANES = 8
VMEM_LIMIT_BYTES = 48 * 2**20

EPS = 1e-6
ROPE_THETA = 10000.0
NEG_INF = -1e30
FORCE = 1e4

RET_HEADS = 4
RET_DK = 64
RET_DV = 128
RET_CHUNK = 128
RET_TILES_PER_STEP = 4
A_WIDTH = RET_HEADS * RET_DV

NSA_HEADS = 8
NSA_KV_HEADS = 2
NSA_DH = 64
NSA_HPG = NSA_HEADS // NSA_KV_HEADS
B_WIDTH = NSA_HEADS * NSA_DH
L_CMP = 32
CMP_STRIDE = 16
CMP_RATIO = L_CMP // CMP_STRIDE
L_SEL = 64
N_SEL = 8
WINDOW = 512
PAGE_SIZE = 128
KV_ROW = 4 * NSA_KV_HEADS * NSA_DH
WIN_ROW = 2 * NSA_KV_HEADS * NSA_DH
KEY_TILE = 128
PROMPT_Q_TILE = 256
WIN_KEYS = WINDOW + PROMPT_Q_TILE
SEL_CHUNK = 512
CMP_SLOT = CMP_STRIDE + 4
SAMPLE_REQS_PER_STEP = 2

S5_GROUP = 16
S5_STATE = 64
S5_LANE_GROUPS = LANES // S5_GROUP
S5_BLOCK_STATE = S5_LANE_GROUPS * S5_STATE
S5_ROW_PAD = 4
S5_MXU_COLS = 256

QA0, KA0, VA0, ZA0, QN0, KVB0, ZB0, GL0 = 0, 256, 512, 1024, 1536, 2048, 2816, 3328
EVEN_COLS = GL0 + LANES
N_GATES = 3 * NSA_HEADS

ROW_TILE = 512
EVEN_IN_TILE = 256


def _cparams(*sem):
    return pltpu.CompilerParams(dimension_semantics=sem, vmem_limit_bytes=VMEM_LIMIT_BYTES)


def _lane_iota(shape):
    return lax.broadcasted_iota(jnp.int32, shape, len(shape) - 1)


def _row_iota(shape):
    return lax.broadcasted_iota(jnp.int32, shape, len(shape) - 2)


def _dot(a, b):
    return jnp.dot(a, b, preferred_element_type=F32)


def _dot_nt(a, b):
    return lax.dot_general(a, b, (((1,), (1,)), ((), ())), preferred_element_type=F32)


def _rmsnorm_rows(x, g):
    return x * lax.rsqrt(jnp.mean(x * x, axis=-1, keepdims=True) + EPS) * g


def _swap_halves(x):
    return pltpu.roll(x, NSA_DH, axis=1)


def _rope_block(x, cos, sin_signed):
    half = NSA_DH // 2
    lane = _lane_iota(x.shape)
    first = (lane % NSA_DH) < half
    partner = jnp.where(first, pltpu.roll(x, LANES - half, axis=1), pltpu.roll(x, half, axis=1))
    return x * cos + partner * sin_signed


def _head_rms_block(x, g):
    lane = _lane_iota(x.shape)
    lo = lane < NSA_DH
    sq = x * x
    s_lo = jnp.sum(jnp.where(lo, sq, 0.0), axis=-1, keepdims=True)
    s_hi = jnp.sum(jnp.where(lo, 0.0, sq), axis=-1, keepdims=True)
    ms = jnp.where(lo, s_lo, s_hi) * (1.0 / NSA_DH)
    return x * lax.rsqrt(ms + EPS) * g


def _even_in_kernel(x_ref, g_ref, w_ref, cos_ref, sin_ref, qg_ref, kg_ref, *out_refs, transposed_kv):
    if transposed_kv:
        qa_ref, ka_ref, va_ref, za_ref, qn_ref, full_ref, win_ref, zb_ref, gl_ref, cmp_ref = out_refs
    else:
        qa_ref, ka_ref, va_ref, za_ref, qn_ref, full_ref, win_ref, zb_ref, gl_ref = out_refs
    hb = _rmsnorm_rows(x_ref[...], g_ref[...]).astype(BF16)
    cos = cos_ref[...]
    sin = sin_ref[...]

    def proj(c0):
        return _dot(hb, w_ref[:, c0:c0 + LANES])

    for j in range(RET_HEADS * RET_DK // LANES):
        c = j * LANES
        qa_ref[:, c:c + LANES] = _rope_block(proj(QA0 + c), cos, sin)
        ka_ref[:, c:c + LANES] = _rope_block(proj(KA0 + c), cos, sin) * (RET_DK ** -0.5)
    for j in range(A_WIDTH // LANES):
        c = j * LANES
        va_ref[:, c:c + LANES] = proj(VA0 + c)
        za_ref[:, c:c + LANES] = jax.nn.silu(proj(ZA0 + c))
    for j in range(B_WIDTH // LANES):
        c = j * LANES
        qn_ref[:, c:c + LANES] = _rope_block(_head_rms_block(proj(QN0 + c), qg_ref[...]), cos, sin)
        zb_ref[:, c:c + LANES] = jax.nn.silu(proj(ZB0 + c))
    for j in range(6):
        y = proj(KVB0 + j * LANES)
        if j % 2 == 0:
            y = _rope_block(_head_rms_block(y, kg_ref[j // 2:j // 2 + 1, :]), cos, sin)
        if transposed_kv:
            if j < 4:
                full_ref[0, j * LANES:(j + 1) * LANES, :] = y.T
            else:
                win_ref[0, (j - 4) * LANES:(j - 3) * LANES, :] = y.T
            if j < 2:
                cmp_ref[:, j * LANES:(j + 1) * LANES] = y
        elif j < 4:
            full_ref[:, j * LANES:(j + 1) * LANES] = y
        else:
            win_ref[:, (j - 4) * LANES:(j - 3) * LANES] = y
    gl_ref[...] = jax.nn.sigmoid(proj(GL0))


def _even_in(x2d, gain, w_bf, cos_t, sin_t, qg, kg, n_table_blocks, seq_tiles=None):
    m, d = x2d.shape
    tile = EVEN_IN_TILE
    grid = (m // tile,)
    row = lambda i: (i, 0)
    fixed = lambda i: (0, 0)
    table = lambda i: (i % n_table_blocks, 0)
    widths = (RET_HEADS * RET_DK, RET_HEADS * RET_DK, A_WIDTH, A_WIDTH, B_WIDTH, KV_ROW, WIN_ROW, B_WIDTH, LANES)
    out_specs = [pl.BlockSpec((tile, w), row) for w in widths]
    out_shape = [jax.ShapeDtypeStruct((m, w), F32) for w in widths]
    if seq_tiles is not None:
        batch, seq = grid[0] // seq_tiles, seq_tiles * tile
        fmajor = lambda i: (i // seq_tiles, 0, i % seq_tiles)
        for idx, w in ((5, KV_ROW), (6, WIN_ROW)):
            out_specs[idx] = pl.BlockSpec((1, w, tile), fmajor)
            out_shape[idx] = jax.ShapeDtypeStruct((batch, w, seq), F32)
        out_specs.append(pl.BlockSpec((tile, 2 * LANES), row))
        out_shape.append(jax.ShapeDtypeStruct((m, 2 * LANES), F32))
    return pl.pallas_call(
        functools.partial(_even_in_kernel, transposed_kv=seq_tiles is not None),
        grid=grid,
        in_specs=[
            pl.BlockSpec((tile, d), row),
            pl.BlockSpec((1, d), fixed),
            pl.BlockSpec((d, EVEN_COLS), fixed),
            pl.BlockSpec((tile, LANES), table),
            pl.BlockSpec((tile, LANES), table),
            pl.BlockSpec((1, LANES), fixed),
            pl.BlockSpec((3, LANES), fixed),
        ],
        out_specs=out_specs,
        out_shape=out_shape,
        compiler_params=_cparams("parallel"),
        name="even_in_proj",
    )(x2d, gain, w_bf, cos_t, sin_t, qg, kg)


def _retention_kernel(q_ref, k_ref, v_ref, z_ref, gn_ref, s0_ref, dmat_ref, qdec_ref, kdec_ref, cdec_ref,
                      o_ref, sfin_ref, s_scr, *, rows, n_tiles):
    n_seq = RET_CHUNK // rows
    c = pl.program_id(1)

    @pl.when(c == 0)
    def _():
        s_scr[...] = s0_ref[...]

    lane = _lane_iota((RET_CHUNK, LANES))
    col_seq = _lane_iota((RET_DK, RET_CHUNK)) >> int(math.log2(rows)) if n_seq > 1 else None
    state = [[s_scr[r, h] for h in range(RET_HEADS)] for r in range(n_seq)]
    for t in range(n_tiles):
        tile = slice(t * RET_CHUNK, (t + 1) * RET_CHUNK)
        q, k, v = q_ref[tile, :], k_ref[tile, :], v_ref[tile, :]
        qd = q * qdec_ref[...]
        kd = k * kdec_ref[...]
        for pair in range(RET_HEADS // 2):
            cols = slice(pair * LANES, (pair + 1) * LANES)
            q2, k2b, qd2 = q[:, cols], k[:, cols].astype(BF16), qd[:, cols]
            kd2_t = kd[:, cols].T
            s_pair = [jnp.concatenate([state[r][2 * pair], state[r][2 * pair + 1]], axis=0).astype(BF16)
                      for r in range(n_seq)]
            for sub in range(2):
                h = 2 * pair + sub
                mine = (lane >= sub * RET_DK) & (lane < (sub + 1) * RET_DK)
                qm = jnp.where(mine, q2, 0.0).astype(BF16)
                qdm = jnp.where(mine, qd2, 0.0).astype(BF16)
                vhb = v[:, h * RET_DV:(h + 1) * RET_DV].astype(BF16)
                intra = _dot((_dot_nt(qm, k2b) * dmat_ref[h]).astype(BF16), vhb)
                cross = jnp.concatenate([_dot(qdm[r * rows:(r + 1) * rows], s_pair[r]) for r in range(n_seq)],
                                        axis=0)
                out = intra + cross
                kt = kd2_t[sub * RET_DK:(sub + 1) * RET_DK, :]
                if n_seq > 1:
                    kt = jnp.concatenate([jnp.where(col_seq == r, kt, 0.0) for r in range(n_seq)], axis=0)
                kv = _dot(kt.astype(BF16), vhb)
                for r in range(n_seq):
                    state[r][h] = state[r][h] * cdec_ref[h] + kv[r * RET_DK:(r + 1) * RET_DK]
                mu = jnp.mean(out, axis=-1, keepdims=True)
                cen = out - mu
                var = jnp.mean(cen * cen, axis=-1, keepdims=True)
                y = cen * lax.rsqrt(var + EPS) * gn_ref[:, h * RET_DV:(h + 1) * RET_DV]
                o_ref[tile, h * RET_DV:(h + 1) * RET_DV] = y * z_ref[tile, h * RET_DV:(h + 1) * RET_DV]
    for r in range(n_seq):
        for h in range(RET_HEADS):
            s_scr[r, h] = state[r][h]

    @pl.when(c == pl.num_programs(1) - 1)
    def _():
        sfin_ref[...] = s_scr[...]


def _retention_tables(rows):
    log_g = jnp.log(1.0 - 2.0 ** (-5.0 - jnp.arange(RET_HEADS, dtype=F32)))
    idx = jnp.arange(RET_CHUNK, dtype=jnp.int32)
    pos = (idx % rows).astype(F32)
    seq = idx // rows
    diff = pos[:, None] - pos[None, :]
    causal = (diff >= 0) & (seq[:, None] == seq[None, :])
    dmat = jnp.exp(jnp.where(causal, diff, 0.0)[None] * log_g[:, None, None]) * causal[None]
    qdec = jnp.exp((pos + 1.0)[:, None] * log_g[None, :])
    kdec = jnp.exp((rows - 1.0 - pos)[:, None] * log_g[None, :])
    qdec = jnp.repeat(qdec, RET_DK, axis=1)
    kdec = jnp.repeat(kdec, RET_DK, axis=1)
    cdec = jnp.broadcast_to(jnp.exp(rows * log_g)[:, None, None], (RET_HEADS, RET_DK, RET_DV))
    return dmat, qdec, kdec, cdec


def _retention(qa, ka, va, za, gn_gain, s0, batch, seq):
    rows = min(RET_CHUNK, seq)
    n_seq = RET_CHUNK // rows
    n_chunks = seq // rows
    n_tiles = min(RET_TILES_PER_STEP, n_chunks)
    steps = n_chunks // n_tiles
    dmat, qdec, kdec, cdec = _retention_tables(rows)
    tok = lambda b, c: (b * steps + c, 0)
    fixed2 = lambda b, c: (0, 0)
    fixed3 = lambda b, c: (0, 0, 0)
    state = lambda b, c: (b, 0, 0, 0)
    qk_w = RET_HEADS * RET_DK
    step_rows = n_tiles * RET_CHUNK
    return pl.pallas_call(
        functools.partial(_retention_kernel, rows=rows, n_tiles=n_tiles),
        grid=(batch // n_seq, steps),
        in_specs=[
            pl.BlockSpec((step_rows, qk_w), tok),
            pl.BlockSpec((step_rows, qk_w), tok),
            pl.BlockSpec((step_rows, A_WIDTH), tok),
            pl.BlockSpec((step_rows, A_WIDTH), tok),
            pl.BlockSpec((1, A_WIDTH), fixed2),
            pl.BlockSpec((n_seq, RET_HEADS, RET_DK, RET_DV), state),
            pl.BlockSpec((RET_HEADS, RET_CHUNK, RET_CHUNK), fixed3),
            pl.BlockSpec((RET_CHUNK, qk_w), fixed2),
            pl.BlockSpec((RET_CHUNK, qk_w), fixed2),
            pl.BlockSpec((RET_HEADS, RET_DK, RET_DV), fixed3),
        ],
        out_specs=[
            pl.BlockSpec((step_rows, A_WIDTH), tok),
            pl.BlockSpec((n_seq, RET_HEADS, RET_DK, RET_DV), state),
        ],
        out_shape=[
            jax.ShapeDtypeStruct((batch * seq, A_WIDTH), F32),
            jax.ShapeDtypeStruct((batch, RET_HEADS, RET_DK, RET_DV), F32),
        ],
        scratch_shapes=[pltpu.VMEM((n_seq, RET_HEADS, RET_DK, RET_DV), F32)],
        compiler_params=_cparams("parallel", "arbitrary"),
        name="retention",
    )(qa, ka, va, za, gn_gain, s0, dmat, qdec, kdec, cdec)


def _compress_kernel(*refs, n_pref, n_src, src_rows):
    refs = refs[n_pref:]
    src_k = refs[:n_src]
    src_v = refs[n_src:2 * n_src]
    pe_ref, w_ref, o_ref = refs[2 * n_src:]
    per_src = src_rows // CMP_STRIDE
    n_half = n_src * per_src
    acc = [jnp.zeros((n_half, 2 * LANES), F32) for _ in range(CMP_RATIO)]

    def strided_rows(srcs, l):
        rows = [s[0, pl.ds(l, per_src, stride=CMP_STRIDE), :] for s in srcs]
        return rows[0] if n_src == 1 else jnp.concatenate(rows, axis=0)

    for l in range(CMP_STRIDE):
        x = jnp.concatenate([strided_rows(src_k, l), strided_rows(src_v, l)], axis=1)
        for r in range(CMP_RATIO):
            i = r * CMP_STRIDE + l
            acc[r] = acc[r] + _dot((x + pe_ref[i:i + 1, :]).astype(BF16), w_ref[i])
    out = acc[0]
    for r in range(1, CMP_RATIO):
        out = out + pltpu.roll(acc[r], n_half - r, axis=0)
    live = _row_iota(out.shape) < n_half - CMP_RATIO + 1
    o_ref[0] = jnp.where(live, out, 0.0)


def _compress(src_arrays, src_specs, grid, extra_prefetch, pe, w_bd, batch, src_rows, n_half):
    n_src = len(src_specs) // 2
    n_pref = len(extra_prefetch)
    kern = functools.partial(_compress_kernel, n_pref=n_pref, n_src=n_src, src_rows=src_rows)
    fixed2 = lambda *a: (0, 0)
    fixed3 = lambda *a: (0, 0, 0)
    out_map = lambda b, *a: (b, 0, 0)
    return pl.pallas_call(
        kern,
        grid_spec=pltpu.PrefetchScalarGridSpec(
            num_scalar_prefetch=n_pref,
            grid=grid,
            in_specs=list(src_specs) + [
                pl.BlockSpec((L_CMP, 2 * LANES), fixed2),
                pl.BlockSpec((L_CMP, 2 * LANES, 2 * LANES), fixed3),
            ],
            out_specs=pl.BlockSpec((1, n_half, 2 * LANES), out_map),
        ),
        out_shape=jax.ShapeDtypeStruct((batch, n_half, 2 * LANES), F32),
        compiler_params=_cparams("parallel"),
        name="kv_compress",
    )(*extra_prefetch, *src_arrays, pe, w_bd)


def _split3_bf16(x):
    hi = x.astype(BF16)
    r1 = x - hi.astype(F32)
    mid = r1.astype(BF16)
    lo = (r1 - mid.astype(F32)).astype(BF16)
    return hi, mid, lo


def _stack_group_queries(q, g):
    lane = _lane_iota((q.shape[0], LANES))
    in_g = (lane >= g * NSA_DH) & (lane < (g + 1) * NSA_DH)
    parts = []
    for hh in range(NSA_HPG):
        h = g * NSA_HPG + hh
        two = q[:, (h // 2) * LANES:(h // 2 + 1) * LANES]
        if h % 2 != g:
            two = _swap_halves(two)
        parts.append(jnp.where(in_g, two, 0.0))
    return (jnp.concatenate(parts, axis=0) * (NSA_DH ** -0.5)).astype(BF16)


def _masked_softmax(s, ok):
    s = jnp.where(ok, s, NEG_INF)
    e = jnp.exp(s - jnp.max(s, axis=-1, keepdims=True))
    return jnp.where(ok, e * (1.0 / jnp.sum(e, axis=-1, keepdims=True)), 0.0)


def _sum_heads(p1, tq):
    psum = p1[0:tq]
    for hh in range(1, NSA_HPG):
        psum = psum + p1[hh * tq:(hh + 1) * tq]
    return psum


def _choose_blocks(psum, t0, tq, overlap, n_blocks):
    rows = psum.shape[0]
    imp = sum(_dot(t, overlap) for t in _split3_bf16(psum))
    shift = int(math.log2(L_SEL))

    def ranked(score, blk, valid, take):
        rank = jnp.zeros(score.shape, F32)
        for s in range(n_blocks):
            cand = take(s)
            rank = rank + ((cand > score) | ((cand == score) & (blk > s))).astype(F32)
        return ((rank < N_SEL) & valid).astype(F32)

    if rows % LANES:
        tpos = jnp.concatenate([t0 + _row_iota((tq, 1))] * (rows // tq), axis=0)
        blk = _lane_iota((rows, LANES))
        cur = tpos >> shift
        forced = (blk == 0) | (blk == cur) | (blk == cur - 1)
        valid = (blk * L_SEL <= tpos) & (blk < n_blocks)
        score = jnp.where(valid, jnp.where(forced, FORCE, imp), -FORCE)
        return ranked(score, blk, valid, lambda s: score[:, s:s + 1]).astype(BF16)

    nbp = -(-n_blocks // SUBLANES) * SUBLANES
    tiles = rows // LANES
    imp_t = jnp.concatenate([imp[i * LANES:(i + 1) * LANES].T[:nbp] for i in range(tiles)], axis=1)
    tpos = jnp.concatenate([t0 + _lane_iota((1, tq))] * (rows // tq), axis=1)
    blk = _row_iota((nbp, rows))
    cur = tpos >> shift
    forced = (blk == 0) | (blk == cur) | (blk == cur - 1)
    valid = (blk * L_SEL <= tpos) & (blk < n_blocks)
    score = jnp.where(valid, jnp.where(forced, FORCE, imp_t), -FORCE)
    chosen_t = ranked(score, blk, valid, lambda s: score[s:s + 1, :])
    chosen_t = jnp.concatenate([chosen_t, jnp.zeros((LANES - nbp, rows), F32)], axis=0)
    return jnp.concatenate([chosen_t[:, i * LANES:(i + 1) * LANES].T for i in range(tiles)],
                           axis=0).astype(BF16)


def _gate_and_store(gates, zs, branch_outs, o_ref, row0=0):
    tq = gates.shape[0]
    lane = _lane_iota((tq, LANES))
    ext = []
    for g in range(NSA_KV_HEADS):
        o_cmp, o_sel, o_win = branch_outs[g]
        for hh in range(NSA_HPG):
            h = g * NSA_HPG + hh
            rows = slice(hh * tq, (hh + 1) * tq)
            o = (gates[:, 3 * h:3 * h + 1] * o_cmp[rows] + gates[:, 3 * h + 1:3 * h + 2] * o_sel[rows]
                 + gates[:, 3 * h + 2:3 * h + 3] * o_win[rows])
            ext.append(_swap_halves(o) if h % 2 != g else o)
    for j in range(NSA_HEADS // 2):
        both = jnp.where(lane < NSA_DH, ext[2 * j], ext[2 * j + 1])
        o_ref[row0:row0 + tq, j * LANES:(j + 1) * LANES] = both * zs[:, j * LANES:(j + 1) * LANES]


def _nsa_prompt_kernel(q_ref, g_ref, z_ref, cmp_ref, sel_ref, win_ref, ov_ref, ex_ref, o_ref, *, n_blocks):
    i = pl.program_id(1)
    tq = q_ref.shape[0]
    q = q_ref[...]
    t0 = i * tq
    tpos = t0 + _row_iota((tq, 1))
    tpos_all = jnp.concatenate([tpos] * NSA_HEADS, axis=0)
    grp_rows = NSA_HPG * tq
    qs = jnp.concatenate([_stack_group_queries(q, g) for g in range(NSA_KV_HEADS)], axis=0)

    cmp_blk = cmp_ref[0].astype(BF16)
    cmp_ok = (_lane_iota((1, cmp_blk.shape[0])) * CMP_STRIDE + (L_CMP - 1)) <= tpos_all
    p1 = _masked_softmax(_dot_nt(qs, cmp_blk[:, :LANES]), cmp_ok)
    o_cmp = _dot(p1.astype(BF16), cmp_blk[:, LANES:])
    psum = jnp.concatenate([_sum_heads(p1[g * grp_rows:(g + 1) * grp_rows], tq)
                            for g in range(NSA_KV_HEADS)], axis=0)
    chosen = _choose_blocks(psum, t0, tq, ov_ref[...], n_blocks)

    def sel_chunk(c, carry):
        m, l, acc = carry
        k0 = pl.multiple_of(c * SEL_CHUNK, SEL_CHUNK)
        k_t = sel_ref[0, 0:LANES, pl.ds(k0, SEL_CHUNK)].astype(BF16)
        v_t = sel_ref[0, LANES:2 * LANES, pl.ds(k0, SEL_CHUNK)].astype(BF16)
        causal = jnp.concatenate([(k0 + _lane_iota((1, SEL_CHUNK))) <= tpos] * NSA_KV_HEADS, axis=0)
        key_on = (_dot(chosen, ex_ref[:, pl.ds(k0, SEL_CHUNK)]) > 0.5) & causal
        bias = jnp.where(key_on, 0.0, NEG_INF)
        bias = jnp.concatenate([bias[g * tq:(g + 1) * tq] for g in range(NSA_KV_HEADS)
                                for _ in range(NSA_HPG)], axis=0)
        s = _dot(qs, k_t) + bias
        m_new = jnp.maximum(m, jnp.max(s, axis=-1, keepdims=True))
        alpha = jnp.exp(m - m_new)
        e = jnp.exp(s - m_new)
        l_new = alpha * l + jnp.sum(e, axis=-1, keepdims=True)
        return m_new, l_new, alpha * acc + _dot_nt(e.astype(BF16), v_t)

    rows = NSA_HEADS * tq
    init = (jnp.full((rows, 1), NEG_INF, F32), jnp.zeros((rows, 1), F32), jnp.zeros((rows, LANES), F32))
    n_chunks = t0 // SEL_CHUNK + 1
    _, l2, acc2 = lax.fori_loop(0, n_chunks, sel_chunk, init)
    o_sel = acc2 / l2

    w0 = pl.multiple_of(jnp.maximum(t0 - WINDOW, 0), KEY_TILE)
    wk_t = win_ref[0, 0:LANES, pl.ds(w0, WIN_KEYS)].astype(BF16)
    wv_t = win_ref[0, LANES:2 * LANES, pl.ds(w0, WIN_KEYS)].astype(BF16)
    win_kpos = w0 + _lane_iota((1, WIN_KEYS))
    win_ok = (win_kpos <= tpos_all) & (win_kpos > tpos_all - WINDOW)
    s3 = jnp.where(win_ok, _dot(qs, wk_t), NEG_INF)
    e3 = jnp.exp(s3 - jnp.max(s3, axis=-1, keepdims=True))
    o_win = _dot_nt(e3.astype(BF16), wv_t) / jnp.sum(e3, axis=-1, keepdims=True)

    branch_outs = [(o_cmp[g * grp_rows:(g + 1) * grp_rows], o_sel[g * grp_rows:(g + 1) * grp_rows],
                    o_win[g * grp_rows:(g + 1) * grp_rows]) for g in range(NSA_KV_HEADS)]
    _gate_and_store(g_ref[...], z_ref[...], branch_outs, o_ref)


def _nsa_prompt(qn, gates, zb, cmp_kv, full_t, win_t, overlap, expand, batch, seq):
    tq = PROMPT_Q_TILE
    nq = seq // tq
    tok = lambda b, i: (b * nq + i, 0)
    per_b = lambda b, i: (b, 0, 0)
    sel_half = lambda b, i: (b, 1, 0)
    fixed = lambda b, i: (0, 0)
    n_blocks = -(-seq // L_SEL)
    return pl.pallas_call(
        functools.partial(_nsa_prompt_kernel, n_blocks=n_blocks),
        grid=(batch, nq),
        in_specs=[
            pl.BlockSpec((tq, B_WIDTH), tok),
            pl.BlockSpec((tq, LANES), tok),
            pl.BlockSpec((tq, B_WIDTH), tok),
            pl.BlockSpec((1, cmp_kv.shape[1], 2 * LANES), per_b),
            pl.BlockSpec((1, 2 * LANES, seq), sel_half),
            pl.BlockSpec((1, WIN_ROW, seq), per_b),
            pl.BlockSpec(overlap.shape, fixed),
            pl.BlockSpec(expand.shape, fixed),
        ],
        out_specs=pl.BlockSpec((tq, B_WIDTH), tok),
        out_shape=jax.ShapeDtypeStruct((batch * seq, B_WIDTH), F32),
        compiler_params=_cparams("parallel", "arbitrary"),
        name="nsa_prompt",
    )(qn, gates, zb, cmp_kv, full_t, win_t, overlap, expand)


def _nsa_sample_kernel(*refs, nb, n_pages, past_len, n_blocks):
    q_ref, g_ref, z_ref, newf_ref, neww_ref, winp_ref = refs[1:7]
    pages = refs[7:7 + nb * n_pages]
    pe_ref, w_ref, ov_ref, ex_ref, o_ref, wout_ref, xs_ref = refs[7 + nb * n_pages:]
    tq = q_ref.shape[0] // nb
    w_buf = winp_ref.shape[3]
    halves_per_page = PAGE_SIZE // CMP_STRIDE
    n_half = past_len // CMP_STRIDE
    total_half = nb * n_half

    pad = jnp.zeros((KEY_TILE - tq, LANES), F32)
    lane = _lane_iota((LANES, LANES))
    per_req = NSA_HEADS * tq
    tpos = past_len + _row_iota((tq, 1))
    tpos_all = jnp.concatenate([tpos] * (nb * NSA_HEADS), axis=0)
    tpos_grp = jnp.concatenate([tpos] * (nb * NSA_KV_HEADS), axis=0)

    def padded_rows(x):
        return jnp.concatenate([x, pad], axis=0)

    def per_request(fn):
        return jnp.concatenate([fn(j, slice(j * per_req, (j + 1) * per_req)) for j in range(nb)], axis=0)

    tok = [slice(j * tq, (j + 1) * tq) for j in range(nb)]
    qs = jnp.concatenate([_stack_group_queries(q_ref[tok[j], :], g)
                          for j in range(nb) for g in range(NSA_KV_HEADS)], axis=0)
    mine = [pages[j * n_pages:(j + 1) * n_pages] for j in range(nb)]
    newf = [newf_ref[tok[j], :] for j in range(nb)]
    neww = [neww_ref[tok[j], :] for j in range(nb)]

    sel_pieces = [[] for _ in range(nb)]
    for j in range(nb * n_pages):
        for c in range(2):
            x = pages[j][0, c].T
            for hb in range(halves_per_page):
                slot = (j * halves_per_page + hb) * CMP_SLOT
                xs_ref[c, slot:slot + CMP_STRIDE, :] = x[hb * CMP_STRIDE:(hb + 1) * CMP_STRIDE]
        req = j // n_pages
        sel_pieces[req].append(_dot(qs[req * per_req:(req + 1) * per_req], pages[j][0, 2].astype(BF16)))
    s2_raw = per_request(lambda j, r: jnp.concatenate(
        sel_pieces[j] + [_dot_nt(qs[r], padded_rows(newf[j][:, 2 * LANES:3 * LANES]).astype(BF16))], axis=1))
    s3_raw = per_request(lambda j, r: jnp.concatenate(
        [_dot(qs[r], winp_ref[j, 0].astype(BF16)),
         _dot_nt(qs[r], padded_rows(neww[j][:, :LANES]).astype(BF16))], axis=1))

    for j in range(nb):
        for c in range(2):
            shifted = pltpu.roll(winp_ref[j, c], w_buf - tq, axis=1)
            new_cols = pltpu.roll(padded_rows(neww[j][:, c * LANES:(c + 1) * LANES]).T, LANES - tq, axis=1)
            wout_ref[j, c, :, 0:w_buf - LANES] = shifted[:, 0:w_buf - LANES]
            wout_ref[j, c, :, w_buf - LANES:w_buf] = jnp.where(lane >= LANES - tq, new_cols,
                                                               shifted[:, w_buf - LANES:w_buf])

    acc = [jnp.zeros((total_half, 2 * LANES), F32) for _ in range(CMP_RATIO)]
    for l in range(CMP_STRIDE):
        x = jnp.concatenate([xs_ref[c, pl.ds(l, total_half, stride=CMP_SLOT), :] for c in range(2)], axis=1)
        for r in range(CMP_RATIO):
            i = r * CMP_STRIDE + l
            acc[r] = acc[r] + _dot((x + pe_ref[i:i + 1, :]).astype(BF16), w_ref[i])
    cmp_all = acc[0]
    for r in range(1, CMP_RATIO):
        cmp_all = cmp_all + pltpu.roll(acc[r], total_half - r, axis=0)
    live = _row_iota((n_half, 2 * LANES)) < n_half - CMP_RATIO + 1
    cmp_blk = [jnp.where(live, cmp_all[j * n_half:(j + 1) * n_half], 0.0).astype(BF16) for j in range(nb)]

    cmp_ok = (_lane_iota((1, n_half)) * CMP_STRIDE + (L_CMP - 1)) <= tpos_all
    p1 = _masked_softmax(per_request(lambda j, r: _dot_nt(qs[r], cmp_blk[j][:, :LANES])), cmp_ok)
    p1b = p1.astype(BF16)
    o_cmp = per_request(lambda j, r: _dot(p1b[r], cmp_blk[j][:, LANES:]))

    grp_rows = NSA_HPG * tq
    psum = jnp.concatenate([_sum_heads(p1[i * grp_rows:(i + 1) * grp_rows], tq)
                            for i in range(nb * NSA_KV_HEADS)], axis=0)
    chosen = _choose_blocks(psum, past_len, tq, ov_ref[...], n_blocks)
    key_on = (_dot(chosen, ex_ref[...]) > 0.5) & (_lane_iota((1, past_len + KEY_TILE)) <= tpos_grp)
    bias = jnp.where(key_on, 0.0, NEG_INF)
    bias = jnp.concatenate([bias[i * tq:(i + 1) * tq] for i in range(nb * NSA_KV_HEADS)
                            for _ in range(NSA_HPG)], axis=0)

    s2 = s2_raw + bias
    e2 = jnp.exp(s2 - jnp.max(s2, axis=-1, keepdims=True))
    l2 = jnp.sum(e2, axis=-1, keepdims=True)
    e2 = e2.astype(BF16)

    def sel_values(j, r):
        v_t = jnp.concatenate([p[0, 3].astype(BF16) for p in mine[j]], axis=1)
        v_new = padded_rows(newf[j][:, 3 * LANES:4 * LANES]).astype(BF16)
        return _dot_nt(e2[r, :past_len], v_t) + _dot(e2[r, past_len:], v_new)

    o_sel = per_request(sel_values) / l2

    win_kpos = (past_len - w_buf) + _lane_iota((1, w_buf + KEY_TILE))
    win_ok = (win_kpos <= tpos_all) & (win_kpos > tpos_all - WINDOW)
    s3 = jnp.where(win_ok, s3_raw, NEG_INF)
    e3 = jnp.exp(s3 - jnp.max(s3, axis=-1, keepdims=True))
    l3 = jnp.sum(e3, axis=-1, keepdims=True)
    e3 = e3.astype(BF16)
    o_win = per_request(lambda j, r: _dot_nt(e3[r, :w_buf], winp_ref[j, 1].astype(BF16))
                        + _dot(e3[r, w_buf:], padded_rows(neww[j][:, LANES:]).astype(BF16))) / l3

    for j in range(nb):
        branch_outs = []
        for g in range(NSA_KV_HEADS):
            r = slice((j * NSA_KV_HEADS + g) * grp_rows, (j * NSA_KV_HEADS + g + 1) * grp_rows)
            branch_outs.append((o_cmp[r], o_sel[r], o_win[r]))
        _gate_and_store(g_ref[tok[j], :], z_ref[tok[j], :], branch_outs, o_ref, row0=j * tq)


def _nsa_sample(page_table, qn, gates, zb, full_new, win_new, win_t, cache_t, pe, w_bd, overlap, expand,
                batch, tq, n_pages):
    nb = SAMPLE_REQS_PER_STEP
    past_len = n_pages * PAGE_SIZE
    w_buf = win_t.shape[3]
    n_blocks = -(-(past_len + tq) // L_SEL)
    tok = lambda b, pt: (b, 0)
    per_b = lambda b, pt: (b, 0, 0, 0)
    fixed2 = lambda b, pt: (0, 0)
    fixed3 = lambda b, pt: (0, 0, 0)

    def page_spec(j, p):
        return pl.BlockSpec((1, 4, LANES, PAGE_SIZE), lambda b, pt: (pt[b * nb + j, p], 0, 0, 0))

    n_slots = nb * n_pages * (PAGE_SIZE // CMP_STRIDE)
    return pl.pallas_call(
        functools.partial(_nsa_sample_kernel, nb=nb, n_pages=n_pages, past_len=past_len, n_blocks=n_blocks),
        grid_spec=pltpu.PrefetchScalarGridSpec(
            num_scalar_prefetch=1,
            grid=(batch // nb,),
            in_specs=[
                pl.BlockSpec((nb * tq, B_WIDTH), tok),
                pl.BlockSpec((nb * tq, LANES), tok),
                pl.BlockSpec((nb * tq, B_WIDTH), tok),
                pl.BlockSpec((nb * tq, KV_ROW), tok),
                pl.BlockSpec((nb * tq, WIN_ROW), tok),
                pl.BlockSpec((nb, 2, LANES, w_buf), per_b),
            ] + [page_spec(j, p) for j in range(nb) for p in range(n_pages)] + [
                pl.BlockSpec(pe.shape, fixed2),
                pl.BlockSpec(w_bd.shape, fixed3),
                pl.BlockSpec(overlap.shape, fixed2),
                pl.BlockSpec(expand.shape, fixed2),
            ],
            out_specs=[
                pl.BlockSpec((nb * tq, B_WIDTH), tok),
                pl.BlockSpec((nb, 2, LANES, w_buf), per_b),
            ],
            scratch_shapes=[pltpu.VMEM((2, n_slots * CMP_SLOT, LANES), F32)],
        ),
        out_shape=[
            jax.ShapeDtypeStruct((batch * tq, B_WIDTH), F32),
            jax.ShapeDtypeStruct((batch, 2, LANES, w_buf), F32),
        ],
        compiler_params=_cparams("parallel"),
        name="nsa_sample",
    )(page_table, qn, gates, zb, full_new, win_new, win_t, *([cache_t] * (nb * n_pages)), pe, w_bd,
      overlap, expand)


def _selection_tables(n_cmp_rows, n_sel_keys):
    n = jnp.arange(n_cmp_rows, dtype=jnp.int32)[:, None]
    s = jnp.arange(LANES, dtype=jnp.int32)[None, :]
    c_start = n * CMP_STRIDE
    s_start = s * L_SEL
    overlap = ((c_start < s_start + L_SEL) & (s_start < c_start + L_CMP)).astype(BF16)
    key = jnp.arange(n_sel_keys, dtype=jnp.int32)[None, :]
    expand = ((key // L_SEL) == jnp.arange(LANES, dtype=jnp.int32)[:, None]).astype(BF16)
    return overlap, expand


def _even_out_kernel(x_ref, oa_ref, ob_ref, wa_ref, wb_ref, *rest, with_next):
    y = (x_ref[...] + _dot(oa_ref[...].astype(BF16), wa_ref[...])
         + _dot(ob_ref[...].astype(BF16), wb_ref[...]))
    if not with_next:
        (y_ref,) = rest
        y_ref[...] = y
        return
    g_ref, w_ref, y_ref, u_ref, z_ref = rest
    y_ref[...] = y
    hb = _rmsnorm_rows(y, g_ref[...]).astype(BF16)
    e = u_ref.shape[1]
    u_ref[...] = _dot(hb, w_ref[:, :e])
    z_ref[...] = jax.nn.silu(_dot(hb, w_ref[:, e:]))


def _even_out(x2d, oa, ob, wa, wb, next_odd=None):
    m, d = x2d.shape
    row = lambda i: (i, 0)
    fixed = lambda i: (0, 0)
    in_specs = [
        pl.BlockSpec((ROW_TILE, d), row),
        pl.BlockSpec((ROW_TILE, A_WIDTH), row),
        pl.BlockSpec((ROW_TILE, B_WIDTH), row),
        pl.BlockSpec((A_WIDTH, d), fixed),
        pl.BlockSpec((B_WIDTH, d), fixed),
    ]
    out_specs = [pl.BlockSpec((ROW_TILE, d), row)]
    out_shape = [jax.ShapeDtypeStruct((m, d), F32)]
    args = [x2d, oa, ob, wa, wb]
    if next_odd is not None:
        gain, w_bf = next_odd
        e = w_bf.shape[1] // 2
        in_specs += [pl.BlockSpec((1, d), fixed), pl.BlockSpec((d, 2 * e), fixed)]
        out_specs += [pl.BlockSpec((ROW_TILE, e), row)] * 2
        out_shape += [jax.ShapeDtypeStruct((m, e), F32)] * 2
        args += [gain, w_bf]
    return pl.pallas_call(
        functools.partial(_even_out_kernel, with_next=next_odd is not None),
        grid=(m // ROW_TILE,),
        in_specs=in_specs,
        out_specs=out_specs,
        out_shape=out_shape,
        compiler_params=_cparams("parallel"),
        name="even_out_proj",
    )(*args)


def _s5_kernel(u_ref, x0r_ref, x0i_ref, ar_ref, ai_ref, bm_ref, cm_ref, d_ref,
               y_ref, fr_ref, fi_ref, st_re, st_im, ubuf_a, ubuf_b, xbuf_a, xbuf_b, ybuf, *, nb, tt):
    t_idx = pl.program_id(1)
    n_blk = bm_ref.shape[0]
    half = S5_BLOCK_STATE
    n_chunk = half // LANES
    pitch = tt + S5_ROW_PAD
    ubufs, xbufs = (ubuf_a, ubuf_b), (xbuf_a, xbuf_b)
    wide = S5_MXU_COLS // LANES
    n_slices = 2 * n_chunk // wide

    @pl.when(t_idx == 0)
    def _():
        for kb in range(n_blk):
            st_re[kb] = x0r_ref[:, kb * half:(kb + 1) * half]
            st_im[kb] = x0i_ref[:, kb * half:(kb + 1) * half]

    for ubuf in ubufs:
        for b in range(nb):
            ubuf[b * pitch + tt:(b + 1) * pitch, :] = jnp.zeros((S5_ROW_PAD, LANES), F32)

    def lanes_of(kb):
        return slice(kb * LANES, (kb + 1) * LANES)

    def stage_u(kb):
        for b in range(nb):
            ubufs[kb % 2][b * pitch:b * pitch + tt, :] = u_ref[b, :, lanes_of(kb)]

    def input_slice(kb, j):
        cols = slice(j * S5_MXU_COLS, (j + 1) * S5_MXU_COLS)
        bu = _dot(ubufs[kb % 2][...].astype(BF16), bm_ref[kb, :, cols])
        for i in range(wide):
            xbufs[kb % 2][j * wide + i] = bu[:, i * LANES:(i + 1) * LANES]

    def output_slice(kb, j, acc):
        x = jnp.concatenate([xbufs[kb % 2][j * wide + i] for i in range(wide)], axis=1).astype(BF16)
        part = _dot(x, cm_ref[kb, j * S5_MXU_COLS:(j + 1) * S5_MXU_COLS, :])
        return part if acc is None else acc + part

    def finish_y(kb, acc):
        ybuf[...] = acc + d_ref[:, lanes_of(kb)] * ubufs[kb % 2][...]
        for b in range(nb):
            y_ref[b, :, lanes_of(kb)] = ybuf[b * pitch:b * pitch + tt, :]

    stage_u(0)
    for j in range(n_slices):
        input_slice(0, j)
    for kb in range(n_blk + 1):
        side, acc = [], [None]
        if kb >= 1:
            def out_task(j, kb=kb):
                acc[0] = output_slice(kb - 1, j, acc[0])
            side += [functools.partial(out_task, j) for j in range(n_slices)]
            side.append(lambda kb=kb: finish_y(kb - 1, acc[0]))
        if kb + 1 < n_blk:
            side.append(functools.partial(stage_u, kb + 1))
            side += [functools.partial(input_slice, kb + 1, j) for j in range(n_slices)]
        if kb == n_blk:
            for task in side:
                task()
            break
        xbuf = xbufs[kb % 2]
        a_re = jnp.broadcast_to(ar_ref[kb], (nb, half))
        a_im = jnp.broadcast_to(ai_ref[kb], (nb, half))
        s_re, s_im = st_re[kb], st_im[kb]
        every = -(-tt // (len(side) + 1))
        for t in range(tt):
            if t % every == every - 1 and side:
                side.pop(0)()
            rows = pl.ds(t, nb, stride=pitch)
            b_re = jnp.concatenate([xbuf[j, rows, :] for j in range(n_chunk)], axis=1)
            b_im = jnp.concatenate([xbuf[n_chunk + j, rows, :] for j in range(n_chunk)], axis=1)
            s_re, s_im = a_re * s_re - a_im * s_im + b_re, a_re * s_im + a_im * s_re + b_im
            for j in range(n_chunk):
                xbuf[j, rows, :] = s_re[:, j * LANES:(j + 1) * LANES]
                xbuf[n_chunk + j, rows, :] = s_im[:, j * LANES:(j + 1) * LANES]
        for task in side:
            task()
        st_re[kb] = s_re
        st_im[kb] = s_im

    @pl.when(t_idx == pl.num_programs(1) - 1)
    def _():
        for kb in range(n_blk):
            fr_ref[:, kb * half:(kb + 1) * half] = st_re[kb]
            fi_ref[:, kb * half:(kb + 1) * half] = st_im[kb]


def _s5(u3, x0_re, x0_im, a_re, a_im, bmat, cmat, d_row, tt):
    batch, seq, e = u3.shape
    nb = SUBLANES
    n_blk = e // LANES
    n_state = n_blk * S5_BLOCK_STATE
    seq_map = lambda b, t: (b, t, 0)
    st_map = lambda b, t: (b, 0)
    fixed2 = lambda b, t: (0, 0)
    fixed3 = lambda b, t: (0, 0, 0)
    return pl.pallas_call(
        functools.partial(_s5_kernel, nb=nb, tt=tt),
        grid=(batch // nb, seq // tt),
        in_specs=[
            pl.BlockSpec((nb, tt, e), seq_map),
            pl.BlockSpec((nb, n_state), st_map),
            pl.BlockSpec((nb, n_state), st_map),
            pl.BlockSpec((n_blk, 1, S5_BLOCK_STATE), fixed3),
            pl.BlockSpec((n_blk, 1, S5_BLOCK_STATE), fixed3),
            pl.BlockSpec((n_blk, LANES, 2 * S5_BLOCK_STATE), fixed3),
            pl.BlockSpec((n_blk, 2 * S5_BLOCK_STATE, LANES), fixed3),
            pl.BlockSpec((1, e), fixed2),
        ],
        out_specs=[
            pl.BlockSpec((nb, tt, e), seq_map),
            pl.BlockSpec((nb, n_state), st_map),
            pl.BlockSpec((nb, n_state), st_map),
        ],
        out_shape=[
            jax.ShapeDtypeStruct((batch, seq, e), F32),
            jax.ShapeDtypeStruct((batch, n_state), F32),
            jax.ShapeDtypeStruct((batch, n_state), F32),
        ],
        scratch_shapes=[
            pltpu.VMEM((n_blk, nb, S5_BLOCK_STATE), F32),
            pltpu.VMEM((n_blk, nb, S5_BLOCK_STATE), F32),
            pltpu.VMEM((nb * (tt + S5_ROW_PAD), LANES), F32),
            pltpu.VMEM((nb * (tt + S5_ROW_PAD), LANES), F32),
            pltpu.VMEM((2 * S5_BLOCK_STATE // LANES, nb * (tt + S5_ROW_PAD), LANES), F32),
            pltpu.VMEM((2 * S5_BLOCK_STATE // LANES, nb * (tt + S5_ROW_PAD), LANES), F32),
            pltpu.VMEM((nb * (tt + S5_ROW_PAD), LANES), F32),
        ],
        compiler_params=_cparams("parallel", "arbitrary"),
        name="s5_scan",
    )(u3, x0_re, x0_im, a_re, a_im, bmat, cmat, d_row)


def _s5_params(lam_re, lam_im, b_re, b_im, c_re, c_im, log_step):
    n_groups = lam_re.shape[0]
    n_blk = n_groups // S5_LANE_GROUPS
    dt = jnp.exp(log_step)[:, None]
    mag = jnp.exp(lam_re * dt)
    ang = lam_im * dt
    ab_re, ab_im = mag * jnp.cos(ang), mag * jnp.sin(ang)
    den = lam_re * lam_re + lam_im * lam_im
    nr = ab_re - 1.0
    f_re = (nr * lam_re + ab_im * lam_im) / den
    f_im = (ab_im * lam_re - nr * lam_im) / den
    bb_re = f_re[..., None] * b_re - f_im[..., None] * b_im
    bb_im = f_re[..., None] * b_im + f_im[..., None] * b_re
    eye = jnp.eye(S5_LANE_GROUPS, dtype=lam_re.dtype)

    def in_map(bb):
        bb = bb.reshape(n_blk, S5_LANE_GROUPS, S5_STATE, S5_GROUP)
        m = jnp.einsum('kgpc,gh->kgchp', bb, eye)
        return m.reshape(n_blk, LANES, S5_BLOCK_STATE)

    def out_map(cc):
        cc = cc.reshape(n_blk, S5_LANE_GROUPS, S5_GROUP, S5_STATE)
        m = jnp.einsum('kgcp,gh->kgphc', cc, eye)
        return m.reshape(n_blk, S5_BLOCK_STATE, LANES)

    bmat = jnp.concatenate([in_map(bb_re), in_map(bb_im)], axis=2).astype(BF16)
    cmat = jnp.concatenate([out_map(c_re), out_map(-c_im)], axis=1).astype(BF16)
    a_re = ab_re.reshape(n_blk, 1, S5_BLOCK_STATE)
    a_im = ab_im.reshape(n_blk, 1, S5_BLOCK_STATE)
    return a_re, a_im, bmat, cmat


def _odd_out_kernel(x_ref, y_ref, z_ref, w1_ref, w2_ref, wo_ref, o_ref):
    yb = jax.nn.gelu(y_ref[...]).astype(BF16)
    t = _dot(yb, w1_ref[...]) * jax.nn.sigmoid(_dot(yb, w2_ref[...])) * z_ref[...]
    o_ref[...] = x_ref[...] + _dot(t.astype(BF16), wo_ref[...])


def _odd_out(x2d, y2d, z2d, w1, w2, wo):
    m, d = x2d.shape
    e = y2d.shape[1]
    row = lambda i: (i, 0)
    fixed = lambda i: (0, 0)
    return pl.pallas_call(
        _odd_out_kernel,
        grid=(m // ROW_TILE,),
        in_specs=[
            pl.BlockSpec((ROW_TILE, d), row),
            pl.BlockSpec((ROW_TILE, e), row),
            pl.BlockSpec((ROW_TILE, e), row),
            pl.BlockSpec((e, e), fixed),
            pl.BlockSpec((e, e), fixed),
            pl.BlockSpec((e, d), fixed),
        ],
        out_specs=pl.BlockSpec((ROW_TILE, d), row),
        out_shape=jax.ShapeDtypeStruct((m, d), F32),
        compiler_params=_cparams("parallel"),
        name="odd_out_proj",
    )(x2d, y2d, z2d, w1, w2, wo)


def _rope_tables(pos):
    half = NSA_DH // 2
    inv = ROPE_THETA ** (-jnp.arange(half, dtype=F32) / half)
    ang = pos.astype(F32)[:, None] * inv[None, :]
    cos, sin = jnp.cos(ang), jnp.sin(ang)
    reps = LANES // NSA_DH
    return jnp.tile(cos, (1, 2 * reps)), jnp.tile(jnp.concatenate([-sin, sin], axis=1), (1, reps))


def _even_weights(norm_g, w_in, w_out, q_norm, k_norm, cmp_pos, cmp_w):
    d = w_in.shape[0]
    sizes = (RET_HEADS * RET_DK, RET_HEADS * RET_DK, A_WIDTH, A_WIDTH, B_WIDTH, 6 * NSA_KV_HEADS * NSA_DH,
             N_GATES, B_WIDTH)
    parts, o = [], 0
    for s in sizes:
        parts.append(w_in[:, o:o + s])
        o += s
    qa, ka, va, za, qn, kvb, gl, zb = parts
    w_perm = jnp.concatenate([qa, ka, va, za, qn, kvb, zb, gl, jnp.zeros((d, LANES - N_GATES), w_in.dtype)],
                             axis=1).astype(BF16)
    reps = LANES // NSA_DH
    qg = jnp.tile(q_norm[None, :], (1, reps))
    kg = jnp.tile(k_norm, (1, reps))
    cw = cmp_w.astype(BF16)
    row_blocks = []
    for c in range(2):
        for g in range(NSA_KV_HEADS):
            r0 = (c * NSA_KV_HEADS + g) * NSA_DH
            row_blocks.append(jnp.pad(cw[c], ((0, 0), (0, 0), (r0, 2 * LANES - NSA_DH - r0))))
    w_bd = jnp.concatenate(row_blocks, axis=1)
    pe = jnp.broadcast_to(cmp_pos.transpose(1, 0, 2)[:, :, None, :], (L_CMP, 2, NSA_KV_HEADS, NSA_DH))
    pe = pe.reshape(L_CMP, 2 * LANES)
    wa = w_out[:A_WIDTH].astype(BF16)
    wb = w_out[A_WIDTH:].astype(BF16)
    return norm_g[None, :], w_perm, qg, kg, w_bd, pe, wa, wb


def _even_layer(x, pos0, s_ret, ew, gn_gain, next_odd, cache=None, page_table=None, win_past=None):
    batch, seq, d = x.shape
    gain, w_perm, qg, kg, w_bd, pe, wa, wb = ew
    x2d = x.reshape(batch * seq, d)
    pos = pos0 + jnp.arange(seq, dtype=jnp.int32)
    cos_t, sin_t = _rope_tables(pos)
    if seq >= EVEN_IN_TILE:
        n_table_blocks = seq // EVEN_IN_TILE
    else:
        cos_t = jnp.tile(cos_t, (EVEN_IN_TILE // seq, 1))
        sin_t = jnp.tile(sin_t, (EVEN_IN_TILE // seq, 1))
        n_table_blocks = 1
    g, dh = NSA_KV_HEADS, NSA_DH
    if cache is None:
        qa, ka, va, za, qn, full_t, win_t, zb, gates, cmp_rows = _even_in(
            x2d, gain, w_perm, cos_t, sin_t, qg, kg, n_table_blocks, seq_tiles=seq // EVEN_IN_TILE)
        oa, s_fin = _retention(qa, ka, va, za, gn_gain[None, :], s_ret, batch, seq)
        cmp3 = cmp_rows.reshape(batch, seq, 2 * LANES)
        n_half = seq // CMP_STRIDE
        src_specs = [pl.BlockSpec((1, seq, LANES), lambda b: (b, 0, 0)),
                     pl.BlockSpec((1, seq, LANES), lambda b: (b, 0, 1))]
        cmp_kv = _compress([cmp3, cmp3], src_specs, (batch,), (), pe, w_bd, batch, seq, n_half)
        overlap, expand = _selection_tables(n_half, seq)
        ob = _nsa_prompt(qn, gates, zb, cmp_kv, full_t, win_t, overlap, expand, batch, seq)
        keep = min(WINDOW, seq)
        y, *uz = _even_out(x2d, oa, ob, wa, wb, next_odd)
        full_rows = full_t.reshape(batch, 4, g, dh, seq).transpose(0, 4, 1, 2, 3)
        win_rows = win_t[:, :, seq - keep:].reshape(batch, 2, g, dh, keep).transpose(0, 4, 1, 2, 3)
        return y.reshape(batch, seq, d), s_fin, full_rows, win_rows, uz
    else:
        qa, ka, va, za, qn, full_new, win_new, zb, gates = _even_in(x2d, gain, w_perm, cos_t, sin_t, qg, kg,
                                                                     n_table_blocks)
        oa, s_fin = _retention(qa, ka, va, za, gn_gain[None, :], s_ret, batch, seq)
        n_pages = page_table.shape[1]
        past_len = n_pages * PAGE_SIZE
        overlap, expand = _selection_tables(past_len // CMP_STRIDE, past_len + KEY_TILE)
        ob, win_out_t = _nsa_sample(page_table, qn, gates, zb, full_new, win_new, win_past, cache, pe, w_bd,
                                    overlap, expand, batch, seq, n_pages)
    y, *uz = _even_out(x2d, oa, ob, wa, wb, next_odd)
    w_buf = win_out_t.shape[3]
    return (y.reshape(batch, seq, d), s_fin, full_new.reshape(batch, seq, 4, g, dh),
            win_out_t.reshape(batch, 2, g, dh, w_buf).transpose(0, 4, 1, 2, 3), uz)


def _odd_layer(x, uz, s_re, s_im, s5p, d_row, w1, w2, wo, tt):
    batch, seq, d = x.shape
    x2d = x.reshape(batch * seq, d)
    u, zs = uz
    e = u.shape[1]
    a_re, a_im, bmat, cmat = s5p
    n_groups, n_state = s_re.shape[1], s_re.shape[2]
    y, f_re, f_im = _s5(u.reshape(batch, seq, e), s_re.reshape(batch, n_groups * n_state),
                        s_im.reshape(batch, n_groups * n_state), a_re, a_im, bmat, cmat, d_row, tt)
    out = _odd_out(x2d, y.reshape(batch * seq, e), zs, w1, w2, wo).reshape(batch, seq, d)
    return out, f_re.reshape(batch, n_groups, n_state), f_im.reshape(batch, n_groups, n_state)


S5_TIME_TILE = 128


def kernel(x_prompt, x_sample, cache_nsa_kv, cache_nsa_win, state_ret, state_ssm_re, state_ssm_im, page_table,
           norm_even, w_in_even, w_out_even, ret_gn_gain, nsa_q_norm, nsa_k_norm, nsa_cmp_pos, nsa_cmp_w,
           norm_odd, w_in_odd, ssm_lambda_re, ssm_lambda_im, ssm_b_re, ssm_b_im, ssm_c_re, ssm_c_im, ssm_d,
           ssm_log_step, glu_w1, glu_w2, w_out_odd):
    bp, seq_p, _ = x_prompt.shape
    db, seq_s, _ = x_sample.shape
    n_pages = page_table.shape[1]
    past_len = n_pages * PAGE_SIZE
    depth = norm_even.shape[0] + norm_odd.shape[0]
    yp, ys = x_prompt, x_sample
    ret_p, ret_s, kv_p, kv_s, win_p, win_s = [], [], [], [], [], []
    sre_p, sim_p, sre_s, sim_s = [], [], [], []
    for layer in range(depth):
        li = layer // 2
        if layer % 2 == 0:
            ew = _even_weights(norm_even[li], w_in_even[li], w_out_even[li], nsa_q_norm[li], nsa_k_norm[li],
                               nsa_cmp_pos[li], nsa_cmp_w[li])
            next_odd = None
            if layer + 1 < depth:
                next_odd = (norm_odd[li][None, :], w_in_odd[li].astype(BF16))
            s0 = jnp.zeros((bp, RET_HEADS, RET_DK, RET_DV), F32)
            yp, sr, kvr, wr, uz_p = _even_layer(yp, 0, s0, ew, ret_gn_gain[li], next_odd)
            ret_p.append(sr); kv_p.append(kvr); win_p.append(wr)
            cache_t = cache_nsa_kv[li].transpose(0, 2, 3, 4, 1).reshape(cache_nsa_kv.shape[1], 4, LANES, PAGE_SIZE)
            win_t = cache_nsa_win[li].transpose(0, 2, 3, 4, 1).reshape(db, 2, LANES, cache_nsa_win.shape[2])
            ys, sr2, kvr2, wr2, uz_s = _even_layer(ys, past_len, state_ret[li], ew, ret_gn_gain[li], next_odd,
                                                   cache=cache_t, page_table=page_table, win_past=win_t)
            ret_s.append(sr2); kv_s.append(kvr2); win_s.append(wr2)
        else:
            s5p = _s5_params(ssm_lambda_re[li], ssm_lambda_im[li], ssm_b_re[li], ssm_b_im[li],
                             ssm_c_re[li], ssm_c_im[li], ssm_log_step[li])
            w1, w2, wo = glu_w1[li].astype(BF16), glu_w2[li].astype(BF16), w_out_odd[li].astype(BF16)
            d_row = ssm_d[li][None, :]
            n_groups = ssm_lambda_re.shape[1]
            z0 = jnp.zeros((bp, n_groups, S5_STATE), F32)
            yp, fr, fi = _odd_layer(yp, uz_p, z0, z0, s5p, d_row, w1, w2, wo, min(S5_TIME_TILE, seq_p))
            sre_p.append(fr); sim_p.append(fi)
            ys, fr2, fi2 = _odd_layer(ys, uz_s, state_ssm_re[li], state_ssm_im[li], s5p, d_row,
                                      w1, w2, wo, min(S5_TIME_TILE, seq_s))
            sre_s.append(fr2); sim_s.append(fi2)
    return (yp, ys, jnp.stack(ret_p), jnp.stack(ret_s), jnp.stack(kv_p), jnp.stack(kv_s), jnp.stack(win_p),
            jnp.stack(win_s), jnp.stack(sre_p), jnp.stack(sim_p), jnp.stack(sre_s), jnp.stack(sim_s))
```

```python
import functools
import math

import jax
import jax.numpy as jnp
from jax import lax
from jax.experimental import pallas as pl
from jax.experimental.pallas import tpu as pltpu

F32 = jnp.float32
BF16 = jnp.bfloat16

LANES = 128
SUBLANES = 8
VMEM_LIMIT_BYTES = 48 * 2**20

EPS = 1e-6
ROPE_THETA = 10000.0
NEG_INF = -1e30
FORCE = 1e4

RET_HEADS = 4
RET_DK = 64
RET_DV = 128
RET_CHUNK = 128
RET_TILES_PER_STEP = 8
A_WIDTH = RET_HEADS * RET_DV

NSA_HEADS = 8
NSA_KV_HEADS = 2
NSA_DH = 64
NSA_HPG = NSA_HEADS // NSA_KV_HEADS
B_WIDTH = NSA_HEADS * NSA_DH
L_CMP = 32
CMP_STRIDE = 16
CMP_RATIO = L_CMP // CMP_STRIDE
L_SEL = 64
N_SEL = 8
WINDOW = 512
PAGE_SIZE = 128
N_FULL_KV = 4
N_WIN_KV = 2
K_SEL, V_SEL = 2, 3
KV_ROW = N_FULL_KV * NSA_KV_HEADS * NSA_DH
WIN_ROW = N_WIN_KV * NSA_KV_HEADS * NSA_DH
KEY_TILE = 128
PROMPT_Q_TILE = 256
WIN_KEYS = WINDOW + PROMPT_Q_TILE
SEL_CHUNK = 512
CMP_SLOT = CMP_STRIDE + 4
SAMPLE_REQS_PER_STEP = 2

S5_GROUP = 16
S5_STATE = 64
S5_LANE_GROUPS = LANES // S5_GROUP
S5_BLOCK_STATE = S5_LANE_GROUPS * S5_STATE
S5_ROW_PAD = 4
S5_TIME_TILE = 128
S5_MXU_COLS = 256

QA0, KA0, VA0, ZA0, QN0, KVB0, ZB0, GL0 = 0, 256, 512, 1024, 1536, 2048, 2816, 3328
EVEN_COLS = GL0 + LANES
N_GATES = 3 * NSA_HEADS

ROW_TILE = 512
EVEN_IN_TILE = 256


def _cparams(*sem):
    return pltpu.CompilerParams(dimension_semantics=sem, vmem_limit_bytes=VMEM_LIMIT_BYTES)


def _lane_iota(shape):
    return lax.broadcasted_iota(jnp.int32, shape, len(shape) - 1)


def _row_iota(shape):
    return lax.broadcasted_iota(jnp.int32, shape, len(shape) - 2)


def _dot(a, b):
    return jnp.dot(a, b, preferred_element_type=F32)


def _dot_nt(a, b):
    return lax.dot_general(a, b, (((1,), (1,)), ((), ())), preferred_element_type=F32)


def _rmsnorm_rows(x, g):
    return x * lax.rsqrt(jnp.mean(x * x, axis=-1, keepdims=True) + EPS) * g


def _swap_halves(x):
    return pltpu.roll(x, NSA_DH, axis=1)


def _rope_block(x, cos, sin_signed):
    half = NSA_DH // 2
    lane = _lane_iota(x.shape)
    first = (lane % NSA_DH) < half
    partner = jnp.where(first, pltpu.roll(x, LANES - half, axis=1), pltpu.roll(x, half, axis=1))
    return x * cos + partner * sin_signed


def _head_rms_block(x, g):
    lane = _lane_iota(x.shape)
    lo = lane < NSA_DH
    sq = x * x
    s_lo = jnp.sum(jnp.where(lo, sq, 0.0), axis=-1, keepdims=True)
    s_hi = jnp.sum(jnp.where(lo, 0.0, sq), axis=-1, keepdims=True)
    ms = jnp.where(lo, s_lo, s_hi) * (1.0 / NSA_DH)
    return x * lax.rsqrt(ms + EPS) * g


def _even_in_kernel(x_ref, g_ref, w_ref, cos_ref, sin_ref, qg_ref, kg_ref, *out_refs, transposed_kv):
    if transposed_kv:
        qa_ref, ka_ref, va_ref, za_ref, qn_ref, full_ref, win_ref, zb_ref, gl_ref, cmp_ref = out_refs
    else:
        qa_ref, ka_ref, va_ref, za_ref, qn_ref, full_ref, win_ref, zb_ref, gl_ref = out_refs
    hb = _rmsnorm_rows(x_ref[...], g_ref[...]).astype(BF16)
    cos = cos_ref[...]
    sin = sin_ref[...]

    def proj(c0):
        return _dot(hb, w_ref[:, c0:c0 + LANES])

    for j in range(RET_HEADS * RET_DK // LANES):
        c = j * LANES
        qa_ref[:, c:c + LANES] = _rope_block(proj(QA0 + c), cos, sin)
        ka_ref[:, c:c + LANES] = _rope_block(proj(KA0 + c), cos, sin) * (RET_DK ** -0.5)
    for j in range(A_WIDTH // LANES):
        c = j * LANES
        va_ref[:, c:c + LANES] = proj(VA0 + c)
        za_ref[:, c:c + LANES] = jax.nn.silu(proj(ZA0 + c))
    for j in range(B_WIDTH // LANES):
        c = j * LANES
        qn_ref[:, c:c + LANES] = _rope_block(_head_rms_block(proj(QN0 + c), qg_ref[...]), cos, sin)
        zb_ref[:, c:c + LANES] = jax.nn.silu(proj(ZB0 + c))
    for j in range(6):
        y = proj(KVB0 + j * LANES)
        if j % 2 == 0:
            y = _rope_block(_head_rms_block(y, kg_ref[j // 2:j // 2 + 1, :]), cos, sin)
        if transposed_kv:
            if j < 4:
                full_ref[0, j * LANES:(j + 1) * LANES, :] = y.T
            else:
                win_ref[0, (j - 4) * LANES:(j - 3) * LANES, :] = y.T
            if j < 2:
                cmp_ref[:, j * LANES:(j + 1) * LANES] = y
        elif j < 4:
            full_ref[:, j * LANES:(j + 1) * LANES] = y
        else:
            win_ref[:, (j - 4) * LANES:(j - 3) * LANES] = y
    gl_ref[...] = jax.nn.sigmoid(proj(GL0))


def _even_in(x2d, gain, w_bf, cos_t, sin_t, qg, kg, n_table_blocks, seq_tiles=None):
    m, d = x2d.shape
    tile = EVEN_IN_TILE
    grid = (m // tile,)
    row = lambda i: (i, 0)
    fixed = lambda i: (0, 0)
    table = lambda i: (i % n_table_blocks, 0)
    widths = (RET_HEADS * RET_DK, RET_HEADS * RET_DK, A_WIDTH, A_WIDTH, B_WIDTH, KV_ROW, WIN_ROW, B_WIDTH, LANES)
    out_specs = [pl.BlockSpec((tile, w), row) for w in widths]
    out_shape = [jax.ShapeDtypeStruct((m, w), F32) for w in widths]
    if seq_tiles is not None:
        batch, seq = grid[0] // seq_tiles, seq_tiles * tile
        fmajor = lambda i: (i // seq_tiles, 0, i % seq_tiles)
        for idx, w in ((5, KV_ROW), (6, WIN_ROW)):
            out_specs[idx] = pl.BlockSpec((1, w, tile), fmajor)
            out_shape[idx] = jax.ShapeDtypeStruct((batch, w, seq), F32)
        out_specs.append(pl.BlockSpec((tile, 2 * LANES), row))
        out_shape.append(jax.ShapeDtypeStruct((m, 2 * LANES), F32))
    return pl.pallas_call(
        functools.partial(_even_in_kernel, transposed_kv=seq_tiles is not None),
        grid=grid,
        in_specs=[
            pl.BlockSpec((tile, d), row),
            pl.BlockSpec((1, d), fixed),
            pl.BlockSpec((d, EVEN_COLS), fixed),
            pl.BlockSpec((tile, LANES), table),
            pl.BlockSpec((tile, LANES), table),
            pl.BlockSpec((1, LANES), fixed),
            pl.BlockSpec((3, LANES), fixed),
        ],
        out_specs=out_specs,
        out_shape=out_shape,
        compiler_params=_cparams("parallel"),
        name="even_in_proj",
    )(x2d, gain, w_bf, cos_t, sin_t, qg, kg)


def _retention_kernel(q_ref, k_ref, v_ref, z_ref, gn_ref, s0_ref, dmat_ref, qdec_ref, kdec_ref, cdec_ref,
                      o_ref, sfin_ref, s_scr, *, rows, n_tiles):
    n_seq = RET_CHUNK // rows
    c = pl.program_id(1)

    @pl.when(c == 0)
    def _():
        s_scr[...] = s0_ref[...]

    lane = _lane_iota((RET_CHUNK, LANES))
    col_seq = _lane_iota((RET_DK, RET_CHUNK)) >> int(math.log2(rows)) if n_seq > 1 else None
    state = [[s_scr[r, h] for h in range(RET_HEADS)] for r in range(n_seq)]
    for t in range(n_tiles):
        tile = slice(t * RET_CHUNK, (t + 1) * RET_CHUNK)
        q, k, v = q_ref[tile, :], k_ref[tile, :], v_ref[tile, :]
        qd = q * qdec_ref[...]
        kd = k * kdec_ref[...]
        for pair in range(RET_HEADS // 2):
            cols = slice(pair * LANES, (pair + 1) * LANES)
            q2, k2b, qd2 = q[:, cols], k[:, cols].astype(BF16), qd[:, cols]
            kd2_t = kd[:, cols].T
            s_pair = [jnp.concatenate([state[r][2 * pair], state[r][2 * pair + 1]], axis=0).astype(BF16)
                      for r in range(n_seq)]
            for sub in range(2):
                h = 2 * pair + sub
                mine = (lane >= sub * RET_DK) & (lane < (sub + 1) * RET_DK)
                qm = jnp.where(mine, q2, 0.0).astype(BF16)
                qdm = jnp.where(mine, qd2, 0.0).astype(BF16)
                vhb = v[:, h * RET_DV:(h + 1) * RET_DV].astype(BF16)
                intra = _dot((_dot_nt(qm, k2b) * dmat_ref[h]).astype(BF16), vhb)
                cross = jnp.concatenate([_dot(qdm[r * rows:(r + 1) * rows], s_pair[r]) for r in range(n_seq)],
                                        axis=0)
                out = intra + cross
                kt = kd2_t[sub * RET_DK:(sub + 1) * RET_DK, :]
                if n_seq > 1:
                    kt = jnp.concatenate([jnp.where(col_seq == r, kt, 0.0) for r in range(n_seq)], axis=0)
                kv = _dot(kt.astype(BF16), vhb)
                for r in range(n_seq):
                    state[r][h] = state[r][h] * cdec_ref[h] + kv[r * RET_DK:(r + 1) * RET_DK]
                mu = jnp.mean(out, axis=-1, keepdims=True)
                cen = out - mu
                var = jnp.mean(cen * cen, axis=-1, keepdims=True)
                y = cen * lax.rsqrt(var + EPS) * gn_ref[:, h * RET_DV:(h + 1) * RET_DV]
                o_ref[tile, h * RET_DV:(h + 1) * RET_DV] = y * z_ref[tile, h * RET_DV:(h + 1) * RET_DV]
    for r in range(n_seq):
        for h in range(RET_HEADS):
            s_scr[r, h] = state[r][h]

    @pl.when(c == pl.num_programs(1) - 1)
    def _():
        sfin_ref[...] = s_scr[...]


def _retention_tables(rows):
    log_g = jnp.log(1.0 - 2.0 ** (-5.0 - jnp.arange(RET_HEADS, dtype=F32)))
    idx = jnp.arange(RET_CHUNK, dtype=jnp.int32)
    pos = (idx % rows).astype(F32)
    seq = idx // rows
    diff = pos[:, None] - pos[None, :]
    causal = (diff >= 0) & (seq[:, None] == seq[None, :])
    dmat = jnp.exp(jnp.where(causal, diff, 0.0)[None] * log_g[:, None, None]) * causal[None]
    qdec = jnp.exp((pos + 1.0)[:, None] * log_g[None, :])
    kdec = jnp.exp((rows - 1.0 - pos)[:, None] * log_g[None, :])
    qdec = jnp.repeat(qdec, RET_DK, axis=1)
    kdec = jnp.repeat(kdec, RET_DK, axis=1)
    cdec = jnp.broadcast_to(jnp.exp(rows * log_g)[:, None, None], (RET_HEADS, RET_DK, RET_DV))
    return dmat, qdec, kdec, cdec


def _retention(qa, ka, va, za, gn_gain, s0, batch, seq):
    rows = min(RET_CHUNK, seq)
    n_seq = RET_CHUNK // rows
    n_chunks = seq // rows
    n_tiles = min(RET_TILES_PER_STEP, n_chunks)
    steps = n_chunks // n_tiles
    dmat, qdec, kdec, cdec = _retention_tables(rows)
    tok = lambda b, c: (b * steps + c, 0)
    fixed2 = lambda b, c: (0, 0)
    fixed3 = lambda b, c: (0, 0, 0)
    state = lambda b, c: (b, 0, 0, 0)
    qk_w = RET_HEADS * RET_DK
    step_rows = n_tiles * RET_CHUNK
    return pl.pallas_call(
        functools.partial(_retention_kernel, rows=rows, n_tiles=n_tiles),
        grid=(batch // n_seq, steps),
        in_specs=[
            pl.BlockSpec((step_rows, qk_w), tok),
            pl.BlockSpec((step_rows, qk_w), tok),
            pl.BlockSpec((step_rows, A_WIDTH), tok),
            pl.BlockSpec((step_rows, A_WIDTH), tok),
            pl.BlockSpec((1, A_WIDTH), fixed2),
            pl.BlockSpec((n_seq, RET_HEADS, RET_DK, RET_DV), state),
            pl.BlockSpec((RET_HEADS, RET_CHUNK, RET_CHUNK), fixed3),
            pl.BlockSpec((RET_CHUNK, qk_w), fixed2),
            pl.BlockSpec((RET_CHUNK, qk_w), fixed2),
            pl.BlockSpec((RET_HEADS, RET_DK, RET_DV), fixed3),
        ],
        out_specs=[
            pl.BlockSpec((step_rows, A_WIDTH), tok),
            pl.BlockSpec((n_seq, RET_HEADS, RET_DK, RET_DV), state),
        ],
        out_shape=[
            jax.ShapeDtypeStruct((batch * seq, A_WIDTH), F32),
            jax.ShapeDtypeStruct((batch, RET_HEADS, RET_DK, RET_DV), F32),
        ],
        scratch_shapes=[pltpu.VMEM((n_seq, RET_HEADS, RET_DK, RET_DV), F32)],
        compiler_params=_cparams("parallel", "arbitrary"),
        name="retention",
    )(qa, ka, va, za, gn_gain, s0, dmat, qdec, kdec, cdec)


def _compress_kernel(k_ref, v_ref, pe_ref, w_ref, o_ref):
    n_half = k_ref.shape[1] // CMP_STRIDE
    acc = [jnp.zeros((n_half, 2 * LANES), F32) for _ in range(CMP_RATIO)]
    for l in range(CMP_STRIDE):
        rows = pl.ds(l, n_half, stride=CMP_STRIDE)
        x = jnp.concatenate([k_ref[0, rows, :], v_ref[0, rows, :]], axis=1)
        for r in range(CMP_RATIO):
            i = r * CMP_STRIDE + l
            acc[r] = acc[r] + _dot((x + pe_ref[i:i + 1, :]).astype(BF16), w_ref[i])
    out = acc[0]
    for r in range(1, CMP_RATIO):
        out = out + pltpu.roll(acc[r], n_half - r, axis=0)
    live = _row_iota(out.shape) < n_half - CMP_RATIO + 1
    o_ref[0] = jnp.where(live, out, 0.0)


def _compress(cmp_rows, pe, w_bd):
    batch, seq, _ = cmp_rows.shape
    n_half = seq // CMP_STRIDE
    return pl.pallas_call(
        _compress_kernel,
        grid=(batch,),
        in_specs=[
            pl.BlockSpec((1, seq, LANES), lambda b: (b, 0, 0)),
            pl.BlockSpec((1, seq, LANES), lambda b: (b, 0, 1)),
            pl.BlockSpec((L_CMP, 2 * LANES), lambda b: (0, 0)),
            pl.BlockSpec((L_CMP, 2 * LANES, 2 * LANES), lambda b: (0, 0, 0)),
        ],
        out_specs=pl.BlockSpec((1, n_half, 2 * LANES), lambda b: (b, 0, 0)),
        out_shape=jax.ShapeDtypeStruct((batch, n_half, 2 * LANES), F32),
        compiler_params=_cparams("parallel"),
        name="kv_compress",
    )(cmp_rows, cmp_rows, pe, w_bd)


def _split3_bf16(x):
    hi = x.astype(BF16)
    r1 = x - hi.astype(F32)
    mid = r1.astype(BF16)
    lo = (r1 - mid.astype(F32)).astype(BF16)
    return hi, mid, lo


def _stack_group_queries(q, g):
    lane = _lane_iota((q.shape[0], LANES))
    in_g = (lane >= g * NSA_DH) & (lane < (g + 1) * NSA_DH)
    parts = []
    for hh in range(NSA_HPG):
        h = g * NSA_HPG + hh
        two = q[:, (h // 2) * LANES:(h // 2 + 1) * LANES]
        if h % 2 != g:
            two = _swap_halves(two)
        parts.append(jnp.where(in_g, two, 0.0))
    return (jnp.concatenate(parts, axis=0) * (NSA_DH ** -0.5)).astype(BF16)


def _masked_softmax(s, ok):
    s = jnp.where(ok, s, NEG_INF)
    e = jnp.exp(s - jnp.max(s, axis=-1, keepdims=True))
    return jnp.where(ok, e * (1.0 / jnp.sum(e, axis=-1, keepdims=True)), 0.0)


def _sum_heads(p1, tq):
    psum = p1[0:tq]
    for hh in range(1, NSA_HPG):
        psum = psum + p1[hh * tq:(hh + 1) * tq]
    return psum


def _choose_blocks(psum, t0, tq, overlap, n_blocks):
    rows = psum.shape[0]
    imp = sum(_dot(t, overlap) for t in _split3_bf16(psum))
    shift = int(math.log2(L_SEL))

    def ranked(score, blk, valid, take):
        rank = jnp.zeros(score.shape, F32)
        for s in range(n_blocks):
            cand = take(s)
            rank = rank + ((cand > score) | ((cand == score) & (blk > s))).astype(F32)
        return ((rank < N_SEL) & valid).astype(F32)

    if rows % LANES:
        tpos = jnp.concatenate([t0 + _row_iota((tq, 1))] * (rows // tq), axis=0)
        blk = _lane_iota((rows, LANES))
        cur = tpos >> shift
        forced = (blk == 0) | (blk == cur) | (blk == cur - 1)
        valid = (blk * L_SEL <= tpos) & (blk < n_blocks)
        score = jnp.where(valid, jnp.where(forced, FORCE, imp), -FORCE)
        return ranked(score, blk, valid, lambda s: score[:, s:s + 1]).astype(BF16)

    nbp = -(-n_blocks // SUBLANES) * SUBLANES
    tiles = rows // LANES
    imp_t = jnp.concatenate([imp[i * LANES:(i + 1) * LANES].T[:nbp] for i in range(tiles)], axis=1)
    tpos = jnp.concatenate([t0 + _lane_iota((1, tq))] * (rows // tq), axis=1)
    blk = _row_iota((nbp, rows))
    cur = tpos >> shift
    forced = (blk == 0) | (blk == cur) | (blk == cur - 1)
    valid = (blk * L_SEL <= tpos) & (blk < n_blocks)
    score = jnp.where(valid, jnp.where(forced, FORCE, imp_t), -FORCE)
    chosen_t = ranked(score, blk, valid, lambda s: score[s:s + 1, :])
    chosen_t = jnp.concatenate([chosen_t, jnp.zeros((LANES - nbp, rows), F32)], axis=0)
    return jnp.concatenate([chosen_t[:, i * LANES:(i + 1) * LANES].T for i in range(tiles)],
                           axis=0).astype(BF16)


def _gate_and_store(gates, zs, branch_outs, o_ref, row0=0):
    tq = gates.shape[0]
    lane = _lane_iota((tq, LANES))
    ext = []
    for g in range(NSA_KV_HEADS):
        o_cmp, o_sel, o_win = branch_outs[g]
        for hh in range(NSA_HPG):
            h = g * NSA_HPG + hh
            rows = slice(hh * tq, (hh + 1) * tq)
            o = (gates[:, 3 * h:3 * h + 1] * o_cmp[rows] + gates[:, 3 * h + 1:3 * h + 2] * o_sel[rows]
                 + gates[:, 3 * h + 2:3 * h + 3] * o_win[rows])
            ext.append(_swap_halves(o) if h % 2 != g else o)
    for j in range(NSA_HEADS // 2):
        both = jnp.where(lane < NSA_DH, ext[2 * j], ext[2 * j + 1])
        o_ref[row0:row0 + tq, j * LANES:(j + 1) * LANES] = both * zs[:, j * LANES:(j + 1) * LANES]


def _nsa_prompt_kernel(q_ref, g_ref, z_ref, cmp_ref, sel_ref, win_ref, ov_ref, ex_ref, o_ref, *, n_blocks):
    i = pl.program_id(1)
    tq = q_ref.shape[0]
    q = q_ref[...]
    t0 = i * tq
    tpos = t0 + _row_iota((tq, 1))
    tpos_all = jnp.concatenate([tpos] * NSA_HEADS, axis=0)
    grp_rows = NSA_HPG * tq
    qs = jnp.concatenate([_stack_group_queries(q, g) for g in range(NSA_KV_HEADS)], axis=0)

    cmp_blk = cmp_ref[0].astype(BF16)
    cmp_ok = (_lane_iota((1, cmp_blk.shape[0])) * CMP_STRIDE + (L_CMP - 1)) <= tpos_all
    p1 = _masked_softmax(_dot_nt(qs, cmp_blk[:, :LANES]), cmp_ok)
    o_cmp = _dot(p1.astype(BF16), cmp_blk[:, LANES:])
    psum = jnp.concatenate([_sum_heads(p1[g * grp_rows:(g + 1) * grp_rows], tq)
                            for g in range(NSA_KV_HEADS)], axis=0)
    chosen = _choose_blocks(psum, t0, tq, ov_ref[...], n_blocks)

    def sel_chunk(c, carry):
        m, l, acc = carry
        k0 = pl.multiple_of(c * SEL_CHUNK, SEL_CHUNK)
        k_t = sel_ref[0, 0:LANES, pl.ds(k0, SEL_CHUNK)].astype(BF16)
        v_t = sel_ref[0, LANES:2 * LANES, pl.ds(k0, SEL_CHUNK)].astype(BF16)
        causal = jnp.concatenate([(k0 + _lane_iota((1, SEL_CHUNK))) <= tpos] * NSA_KV_HEADS, axis=0)
        key_on = (_dot(chosen, ex_ref[:, pl.ds(k0, SEL_CHUNK)]) > 0.5) & causal
        bias = jnp.where(key_on, 0.0, NEG_INF)
        bias = jnp.concatenate([bias[g * tq:(g + 1) * tq] for g in range(NSA_KV_HEADS)
                                for _ in range(NSA_HPG)], axis=0)
        s = _dot(qs, k_t) + bias
        m_new = jnp.maximum(m, jnp.max(s, axis=-1, keepdims=True))
        alpha = jnp.exp(m - m_new)
        e = jnp.exp(s - m_new)
        l_new = alpha * l + jnp.sum(e, axis=-1, keepdims=True)
        return m_new, l_new, alpha * acc + _dot_nt(e.astype(BF16), v_t)

    rows = NSA_HEADS * tq
    init = (jnp.full((rows, 1), NEG_INF, F32), jnp.zeros((rows, 1), F32), jnp.zeros((rows, LANES), F32))
    n_chunks = t0 // SEL_CHUNK + 1
    _, l2, acc2 = lax.fori_loop(0, n_chunks, sel_chunk, init)
    o_sel = acc2 / l2

    w0 = pl.multiple_of(jnp.maximum(t0 - WINDOW, 0), KEY_TILE)
    wk_t = win_ref[0, 0:LANES, pl.ds(w0, WIN_KEYS)].astype(BF16)
    wv_t = win_ref[0, LANES:2 * LANES, pl.ds(w0, WIN_KEYS)].astype(BF16)
    win_kpos = w0 + _lane_iota((1, WIN_KEYS))
    win_ok = (win_kpos <= tpos_all) & (win_kpos > tpos_all - WINDOW)
    s3 = jnp.where(win_ok, _dot(qs, wk_t), NEG_INF)
    e3 = jnp.exp(s3 - jnp.max(s3, axis=-1, keepdims=True))
    o_win = _dot_nt(e3.astype(BF16), wv_t) / jnp.sum(e3, axis=-1, keepdims=True)

    branch_outs = [(o_cmp[g * grp_rows:(g + 1) * grp_rows], o_sel[g * grp_rows:(g + 1) * grp_rows],
                    o_win[g * grp_rows:(g + 1) * grp_rows]) for g in range(NSA_KV_HEADS)]
    _gate_and_store(g_ref[...], z_ref[...], branch_outs, o_ref)


def _nsa_prompt(qn, gates, zb, cmp_kv, full_t, win_t, overlap, expand, batch, seq):
    tq = PROMPT_Q_TILE
    nq = seq // tq
    tok = lambda b, i: (b * nq + i, 0)
    per_b = lambda b, i: (b, 0, 0)
    sel_half = lambda b, i: (b, 1, 0)
    fixed = lambda b, i: (0, 0)
    n_blocks = -(-seq // L_SEL)
    return pl.pallas_call(
        functools.partial(_nsa_prompt_kernel, n_blocks=n_blocks),
        grid=(batch, nq),
        in_specs=[
            pl.BlockSpec((tq, B_WIDTH), tok),
            pl.BlockSpec((tq, LANES), tok),
            pl.BlockSpec((tq, B_WIDTH), tok),
            pl.BlockSpec((1, cmp_kv.shape[1], 2 * LANES), per_b),
            pl.BlockSpec((1, 2 * LANES, seq), sel_half),
            pl.BlockSpec((1, WIN_ROW, seq), per_b),
            pl.BlockSpec(overlap.shape, fixed),
            pl.BlockSpec(expand.shape, fixed),
        ],
        out_specs=pl.BlockSpec((tq, B_WIDTH), tok),
        out_shape=jax.ShapeDtypeStruct((batch * seq, B_WIDTH), F32),
        compiler_params=_cparams("parallel", "arbitrary"),
        name="nsa_prompt",
    )(qn, gates, zb, cmp_kv, full_t, win_t, overlap, expand)


def _nsa_sample_kernel(*refs, nb, n_pages, past_len, n_blocks):
    q_ref, g_ref, z_ref, newf_ref, neww_ref, winp_ref = refs[1:7]
    pages = refs[7:7 + nb * n_pages]
    pe_ref, w_ref, ov_ref, ex_ref, o_ref, wout_ref, xs_ref = refs[7 + nb * n_pages:]
    tq = q_ref.shape[0] // nb
    w_buf = winp_ref.shape[3]
    halves_per_page = PAGE_SIZE // CMP_STRIDE
    n_half = past_len // CMP_STRIDE
    total_half = nb * n_half

    pad = jnp.zeros((KEY_TILE - tq, LANES), F32)
    lane = _lane_iota((LANES, LANES))
    per_req = NSA_HEADS * tq
    tpos = past_len + _row_iota((tq, 1))
    tpos_all = jnp.concatenate([tpos] * (nb * NSA_HEADS), axis=0)
    tpos_grp = jnp.concatenate([tpos] * (nb * NSA_KV_HEADS), axis=0)

    def padded_rows(x):
        return jnp.concatenate([x, pad], axis=0)

    def per_request(fn):
        return jnp.concatenate([fn(j, slice(j * per_req, (j + 1) * per_req)) for j in range(nb)], axis=0)

    tok = [slice(j * tq, (j + 1) * tq) for j in range(nb)]
    qs = jnp.concatenate([_stack_group_queries(q_ref[tok[j], :], g)
                          for j in range(nb) for g in range(NSA_KV_HEADS)], axis=0)
    mine = [pages[j * n_pages:(j + 1) * n_pages] for j in range(nb)]
    newf = [newf_ref[tok[j], :] for j in range(nb)]
    neww = [neww_ref[tok[j], :] for j in range(nb)]

    sel_pieces = [[] for _ in range(nb)]
    for j in range(nb * n_pages):
        for c in range(2):
            x = pages[j][0, c].T
            for hb in range(halves_per_page):
                slot = (j * halves_per_page + hb) * CMP_SLOT
                xs_ref[c, slot:slot + CMP_STRIDE, :] = x[hb * CMP_STRIDE:(hb + 1) * CMP_STRIDE]
        req = j // n_pages
        sel_pieces[req].append(_dot(qs[req * per_req:(req + 1) * per_req], pages[j][0, K_SEL].astype(BF16)))
    s2_raw = per_request(lambda j, r: jnp.concatenate(
        sel_pieces[j] + [_dot_nt(qs[r], padded_rows(newf[j][:, 2 * LANES:3 * LANES]).astype(BF16))], axis=1))
    s3_raw = per_request(lambda j, r: jnp.concatenate(
        [_dot(qs[r], winp_ref[j, 0].astype(BF16)),
         _dot_nt(qs[r], padded_rows(neww[j][:, :LANES]).astype(BF16))], axis=1))

    for j in range(nb):
        for c in range(2):
            shifted = pltpu.roll(winp_ref[j, c], w_buf - tq, axis=1)
            new_cols = pltpu.roll(padded_rows(neww[j][:, c * LANES:(c + 1) * LANES]).T, LANES - tq, axis=1)
            wout_ref[j, c, :, 0:w_buf - LANES] = shifted[:, 0:w_buf - LANES]
            wout_ref[j, c, :, w_buf - LANES:w_buf] = jnp.where(lane >= LANES - tq, new_cols,
                                                               shifted[:, w_buf - LANES:w_buf])

    acc = [jnp.zeros((total_half, 2 * LANES), F32) for _ in range(CMP_RATIO)]
    for l in range(CMP_STRIDE):
        x = jnp.concatenate([xs_ref[c, pl.ds(l, total_half, stride=CMP_SLOT), :] for c in range(2)], axis=1)
        for r in range(CMP_RATIO):
            i = r * CMP_STRIDE + l
            acc[r] = acc[r] + _dot((x + pe_ref[i:i + 1, :]).astype(BF16), w_ref[i])
    cmp_all = acc[0]
    for r in range(1, CMP_RATIO):
        cmp_all = cmp_all + pltpu.roll(acc[r], total_half - r, axis=0)
    live = _row_iota((n_half, 2 * LANES)) < n_half - CMP_RATIO + 1
    cmp_blk = [jnp.where(live, cmp_all[j * n_half:(j + 1) * n_half], 0.0).astype(BF16) for j in range(nb)]

    cmp_ok = (_lane_iota((1, n_half)) * CMP_STRIDE + (L_CMP - 1)) <= tpos_all
    p1 = _masked_softmax(per_request(lambda j, r: _dot_nt(qs[r], cmp_blk[j][:, :LANES])), cmp_ok)
    p1b = p1.astype(BF16)
    o_cmp = per_request(lambda j, r: _dot(p1b[r], cmp_blk[j][:, LANES:]))

    grp_rows = NSA_HPG * tq
    psum = jnp.concatenate([_sum_heads(p1[i * grp_rows:(i + 1) * grp_rows], tq)
                            for i in range(nb * NSA_KV_HEADS)], axis=0)
    chosen = _choose_blocks(psum, past_len, tq, ov_ref[...], n_blocks)
    key_on = (_dot(chosen, ex_ref[...]) > 0.5) & (_lane_iota((1, past_len + KEY_TILE)) <= tpos_grp)
    bias = jnp.where(key_on, 0.0, NEG_INF)
    bias = jnp.concatenate([bias[i * tq:(i + 1) * tq] for i in range(nb * NSA_KV_HEADS)
                            for _ in range(NSA_HPG)], axis=0)

    s2 = s2_raw + bias
    e2 = jnp.exp(s2 - jnp.max(s2, axis=-1, keepdims=True))
    l2 = jnp.sum(e2, axis=-1, keepdims=True)
    e2 = e2.astype(BF16)

    def sel_values(j, r):
        v_t = jnp.concatenate([p[0, V_SEL].astype(BF16) for p in mine[j]], axis=1)
        v_new = padded_rows(newf[j][:, 3 * LANES:4 * LANES]).astype(BF16)
        return _dot_nt(e2[r, :past_len], v_t) + _dot(e2[r, past_len:], v_new)

    o_sel = per_request(sel_values) / l2

    win_kpos = (past_len - w_buf) + _lane_iota((1, w_buf + KEY_TILE))
    win_ok = (win_kpos <= tpos_all) & (win_kpos > tpos_all - WINDOW)
    s3 = jnp.where(win_ok, s3_raw, NEG_INF)
    e3 = jnp.exp(s3 - jnp.max(s3, axis=-1, keepdims=True))
    l3 = jnp.sum(e3, axis=-1, keepdims=True)
    e3 = e3.astype(BF16)
    o_win = per_request(lambda j, r: _dot_nt(e3[r, :w_buf], winp_ref[j, 1].astype(BF16))
                        + _dot(e3[r, w_buf:], padded_rows(neww[j][:, LANES:]).astype(BF16))) / l3

    for j in range(nb):
        branch_outs = []
        for g in range(NSA_KV_HEADS):
            r = slice((j * NSA_KV_HEADS + g) * grp_rows, (j * NSA_KV_HEADS + g + 1) * grp_rows)
            branch_outs.append((o_cmp[r], o_sel[r], o_win[r]))
        _gate_and_store(g_ref[tok[j], :], z_ref[tok[j], :], branch_outs, o_ref, row0=j * tq)


def _nsa_sample(page_table, qn, gates, zb, full_new, win_new, win_t, cache_t, pe, w_bd, overlap, expand,
                batch, tq, n_pages):
    nb = SAMPLE_REQS_PER_STEP
    past_len = n_pages * PAGE_SIZE
    w_buf = win_t.shape[3]
    n_blocks = -(-(past_len + tq) // L_SEL)
    tok = lambda b, pt: (b, 0)
    per_b = lambda b, pt: (b, 0, 0, 0)
    fixed2 = lambda b, pt: (0, 0)
    fixed3 = lambda b, pt: (0, 0, 0)

    def page_spec(j, p):
        return pl.BlockSpec((1, N_FULL_KV, LANES, PAGE_SIZE), lambda b, pt: (pt[b * nb + j, p], 0, 0, 0))

    n_slots = nb * n_pages * (PAGE_SIZE // CMP_STRIDE)
    return pl.pallas_call(
        functools.partial(_nsa_sample_kernel, nb=nb, n_pages=n_pages, past_len=past_len, n_blocks=n_blocks),
        grid_spec=pltpu.PrefetchScalarGridSpec(
            num_scalar_prefetch=1,
            grid=(batch // nb,),
            in_specs=[
                pl.BlockSpec((nb * tq, B_WIDTH), tok),
                pl.BlockSpec((nb * tq, LANES), tok),
                pl.BlockSpec((nb * tq, B_WIDTH), tok),
                pl.BlockSpec((nb * tq, KV_ROW), tok),
                pl.BlockSpec((nb * tq, WIN_ROW), tok),
                pl.BlockSpec((nb, N_WIN_KV, LANES, w_buf), per_b),
            ] + [page_spec(j, p) for j in range(nb) for p in range(n_pages)] + [
                pl.BlockSpec(pe.shape, fixed2),
                pl.BlockSpec(w_bd.shape, fixed3),
                pl.BlockSpec(overlap.shape, fixed2),
                pl.BlockSpec(expand.shape, fixed2),
            ],
            out_specs=[
                pl.BlockSpec((nb * tq, B_WIDTH), tok),
                pl.BlockSpec((nb, N_WIN_KV, LANES, w_buf), per_b),
            ],
            scratch_shapes=[pltpu.VMEM((2, n_slots * CMP_SLOT, LANES), F32)],
        ),
        out_shape=[
            jax.ShapeDtypeStruct((batch * tq, B_WIDTH), F32),
            jax.ShapeDtypeStruct((batch, N_WIN_KV, LANES, w_buf), F32),
        ],
        compiler_params=_cparams("parallel"),
        name="nsa_sample",
    )(page_table, qn, gates, zb, full_new, win_new, win_t, *([cache_t] * (nb * n_pages)), pe, w_bd,
      overlap, expand)


def _selection_tables(n_cmp_rows, n_sel_keys):
    n = jnp.arange(n_cmp_rows, dtype=jnp.int32)[:, None]
    s = jnp.arange(LANES, dtype=jnp.int32)[None, :]
    c_start = n * CMP_STRIDE
    s_start = s * L_SEL
    overlap = ((c_start < s_start + L_SEL) & (s_start < c_start + L_CMP)).astype(BF16)
    key = jnp.arange(n_sel_keys, dtype=jnp.int32)[None, :]
    expand = ((key // L_SEL) == jnp.arange(LANES, dtype=jnp.int32)[:, None]).astype(BF16)
    return overlap, expand


def _even_out_kernel(x_ref, oa_ref, ob_ref, wa_ref, wb_ref, *rest, with_next):
    y = (x_ref[...] + _dot(oa_ref[...].astype(BF16), wa_ref[...])
         + _dot(ob_ref[...].astype(BF16), wb_ref[...]))
    if not with_next:
        (y_ref,) = rest
        y_ref[...] = y
        return
    g_ref, w_ref, y_ref, u_ref, z_ref = rest
    y_ref[...] = y
    hb = _rmsnorm_rows(y, g_ref[...]).astype(BF16)
    e = u_ref.shape[1]
    u_ref[...] = _dot(hb, w_ref[:, :e])
    z_ref[...] = jax.nn.silu(_dot(hb, w_ref[:, e:]))


def _even_out(x2d, oa, ob, wa, wb, next_odd=None):
    m, d = x2d.shape
    row = lambda i: (i, 0)
    fixed = lambda i: (0, 0)
    in_specs = [
        pl.BlockSpec((ROW_TILE, d), row),
        pl.BlockSpec((ROW_TILE, A_WIDTH), row),
        pl.BlockSpec((ROW_TILE, B_WIDTH), row),
        pl.BlockSpec((A_WIDTH, d), fixed),
        pl.BlockSpec((B_WIDTH, d), fixed),
    ]
    out_specs = [pl.BlockSpec((ROW_TILE, d), row)]
    out_shape = [jax.ShapeDtypeStruct((m, d), F32)]
    args = [x2d, oa, ob, wa, wb]
    if next_odd is not None:
        gain, w_bf = next_odd
        e = w_bf.shape[1] // 2
        in_specs += [pl.BlockSpec((1, d), fixed), pl.BlockSpec((d, 2 * e), fixed)]
        out_specs += [pl.BlockSpec((ROW_TILE, e), row)] * 2
        out_shape += [jax.ShapeDtypeStruct((m, e), F32)] * 2
        args += [gain, w_bf]
    return pl.pallas_call(
        functools.partial(_even_out_kernel, with_next=next_odd is not None),
        grid=(m // ROW_TILE,),
        in_specs=in_specs,
        out_specs=out_specs,
        out_shape=out_shape,
        compiler_params=_cparams("parallel"),
        name="even_out_proj",
    )(*args)


def _s5_kernel(u_ref, x0r_ref, x0i_ref, ar_ref, ai_ref, bm_ref, cm_ref, d_ref,
               y_ref, fr_ref, fi_ref, st_re, st_im, ubuf_a, ubuf_b, xbuf_a, xbuf_b, ybuf, *, nb, tt):
    t_idx = pl.program_id(1)
    n_blk = bm_ref.shape[0]
    half = S5_BLOCK_STATE
    n_chunk = half // LANES
    pitch = tt + S5_ROW_PAD
    ubufs, xbufs = (ubuf_a, ubuf_b), (xbuf_a, xbuf_b)
    wide = S5_MXU_COLS // LANES
    n_slices = 2 * n_chunk // wide

    @pl.when(t_idx == 0)
    def _():
        for kb in range(n_blk):
            st_re[kb] = x0r_ref[:, kb * half:(kb + 1) * half]
            st_im[kb] = x0i_ref[:, kb * half:(kb + 1) * half]

    for ubuf in ubufs:
        for b in range(nb):
            ubuf[b * pitch + tt:(b + 1) * pitch, :] = jnp.zeros((S5_ROW_PAD, LANES), F32)

    def lanes_of(kb):
        return slice(kb * LANES, (kb + 1) * LANES)

    def stage_u(kb):
        for b in range(nb):
            ubufs[kb % 2][b * pitch:b * pitch + tt, :] = u_ref[b, :, lanes_of(kb)]

    def input_slice(kb, j):
        cols = slice(j * S5_MXU_COLS, (j + 1) * S5_MXU_COLS)
        bu = _dot(ubufs[kb % 2][...].astype(BF16), bm_ref[kb, :, cols])
        for i in range(wide):
            xbufs[kb % 2][j * wide + i] = bu[:, i * LANES:(i + 1) * LANES]

    def output_slice(kb, j, acc):
        x = jnp.concatenate([xbufs[kb % 2][j * wide + i] for i in range(wide)], axis=1).astype(BF16)
        part = _dot(x, cm_ref[kb, j * S5_MXU_COLS:(j + 1) * S5_MXU_COLS, :])
        return part if acc is None else acc + part

    def finish_y(kb, acc):
        ybuf[...] = acc + d_ref[:, lanes_of(kb)] * ubufs[kb % 2][...]
        for b in range(nb):
            y_ref[b, :, lanes_of(kb)] = ybuf[b * pitch:b * pitch + tt, :]

    stage_u(0)
    for j in range(n_slices):
        input_slice(0, j)
    for kb in range(n_blk + 1):
        side, acc = [], [None]
        if kb >= 1:
            def out_task(j, kb=kb):
                acc[0] = output_slice(kb - 1, j, acc[0])
            side += [functools.partial(out_task, j) for j in range(n_slices)]
            side.append(lambda kb=kb: finish_y(kb - 1, acc[0]))
        if kb + 1 < n_blk:
            side.append(functools.partial(stage_u, kb + 1))
            side += [functools.partial(input_slice, kb + 1, j) for j in range(n_slices)]
        if kb == n_blk:
            for task in side:
                task()
            break
        xbuf = xbufs[kb % 2]
        a_re = jnp.broadcast_to(ar_ref[kb], (nb, half))
        a_im = jnp.broadcast_to(ai_ref[kb], (nb, half))
        s_re, s_im = st_re[kb], st_im[kb]
        every = -(-tt // (len(side) + 1))
        for t in range(tt):
            if t % every == every - 1 and side:
                side.pop(0)()
            rows = pl.ds(t, nb, stride=pitch)
            b_re = jnp.concatenate([xbuf[j, rows, :] for j in range(n_chunk)], axis=1)
            b_im = jnp.concatenate([xbuf[n_chunk + j, rows, :] for j in range(n_chunk)], axis=1)
            s_re, s_im = a_re * s_re - a_im * s_im + b_re, a_re * s_im + a_im * s_re + b_im
            for j in range(n_chunk):
                xbuf[j, rows, :] = s_re[:, j * LANES:(j + 1) * LANES]
                xbuf[n_chunk + j, rows, :] = s_im[:, j * LANES:(j + 1) * LANES]
        for task in side:
            task()
        st_re[kb] = s_re
        st_im[kb] = s_im

    @pl.when(t_idx == pl.num_programs(1) - 1)
    def _():
        for kb in range(n_blk):
            fr_ref[:, kb * half:(kb + 1) * half] = st_re[kb]
            fi_ref[:, kb * half:(kb + 1) * half] = st_im[kb]


def _s5(u3, x0_re, x0_im, a_re, a_im, bmat, cmat, d_row, tt):
    batch, seq, e = u3.shape
    nb = SUBLANES
    n_blk = e // LANES
    n_state = n_blk * S5_BLOCK_STATE
    seq_map = lambda b, t: (b, t, 0)
    st_map = lambda b, t: (b, 0)
    fixed2 = lambda b, t: (0, 0)
    fixed3 = lambda b, t: (0, 0, 0)
    return pl.pallas_call(
        functools.partial(_s5_kernel, nb=nb, tt=tt),
        grid=(batch // nb, seq // tt),
        in_specs=[
            pl.BlockSpec((nb, tt, e), seq_map),
            pl.BlockSpec((nb, n_state), st_map),
            pl.BlockSpec((nb, n_state), st_map),
            pl.BlockSpec((n_blk, 1, S5_BLOCK_STATE), fixed3),
            pl.BlockSpec((n_blk, 1, S5_BLOCK_STATE), fixed3),
            pl.BlockSpec((n_blk, LANES, 2 * S5_BLOCK_STATE), fixed3),
            pl.BlockSpec((n_blk, 2 * S5_BLOCK_STATE, LANES), fixed3),
            pl.BlockSpec((1, e), fixed2),
        ],
        out_specs=[
            pl.BlockSpec((nb, tt, e), seq_map),
            pl.BlockSpec((nb, n_state), st_map),
            pl.BlockSpec((nb, n_state), st_map),
        ],
        out_shape=[
            jax.ShapeDtypeStruct((batch, seq, e), F32),
            jax.ShapeDtypeStruct((batch, n_state), F32),
            jax.ShapeDtypeStruct((batch, n_state), F32),
        ],
        scratch_shapes=[
            pltpu.VMEM((n_blk, nb, S5_BLOCK_STATE), F32),
            pltpu.VMEM((n_blk, nb, S5_BLOCK_STATE), F32),
            pltpu.VMEM((nb * (tt + S5_ROW_PAD), LANES), F32),
            pltpu.VMEM((nb * (tt + S5_ROW_PAD), LANES), F32),
            pltpu.VMEM((2 * S5_BLOCK_STATE // LANES, nb * (tt + S5_ROW_PAD), LANES), F32),
            pltpu.VMEM((2 * S5_BLOCK_STATE // LANES, nb * (tt + S5_ROW_PAD), LANES), F32),
            pltpu.VMEM((nb * (tt + S5_ROW_PAD), LANES), F32),
        ],
        compiler_params=_cparams("parallel", "arbitrary"),
        name="s5_scan",
    )(u3, x0_re, x0_im, a_re, a_im, bmat, cmat, d_row)


def _s5_params(lam_re, lam_im, b_re, b_im, c_re, c_im, log_step):
    n_groups = lam_re.shape[0]
    n_blk = n_groups // S5_LANE_GROUPS
    dt = jnp.exp(log_step)[:, None]
    mag = jnp.exp(lam_re * dt)
    ang = lam_im * dt
    ab_re, ab_im = mag * jnp.cos(ang), mag * jnp.sin(ang)
    den = lam_re * lam_re + lam_im * lam_im
    nr = ab_re - 1.0
    f_re = (nr * lam_re + ab_im * lam_im) / den
    f_im = (ab_im * lam_re - nr * lam_im) / den
    bb_re = f_re[..., None] * b_re - f_im[..., None] * b_im
    bb_im = f_re[..., None] * b_im + f_im[..., None] * b_re
    eye = jnp.eye(S5_LANE_GROUPS, dtype=lam_re.dtype)

    def in_map(bb):
        bb = bb.reshape(n_blk, S5_LANE_GROUPS, S5_STATE, S5_GROUP)
        m = jnp.einsum('kgpc,gh->kgchp', bb, eye)
        return m.reshape(n_blk, LANES, S5_BLOCK_STATE)

    def out_map(cc):
        cc = cc.reshape(n_blk, S5_LANE_GROUPS, S5_GROUP, S5_STATE)
        m = jnp.einsum('kgcp,gh->kgphc', cc, eye)
        return m.reshape(n_blk, S5_BLOCK_STATE, LANES)

    bmat = jnp.concatenate([in_map(bb_re), in_map(bb_im)], axis=2).astype(BF16)
    cmat = jnp.concatenate([out_map(c_re), out_map(-c_im)], axis=1).astype(BF16)
    a_re = ab_re.reshape(n_blk, 1, S5_BLOCK_STATE)
    a_im = ab_im.reshape(n_blk, 1, S5_BLOCK_STATE)
    return a_re, a_im, bmat, cmat


def _odd_out_kernel(x_ref, y_ref, z_ref, w1_ref, w2_ref, wo_ref, o_ref):
    yb = jax.nn.gelu(y_ref[...]).astype(BF16)
    t = _dot(yb, w1_ref[...]) * jax.nn.sigmoid(_dot(yb, w2_ref[...])) * z_ref[...]
    o_ref[...] = x_ref[...] + _dot(t.astype(BF16), wo_ref[...])


def _odd_out(x2d, y2d, z2d, w1, w2, wo):
    m, d = x2d.shape
    e = y2d.shape[1]
    row = lambda i: (i, 0)
    fixed = lambda i: (0, 0)
    return pl.pallas_call(
        _odd_out_kernel,
        grid=(m // ROW_TILE,),
        in_specs=[
            pl.BlockSpec((ROW_TILE, d), row),
            pl.BlockSpec((ROW_TILE, e), row),
            pl.BlockSpec((ROW_TILE, e), row),
            pl.BlockSpec((e, e), fixed),
            pl.BlockSpec((e, e), fixed),
            pl.BlockSpec((e, d), fixed),
        ],
        out_specs=pl.BlockSpec((ROW_TILE, d), row),
        out_shape=jax.ShapeDtypeStruct((m, d), F32),
        compiler_params=_cparams("parallel"),
        name="odd_out_proj",
    )(x2d, y2d, z2d, w1, w2, wo)


def _rope_tables(pos):
    half = NSA_DH // 2
    inv = ROPE_THETA ** (-jnp.arange(half, dtype=F32) / half)
    ang = pos.astype(F32)[:, None] * inv[None, :]
    cos, sin = jnp.cos(ang), jnp.sin(ang)
    reps = LANES // NSA_DH
    return jnp.tile(cos, (1, 2 * reps)), jnp.tile(jnp.concatenate([-sin, sin], axis=1), (1, reps))


def _even_weights(norm_g, w_in, w_out, q_norm, k_norm, cmp_pos, cmp_w):
    d = w_in.shape[0]
    sizes = (RET_HEADS * RET_DK, RET_HEADS * RET_DK, A_WIDTH, A_WIDTH, B_WIDTH, 6 * NSA_KV_HEADS * NSA_DH,
             N_GATES, B_WIDTH)
    parts, o = [], 0
    for s in sizes:
        parts.append(w_in[:, o:o + s])
        o += s
    qa, ka, va, za, qn, kvb, gl, zb = parts
    w_perm = jnp.concatenate([qa, ka, va, za, qn, kvb, zb, gl, jnp.zeros((d, LANES - N_GATES), w_in.dtype)],
                             axis=1).astype(BF16)
    reps = LANES // NSA_DH
    qg = jnp.tile(q_norm[None, :], (1, reps))
    kg = jnp.tile(k_norm, (1, reps))
    cw = cmp_w.astype(BF16)
    row_blocks = []
    for c in range(2):
        for g in range(NSA_KV_HEADS):
            r0 = (c * NSA_KV_HEADS + g) * NSA_DH
            row_blocks.append(jnp.pad(cw[c], ((0, 0), (0, 0), (r0, 2 * LANES - NSA_DH - r0))))
    w_bd = jnp.concatenate(row_blocks, axis=1)
    pe = jnp.broadcast_to(cmp_pos.transpose(1, 0, 2)[:, :, None, :], (L_CMP, 2, NSA_KV_HEADS, NSA_DH))
    pe = pe.reshape(L_CMP, 2 * LANES)
    wa = w_out[:A_WIDTH].astype(BF16)
    wb = w_out[A_WIDTH:].astype(BF16)
    return norm_g[None, :], w_perm, qg, kg, w_bd, pe, wa, wb


def _even_layer(x, pos0, s_ret, ew, gn_gain, next_odd, cache=None, page_table=None, win_past=None):
    batch, seq, d = x.shape
    gain, w_perm, qg, kg, w_bd, pe, wa, wb = ew
    x2d = x.reshape(batch * seq, d)
    pos = pos0 + jnp.arange(seq, dtype=jnp.int32)
    cos_t, sin_t = _rope_tables(pos)
    if seq >= EVEN_IN_TILE:
        n_table_blocks = seq // EVEN_IN_TILE
    else:
        cos_t = jnp.tile(cos_t, (EVEN_IN_TILE // seq, 1))
        sin_t = jnp.tile(sin_t, (EVEN_IN_TILE // seq, 1))
        n_table_blocks = 1
    g, dh = NSA_KV_HEADS, NSA_DH
    if cache is None:
        qa, ka, va, za, qn, full_t, win_t, zb, gates, cmp_rows = _even_in(
            x2d, gain, w_perm, cos_t, sin_t, qg, kg, n_table_blocks, seq_tiles=seq // EVEN_IN_TILE)
        oa, s_fin = _retention(qa, ka, va, za, gn_gain[None, :], s_ret, batch, seq)
        cmp_kv = _compress(cmp_rows.reshape(batch, seq, 2 * LANES), pe, w_bd)
        overlap, expand = _selection_tables(seq // CMP_STRIDE, seq)
        ob = _nsa_prompt(qn, gates, zb, cmp_kv, full_t, win_t, overlap, expand, batch, seq)
        keep = min(WINDOW, seq)
        y, *uz = _even_out(x2d, oa, ob, wa, wb, next_odd)
        full_rows = full_t.reshape(batch, N_FULL_KV, g, dh, seq).transpose(0, 4, 1, 2, 3)
        win_rows = win_t[:, :, seq - keep:].reshape(batch, N_WIN_KV, g, dh, keep).transpose(0, 4, 1, 2, 3)
        return y.reshape(batch, seq, d), s_fin, full_rows, win_rows, uz
    else:
        qa, ka, va, za, qn, full_new, win_new, zb, gates = _even_in(x2d, gain, w_perm, cos_t, sin_t, qg, kg,
                                                                     n_table_blocks)
        oa, s_fin = _retention(qa, ka, va, za, gn_gain[None, :], s_ret, batch, seq)
        n_pages = page_table.shape[1]
        past_len = n_pages * PAGE_SIZE
        overlap, expand = _selection_tables(past_len // CMP_STRIDE, past_len + KEY_TILE)
        ob, win_out_t = _nsa_sample(page_table, qn, gates, zb, full_new, win_new, win_past, cache, pe, w_bd,
                                    overlap, expand, batch, seq, n_pages)
    y, *uz = _even_out(x2d, oa, ob, wa, wb, next_odd)
    w_buf = win_out_t.shape[3]
    return (y.reshape(batch, seq, d), s_fin, full_new.reshape(batch, seq, N_FULL_KV, g, dh),
            win_out_t.reshape(batch, N_WIN_KV, g, dh, w_buf).transpose(0, 4, 1, 2, 3), uz)


def _odd_layer(x, uz, s_re, s_im, s5p, d_row, w1, w2, wo, tt):
    batch, seq, d = x.shape
    x2d = x.reshape(batch * seq, d)
    u, zs = uz
    e = u.shape[1]
    a_re, a_im, bmat, cmat = s5p
    n_groups, n_state = s_re.shape[1], s_re.shape[2]
    y, f_re, f_im = _s5(u.reshape(batch, seq, e), s_re.reshape(batch, n_groups * n_state),
                        s_im.reshape(batch, n_groups * n_state), a_re, a_im, bmat, cmat, d_row, tt)
    out = _odd_out(x2d, y.reshape(batch * seq, e), zs, w1, w2, wo).reshape(batch, seq, d)
    return out, f_re.reshape(batch, n_groups, n_state), f_im.reshape(batch, n_groups, n_state)


def kernel(x_prompt, x_sample, cache_nsa_kv, cache_nsa_win, state_ret, state_ssm_re, state_ssm_im, page_table,
           norm_even, w_in_even, w_out_even, ret_gn_gain, nsa_q_norm, nsa_k_norm, nsa_cmp_pos, nsa_cmp_w,
           norm_odd, w_in_odd, ssm_lambda_re, ssm_lambda_im, ssm_b_re, ssm_b_im, ssm_c_re, ssm_c_im, ssm_d,
           ssm_log_step, glu_w1, glu_w2, w_out_odd):
    bp, seq_p, _ = x_prompt.shape
    db, seq_s, _ = x_sample.shape
    n_pages = page_table.shape[1]
    past_len = n_pages * PAGE_SIZE
    depth = norm_even.shape[0] + norm_odd.shape[0]
    yp, ys = x_prompt, x_sample
    ret_p, ret_s, kv_p, kv_s, win_p, win_s = [], [], [], [], [], []
    sre_p, sim_p, sre_s, sim_s = [], [], [], []
    for layer in range(depth):
        li = layer // 2
        if layer % 2 == 0:
            ew = _even_weights(norm_even[li], w_in_even[li], w_out_even[li], nsa_q_norm[li], nsa_k_norm[li],
                               nsa_cmp_pos[li], nsa_cmp_w[li])
            next_odd = None
            if layer + 1 < depth:
                next_odd = (norm_odd[li][None, :], w_in_odd[li].astype(BF16))
            s0 = jnp.zeros((bp, RET_HEADS, RET_DK, RET_DV), F32)
            yp, sr, kvr, wr, uz_p = _even_layer(yp, 0, s0, ew, ret_gn_gain[li], next_odd)
            ret_p.append(sr); kv_p.append(kvr); win_p.append(wr)
            cache_t = cache_nsa_kv[li].transpose(0, 2, 3, 4, 1).reshape(
                cache_nsa_kv.shape[1], N_FULL_KV, LANES, PAGE_SIZE)
            win_t = cache_nsa_win[li].transpose(0, 2, 3, 4, 1).reshape(
                db, N_WIN_KV, LANES, cache_nsa_win.shape[2])
            ys, sr2, kvr2, wr2, uz_s = _even_layer(ys, past_len, state_ret[li], ew, ret_gn_gain[li], next_odd,
                                                   cache=cache_t, page_table=page_table, win_past=win_t)
            ret_s.append(sr2); kv_s.append(kvr2); win_s.append(wr2)
        else:
            s5p = _s5_params(ssm_lambda_re[li], ssm_lambda_im[li], ssm_b_re[li], ssm_b_im[li],
                             ssm_c_re[li], ssm_c_im[li], ssm_log_step[li])
            w1, w2, wo = glu_w1[li].astype(BF16), glu_w2[li].astype(BF16), w_out_odd[li].astype(BF16)
            d_row = ssm_d[li][None, :]
            n_groups = ssm_lambda_re.shape[1]
            z0 = jnp.zeros((bp, n_groups, S5_STATE), F32)
            yp, fr, fi = _odd_layer(yp, uz_p, z0, z0, s5p, d_row, w1, w2, wo, min(S5_TIME_TILE, seq_p))
            sre_p.append(fr); sim_p.append(fi)
            ys, fr2, fi2 = _odd_layer(ys, uz_s, state_ssm_re[li], state_ssm_im[li], s5p, d_row,
                                      w1, w2, wo, min(S5_TIME_TILE, seq_s))
            sre_s.append(fr2); sim_s.append(fi2)
    return (yp, ys, jnp.stack(ret_p), jnp.stack(ret_s), jnp.stack(kv_p), jnp.stack(kv_s), jnp.stack(win_p),
            jnp.stack(win_s), jnp.stack(sre_p), jnp.stack(sim_p), jnp.stack(sre_s), jnp.stack(sim_s))
```

```python
import functools
import math

import jax
import jax.numpy as jnp
from jax import lax
from jax.experimental import pallas as pl
from jax.experimental.pallas import tpu as pltpu

F32 = jnp.float32
BF16 = jnp.bfloat16

LANES = 128
SUBLANES = 8
VMEM_LIMIT_BYTES = 48 * 2**20

EPS = 1e-6
ROPE_THETA = 10000.0
NEG_INF = -1e30
FORCE = 1e4

RET_HEADS = 4
RET_DK = 64
RET_DV = 128
RET_CHUNK = 128
RET_TILES_PER_STEP = 8
A_WIDTH = RET_HEADS * RET_DV

NSA_HEADS = 8
NSA_KV_HEADS = 2
NSA_DH = 64
NSA_HPG = NSA_HEADS // NSA_KV_HEADS
B_WIDTH = NSA_HEADS * NSA_DH
L_CMP = 32
CMP_STRIDE = 16
CMP_RATIO = L_CMP // CMP_STRIDE
L_SEL = 64
N_SEL = 8
WINDOW = 512
PAGE_SIZE = 128
N_FULL_KV = 4
N_WIN_KV = 2
K_SEL, V_SEL = 2, 3
KV_ROW = N_FULL_KV * NSA_KV_HEADS * NSA_DH
WIN_ROW = N_WIN_KV * NSA_KV_HEADS * NSA_DH
KEY_TILE = 128
PROMPT_Q_TILE = 256
WIN_KEYS = WINDOW + PROMPT_Q_TILE
SEL_CHUNK = 512
CMP_SLOT = CMP_STRIDE + 4
SAMPLE_REQS_PER_STEP = 2

S5_GROUP = 16
S5_STATE = 64
S5_LANE_GROUPS = LANES // S5_GROUP
S5_BLOCK_STATE = S5_LANE_GROUPS * S5_STATE
S5_ROW_PAD = 4
S5_TIME_TILE = 128
S5_MXU_COLS = 256

QA0, KA0, VA0, ZA0, QN0, KVB0, GL0 = 0, 256, 512, 1024, 1536, 2048, 2816
N_GATES = 3 * NSA_HEADS
ZB_COL0 = GL0 + N_GATES

ROW_TILE = 512
EVEN_IN_TILE = 256


def _cparams(*sem):
    return pltpu.CompilerParams(dimension_semantics=sem, vmem_limit_bytes=VMEM_LIMIT_BYTES)


def _lane_iota(shape):
    return lax.broadcasted_iota(jnp.int32, shape, len(shape) - 1)


def _row_iota(shape):
    return lax.broadcasted_iota(jnp.int32, shape, len(shape) - 2)


def _dot(a, b):
    return jnp.dot(a, b, preferred_element_type=F32)


def _dot_nt(a, b):
    return lax.dot_general(a, b, (((1,), (1,)), ((), ())), preferred_element_type=F32)


def _rmsnorm_rows(x, g):
    return x * lax.rsqrt(jnp.mean(x * x, axis=-1, keepdims=True) + EPS) * g


def _swap_halves(x):
    return pltpu.roll(x, NSA_DH, axis=1)


def _rope_block(x, cos, sin_signed):
    half = NSA_DH // 2
    lane = _lane_iota(x.shape)
    first = (lane % NSA_DH) < half
    partner = jnp.where(first, pltpu.roll(x, LANES - half, axis=1), pltpu.roll(x, half, axis=1))
    return x * cos + partner * sin_signed


def _head_rms_block(x, g):
    lane = _lane_iota(x.shape)
    lo = lane < NSA_DH
    sq = x * x
    s_lo = jnp.sum(jnp.where(lo, sq, 0.0), axis=-1, keepdims=True)
    s_hi = jnp.sum(jnp.where(lo, 0.0, sq), axis=-1, keepdims=True)
    ms = jnp.where(lo, s_lo, s_hi) * (1.0 / NSA_DH)
    return x * lax.rsqrt(ms + EPS) * g


def _even_in_kernel(x_ref, g_ref, w_ref, wz_ref, cos_ref, sin_ref, qg_ref, kg_ref, *out_refs, transposed_kv):
    if transposed_kv:
        qa_ref, ka_ref, va_ref, za_ref, qn_ref, full_ref, win_ref, zb_ref, gl_ref, cmp_ref = out_refs
    else:
        qa_ref, ka_ref, va_ref, za_ref, qn_ref, full_ref, win_ref, zb_ref, gl_ref = out_refs
    hb = _rmsnorm_rows(x_ref[...], g_ref[...]).astype(BF16)
    cos = cos_ref[...]
    sin = sin_ref[...]

    def proj(c0):
        return _dot(hb, w_ref[:, c0:c0 + LANES])

    for j in range(RET_HEADS * RET_DK // LANES):
        c = j * LANES
        qa_ref[:, c:c + LANES] = _rope_block(proj(QA0 + c), cos, sin)
        ka_ref[:, c:c + LANES] = _rope_block(proj(KA0 + c), cos, sin) * (RET_DK ** -0.5)
    for j in range(A_WIDTH // LANES):
        c = j * LANES
        va_ref[:, c:c + LANES] = proj(VA0 + c)
        za_ref[:, c:c + LANES] = jax.nn.silu(proj(ZA0 + c))
    for j in range(B_WIDTH // LANES):
        c = j * LANES
        qn_ref[:, c:c + LANES] = _rope_block(_head_rms_block(proj(QN0 + c), qg_ref[...]), cos, sin)
        zb_ref[:, c:c + LANES] = jax.nn.silu(_dot(hb, wz_ref[:, c:c + LANES]))
    for j in range(6):
        y = proj(KVB0 + j * LANES)
        if j % 2 == 0:
            y = _rope_block(_head_rms_block(y, kg_ref[j // 2:j // 2 + 1, :]), cos, sin)
        if transposed_kv:
            if j < 4:
                full_ref[0, j * LANES:(j + 1) * LANES, :] = y.T
            else:
                win_ref[0, (j - 4) * LANES:(j - 3) * LANES, :] = y.T
            if j < 2:
                cmp_ref[:, j * LANES:(j + 1) * LANES] = y
        elif j < 4:
            full_ref[:, j * LANES:(j + 1) * LANES] = y
        else:
            win_ref[:, (j - 4) * LANES:(j - 3) * LANES] = y
    gl_ref[...] = jax.nn.sigmoid(proj(GL0))


def _even_in(x2d, gain, w_bf, w_zb, cos_t, sin_t, qg, kg, n_table_blocks, seq_tiles=None):
    m, d = x2d.shape
    tile = EVEN_IN_TILE
    grid = (m // tile,)
    row = lambda i: (i, 0)
    fixed = lambda i: (0, 0)
    table = lambda i: (i % n_table_blocks, 0)
    widths = (RET_HEADS * RET_DK, RET_HEADS * RET_DK, A_WIDTH, A_WIDTH, B_WIDTH, KV_ROW, WIN_ROW, B_WIDTH, LANES)
    out_specs = [pl.BlockSpec((tile, w), row) for w in widths]
    out_shape = [jax.ShapeDtypeStruct((m, w), F32) for w in widths]
    if seq_tiles is not None:
        batch, seq = grid[0] // seq_tiles, seq_tiles * tile
        fmajor = lambda i: (i // seq_tiles, 0, i % seq_tiles)
        for idx, w in ((5, KV_ROW), (6, WIN_ROW)):
            out_specs[idx] = pl.BlockSpec((1, w, tile), fmajor)
            out_shape[idx] = jax.ShapeDtypeStruct((batch, w, seq), F32)
        out_specs.append(pl.BlockSpec((tile, 2 * LANES), row))
        out_shape.append(jax.ShapeDtypeStruct((m, 2 * LANES), F32))
    return pl.pallas_call(
        functools.partial(_even_in_kernel, transposed_kv=seq_tiles is not None),
        grid=grid,
        in_specs=[
            pl.BlockSpec((tile, d), row),
            pl.BlockSpec((1, d), fixed),
            pl.BlockSpec(w_bf.shape, fixed),
            pl.BlockSpec(w_zb.shape, fixed),
            pl.BlockSpec((tile, LANES), table),
            pl.BlockSpec((tile, LANES), table),
            pl.BlockSpec((1, LANES), fixed),
            pl.BlockSpec((3, LANES), fixed),
        ],
        out_specs=out_specs,
        out_shape=out_shape,
        compiler_params=_cparams("parallel"),
        name="even_in_proj",
    )(x2d, gain, w_bf, w_zb, cos_t, sin_t, qg, kg)


def _retention_kernel(q_ref, k_ref, v_ref, z_ref, gn_ref, s0_ref, dmat_ref, qdec_ref, kdec_ref, cdec_ref,
                      o_ref, sfin_ref, s_scr, *, rows, n_tiles):
    n_seq = RET_CHUNK // rows
    c = pl.program_id(1)

    @pl.when(c == 0)
    def _():
        s_scr[...] = s0_ref[...]

    lane = _lane_iota((RET_CHUNK, LANES))
    col_seq = _lane_iota((RET_DK, RET_CHUNK)) >> int(math.log2(rows)) if n_seq > 1 else None
    state = [[s_scr[r, h] for h in range(RET_HEADS)] for r in range(n_seq)]
    for t in range(n_tiles):
        tile = slice(t * RET_CHUNK, (t + 1) * RET_CHUNK)
        q, k, v = q_ref[tile, :], k_ref[tile, :], v_ref[tile, :]
        qd = q * qdec_ref[...]
        kd = k * kdec_ref[...]
        for pair in range(RET_HEADS // 2):
            cols = slice(pair * LANES, (pair + 1) * LANES)
            q2, k2b, qd2 = q[:, cols], k[:, cols].astype(BF16), qd[:, cols]
            kd2_t = kd[:, cols].T
            s_pair = [jnp.concatenate([state[r][2 * pair], state[r][2 * pair + 1]], axis=0).astype(BF16)
                      for r in range(n_seq)]
            for sub in range(2):
                h = 2 * pair + sub
                mine = (lane >= sub * RET_DK) & (lane < (sub + 1) * RET_DK)
                qm = jnp.where(mine, q2, 0.0).astype(BF16)
                qdm = jnp.where(mine, qd2, 0.0).astype(BF16)
                vhb = v[:, h * RET_DV:(h + 1) * RET_DV].astype(BF16)
                intra = _dot((_dot_nt(qm, k2b) * dmat_ref[h]).astype(BF16), vhb)
                cross = jnp.concatenate([_dot(qdm[r * rows:(r + 1) * rows], s_pair[r]) for r in range(n_seq)],
                                        axis=0)
                out = intra + cross
                kt = kd2_t[sub * RET_DK:(sub + 1) * RET_DK, :]
                if n_seq > 1:
                    kt = jnp.concatenate([jnp.where(col_seq == r, kt, 0.0) for r in range(n_seq)], axis=0)
                kv = _dot(kt.astype(BF16), vhb)
                for r in range(n_seq):
                    state[r][h] = state[r][h] * cdec_ref[h] + kv[r * RET_DK:(r + 1) * RET_DK]
                mu = jnp.mean(out, axis=-1, keepdims=True)
                cen = out - mu
                var = jnp.mean(cen * cen, axis=-1, keepdims=True)
                y = cen * lax.rsqrt(var + EPS) * gn_ref[:, h * RET_DV:(h + 1) * RET_DV]
                o_ref[tile, h * RET_DV:(h + 1) * RET_DV] = y * z_ref[tile, h * RET_DV:(h + 1) * RET_DV]
    for r in range(n_seq):
        for h in range(RET_HEADS):
            s_scr[r, h] = state[r][h]

    @pl.when(c == pl.num_programs(1) - 1)
    def _():
        sfin_ref[...] = s_scr[...]


def _retention_tables(rows):
    log_g = jnp.log(1.0 - 2.0 ** (-5.0 - jnp.arange(RET_HEADS, dtype=F32)))
    idx = jnp.arange(RET_CHUNK, dtype=jnp.int32)
    pos = (idx % rows).astype(F32)
    seq = idx // rows
    diff = pos[:, None] - pos[None, :]
    causal = (diff >= 0) & (seq[:, None] == seq[None, :])
    dmat = jnp.exp(jnp.where(causal, diff, 0.0)[None] * log_g[:, None, None]) * causal[None]
    qdec = jnp.exp((pos + 1.0)[:, None] * log_g[None, :])
    kdec = jnp.exp((rows - 1.0 - pos)[:, None] * log_g[None, :])
    qdec = jnp.repeat(qdec, RET_DK, axis=1)
    kdec = jnp.repeat(kdec, RET_DK, axis=1)
    cdec = jnp.broadcast_to(jnp.exp(rows * log_g)[:, None, None], (RET_HEADS, RET_DK, RET_DV))
    return dmat, qdec, kdec, cdec


def _retention(qa, ka, va, za, gn_gain, s0, batch, seq):
    rows = min(RET_CHUNK, seq)
    n_seq = RET_CHUNK // rows
    n_chunks = seq // rows
    n_tiles = min(RET_TILES_PER_STEP, n_chunks)
    steps = n_chunks // n_tiles
    dmat, qdec, kdec, cdec = _retention_tables(rows)
    tok = lambda b, c: (b * steps + c, 0)
    fixed2 = lambda b, c: (0, 0)
    fixed3 = lambda b, c: (0, 0, 0)
    state = lambda b, c: (b, 0, 0, 0)
    qk_w = RET_HEADS * RET_DK
    step_rows = n_tiles * RET_CHUNK
    return pl.pallas_call(
        functools.partial(_retention_kernel, rows=rows, n_tiles=n_tiles),
        grid=(batch // n_seq, steps),
        in_specs=[
            pl.BlockSpec((step_rows, qk_w), tok),
            pl.BlockSpec((step_rows, qk_w), tok),
            pl.BlockSpec((step_rows, A_WIDTH), tok),
            pl.BlockSpec((step_rows, A_WIDTH), tok),
            pl.BlockSpec((1, A_WIDTH), fixed2),
            pl.BlockSpec((n_seq, RET_HEADS, RET_DK, RET_DV), state),
            pl.BlockSpec((RET_HEADS, RET_CHUNK, RET_CHUNK), fixed3),
            pl.BlockSpec((RET_CHUNK, qk_w), fixed2),
            pl.BlockSpec((RET_CHUNK, qk_w), fixed2),
            pl.BlockSpec((RET_HEADS, RET_DK, RET_DV), fixed3),
        ],
        out_specs=[
            pl.BlockSpec((step_rows, A_WIDTH), tok),
            pl.BlockSpec((n_seq, RET_HEADS, RET_DK, RET_DV), state),
        ],
        out_shape=[
            jax.ShapeDtypeStruct((batch * seq, A_WIDTH), F32),
            jax.ShapeDtypeStruct((batch, RET_HEADS, RET_DK, RET_DV), F32),
        ],
        scratch_shapes=[pltpu.VMEM((n_seq, RET_HEADS, RET_DK, RET_DV), F32)],
        compiler_params=_cparams("parallel", "arbitrary"),
        name="retention",
    )(qa, ka, va, za, gn_gain, s0, dmat, qdec, kdec, cdec)


def _compress_kernel(k_ref, v_ref, pe_ref, w_ref, o_ref):
    n_half = k_ref.shape[1] // CMP_STRIDE
    acc = [jnp.zeros((n_half, 2 * LANES), F32) for _ in range(CMP_RATIO)]
    for l in range(CMP_STRIDE):
        rows = pl.ds(l, n_half, stride=CMP_STRIDE)
        x = jnp.concatenate([k_ref[0, rows, :], v_ref[0, rows, :]], axis=1)
        for r in range(CMP_RATIO):
            i = r * CMP_STRIDE + l
            acc[r] = acc[r] + _dot((x + pe_ref[i:i + 1, :]).astype(BF16), w_ref[i])
    out = acc[0]
    for r in range(1, CMP_RATIO):
        out = out + pltpu.roll(acc[r], n_half - r, axis=0)
    live = _row_iota(out.shape) < n_half - CMP_RATIO + 1
    o_ref[0] = jnp.where(live, out, 0.0)


def _compress(cmp_rows, pe, w_bd):
    batch, seq, _ = cmp_rows.shape
    n_half = seq // CMP_STRIDE
    return pl.pallas_call(
        _compress_kernel,
        grid=(batch,),
        in_specs=[
            pl.BlockSpec((1, seq, LANES), lambda b: (b, 0, 0)),
            pl.BlockSpec((1, seq, LANES), lambda b: (b, 0, 1)),
            pl.BlockSpec((L_CMP, 2 * LANES), lambda b: (0, 0)),
            pl.BlockSpec((L_CMP, 2 * LANES, 2 * LANES), lambda b: (0, 0, 0)),
        ],
        out_specs=pl.BlockSpec((1, n_half, 2 * LANES), lambda b: (b, 0, 0)),
        out_shape=jax.ShapeDtypeStruct((batch, n_half, 2 * LANES), F32),
        compiler_params=_cparams("parallel"),
        name="kv_compress",
    )(cmp_rows, cmp_rows, pe, w_bd)


def _split3_bf16(x):
    hi = x.astype(BF16)
    r1 = x - hi.astype(F32)
    mid = r1.astype(BF16)
    lo = (r1 - mid.astype(F32)).astype(BF16)
    return hi, mid, lo


def _stack_group_queries(q, g):
    lane = _lane_iota((q.shape[0], LANES))
    in_g = (lane >= g * NSA_DH) & (lane < (g + 1) * NSA_DH)
    parts = []
    for hh in range(NSA_HPG):
        h = g * NSA_HPG + hh
        two = q[:, (h // 2) * LANES:(h // 2 + 1) * LANES]
        if h % 2 != g:
            two = _swap_halves(two)
        parts.append(jnp.where(in_g, two, 0.0))
    return (jnp.concatenate(parts, axis=0) * (NSA_DH ** -0.5)).astype(BF16)


def _masked_softmax(s, ok):
    s = jnp.where(ok, s, NEG_INF)
    e = jnp.exp(s - jnp.max(s, axis=-1, keepdims=True))
    return jnp.where(ok, e * (1.0 / jnp.sum(e, axis=-1, keepdims=True)), 0.0)


def _sum_heads(p1, tq):
    psum = p1[0:tq]
    for hh in range(1, NSA_HPG):
        psum = psum + p1[hh * tq:(hh + 1) * tq]
    return psum


def _choose_blocks(psum, t0, tq, overlap, n_blocks):
    rows = psum.shape[0]
    imp = sum(_dot(t, overlap) for t in _split3_bf16(psum))
    shift = int(math.log2(L_SEL))

    def ranked(score, blk, valid, take):
        rank = jnp.zeros(score.shape, F32)
        for s in range(n_blocks):
            cand = take(s)
            rank = rank + ((cand > score) | ((cand == score) & (blk > s))).astype(F32)
        return ((rank < N_SEL) & valid).astype(F32)

    if rows % LANES:
        tpos = jnp.concatenate([t0 + _row_iota((tq, 1))] * (rows // tq), axis=0)
        blk = _lane_iota((rows, LANES))
        cur = tpos >> shift
        forced = (blk == 0) | (blk == cur) | (blk == cur - 1)
        valid = (blk * L_SEL <= tpos) & (blk < n_blocks)
        score = jnp.where(valid, jnp.where(forced, FORCE, imp), -FORCE)
        return ranked(score, blk, valid, lambda s: score[:, s:s + 1]).astype(BF16)

    nbp = -(-n_blocks // SUBLANES) * SUBLANES
    tiles = rows // LANES
    imp_t = jnp.concatenate([imp[i * LANES:(i + 1) * LANES].T[:nbp] for i in range(tiles)], axis=1)
    tpos = jnp.concatenate([t0 + _lane_iota((1, tq))] * (rows // tq), axis=1)
    blk = _row_iota((nbp, rows))
    cur = tpos >> shift
    forced = (blk == 0) | (blk == cur) | (blk == cur - 1)
    valid = (blk * L_SEL <= tpos) & (blk < n_blocks)
    score = jnp.where(valid, jnp.where(forced, FORCE, imp_t), -FORCE)
    chosen_t = ranked(score, blk, valid, lambda s: score[s:s + 1, :])
    chosen_t = jnp.concatenate([chosen_t, jnp.zeros((LANES - nbp, rows), F32)], axis=0)
    return jnp.concatenate([chosen_t[:, i * LANES:(i + 1) * LANES].T for i in range(tiles)],
                           axis=0).astype(BF16)


def _gate_and_store(gates, zs, branch_outs, o_ref, row0=0):
    tq = gates.shape[0]
    lane = _lane_iota((tq, LANES))
    ext = []
    for g in range(NSA_KV_HEADS):
        o_cmp, o_sel, o_win = branch_outs[g]
        for hh in range(NSA_HPG):
            h = g * NSA_HPG + hh
            rows = slice(hh * tq, (hh + 1) * tq)
            o = (gates[:, 3 * h:3 * h + 1] * o_cmp[rows] + gates[:, 3 * h + 1:3 * h + 2] * o_sel[rows]
                 + gates[:, 3 * h + 2:3 * h + 3] * o_win[rows])
            ext.append(_swap_halves(o) if h % 2 != g else o)
    for j in range(NSA_HEADS // 2):
        both = jnp.where(lane < NSA_DH, ext[2 * j], ext[2 * j + 1])
        o_ref[row0:row0 + tq, j * LANES:(j + 1) * LANES] = both * zs[:, j * LANES:(j + 1) * LANES]


def _nsa_prompt_kernel(q_ref, g_ref, z_ref, cmp_ref, sel_ref, win_ref, ov_ref, ex_ref, o_ref, *, n_blocks):
    i = pl.program_id(1)
    tq = q_ref.shape[0]
    q = q_ref[...]
    t0 = i * tq
    tpos = t0 + _row_iota((tq, 1))
    tpos_all = jnp.concatenate([tpos] * NSA_HEADS, axis=0)
    grp_rows = NSA_HPG * tq
    qs = jnp.concatenate([_stack_group_queries(q, g) for g in range(NSA_KV_HEADS)], axis=0)

    cmp_blk = cmp_ref[0].astype(BF16)
    cmp_ok = (_lane_iota((1, cmp_blk.shape[0])) * CMP_STRIDE + (L_CMP - 1)) <= tpos_all
    p1 = _masked_softmax(_dot_nt(qs, cmp_blk[:, :LANES]), cmp_ok)
    o_cmp = _dot(p1.astype(BF16), cmp_blk[:, LANES:])
    psum = jnp.concatenate([_sum_heads(p1[g * grp_rows:(g + 1) * grp_rows], tq)
                            for g in range(NSA_KV_HEADS)], axis=0)
    chosen = _choose_blocks(psum, t0, tq, ov_ref[...], n_blocks)

    def sel_chunk(c, carry):
        m, l, acc = carry
        k0 = pl.multiple_of(c * SEL_CHUNK, SEL_CHUNK)
        k_t = sel_ref[0, 0:LANES, pl.ds(k0, SEL_CHUNK)].astype(BF16)
        v_t = sel_ref[0, LANES:2 * LANES, pl.ds(k0, SEL_CHUNK)].astype(BF16)
        causal = jnp.concatenate([(k0 + _lane_iota((1, SEL_CHUNK))) <= tpos] * NSA_KV_HEADS, axis=0)
        key_on = (_dot(chosen, ex_ref[:, pl.ds(k0, SEL_CHUNK)]) > 0.5) & causal
        bias = jnp.where(key_on, 0.0, NEG_INF)
        bias = jnp.concatenate([bias[g * tq:(g + 1) * tq] for g in range(NSA_KV_HEADS)
                                for _ in range(NSA_HPG)], axis=0)
        s = _dot(qs, k_t) + bias
        m_new = jnp.maximum(m, jnp.max(s, axis=-1, keepdims=True))
        alpha = jnp.exp(m - m_new)
        e = jnp.exp(s - m_new)
        l_new = alpha * l + jnp.sum(e, axis=-1, keepdims=True)
        return m_new, l_new, alpha * acc + _dot_nt(e.astype(BF16), v_t)

    rows = NSA_HEADS * tq
    init = (jnp.full((rows, 1), NEG_INF, F32), jnp.zeros((rows, 1), F32), jnp.zeros((rows, LANES), F32))
    n_chunks = t0 // SEL_CHUNK + 1
    _, l2, acc2 = lax.fori_loop(0, n_chunks, sel_chunk, init)
    o_sel = acc2 / l2

    w0 = pl.multiple_of(jnp.maximum(t0 - WINDOW, 0), KEY_TILE)
    wk_t = win_ref[0, 0:LANES, pl.ds(w0, WIN_KEYS)].astype(BF16)
    wv_t = win_ref[0, LANES:2 * LANES, pl.ds(w0, WIN_KEYS)].astype(BF16)
    win_kpos = w0 + _lane_iota((1, WIN_KEYS))
    win_ok = (win_kpos <= tpos_all) & (win_kpos > tpos_all - WINDOW)
    s3 = jnp.where(win_ok, _dot(qs, wk_t), NEG_INF)
    e3 = jnp.exp(s3 - jnp.max(s3, axis=-1, keepdims=True))
    o_win = _dot_nt(e3.astype(BF16), wv_t) / jnp.sum(e3, axis=-1, keepdims=True)

    branch_outs = [(o_cmp[g * grp_rows:(g + 1) * grp_rows], o_sel[g * grp_rows:(g + 1) * grp_rows],
                    o_win[g * grp_rows:(g + 1) * grp_rows]) for g in range(NSA_KV_HEADS)]
    _gate_and_store(g_ref[...], z_ref[...], branch_outs, o_ref)


def _nsa_prompt(qn, gates, zb, cmp_kv, full_t, win_t, overlap, expand, batch, seq):
    tq = PROMPT_Q_TILE
    nq = seq // tq
    tok = lambda b, i: (b * nq + i, 0)
    per_b = lambda b, i: (b, 0, 0)
    sel_half = lambda b, i: (b, 1, 0)
    fixed = lambda b, i: (0, 0)
    n_blocks = -(-seq // L_SEL)
    return pl.pallas_call(
        functools.partial(_nsa_prompt_kernel, n_blocks=n_blocks),
        grid=(batch, nq),
        in_specs=[
            pl.BlockSpec((tq, B_WIDTH), tok),
            pl.BlockSpec((tq, LANES), tok),
            pl.BlockSpec((tq, B_WIDTH), tok),
            pl.BlockSpec((1, cmp_kv.shape[1], 2 * LANES), per_b),
            pl.BlockSpec((1, 2 * LANES, seq), sel_half),
            pl.BlockSpec((1, WIN_ROW, seq), per_b),
            pl.BlockSpec(overlap.shape, fixed),
            pl.BlockSpec(expand.shape, fixed),
        ],
        out_specs=pl.BlockSpec((tq, B_WIDTH), tok),
        out_shape=jax.ShapeDtypeStruct((batch * seq, B_WIDTH), F32),
        compiler_params=_cparams("parallel", "arbitrary"),
        name="nsa_prompt",
    )(qn, gates, zb, cmp_kv, full_t, win_t, overlap, expand)


def _nsa_sample_kernel(*refs, nb, n_pages, past_len, n_blocks):
    q_ref, g_ref, z_ref, newf_ref, neww_ref, winp_ref = refs[1:7]
    pages = refs[7:7 + nb * n_pages]
    pe_ref, w_ref, ov_ref, ex_ref, o_ref, wout_ref, xs_ref = refs[7 + nb * n_pages:]
    tq = q_ref.shape[0] // nb
    w_buf = winp_ref.shape[3]
    halves_per_page = PAGE_SIZE // CMP_STRIDE
    n_half = past_len // CMP_STRIDE
    total_half = nb * n_half

    pad = jnp.zeros((KEY_TILE - tq, LANES), F32)
    lane = _lane_iota((LANES, LANES))
    per_req = NSA_HEADS * tq
    tpos = past_len + _row_iota((tq, 1))
    tpos_all = jnp.concatenate([tpos] * (nb * NSA_HEADS), axis=0)
    tpos_grp = jnp.concatenate([tpos] * (nb * NSA_KV_HEADS), axis=0)

    def padded_rows(x):
        return jnp.concatenate([x, pad], axis=0)

    def per_request(fn):
        return jnp.concatenate([fn(j, slice(j * per_req, (j + 1) * per_req)) for j in range(nb)], axis=0)

    tok = [slice(j * tq, (j + 1) * tq) for j in range(nb)]
    qs = jnp.concatenate([_stack_group_queries(q_ref[tok[j], :], g)
                          for j in range(nb) for g in range(NSA_KV_HEADS)], axis=0)
    mine = [pages[j * n_pages:(j + 1) * n_pages] for j in range(nb)]
    newf = [newf_ref[tok[j], :] for j in range(nb)]
    neww = [neww_ref[tok[j], :] for j in range(nb)]

    sel_pieces = [[] for _ in range(nb)]
    for j in range(nb * n_pages):
        for c in range(2):
            x = pages[j][0, c].T
            for hb in range(halves_per_page):
                slot = (j * halves_per_page + hb) * CMP_SLOT
                xs_ref[c, slot:slot + CMP_STRIDE, :] = x[hb * CMP_STRIDE:(hb + 1) * CMP_STRIDE]
        req = j // n_pages
        sel_pieces[req].append(_dot(qs[req * per_req:(req + 1) * per_req], pages[j][0, K_SEL].astype(BF16)))
    s2_raw = per_request(lambda j, r: jnp.concatenate(
        sel_pieces[j] + [_dot_nt(qs[r], padded_rows(newf[j][:, 2 * LANES:3 * LANES]).astype(BF16))], axis=1))
    s3_raw = per_request(lambda j, r: jnp.concatenate(
        [_dot(qs[r], winp_ref[j, 0].astype(BF16)),
         _dot_nt(qs[r], padded_rows(neww[j][:, :LANES]).astype(BF16))], axis=1))

    for j in range(nb):
        for c in range(2):
            shifted = pltpu.roll(winp_ref[j, c], w_buf - tq, axis=1)
            new_cols = pltpu.roll(padded_rows(neww[j][:, c * LANES:(c + 1) * LANES]).T, LANES - tq, axis=1)
            wout_ref[j, c, :, 0:w_buf - LANES] = shifted[:, 0:w_buf - LANES]
            wout_ref[j, c, :, w_buf - LANES:w_buf] = jnp.where(lane >= LANES - tq, new_cols,
                                                               shifted[:, w_buf - LANES:w_buf])

    acc = [jnp.zeros((total_half, 2 * LANES), F32) for _ in range(CMP_RATIO)]
    for l in range(CMP_STRIDE):
        x = jnp.concatenate([xs_ref[c, pl.ds(l, total_half, stride=CMP_SLOT), :] for c in range(2)], axis=1)
        for r in range(CMP_RATIO):
            i = r * CMP_STRIDE + l
            acc[r] = acc[r] + _dot((x + pe_ref[i:i + 1, :]).astype(BF16), w_ref[i])
    cmp_all = acc[0]
    for r in range(1, CMP_RATIO):
        cmp_all = cmp_all + pltpu.roll(acc[r], total_half - r, axis=0)
    live = _row_iota((n_half, 2 * LANES)) < n_half - CMP_RATIO + 1
    cmp_blk = [jnp.where(live, cmp_all[j * n_half:(j + 1) * n_half], 0.0).astype(BF16) for j in range(nb)]

    cmp_ok = (_lane_iota((1, n_half)) * CMP_STRIDE + (L_CMP - 1)) <= tpos_all
    p1 = _masked_softmax(per_request(lambda j, r: _dot_nt(qs[r], cmp_blk[j][:, :LANES])), cmp_ok)
    p1b = p1.astype(BF16)
    o_cmp = per_request(lambda j, r: _dot(p1b[r], cmp_blk[j][:, LANES:]))

    grp_rows = NSA_HPG * tq
    psum = jnp.concatenate([_sum_heads(p1[i * grp_rows:(i + 1) * grp_rows], tq)
                            for i in range(nb * NSA_KV_HEADS)], axis=0)
    chosen = _choose_blocks(psum, past_len, tq, ov_ref[...], n_blocks)
    key_on = (_dot(chosen, ex_ref[...]) > 0.5) & (_lane_iota((1, past_len + KEY_TILE)) <= tpos_grp)
    bias = jnp.where(key_on, 0.0, NEG_INF)
    bias = jnp.concatenate([bias[i * tq:(i + 1) * tq] for i in range(nb * NSA_KV_HEADS)
                            for _ in range(NSA_HPG)], axis=0)

    s2 = s2_raw + bias
    e2 = jnp.exp(s2 - jnp.max(s2, axis=-1, keepdims=True))
    l2 = jnp.sum(e2, axis=-1, keepdims=True)
    e2 = e2.astype(BF16)

    def sel_values(j, r):
        v_t = jnp.concatenate([p[0, V_SEL].astype(BF16) for p in mine[j]], axis=1)
        v_new = padded_rows(newf[j][:, 3 * LANES:4 * LANES]).astype(BF16)
        return _dot_nt(e2[r, :past_len], v_t) + _dot(e2[r, past_len:], v_new)

    o_sel = per_request(sel_values) / l2

    win_kpos = (past_len - w_buf) + _lane_iota((1, w_buf + KEY_TILE))
    win_ok = (win_kpos <= tpos_all) & (win_kpos > tpos_all - WINDOW)
    s3 = jnp.where(win_ok, s3_raw, NEG_INF)
    e3 = jnp.exp(s3 - jnp.max(s3, axis=-1, keepdims=True))
    l3 = jnp.sum(e3, axis=-1, keepdims=True)
    e3 = e3.astype(BF16)
    o_win = per_request(lambda j, r: _dot_nt(e3[r, :w_buf], winp_ref[j, 1].astype(BF16))
                        + _dot(e3[r, w_buf:], padded_rows(neww[j][:, LANES:]).astype(BF16))) / l3

    for j in range(nb):
        branch_outs = []
        for g in range(NSA_KV_HEADS):
            r = slice((j * NSA_KV_HEADS + g) * grp_rows, (j * NSA_KV_HEADS + g + 1) * grp_rows)
            branch_outs.append((o_cmp[r], o_sel[r], o_win[r]))
        _gate_and_store(g_ref[tok[j], :], z_ref[tok[j], :], branch_outs, o_ref, row0=j * tq)


def _nsa_sample(page_table, qn, gates, zb, full_new, win_new, win_t, cache_t, pe, w_bd, overlap, expand,
                batch, tq, n_pages):
    nb = SAMPLE_REQS_PER_STEP
    past_len = n_pages * PAGE_SIZE
    w_buf = win_t.shape[3]
    n_blocks = -(-(past_len + tq) // L_SEL)
    tok = lambda b, pt: (b, 0)
    per_b = lambda b, pt: (b, 0, 0, 0)
    fixed2 = lambda b, pt: (0, 0)
    fixed3 = lambda b, pt: (0, 0, 0)

    def page_spec(j, p):
        return pl.BlockSpec((1, N_FULL_KV, LANES, PAGE_SIZE), lambda b, pt: (pt[b * nb + j, p], 0, 0, 0))

    n_slots = nb * n_pages * (PAGE_SIZE // CMP_STRIDE)
    return pl.pallas_call(
        functools.partial(_nsa_sample_kernel, nb=nb, n_pages=n_pages, past_len=past_len, n_blocks=n_blocks),
        grid_spec=pltpu.PrefetchScalarGridSpec(
            num_scalar_prefetch=1,
            grid=(batch // nb,),
            in_specs=[
                pl.BlockSpec((nb * tq, B_WIDTH), tok),
                pl.BlockSpec((nb * tq, LANES), tok),
                pl.BlockSpec((nb * tq, B_WIDTH), tok),
                pl.BlockSpec((nb * tq, KV_ROW), tok),
                pl.BlockSpec((nb * tq, WIN_ROW), tok),
                pl.BlockSpec((nb, N_WIN_KV, LANES, w_buf), per_b),
            ] + [page_spec(j, p) for j in range(nb) for p in range(n_pages)] + [
                pl.BlockSpec(pe.shape, fixed2),
                pl.BlockSpec(w_bd.shape, fixed3),
                pl.BlockSpec(overlap.shape, fixed2),
                pl.BlockSpec(expand.shape, fixed2),
            ],
            out_specs=[
                pl.BlockSpec((nb * tq, B_WIDTH), tok),
                pl.BlockSpec((nb, N_WIN_KV, LANES, w_buf), per_b),
            ],
            scratch_shapes=[pltpu.VMEM((2, n_slots * CMP_SLOT, LANES), F32)],
        ),
        out_shape=[
            jax.ShapeDtypeStruct((batch * tq, B_WIDTH), F32),
            jax.ShapeDtypeStruct((batch, N_WIN_KV, LANES, w_buf), F32),
        ],
        compiler_params=_cparams("parallel"),
        name="nsa_sample",
    )(page_table, qn, gates, zb, full_new, win_new, win_t, *([cache_t] * (nb * n_pages)), pe, w_bd,
      overlap, expand)


def _selection_tables(n_cmp_rows, n_sel_keys):
    n = jnp.arange(n_cmp_rows, dtype=jnp.int32)[:, None]
    s = jnp.arange(LANES, dtype=jnp.int32)[None, :]
    c_start = n * CMP_STRIDE
    s_start = s * L_SEL
    overlap = ((c_start < s_start + L_SEL) & (s_start < c_start + L_CMP)).astype(BF16)
    key = jnp.arange(n_sel_keys, dtype=jnp.int32)[None, :]
    expand = ((key // L_SEL) == jnp.arange(LANES, dtype=jnp.int32)[:, None]).astype(BF16)
    return overlap, expand


def _even_out_kernel(x_ref, oa_ref, ob_ref, wa_ref, wb_ref, *rest, with_next):
    y = (x_ref[...] + _dot(oa_ref[...].astype(BF16), wa_ref[...])
         + _dot(ob_ref[...].astype(BF16), wb_ref[...]))
    if not with_next:
        (y_ref,) = rest
        y_ref[...] = y
        return
    g_ref, w_ref, y_ref, u_ref, z_ref = rest
    y_ref[...] = y
    hb = _rmsnorm_rows(y, g_ref[...]).astype(BF16)
    e = u_ref.shape[1]
    u_ref[...] = _dot(hb, w_ref[:, :e])
    z_ref[...] = jax.nn.silu(_dot(hb, w_ref[:, e:]))


def _even_out(x2d, oa, ob, wa, wb, next_odd=None):
    m, d = x2d.shape
    row = lambda i: (i, 0)
    fixed = lambda i: (0, 0)
    in_specs = [
        pl.BlockSpec((ROW_TILE, d), row),
        pl.BlockSpec((ROW_TILE, A_WIDTH), row),
        pl.BlockSpec((ROW_TILE, B_WIDTH), row),
        pl.BlockSpec((A_WIDTH, d), fixed),
        pl.BlockSpec((B_WIDTH, d), fixed),
    ]
    out_specs = [pl.BlockSpec((ROW_TILE, d), row)]
    out_shape = [jax.ShapeDtypeStruct((m, d), F32)]
    args = [x2d, oa, ob, wa, wb]
    if next_odd is not None:
        gain, w_bf = next_odd
        e = w_bf.shape[1] // 2
        in_specs += [pl.BlockSpec((1, d), fixed), pl.BlockSpec((d, 2 * e), fixed)]
        out_specs += [pl.BlockSpec((ROW_TILE, e), row)] * 2
        out_shape += [jax.ShapeDtypeStruct((m, e), F32)] * 2
        args += [gain, w_bf]
    return pl.pallas_call(
        functools.partial(_even_out_kernel, with_next=next_odd is not None),
        grid=(m // ROW_TILE,),
        in_specs=in_specs,
        out_specs=out_specs,
        out_shape=out_shape,
        compiler_params=_cparams("parallel"),
        name="even_out_proj",
    )(*args)


def _s5_kernel(u_ref, x0r_ref, x0i_ref, ar_ref, ai_ref, bm_ref, cm_ref, d_ref,
               y_ref, fr_ref, fi_ref, st_re, st_im, ubuf_a, ubuf_b, xbuf_a, xbuf_b, ybuf, *, nb, tt):
    t_idx = pl.program_id(1)
    n_blk = bm_ref.shape[0]
    half = S5_BLOCK_STATE
    n_chunk = half // LANES
    pitch = tt + S5_ROW_PAD
    ubufs, xbufs = (ubuf_a, ubuf_b), (xbuf_a, xbuf_b)
    wide = S5_MXU_COLS // LANES
    n_slices = 2 * n_chunk // wide

    @pl.when(t_idx == 0)
    def _():
        for kb in range(n_blk):
            st_re[kb] = x0r_ref[:, kb * half:(kb + 1) * half]
            st_im[kb] = x0i_ref[:, kb * half:(kb + 1) * half]

    for ubuf in ubufs:
        for b in range(nb):
            ubuf[b * pitch + tt:(b + 1) * pitch, :] = jnp.zeros((S5_ROW_PAD, LANES), F32)

    def lanes_of(kb):
        return slice(kb * LANES, (kb + 1) * LANES)

    def stage_u(kb):
        for b in range(nb):
            ubufs[kb % 2][b * pitch:b * pitch + tt, :] = u_ref[b, :, lanes_of(kb)]

    def input_slice(kb, j):
        cols = slice(j * S5_MXU_COLS, (j + 1) * S5_MXU_COLS)
        bu = _dot(ubufs[kb % 2][...].astype(BF16), bm_ref[kb, :, cols])
        for i in range(wide):
            xbufs[kb % 2][j * wide + i] = bu[:, i * LANES:(i + 1) * LANES]

    def output_slice(kb, j, acc):
        x = jnp.concatenate([xbufs[kb % 2][j * wide + i] for i in range(wide)], axis=1).astype(BF16)
        part = _dot(x, cm_ref[kb, j * S5_MXU_COLS:(j + 1) * S5_MXU_COLS, :])
        return part if acc is None else acc + part

    def finish_y(kb, acc):
        ybuf[...] = acc + d_ref[:, lanes_of(kb)] * ubufs[kb % 2][...]
        for b in range(nb):
            y_ref[b, :, lanes_of(kb)] = ybuf[b * pitch:b * pitch + tt, :]

    stage_u(0)
    for j in range(n_slices):
        input_slice(0, j)
    for kb in range(n_blk + 1):
        side, acc = [], [None]
        if kb >= 1:
            def out_task(j, kb=kb):
                acc[0] = output_slice(kb - 1, j, acc[0])
            side += [functools.partial(out_task, j) for j in range(n_slices)]
            side.append(lambda kb=kb: finish_y(kb - 1, acc[0]))
        if kb + 1 < n_blk:
            side.append(functools.partial(stage_u, kb + 1))
            side += [functools.partial(input_slice, kb + 1, j) for j in range(n_slices)]
        if kb == n_blk:
            for task in side:
                task()
            break
        xbuf = xbufs[kb % 2]
        a_re = jnp.broadcast_to(ar_ref[kb], (nb, half))
        a_im = jnp.broadcast_to(ai_ref[kb], (nb, half))
        s_re, s_im = st_re[kb], st_im[kb]
        every = -(-tt // (len(side) + 1))
        for t in range(tt):
            if t % every == every - 1 and side:
                side.pop(0)()
            rows = pl.ds(t, nb, stride=pitch)
            b_re = jnp.concatenate([xbuf[j, rows, :] for j in range(n_chunk)], axis=1)
            b_im = jnp.concatenate([xbuf[n_chunk + j, rows, :] for j in range(n_chunk)], axis=1)
            s_re, s_im = a_re * s_re - a_im * s_im + b_re, a_re * s_im + a_im * s_re + b_im
            for j in range(n_chunk):
                xbuf[j, rows, :] = s_re[:, j * LANES:(j + 1) * LANES]
                xbuf[n_chunk + j, rows, :] = s_im[:, j * LANES:(j + 1) * LANES]
        for task in side:
            task()
        st_re[kb] = s_re
        st_im[kb] = s_im

    @pl.when(t_idx == pl.num_programs(1) - 1)
    def _():
        for kb in range(n_blk):
            fr_ref[:, kb * half:(kb + 1) * half] = st_re[kb]
            fi_ref[:, kb * half:(kb + 1) * half] = st_im[kb]


def _s5(u3, x0_re, x0_im, a_re, a_im, bmat, cmat, d_row, tt):
    batch, seq, e = u3.shape
    nb = SUBLANES
    n_blk = e // LANES
    n_state = n_blk * S5_BLOCK_STATE
    seq_map = lambda b, t: (b, t, 0)
    st_map = lambda b, t: (b, 0)
    fixed2 = lambda b, t: (0, 0)
    fixed3 = lambda b, t: (0, 0, 0)
    return pl.pallas_call(
        functools.partial(_s5_kernel, nb=nb, tt=tt),
        grid=(batch // nb, seq // tt),
        in_specs=[
            pl.BlockSpec((nb, tt, e), seq_map),
            pl.BlockSpec((nb, n_state), st_map),
            pl.BlockSpec((nb, n_state), st_map),
            pl.BlockSpec((n_blk, 1, S5_BLOCK_STATE), fixed3),
            pl.BlockSpec((n_blk, 1, S5_BLOCK_STATE), fixed3),
            pl.BlockSpec((n_blk, LANES, 2 * S5_BLOCK_STATE), fixed3),
            pl.BlockSpec((n_blk, 2 * S5_BLOCK_STATE, LANES), fixed3),
            pl.BlockSpec((1, e), fixed2),
        ],
        out_specs=[
            pl.BlockSpec((nb, tt, e), seq_map),
            pl.BlockSpec((nb, n_state), st_map),
            pl.BlockSpec((nb, n_state), st_map),
        ],
        out_shape=[
            jax.ShapeDtypeStruct((batch, seq, e), F32),
            jax.ShapeDtypeStruct((batch, n_state), F32),
            jax.ShapeDtypeStruct((batch, n_state), F32),
        ],
        scratch_shapes=[
            pltpu.VMEM((n_blk, nb, S5_BLOCK_STATE), F32),
            pltpu.VMEM((n_blk, nb, S5_BLOCK_STATE), F32),
            pltpu.VMEM((nb * (tt + S5_ROW_PAD), LANES), F32),
            pltpu.VMEM((nb * (tt + S5_ROW_PAD), LANES), F32),
            pltpu.VMEM((2 * S5_BLOCK_STATE // LANES, nb * (tt + S5_ROW_PAD), LANES), F32),
            pltpu.VMEM((2 * S5_BLOCK_STATE // LANES, nb * (tt + S5_ROW_PAD), LANES), F32),
            pltpu.VMEM((nb * (tt + S5_ROW_PAD), LANES), F32),
        ],
        compiler_params=_cparams("parallel", "arbitrary"),
        name="s5_scan",
    )(u3, x0_re, x0_im, a_re, a_im, bmat, cmat, d_row)


def _s5_params(lam_re, lam_im, b_re, b_im, c_re, c_im, log_step):
    n_groups = lam_re.shape[0]
    n_blk = n_groups // S5_LANE_GROUPS
    dt = jnp.exp(log_step)[:, None]
    mag = jnp.exp(lam_re * dt)
    ang = lam_im * dt
    ab_re, ab_im = mag * jnp.cos(ang), mag * jnp.sin(ang)
    den = lam_re * lam_re + lam_im * lam_im
    nr = ab_re - 1.0
    f_re = (nr * lam_re + ab_im * lam_im) / den
    f_im = (ab_im * lam_re - nr * lam_im) / den
    bb_re = f_re[..., None] * b_re - f_im[..., None] * b_im
    bb_im = f_re[..., None] * b_im + f_im[..., None] * b_re
    eye = jnp.eye(S5_LANE_GROUPS, dtype=lam_re.dtype)

    def in_map(bb):
        bb = bb.reshape(n_blk, S5_LANE_GROUPS, S5_STATE, S5_GROUP)
        m = jnp.einsum('kgpc,gh->kgchp', bb, eye)
        return m.reshape(n_blk, LANES, S5_BLOCK_STATE)

    def out_map(cc):
        cc = cc.reshape(n_blk, S5_LANE_GROUPS, S5_GROUP, S5_STATE)
        m = jnp.einsum('kgcp,gh->kgphc', cc, eye)
        return m.reshape(n_blk, S5_BLOCK_STATE, LANES)

    bmat = jnp.concatenate([in_map(bb_re), in_map(bb_im)], axis=2).astype(BF16)
    cmat = jnp.concatenate([out_map(c_re), out_map(-c_im)], axis=1).astype(BF16)
    a_re = ab_re.reshape(n_blk, 1, S5_BLOCK_STATE)
    a_im = ab_im.reshape(n_blk, 1, S5_BLOCK_STATE)
    return a_re, a_im, bmat, cmat


def _odd_out_kernel(x_ref, y_ref, z_ref, w1_ref, w2_ref, wo_ref, o_ref):
    yb = jax.nn.gelu(y_ref[...]).astype(BF16)
    t = _dot(yb, w1_ref[...]) * jax.nn.sigmoid(_dot(yb, w2_ref[...])) * z_ref[...]
    o_ref[...] = x_ref[...] + _dot(t.astype(BF16), wo_ref[...])


def _odd_out(x2d, y2d, z2d, w1, w2, wo):
    m, d = x2d.shape
    e = y2d.shape[1]
    row = lambda i: (i, 0)
    fixed = lambda i: (0, 0)
    return pl.pallas_call(
        _odd_out_kernel,
        grid=(m // ROW_TILE,),
        in_specs=[
            pl.BlockSpec((ROW_TILE, d), row),
            pl.BlockSpec((ROW_TILE, e), row),
            pl.BlockSpec((ROW_TILE, e), row),
            pl.BlockSpec((e, e), fixed),
            pl.BlockSpec((e, e), fixed),
            pl.BlockSpec((e, d), fixed),
        ],
        out_specs=pl.BlockSpec((ROW_TILE, d), row),
        out_shape=jax.ShapeDtypeStruct((m, d), F32),
        compiler_params=_cparams("parallel"),
        name="odd_out_proj",
    )(x2d, y2d, z2d, w1, w2, wo)


def _rope_tables(pos):
    half = NSA_DH // 2
    inv = ROPE_THETA ** (-jnp.arange(half, dtype=F32) / half)
    ang = pos.astype(F32)[:, None] * inv[None, :]
    cos, sin = jnp.cos(ang), jnp.sin(ang)
    reps = LANES // NSA_DH
    return jnp.tile(cos, (1, 2 * reps)), jnp.tile(jnp.concatenate([-sin, sin], axis=1), (1, reps))


def _even_weights(norm_g, w_in, w_out, q_norm, k_norm, cmp_pos, cmp_w):
    assert w_in.shape[1] == ZB_COL0 + B_WIDTH
    w_bf = w_in.astype(BF16)
    w_zb = w_bf[:, ZB_COL0:]
    reps = LANES // NSA_DH
    qg = jnp.tile(q_norm[None, :], (1, reps))
    kg = jnp.tile(k_norm, (1, reps))
    cw = cmp_w.astype(BF16)
    row_blocks = []
    for c in range(2):
        for g in range(NSA_KV_HEADS):
            r0 = (c * NSA_KV_HEADS + g) * NSA_DH
            row_blocks.append(jnp.pad(cw[c], ((0, 0), (0, 0), (r0, 2 * LANES - NSA_DH - r0))))
    w_bd = jnp.concatenate(row_blocks, axis=1)
    pe = jnp.broadcast_to(cmp_pos.transpose(1, 0, 2)[:, :, None, :], (L_CMP, 2, NSA_KV_HEADS, NSA_DH))
    pe = pe.reshape(L_CMP, 2 * LANES)
    wa = w_out[:A_WIDTH].astype(BF16)
    wb = w_out[A_WIDTH:].astype(BF16)
    return norm_g[None, :], w_bf, w_zb, qg, kg, w_bd, pe, wa, wb


def _even_layer(x, pos0, s_ret, ew, gn_gain, next_odd, cache=None, page_table=None, win_past=None):
    batch, seq, d = x.shape
    gain, w_bf, w_zb, qg, kg, w_bd, pe, wa, wb = ew
    x2d = x.reshape(batch * seq, d)
    pos = pos0 + jnp.arange(seq, dtype=jnp.int32)
    cos_t, sin_t = _rope_tables(pos)
    if seq >= EVEN_IN_TILE:
        n_table_blocks = seq // EVEN_IN_TILE
    else:
        cos_t = jnp.tile(cos_t, (EVEN_IN_TILE // seq, 1))
        sin_t = jnp.tile(sin_t, (EVEN_IN_TILE // seq, 1))
        n_table_blocks = 1
    g, dh = NSA_KV_HEADS, NSA_DH
    if cache is None:
        qa, ka, va, za, qn, full_t, win_t, zb, gates, cmp_rows = _even_in(
            x2d, gain, w_bf, w_zb, cos_t, sin_t, qg, kg, n_table_blocks, seq_tiles=seq // EVEN_IN_TILE)
        oa, s_fin = _retention(qa, ka, va, za, gn_gain[None, :], s_ret, batch, seq)
        cmp_kv = _compress(cmp_rows.reshape(batch, seq, 2 * LANES), pe, w_bd)
        overlap, expand = _selection_tables(seq // CMP_STRIDE, seq)
        ob = _nsa_prompt(qn, gates, zb, cmp_kv, full_t, win_t, overlap, expand, batch, seq)
        keep = min(WINDOW, seq)
        y, *uz = _even_out(x2d, oa, ob, wa, wb, next_odd)
        full_rows = full_t.reshape(batch, N_FULL_KV, g, dh, seq).transpose(0, 4, 1, 2, 3)
        win_rows = win_t[:, :, seq - keep:].reshape(batch, N_WIN_KV, g, dh, keep).transpose(0, 4, 1, 2, 3)
        return y.reshape(batch, seq, d), s_fin, full_rows, win_rows, uz
    else:
        qa, ka, va, za, qn, full_new, win_new, zb, gates = _even_in(x2d, gain, w_bf, w_zb, cos_t, sin_t, qg, kg,
                                                                     n_table_blocks)
        oa, s_fin = _retention(qa, ka, va, za, gn_gain[None, :], s_ret, batch, seq)
        n_pages = page_table.shape[1]
        past_len = n_pages * PAGE_SIZE
        overlap, expand = _selection_tables(past_len // CMP_STRIDE, past_len + KEY_TILE)
        ob, win_out_t = _nsa_sample(page_table, qn, gates, zb, full_new, win_new, win_past, cache, pe, w_bd,
                                    overlap, expand, batch, seq, n_pages)
    y, *uz = _even_out(x2d, oa, ob, wa, wb, next_odd)
    w_buf = win_out_t.shape[3]
    return (y.reshape(batch, seq, d), s_fin, full_new.reshape(batch, seq, N_FULL_KV, g, dh),
            win_out_t.reshape(batch, N_WIN_KV, g, dh, w_buf).transpose(0, 4, 1, 2, 3), uz)


def _odd_layer(x, uz, s_re, s_im, s5p, d_row, w1, w2, wo, tt):
    batch, seq, d = x.shape
    x2d = x.reshape(batch * seq, d)
    u, zs = uz
    e = u.shape[1]
    a_re, a_im, bmat, cmat = s5p
    n_groups, n_state = s_re.shape[1], s_re.shape[2]
    y, f_re, f_im = _s5(u.reshape(batch, seq, e), s_re.reshape(batch, n_groups * n_state),
                        s_im.reshape(batch, n_groups * n_state), a_re, a_im, bmat, cmat, d_row, tt)
    out = _odd_out(x2d, y.reshape(batch * seq, e), zs, w1, w2, wo).reshape(batch, seq, d)
    return out, f_re.reshape(batch, n_groups, n_state), f_im.reshape(batch, n_groups, n_state)


def kernel(x_prompt, x_sample, cache_nsa_kv, cache_nsa_win, state_ret, state_ssm_re, state_ssm_im, page_table,
           norm_even, w_in_even, w_out_even, ret_gn_gain, nsa_q_norm, nsa_k_norm, nsa_cmp_pos, nsa_cmp_w,
           norm_odd, w_in_odd, ssm_lambda_re, ssm_lambda_im, ssm_b_re, ssm_b_im, ssm_c_re, ssm_c_im, ssm_d,
           ssm_log_step, glu_w1, glu_w2, w_out_odd):
    bp, seq_p, _ = x_prompt.shape
    db, seq_s, _ = x_sample.shape
    n_pages = page_table.shape[1]
    past_len = n_pages * PAGE_SIZE
    depth = norm_even.shape[0] + norm_odd.shape[0]
    yp, ys = x_prompt, x_sample
    ret_p, ret_s, kv_p, kv_s, win_p, win_s = [], [], [], [], [], []
    sre_p, sim_p, sre_s, sim_s = [], [], [], []
    for layer in range(depth):
        li = layer // 2
        if layer % 2 == 0:
            ew = _even_weights(norm_even[li], w_in_even[li], w_out_even[li], nsa_q_norm[li], nsa_k_norm[li],
                               nsa_cmp_pos[li], nsa_cmp_w[li])
            next_odd = None
            if layer + 1 < depth:
                next_odd = (norm_odd[li][None, :], w_in_odd[li].astype(BF16))
            s0 = jnp.zeros((bp, RET_HEADS, RET_DK, RET_DV), F32)
            yp, sr, kvr, wr, uz_p = _even_layer(yp, 0, s0, ew, ret_gn_gain[li], next_odd)
            ret_p.append(sr); kv_p.append(kvr); win_p.append(wr)
            cache_t = cache_nsa_kv[li].transpose(0, 2, 3, 4, 1).reshape(
                cache_nsa_kv.shape[1], N_FULL_KV, LANES, PAGE_SIZE)
            win_t = cache_nsa_win[li].transpose(0, 2, 3, 4, 1).reshape(
                db, N_WIN_KV, LANES, cache_nsa_win.shape[2])
            ys, sr2, kvr2, wr2, uz_s = _even_layer(ys, past_len, state_ret[li], ew, ret_gn_gain[li], next_odd,
                                                   cache=cache_t, page_table=page_table, win_past=win_t)
            ret_s.append(sr2); kv_s.append(kvr2); win_s.append(wr2)
        else:
            s5p = _s5_params(ssm_lambda_re[li], ssm_lambda_im[li], ssm_b_re[li], ssm_b_im[li],
                             ssm_c_re[li], ssm_c_im[li], ssm_log_step[li])
            w1, w2, wo = glu_w1[li].astype(BF16), glu_w2[li].astype(BF16), w_out_odd[li].astype(BF16)
            d_row = ssm_d[li][None, :]
            n_groups = ssm_lambda_re.shape[1]
            z0 = jnp.zeros((bp, n_groups, S5_STATE), F32)
            yp, fr, fi = _odd_layer(yp, uz_p, z0, z0, s5p, d_row, w1, w2, wo, min(S5_TIME_TILE, seq_p))
            sre_p.append(fr); sim_p.append(fi)
            ys, fr2, fi2 = _odd_layer(ys, uz_s, state_ssm_re[li], state_ssm_im[li], s5p, d_row,
                                      w1, w2, wo, min(S5_TIME_TILE, seq_s))
            sre_s.append(fr2); sim_s.append(fi2)
    return (yp, ys, jnp.stack(ret_p), jnp.stack(ret_s), jnp.stack(kv_p), jnp.stack(kv_s), jnp.stack(win_p),
            jnp.stack(win_s), jnp.stack(sre_p), jnp.stack(sim_p), jnp.stack(sre_s), jnp.stack(sim_s))
```

```python
import functools
import math

import jax
import jax.numpy as jnp
from jax import lax
from jax.experimental import pallas as pl
from jax.experimental.pallas import tpu as pltpu

F32 = jnp.float32
BF16 = jnp.bfloat16

LANES = 128
SUBLANES = 8
VMEM_LIMIT_BYTES = 48 * 2**20

EPS = 1e-6
ROPE_THETA = 10000.0
NEG_INF = -1e30
FORCE = 1e4

RET_HEADS = 4
RET_DK = 64
RET_DV = 128
RET_CHUNK = 128
RET_TILES_PER_STEP = 8
A_WIDTH = RET_HEADS * RET_DV

NSA_HEADS = 8
NSA_KV_HEADS = 2
NSA_DH = 64
NSA_HPG = NSA_HEADS // NSA_KV_HEADS
B_WIDTH = NSA_HEADS * NSA_DH
L_CMP = 32
CMP_STRIDE = 16
CMP_RATIO = L_CMP // CMP_STRIDE
L_SEL = 64
N_SEL = 8
WINDOW = 512
PAGE_SIZE = 128
N_FULL_KV = 4
N_WIN_KV = 2
K_SEL, V_SEL = 2, 3
KV_ROW = N_FULL_KV * NSA_KV_HEADS * NSA_DH
WIN_ROW = N_WIN_KV * NSA_KV_HEADS * NSA_DH
KEY_TILE = 128
PROMPT_Q_TILE = 256
WIN_KEYS = WINDOW + PROMPT_Q_TILE
SEL_CHUNK = 512
CMP_SLOT = CMP_STRIDE + 4
SAMPLE_REQS_PER_STEP = 2

S5_GROUP = 16
S5_STATE = 64
S5_LANE_GROUPS = LANES // S5_GROUP
S5_BLOCK_STATE = S5_LANE_GROUPS * S5_STATE
S5_ROW_PAD = 4
S5_TIME_TILE = 128
S5_MXU_COLS = 256

QA0, KA0, VA0, ZA0, QN0, KVB0, GL0 = 0, 256, 512, 1024, 1536, 2048, 2816
N_GATES = 3 * NSA_HEADS
ZB_COL0 = GL0 + N_GATES

ROW_TILE = 512
EVEN_IN_TILE = 256


def _cparams(*sem):
    return pltpu.CompilerParams(dimension_semantics=sem, vmem_limit_bytes=VMEM_LIMIT_BYTES)


def _lane_iota(shape):
    return lax.broadcasted_iota(jnp.int32, shape, len(shape) - 1)


def _row_iota(shape):
    return lax.broadcasted_iota(jnp.int32, shape, len(shape) - 2)


def _dot(a, b):
    return jnp.dot(a, b, preferred_element_type=F32)


def _dot_nt(a, b):
    return lax.dot_general(a, b, (((1,), (1,)), ((), ())), preferred_element_type=F32)


def _rmsnorm_rows(x, g):
    return x * lax.rsqrt(jnp.mean(x * x, axis=-1, keepdims=True) + EPS) * g


def _swap_halves(x):
    return pltpu.roll(x, NSA_DH, axis=1)


def _rope_block(x, cos, sin_signed):
    half = NSA_DH // 2
    lane = _lane_iota(x.shape)
    first = (lane % NSA_DH) < half
    partner = jnp.where(first, pltpu.roll(x, LANES - half, axis=1), pltpu.roll(x, half, axis=1))
    return x * cos + partner * sin_signed


def _head_rms_block(x, g):
    lane = _lane_iota(x.shape)
    lo = lane < NSA_DH
    sq = x * x
    s_lo = jnp.sum(jnp.where(lo, sq, 0.0), axis=-1, keepdims=True)
    s_hi = jnp.sum(jnp.where(lo, 0.0, sq), axis=-1, keepdims=True)
    ms = jnp.where(lo, s_lo, s_hi) * (1.0 / NSA_DH)
    return x * lax.rsqrt(ms + EPS) * g


def _even_in_kernel(x_ref, g_ref, w_ref, wz_ref, cos_ref, sin_ref, qg_ref, kg_ref, *out_refs, transposed_kv):
    if transposed_kv:
        qa_ref, ka_ref, va_ref, za_ref, qn_ref, full_ref, win_ref, zb_ref, gl_ref, cmp_ref = out_refs
    else:
        qa_ref, ka_ref, va_ref, za_ref, qn_ref, full_ref, win_ref, zb_ref, gl_ref = out_refs
    hb = _rmsnorm_rows(x_ref[...], g_ref[...]).astype(BF16)
    cos = cos_ref[...]
    sin = sin_ref[...]

    def proj(c0):
        return _dot(hb, w_ref[:, c0:c0 + LANES])

    for j in range(RET_HEADS * RET_DK // LANES):
        c = j * LANES
        qa_ref[:, c:c + LANES] = _rope_block(proj(QA0 + c), cos, sin)
        ka_ref[:, c:c + LANES] = _rope_block(proj(KA0 + c), cos, sin) * (RET_DK ** -0.5)
    for j in range(A_WIDTH // LANES):
        c = j * LANES
        va_ref[:, c:c + LANES] = proj(VA0 + c)
        za_ref[:, c:c + LANES] = jax.nn.silu(proj(ZA0 + c))
    for j in range(B_WIDTH // LANES):
        c = j * LANES
        qn_ref[:, c:c + LANES] = _rope_block(_head_rms_block(proj(QN0 + c), qg_ref[...]), cos, sin)
        zb_ref[:, c:c + LANES] = jax.nn.silu(_dot(hb, wz_ref[:, c:c + LANES]))
    for j in range(6):
        y = proj(KVB0 + j * LANES)
        if j % 2 == 0:
            y = _rope_block(_head_rms_block(y, kg_ref[j // 2:j // 2 + 1, :]), cos, sin)
        if transposed_kv:
            if j < 4:
                full_ref[0, j * LANES:(j + 1) * LANES, :] = y.T
            else:
                win_ref[0, (j - 4) * LANES:(j - 3) * LANES, :] = y.T
            if j < 2:
                cmp_ref[:, j * LANES:(j + 1) * LANES] = y
        elif j < 4:
            full_ref[:, j * LANES:(j + 1) * LANES] = y
        else:
            win_ref[:, (j - 4) * LANES:(j - 3) * LANES] = y
    gl_ref[...] = jax.nn.sigmoid(proj(GL0))


def _even_in(x2d, gain, w_bf, w_zb, cos_t, sin_t, qg, kg, n_table_blocks, seq_tiles=None):
    m, d = x2d.shape
    tile = EVEN_IN_TILE
    grid = (m // tile,)
    row = lambda i: (i, 0)
    fixed = lambda i: (0, 0)
    table = lambda i: (i % n_table_blocks, 0)
    widths = (RET_HEADS * RET_DK, RET_HEADS * RET_DK, A_WIDTH, A_WIDTH, B_WIDTH, KV_ROW, WIN_ROW, B_WIDTH, LANES)
    out_specs = [pl.BlockSpec((tile, w), row) for w in widths]
    out_shape = [jax.ShapeDtypeStruct((m, w), F32) for w in widths]
    if seq_tiles is not None:
        batch, seq = grid[0] // seq_tiles, seq_tiles * tile
        fmajor = lambda i: (i // seq_tiles, 0, i % seq_tiles)
        for idx, w in ((5, KV_ROW), (6, WIN_ROW)):
            out_specs[idx] = pl.BlockSpec((1, w, tile), fmajor)
            out_shape[idx] = jax.ShapeDtypeStruct((batch, w, seq), F32)
        out_specs.append(pl.BlockSpec((tile, 2 * LANES), row))
        out_shape.append(jax.ShapeDtypeStruct((m, 2 * LANES), F32))
    return pl.pallas_call(
        functools.partial(_even_in_kernel, transposed_kv=seq_tiles is not None),
        grid=grid,
        in_specs=[
            pl.BlockSpec((tile, d), row),
            pl.BlockSpec((1, d), fixed),
            pl.BlockSpec(w_bf.shape, fixed),
            pl.BlockSpec(w_zb.shape, fixed),
            pl.BlockSpec((tile, LANES), table),
            pl.BlockSpec((tile, LANES), table),
            pl.BlockSpec((1, LANES), fixed),
            pl.BlockSpec((3, LANES), fixed),
        ],
        out_specs=out_specs,
        out_shape=out_shape,
        compiler_params=_cparams("parallel"),
        name="even_in_proj",
    )(x2d, gain, w_bf, w_zb, cos_t, sin_t, qg, kg)


def _retention_kernel(q_ref, k_ref, v_ref, z_ref, gn_ref, s0_ref, dmat_ref, qdec_ref, kdec_ref, cdec_ref,
                      o_ref, sfin_ref, s_scr, *, rows, n_tiles):
    n_seq = RET_CHUNK // rows
    c = pl.program_id(1)

    @pl.when(c == 0)
    def _():
        s_scr[...] = s0_ref[...]

    lane = _lane_iota((RET_CHUNK, LANES))
    col_seq = _lane_iota((RET_DK, RET_CHUNK)) >> int(math.log2(rows)) if n_seq > 1 else None
    state = [[s_scr[r, h] for h in range(RET_HEADS)] for r in range(n_seq)]
    for t in range(n_tiles):
        tile = slice(t * RET_CHUNK, (t + 1) * RET_CHUNK)
        q, k, v = q_ref[tile, :], k_ref[tile, :], v_ref[tile, :]
        qd = q * qdec_ref[...]
        kd = k * kdec_ref[...]
        for pair in range(RET_HEADS // 2):
            cols = slice(pair * LANES, (pair + 1) * LANES)
            q2, k2b, qd2 = q[:, cols], k[:, cols].astype(BF16), qd[:, cols]
            kd2_t = kd[:, cols].T
            s_pair = [jnp.concatenate([state[r][2 * pair], state[r][2 * pair + 1]], axis=0).astype(BF16)
                      for r in range(n_seq)]
            for sub in range(2):
                h = 2 * pair + sub
                mine = (lane >= sub * RET_DK) & (lane < (sub + 1) * RET_DK)
                qm = jnp.where(mine, q2, 0.0).astype(BF16)
                qdm = jnp.where(mine, qd2, 0.0).astype(BF16)
                vhb = v[:, h * RET_DV:(h + 1) * RET_DV].astype(BF16)
                intra = _dot((_dot_nt(qm, k2b) * dmat_ref[h]).astype(BF16), vhb)
                cross = jnp.concatenate([_dot(qdm[r * rows:(r + 1) * rows], s_pair[r]) for r in range(n_seq)],
                                        axis=0)
                out = intra + cross
                kt = kd2_t[sub * RET_DK:(sub + 1) * RET_DK, :]
                if n_seq > 1:
                    kt = jnp.concatenate([jnp.where(col_seq == r, kt, 0.0) for r in range(n_seq)], axis=0)
                kv = _dot(kt.astype(BF16), vhb)
                for r in range(n_seq):
                    state[r][h] = state[r][h] * cdec_ref[h] + kv[r * RET_DK:(r + 1) * RET_DK]
                mu = jnp.mean(out, axis=-1, keepdims=True)
                cen = out - mu
                var = jnp.mean(cen * cen, axis=-1, keepdims=True)
                y = cen * lax.rsqrt(var + EPS) * gn_ref[:, h * RET_DV:(h + 1) * RET_DV]
                o_ref[tile, h * RET_DV:(h + 1) * RET_DV] = y * z_ref[tile, h * RET_DV:(h + 1) * RET_DV]
    for r in range(n_seq):
        for h in range(RET_HEADS):
            s_scr[r, h] = state[r][h]

    @pl.when(c == pl.num_programs(1) - 1)
    def _():
        sfin_ref[...] = s_scr[...]


def _retention_tables(rows):
    log_g = jnp.log(1.0 - 2.0 ** (-5.0 - jnp.arange(RET_HEADS, dtype=F32)))
    idx = jnp.arange(RET_CHUNK, dtype=jnp.int32)
    pos = (idx % rows).astype(F32)
    seq = idx // rows
    diff = pos[:, None] - pos[None, :]
    causal = (diff >= 0) & (seq[:, None] == seq[None, :])
    dmat = jnp.exp(jnp.where(causal, diff, 0.0)[None] * log_g[:, None, None]) * causal[None]
    qdec = jnp.exp((pos + 1.0)[:, None] * log_g[None, :])
    kdec = jnp.exp((rows - 1.0 - pos)[:, None] * log_g[None, :])
    qdec = jnp.repeat(qdec, RET_DK, axis=1)
    kdec = jnp.repeat(kdec, RET_DK, axis=1)
    cdec = jnp.broadcast_to(jnp.exp(rows * log_g)[:, None, None], (RET_HEADS, RET_DK, RET_DV))
    return dmat, qdec, kdec, cdec


def _retention(qa, ka, va, za, gn_gain, s0, batch, seq):
    rows = min(RET_CHUNK, seq)
    n_seq = RET_CHUNK // rows
    n_chunks = seq // rows
    n_tiles = min(RET_TILES_PER_STEP, n_chunks)
    steps = n_chunks // n_tiles
    dmat, qdec, kdec, cdec = _retention_tables(rows)
    tok = lambda b, c: (b * steps + c, 0)
    fixed2 = lambda b, c: (0, 0)
    fixed3 = lambda b, c: (0, 0, 0)
    state = lambda b, c: (b, 0, 0, 0)
    qk_w = RET_HEADS * RET_DK
    step_rows = n_tiles * RET_CHUNK
    return pl.pallas_call(
        functools.partial(_retention_kernel, rows=rows, n_tiles=n_tiles),
        grid=(batch // n_seq, steps),
        in_specs=[
            pl.BlockSpec((step_rows, qk_w), tok),
            pl.BlockSpec((step_rows, qk_w), tok),
            pl.BlockSpec((step_rows, A_WIDTH), tok),
            pl.BlockSpec((step_rows, A_WIDTH), tok),
            pl.BlockSpec((1, A_WIDTH), fixed2),
            pl.BlockSpec((n_seq, RET_HEADS, RET_DK, RET_DV), state),
            pl.BlockSpec((RET_HEADS, RET_CHUNK, RET_CHUNK), fixed3),
            pl.BlockSpec((RET_CHUNK, qk_w), fixed2),
            pl.BlockSpec((RET_CHUNK, qk_w), fixed2),
            pl.BlockSpec((RET_HEADS, RET_DK, RET_DV), fixed3),
        ],
        out_specs=[
            pl.BlockSpec((step_rows, A_WIDTH), tok),
            pl.BlockSpec((n_seq, RET_HEADS, RET_DK, RET_DV), state),
        ],
        out_shape=[
            jax.ShapeDtypeStruct((batch * seq, A_WIDTH), F32),
            jax.ShapeDtypeStruct((batch, RET_HEADS, RET_DK, RET_DV), F32),
        ],
        scratch_shapes=[pltpu.VMEM((n_seq, RET_HEADS, RET_DK, RET_DV), F32)],
        compiler_params=_cparams("parallel", "arbitrary"),
        name="retention",
    )(qa, ka, va, za, gn_gain, s0, dmat, qdec, kdec, cdec)


def _compress_kernel(k_ref, v_ref, pe_ref, w_ref, o_ref):
    n_half = k_ref.shape[1] // CMP_STRIDE
    acc = [jnp.zeros((n_half, 2 * LANES), F32) for _ in range(CMP_RATIO)]
    for l in range(CMP_STRIDE):
        rows = pl.ds(l, n_half, stride=CMP_STRIDE)
        x = jnp.concatenate([k_ref[0, rows, :], v_ref[0, rows, :]], axis=1)
        for r in range(CMP_RATIO):
            i = r * CMP_STRIDE + l
            acc[r] = acc[r] + _dot((x + pe_ref[i:i + 1, :]).astype(BF16), w_ref[i])
    out = acc[0]
    for r in range(1, CMP_RATIO):
        out = out + pltpu.roll(acc[r], n_half - r, axis=0)
    live = _row_iota(out.shape) < n_half - CMP_RATIO + 1
    o_ref[0] = jnp.where(live, out, 0.0)


def _compress(cmp_rows, pe, w_bd):
    batch, seq, _ = cmp_rows.shape
    n_half = seq // CMP_STRIDE
    return pl.pallas_call(
        _compress_kernel,
        grid=(batch,),
        in_specs=[
            pl.BlockSpec((1, seq, LANES), lambda b: (b, 0, 0)),
            pl.BlockSpec((1, seq, LANES), lambda b: (b, 0, 1)),
            pl.BlockSpec((L_CMP, 2 * LANES), lambda b: (0, 0)),
            pl.BlockSpec((L_CMP, 2 * LANES, 2 * LANES), lambda b: (0, 0, 0)),
        ],
        out_specs=pl.BlockSpec((1, n_half, 2 * LANES), lambda b: (b, 0, 0)),
        out_shape=jax.ShapeDtypeStruct((batch, n_half, 2 * LANES), F32),
        compiler_params=_cparams("parallel"),
        name="kv_compress",
    )(cmp_rows, cmp_rows, pe, w_bd)


def _split3_bf16(x):
    hi = x.astype(BF16)
    r1 = x - hi.astype(F32)
    mid = r1.astype(BF16)
    lo = (r1 - mid.astype(F32)).astype(BF16)
    return hi, mid, lo


def _stack_group_queries(q, g):
    lane = _lane_iota((q.shape[0], LANES))
    in_g = (lane >= g * NSA_DH) & (lane < (g + 1) * NSA_DH)
    parts = []
    for hh in range(NSA_HPG):
        h = g * NSA_HPG + hh
        two = q[:, (h // 2) * LANES:(h // 2 + 1) * LANES]
        if h % 2 != g:
            two = _swap_halves(two)
        parts.append(jnp.where(in_g, two, 0.0))
    return (jnp.concatenate(parts, axis=0) * (NSA_DH ** -0.5)).astype(BF16)


def _masked_softmax(s, ok):
    s = jnp.where(ok, s, NEG_INF)
    e = jnp.exp(s - jnp.max(s, axis=-1, keepdims=True))
    return jnp.where(ok, e * (1.0 / jnp.sum(e, axis=-1, keepdims=True)), 0.0)


def _sum_heads(p1, tq):
    psum = p1[0:tq]
    for hh in range(1, NSA_HPG):
        psum = psum + p1[hh * tq:(hh + 1) * tq]
    return psum


def _choose_blocks(psum, t0, tq, overlap, n_blocks):
    rows = psum.shape[0]
    imp = sum(_dot(t, overlap) for t in _split3_bf16(psum))
    shift = int(math.log2(L_SEL))

    def ranked(score, blk, valid, take):
        rank = jnp.zeros(score.shape, F32)
        for s in range(n_blocks):
            cand = take(s)
            rank = rank + ((cand > score) | ((cand == score) & (blk > s))).astype(F32)
        return ((rank < N_SEL) & valid).astype(F32)

    if rows % LANES:
        tpos = jnp.concatenate([t0 + _row_iota((tq, 1))] * (rows // tq), axis=0)
        blk = _lane_iota((rows, LANES))
        cur = tpos >> shift
        forced = (blk == 0) | (blk == cur) | (blk == cur - 1)
        valid = (blk * L_SEL <= tpos) & (blk < n_blocks)
        score = jnp.where(valid, jnp.where(forced, FORCE, imp), -FORCE)
        return ranked(score, blk, valid, lambda s: score[:, s:s + 1]).astype(BF16)

    nbp = -(-n_blocks // SUBLANES) * SUBLANES
    tiles = rows // LANES
    imp_t = jnp.concatenate([imp[i * LANES:(i + 1) * LANES].T[:nbp] for i in range(tiles)], axis=1)
    tpos = jnp.concatenate([t0 + _lane_iota((1, tq))] * (rows // tq), axis=1)
    blk = _row_iota((nbp, rows))
    cur = tpos >> shift
    forced = (blk == 0) | (blk == cur) | (blk == cur - 1)
    valid = (blk * L_SEL <= tpos) & (blk < n_blocks)
    score = jnp.where(valid, jnp.where(forced, FORCE, imp_t), -FORCE)
    chosen_t = ranked(score, blk, valid, lambda s: score[s:s + 1, :])
    chosen_t = jnp.concatenate([chosen_t, jnp.zeros((LANES - nbp, rows), F32)], axis=0)
    return jnp.concatenate([chosen_t[:, i * LANES:(i + 1) * LANES].T for i in range(tiles)],
                           axis=0).astype(BF16)


def _gate_and_store(gates, zs, branch_outs, o_ref, row0=0):
    tq = gates.shape[0]
    lane = _lane_iota((tq, LANES))
    ext = []
    for g in range(NSA_KV_HEADS):
        o_cmp, o_sel, o_win = branch_outs[g]
        for hh in range(NSA_HPG):
            h = g * NSA_HPG + hh
            rows = slice(hh * tq, (hh + 1) * tq)
            o = (gates[:, 3 * h:3 * h + 1] * o_cmp[rows] + gates[:, 3 * h + 1:3 * h + 2] * o_sel[rows]
                 + gates[:, 3 * h + 2:3 * h + 3] * o_win[rows])
            ext.append(_swap_halves(o) if h % 2 != g else o)
    for j in range(NSA_HEADS // 2):
        both = jnp.where(lane < NSA_DH, ext[2 * j], ext[2 * j + 1])
        o_ref[row0:row0 + tq, j * LANES:(j + 1) * LANES] = both * zs[:, j * LANES:(j + 1) * LANES]


def _nsa_prompt_kernel(q_ref, g_ref, z_ref, cmp_ref, sel_ref, win_ref, ov_ref, ex_ref, o_ref, *, n_blocks):
    i = pl.program_id(1)
    tq = q_ref.shape[0]
    q = q_ref[...]
    t0 = i * tq
    tpos = t0 + _row_iota((tq, 1))
    tpos_all = jnp.concatenate([tpos] * NSA_HEADS, axis=0)
    grp_rows = NSA_HPG * tq
    qs = jnp.concatenate([_stack_group_queries(q, g) for g in range(NSA_KV_HEADS)], axis=0)

    cmp_blk = cmp_ref[0].astype(BF16)
    cmp_ok = (_lane_iota((1, cmp_blk.shape[0])) * CMP_STRIDE + (L_CMP - 1)) <= tpos_all
    p1 = _masked_softmax(_dot_nt(qs, cmp_blk[:, :LANES]), cmp_ok)
    o_cmp = _dot(p1.astype(BF16), cmp_blk[:, LANES:])
    psum = jnp.concatenate([_sum_heads(p1[g * grp_rows:(g + 1) * grp_rows], tq)
                            for g in range(NSA_KV_HEADS)], axis=0)
    chosen = _choose_blocks(psum, t0, tq, ov_ref[...], n_blocks)

    def sel_chunk(c, carry):
        m, l, acc = carry
        k0 = pl.multiple_of(c * SEL_CHUNK, SEL_CHUNK)
        k_t = sel_ref[0, 0:LANES, pl.ds(k0, SEL_CHUNK)].astype(BF16)
        v_t = sel_ref[0, LANES:2 * LANES, pl.ds(k0, SEL_CHUNK)].astype(BF16)
        causal = jnp.concatenate([(k0 + _lane_iota((1, SEL_CHUNK))) <= tpos] * NSA_KV_HEADS, axis=0)
        key_on = (_dot(chosen, ex_ref[:, pl.ds(k0, SEL_CHUNK)]) > 0.5) & causal
        bias = jnp.where(key_on, 0.0, NEG_INF)
        bias = jnp.concatenate([bias[g * tq:(g + 1) * tq] for g in range(NSA_KV_HEADS)
                                for _ in range(NSA_HPG)], axis=0)
        s = _dot(qs, k_t) + bias
        m_new = jnp.maximum(m, jnp.max(s, axis=-1, keepdims=True))
        alpha = jnp.exp(m - m_new)
        e = jnp.exp(s - m_new)
        l_new = alpha * l + jnp.sum(e, axis=-1, keepdims=True)
        return m_new, l_new, alpha * acc + _dot_nt(e.astype(BF16), v_t)

    rows = NSA_HEADS * tq
    init = (jnp.full((rows, 1), NEG_INF, F32), jnp.zeros((rows, 1), F32), jnp.zeros((rows, LANES), F32))
    n_chunks = t0 // SEL_CHUNK + 1
    _, l2, acc2 = lax.fori_loop(0, n_chunks, sel_chunk, init)
    o_sel = acc2 / l2

    w0 = pl.multiple_of(jnp.maximum(t0 - WINDOW, 0), KEY_TILE)
    wk_t = win_ref[0, 0:LANES, pl.ds(w0, WIN_KEYS)].astype(BF16)
    wv_t = win_ref[0, LANES:2 * LANES, pl.ds(w0, WIN_KEYS)].astype(BF16)
    win_kpos = w0 + _lane_iota((1, WIN_KEYS))
    win_ok = (win_kpos <= tpos_all) & (win_kpos > tpos_all - WINDOW)
    s3 = jnp.where(win_ok, _dot(qs, wk_t), NEG_INF)
    e3 = jnp.exp(s3 - jnp.max(s3, axis=-1, keepdims=True))
    o_win = _dot_nt(e3.astype(BF16), wv_t) / jnp.sum(e3, axis=-1, keepdims=True)

    branch_outs = [(o_cmp[g * grp_rows:(g + 1) * grp_rows], o_sel[g * grp_rows:(g + 1) * grp_rows],
                    o_win[g * grp_rows:(g + 1) * grp_rows]) for g in range(NSA_KV_HEADS)]
    _gate_and_store(g_ref[...], z_ref[...], branch_outs, o_ref)


def _nsa_prompt(qn, gates, zb, cmp_kv, full_t, win_t, overlap, expand, batch, seq):
    tq = PROMPT_Q_TILE
    nq = seq // tq
    tok = lambda b, i: (b * nq + i, 0)
    per_b = lambda b, i: (b, 0, 0)
    sel_half = lambda b, i: (b, 1, 0)
    fixed = lambda b, i: (0, 0)
    n_blocks = -(-seq // L_SEL)
    return pl.pallas_call(
        functools.partial(_nsa_prompt_kernel, n_blocks=n_blocks),
        grid=(batch, nq),
        in_specs=[
            pl.BlockSpec((tq, B_WIDTH), tok),
            pl.BlockSpec((tq, LANES), tok),
            pl.BlockSpec((tq, B_WIDTH), tok),
            pl.BlockSpec((1, cmp_kv.shape[1], 2 * LANES), per_b),
            pl.BlockSpec((1, 2 * LANES, seq), sel_half),
            pl.BlockSpec((1, WIN_ROW, seq), per_b),
            pl.BlockSpec(overlap.shape, fixed),
            pl.BlockSpec(expand.shape, fixed),
        ],
        out_specs=pl.BlockSpec((tq, B_WIDTH), tok),
        out_shape=jax.ShapeDtypeStruct((batch * seq, B_WIDTH), F32),
        compiler_params=_cparams("parallel", "arbitrary"),
        name="nsa_prompt",
    )(qn, gates, zb, cmp_kv, full_t, win_t, overlap, expand)


def _nsa_sample_kernel(*refs, nb, n_pages, past_len, n_blocks):
    q_ref, g_ref, z_ref, newf_ref, neww_ref, winp_ref = refs[1:7]
    pages = refs[7:7 + nb * n_pages]
    pe_ref, w_ref, ov_ref, ex_ref, o_ref, wout_ref, xs_ref = refs[7 + nb * n_pages:]
    tq = q_ref.shape[0] // nb
    w_buf = winp_ref.shape[3]
    halves_per_page = PAGE_SIZE // CMP_STRIDE
    n_half = past_len // CMP_STRIDE
    total_half = nb * n_half

    pad = jnp.zeros((KEY_TILE - tq, LANES), F32)
    lane = _lane_iota((LANES, LANES))
    per_req = NSA_HEADS * tq
    tpos = past_len + _row_iota((tq, 1))
    tpos_all = jnp.concatenate([tpos] * (nb * NSA_HEADS), axis=0)
    tpos_grp = jnp.concatenate([tpos] * (nb * NSA_KV_HEADS), axis=0)

    def padded_rows(x):
        return jnp.concatenate([x, pad], axis=0)

    def per_request(fn):
        return jnp.concatenate([fn(j, slice(j * per_req, (j + 1) * per_req)) for j in range(nb)], axis=0)

    tok = [slice(j * tq, (j + 1) * tq) for j in range(nb)]
    qs = jnp.concatenate([_stack_group_queries(q_ref[tok[j], :], g)
                          for j in range(nb) for g in range(NSA_KV_HEADS)], axis=0)
    mine = [pages[j * n_pages:(j + 1) * n_pages] for j in range(nb)]
    newf = [newf_ref[tok[j], :] for j in range(nb)]
    neww = [neww_ref[tok[j], :] for j in range(nb)]

    sel_pieces = [[] for _ in range(nb)]
    for j in range(nb * n_pages):
        for c in range(2):
            x = pages[j][0, c].T
            for hb in range(halves_per_page):
                slot = (j * halves_per_page + hb) * CMP_SLOT
                xs_ref[c, slot:slot + CMP_STRIDE, :] = x[hb * CMP_STRIDE:(hb + 1) * CMP_STRIDE]
        req = j // n_pages
        sel_pieces[req].append(_dot(qs[req * per_req:(req + 1) * per_req], pages[j][0, K_SEL].astype(BF16)))
    s2_raw = per_request(lambda j, r: jnp.concatenate(
        sel_pieces[j] + [_dot_nt(qs[r], padded_rows(newf[j][:, 2 * LANES:3 * LANES]).astype(BF16))], axis=1))
    s3_raw = per_request(lambda j, r: jnp.concatenate(
        [_dot(qs[r], winp_ref[j, 0].astype(BF16)),
         _dot_nt(qs[r], padded_rows(neww[j][:, :LANES]).astype(BF16))], axis=1))

    for j in range(nb):
        for c in range(2):
            shifted = pltpu.roll(winp_ref[j, c], w_buf - tq, axis=1)
            new_cols = pltpu.roll(padded_rows(neww[j][:, c * LANES:(c + 1) * LANES]).T, LANES - tq, axis=1)
            wout_ref[j, c, :, 0:w_buf - LANES] = shifted[:, 0:w_buf - LANES]
            wout_ref[j, c, :, w_buf - LANES:w_buf] = jnp.where(lane >= LANES - tq, new_cols,
                                                               shifted[:, w_buf - LANES:w_buf])

    acc = [jnp.zeros((total_half, 2 * LANES), F32) for _ in range(CMP_RATIO)]
    for l in range(CMP_STRIDE):
        x = jnp.concatenate([xs_ref[c, pl.ds(l, total_half, stride=CMP_SLOT), :] for c in range(2)], axis=1)
        for r in range(CMP_RATIO):
            i = r * CMP_STRIDE + l
            acc[r] = acc[r] + _dot((x + pe_ref[i:i + 1, :]).astype(BF16), w_ref[i])
    cmp_all = acc[0]
    for r in range(1, CMP_RATIO):
        cmp_all = cmp_all + pltpu.roll(acc[r], total_half - r, axis=0)
    live = _row_iota((n_half, 2 * LANES)) < n_half - CMP_RATIO + 1
    cmp_blk = [jnp.where(live, cmp_all[j * n_half:(j + 1) * n_half], 0.0).astype(BF16) for j in range(nb)]

    cmp_ok = (_lane_iota((1, n_half)) * CMP_STRIDE + (L_CMP - 1)) <= tpos_all
    p1 = _masked_softmax(per_request(lambda j, r: _dot_nt(qs[r], cmp_blk[j][:, :LANES])), cmp_ok)
    p1b = p1.astype(BF16)
    o_cmp = per_request(lambda j, r: _dot(p1b[r], cmp_blk[j][:, LANES:]))

    grp_rows = NSA_HPG * tq
    psum = jnp.concatenate([_sum_heads(p1[i * grp_rows:(i + 1) * grp_rows], tq)
                            for i in range(nb * NSA_KV_HEADS)], axis=0)
    chosen = _choose_blocks(psum, past_len, tq, ov_ref[...], n_blocks)
    key_on = (_dot(chosen, ex_ref[...]) > 0.5) & (_lane_iota((1, past_len + KEY_TILE)) <= tpos_grp)
    bias = jnp.where(key_on, 0.0, NEG_INF)
    bias = jnp.concatenate([bias[i * tq:(i + 1) * tq] for i in range(nb * NSA_KV_HEADS)
                            for _ in range(NSA_HPG)], axis=0)

    s2 = s2_raw + bias
    e2 = jnp.exp(s2 - jnp.max(s2, axis=-1, keepdims=True))
    l2 = jnp.sum(e2, axis=-1, keepdims=True)
    e2 = e2.astype(BF16)

    def sel_values(j, r):
        v_t = jnp.concatenate([p[0, V_SEL].astype(BF16) for p in mine[j]], axis=1)
        v_new = padded_rows(newf[j][:, 3 * LANES:4 * LANES]).astype(BF16)
        return _dot_nt(e2[r, :past_len], v_t) + _dot(e2[r, past_len:], v_new)

    o_sel = per_request(sel_values) / l2

    win_kpos = (past_len - w_buf) + _lane_iota((1, w_buf + KEY_TILE))
    win_ok = (win_kpos <= tpos_all) & (win_kpos > tpos_all - WINDOW)
    s3 = jnp.where(win_ok, s3_raw, NEG_INF)
    e3 = jnp.exp(s3 - jnp.max(s3, axis=-1, keepdims=True))
    l3 = jnp.sum(e3, axis=-1, keepdims=True)
    e3 = e3.astype(BF16)
    o_win = per_request(lambda j, r: _dot_nt(e3[r, :w_buf], winp_ref[j, 1].astype(BF16))
                        + _dot(e3[r, w_buf:], padded_rows(neww[j][:, LANES:]).astype(BF16))) / l3

    for j in range(nb):
        branch_outs = []
        for g in range(NSA_KV_HEADS):
            r = slice((j * NSA_KV_HEADS + g) * grp_rows, (j * NSA_KV_HEADS + g + 1) * grp_rows)
            branch_outs.append((o_cmp[r], o_sel[r], o_win[r]))
        _gate_and_store(g_ref[tok[j], :], z_ref[tok[j], :], branch_outs, o_ref, row0=j * tq)


def _nsa_sample(page_table, qn, gates, zb, full_new, win_new, win_t, cache_t, pe, w_bd, overlap, expand,
                batch, tq, n_pages):
    nb = SAMPLE_REQS_PER_STEP
    past_len = n_pages * PAGE_SIZE
    w_buf = win_t.shape[3]
    n_blocks = -(-(past_len + tq) // L_SEL)
    tok = lambda b, pt: (b, 0)
    per_b = lambda b, pt: (b, 0, 0, 0)
    fixed2 = lambda b, pt: (0, 0)
    fixed3 = lambda b, pt: (0, 0, 0)

    def page_spec(j, p):
        return pl.BlockSpec((1, N_FULL_KV, LANES, PAGE_SIZE), lambda b, pt: (pt[b * nb + j, p], 0, 0, 0))

    n_slots = nb * n_pages * (PAGE_SIZE // CMP_STRIDE)
    return pl.pallas_call(
        functools.partial(_nsa_sample_kernel, nb=nb, n_pages=n_pages, past_len=past_len, n_blocks=n_blocks),
        grid_spec=pltpu.PrefetchScalarGridSpec(
            num_scalar_prefetch=1,
            grid=(batch // nb,),
            in_specs=[
                pl.BlockSpec((nb * tq, B_WIDTH), tok),
                pl.BlockSpec((nb * tq, LANES), tok),
                pl.BlockSpec((nb * tq, B_WIDTH), tok),
                pl.BlockSpec((nb * tq, KV_ROW), tok),
                pl.BlockSpec((nb * tq, WIN_ROW), tok),
                pl.BlockSpec((nb, N_WIN_KV, LANES, w_buf), per_b),
            ] + [page_spec(j, p) for j in range(nb) for p in range(n_pages)] + [
                pl.BlockSpec(pe.shape, fixed2),
                pl.BlockSpec(w_bd.shape, fixed3),
                pl.BlockSpec(overlap.shape, fixed2),
                pl.BlockSpec(expand.shape, fixed2),
            ],
            out_specs=[
                pl.BlockSpec((nb * tq, B_WIDTH), tok),
                pl.BlockSpec((nb, N_WIN_KV, LANES, w_buf), per_b),
            ],
            scratch_shapes=[pltpu.VMEM((2, n_slots * CMP_SLOT, LANES), F32)],
        ),
        out_shape=[
            jax.ShapeDtypeStruct((batch * tq, B_WIDTH), F32),
            jax.ShapeDtypeStruct((batch, N_WIN_KV, LANES, w_buf), F32),
        ],
        compiler_params=_cparams("parallel"),
        name="nsa_sample",
    )(page_table, qn, gates, zb, full_new, win_new, win_t, *([cache_t] * (nb * n_pages)), pe, w_bd,
      overlap, expand)


def _selection_tables(n_cmp_rows, n_sel_keys):
    n = jnp.arange(n_cmp_rows, dtype=jnp.int32)[:, None]
    s = jnp.arange(LANES, dtype=jnp.int32)[None, :]
    c_start = n * CMP_STRIDE
    s_start = s * L_SEL
    overlap = ((c_start < s_start + L_SEL) & (s_start < c_start + L_CMP)).astype(BF16)
    key = jnp.arange(n_sel_keys, dtype=jnp.int32)[None, :]
    expand = ((key // L_SEL) == jnp.arange(LANES, dtype=jnp.int32)[:, None]).astype(BF16)
    return overlap, expand


def _even_out_kernel(x_ref, oa_ref, ob_ref, wa_ref, wb_ref, *rest, with_next):
    y = (x_ref[...] + _dot(oa_ref[...].astype(BF16), wa_ref[...])
         + _dot(ob_ref[...].astype(BF16), wb_ref[...]))
    if not with_next:
        (y_ref,) = rest
        y_ref[...] = y
        return
    g_ref, w_ref, y_ref, u_ref, z_ref = rest
    y_ref[...] = y
    hb = _rmsnorm_rows(y, g_ref[...]).astype(BF16)
    e = u_ref.shape[1]
    u_ref[...] = _dot(hb, w_ref[:, :e])
    z_ref[...] = jax.nn.silu(_dot(hb, w_ref[:, e:]))


def _even_out(x2d, oa, ob, wa, wb, next_odd=None):
    m, d = x2d.shape
    row = lambda i: (i, 0)
    fixed = lambda i: (0, 0)
    in_specs = [
        pl.BlockSpec((ROW_TILE, d), row),
        pl.BlockSpec((ROW_TILE, A_WIDTH), row),
        pl.BlockSpec((ROW_TILE, B_WIDTH), row),
        pl.BlockSpec((A_WIDTH, d), fixed),
        pl.BlockSpec((B_WIDTH, d), fixed),
    ]
    out_specs = [pl.BlockSpec((ROW_TILE, d), row)]
    out_shape = [jax.ShapeDtypeStruct((m, d), F32)]
    args = [x2d, oa, ob, wa, wb]
    if next_odd is not None:
        gain, w_bf = next_odd
        e = w_bf.shape[1] // 2
        in_specs += [pl.BlockSpec((1, d), fixed), pl.BlockSpec((d, 2 * e), fixed)]
        out_specs += [pl.BlockSpec((ROW_TILE, e), row)] * 2
        out_shape += [jax.ShapeDtypeStruct((m, e), F32)] * 2
        args += [gain, w_bf]
    return pl.pallas_call(
        functools.partial(_even_out_kernel, with_next=next_odd is not None),
        grid=(m // ROW_TILE,),
        in_specs=in_specs,
        out_specs=out_specs,
        out_shape=out_shape,
        compiler_params=_cparams("parallel"),
        name="even_out_proj",
    )(*args)


def _s5_kernel(u_ref, x0r_ref, x0i_ref, ar_ref, ai_ref, bm_ref, cm_ref, d_ref,
               y_ref, fr_ref, fi_ref, st_re, st_im, ubuf_a, ubuf_b, xbuf_a, xbuf_b, ybuf, *, nb, tt):
    t_idx = pl.program_id(1)
    n_blk = bm_ref.shape[0]
    half = S5_BLOCK_STATE
    n_chunk = half // LANES
    pitch = tt + S5_ROW_PAD
    ubufs, xbufs = (ubuf_a, ubuf_b), (xbuf_a, xbuf_b)
    wide = S5_MXU_COLS // LANES
    n_slices = 2 * n_chunk // wide

    @pl.when(t_idx == 0)
    def _():
        for kb in range(n_blk):
            st_re[kb] = x0r_ref[:, kb * half:(kb + 1) * half]
            st_im[kb] = x0i_ref[:, kb * half:(kb + 1) * half]

    for ubuf in ubufs:
        for b in range(nb):
            ubuf[b * pitch + tt:(b + 1) * pitch, :] = jnp.zeros((S5_ROW_PAD, LANES), F32)

    def lanes_of(kb):
        return slice(kb * LANES, (kb + 1) * LANES)

    def stage_u(kb):
        for b in range(nb):
            ubufs[kb % 2][b * pitch:b * pitch + tt, :] = u_ref[b, :, lanes_of(kb)]

    def input_slice(kb, j):
        cols = slice(j * S5_MXU_COLS, (j + 1) * S5_MXU_COLS)
        bu = _dot(ubufs[kb % 2][...].astype(BF16), bm_ref[kb, :, cols])
        for i in range(wide):
            xbufs[kb % 2][j * wide + i] = bu[:, i * LANES:(i + 1) * LANES]

    def output_slice(kb, j, acc):
        x = jnp.concatenate([xbufs[kb % 2][j * wide + i] for i in range(wide)], axis=1).astype(BF16)
        part = _dot(x, cm_ref[kb, j * S5_MXU_COLS:(j + 1) * S5_MXU_COLS, :])
        return part if acc is None else acc + part

    def finish_y(kb, acc):
        ybuf[...] = acc + d_ref[:, lanes_of(kb)] * ubufs[kb % 2][...]
        for b in range(nb):
            y_ref[b, :, lanes_of(kb)] = ybuf[b * pitch:b * pitch + tt, :]

    stage_u(0)
    for j in range(n_slices):
        input_slice(0, j)
    for kb in range(n_blk + 1):
        side, acc = [], [None]
        if kb >= 1:
            def out_task(j, kb=kb):
                acc[0] = output_slice(kb - 1, j, acc[0])
            side += [functools.partial(out_task, j) for j in range(n_slices)]
            side.append(lambda kb=kb: finish_y(kb - 1, acc[0]))
        if kb + 1 < n_blk:
            side.append(functools.partial(stage_u, kb + 1))
            side += [functools.partial(input_slice, kb + 1, j) for j in range(n_slices)]
        if kb == n_blk:
            for task in side:
                task()
            break
        xbuf = xbufs[kb % 2]
        a_re = jnp.broadcast_to(ar_ref[kb], (nb, half))
        a_im = jnp.broadcast_to(ai_ref[kb], (nb, half))
        s_re, s_im = st_re[kb], st_im[kb]
        every = -(-tt // (len(side) + 1))
        for t in range(tt):
            if t % every == every - 1 and side:
                side.pop(0)()
            rows = pl.ds(t, nb, stride=pitch)
            b_re = jnp.concatenate([xbuf[j, rows, :] for j in range(n_chunk)], axis=1)
            b_im = jnp.concatenate([xbuf[n_chunk + j, rows, :] for j in range(n_chunk)], axis=1)
            s_re, s_im = a_re * s_re - a_im * s_im + b_re, a_re * s_im + a_im * s_re + b_im
            for j in range(n_chunk):
                xbuf[j, rows, :] = s_re[:, j * LANES:(j + 1) * LANES]
                xbuf[n_chunk + j, rows, :] = s_im[:, j * LANES:(j + 1) * LANES]
        for task in side:
            task()
        st_re[kb] = s_re
        st_im[kb] = s_im

    @pl.when(t_idx == pl.num_programs(1) - 1)
    def _():
        for kb in range(n_blk):
            fr_ref[:, kb * half:(kb + 1) * half] = st_re[kb]
            fi_ref[:, kb * half:(kb + 1) * half] = st_im[kb]


def _s5(u3, x0_re, x0_im, a_re, a_im, bmat, cmat, d_row, tt):
    batch, seq, e = u3.shape
    nb = SUBLANES
    n_blk = e // LANES
    n_state = n_blk * S5_BLOCK_STATE
    seq_map = lambda b, t: (b, t, 0)
    st_map = lambda b, t: (b, 0)
    fixed2 = lambda b, t: (0, 0)
    fixed3 = lambda b, t: (0, 0, 0)
    return pl.pallas_call(
        functools.partial(_s5_kernel, nb=nb, tt=tt),
        grid=(batch // nb, seq // tt),
        in_specs=[
            pl.BlockSpec((nb, tt, e), seq_map),
            pl.BlockSpec((nb, n_state), st_map),
            pl.BlockSpec((nb, n_state), st_map),
            pl.BlockSpec((n_blk, 1, S5_BLOCK_STATE), fixed3),
            pl.BlockSpec((n_blk, 1, S5_BLOCK_STATE), fixed3),
            pl.BlockSpec((n_blk, LANES, 2 * S5_BLOCK_STATE), fixed3),
            pl.BlockSpec((n_blk, 2 * S5_BLOCK_STATE, LANES), fixed3),
            pl.BlockSpec((1, e), fixed2),
        ],
        out_specs=[
            pl.BlockSpec((nb, tt, e), seq_map),
            pl.BlockSpec((nb, n_state), st_map),
            pl.BlockSpec((nb, n_state), st_map),
        ],
        out_shape=[
            jax.ShapeDtypeStruct((batch, seq, e), F32),
            jax.ShapeDtypeStruct((batch, n_state), F32),
            jax.ShapeDtypeStruct((batch, n_state), F32),
        ],
        scratch_shapes=[
            pltpu.VMEM((n_blk, nb, S5_BLOCK_STATE), F32),
            pltpu.VMEM((n_blk, nb, S5_BLOCK_STATE), F32),
            pltpu.VMEM((nb * (tt + S5_ROW_PAD), LANES), F32),
            pltpu.VMEM((nb * (tt + S5_ROW_PAD), LANES), F32),
            pltpu.VMEM((2 * S5_BLOCK_STATE // LANES, nb * (tt + S5_ROW_PAD), LANES), F32),
            pltpu.VMEM((2 * S5_BLOCK_STATE // LANES, nb * (tt + S5_ROW_PAD), LANES), F32),
            pltpu.VMEM((nb * (tt + S5_ROW_PAD), LANES), F32),
        ],
        compiler_params=_cparams("parallel", "arbitrary"),
        name="s5_scan",
    )(u3, x0_re, x0_im, a_re, a_im, bmat, cmat, d_row)


def _s5_params(lam_re, lam_im, b_re, b_im, c_re, c_im, log_step):
    n_groups = lam_re.shape[0]
    n_blk = n_groups // S5_LANE_GROUPS
    dt = jnp.exp(log_step)[:, None]
    mag = jnp.exp(lam_re * dt)
    ang = lam_im * dt
    ab_re, ab_im = mag * jnp.cos(ang), mag * jnp.sin(ang)
    den = lam_re * lam_re + lam_im * lam_im
    nr = ab_re - 1.0
    f_re = (nr * lam_re + ab_im * lam_im) / den
    f_im = (ab_im * lam_re - nr * lam_im) / den
    bb_re = f_re[..., None] * b_re - f_im[..., None] * b_im
    bb_im = f_re[..., None] * b_im + f_im[..., None] * b_re
    eye = jnp.eye(S5_LANE_GROUPS, dtype=lam_re.dtype)

    def in_map(bb):
        bb = bb.reshape(n_blk, S5_LANE_GROUPS, S5_STATE, S5_GROUP)
        m = jnp.einsum('kgpc,gh->kgchp', bb, eye)
        return m.reshape(n_blk, LANES, S5_BLOCK_STATE)

    def out_map(cc):
        cc = cc.reshape(n_blk, S5_LANE_GROUPS, S5_GROUP, S5_STATE)
        m = jnp.einsum('kgcp,gh->kgphc', cc, eye)
        return m.reshape(n_blk, S5_BLOCK_STATE, LANES)

    bmat = jnp.concatenate([in_map(bb_re), in_map(bb_im)], axis=2).astype(BF16)
    cmat = jnp.concatenate([out_map(c_re), out_map(-c_im)], axis=1).astype(BF16)
    a_re = ab_re.reshape(n_blk, 1, S5_BLOCK_STATE)
    a_im = ab_im.reshape(n_blk, 1, S5_BLOCK_STATE)
    return a_re, a_im, bmat, cmat


def _odd_out_kernel(x_ref, y_ref, z_ref, w1_ref, w2_ref, wo_ref, o_ref):
    yb = jax.nn.gelu(y_ref[...]).astype(BF16)
    t = _dot(yb, w1_ref[...]) * jax.nn.sigmoid(_dot(yb, w2_ref[...])) * z_ref[...]
    o_ref[...] = x_ref[...] + _dot(t.astype(BF16), wo_ref[...])


def _odd_out(x2d, y2d, z2d, w1, w2, wo):
    m, d = x2d.shape
    e = y2d.shape[1]
    row = lambda i: (i, 0)
    fixed = lambda i: (0, 0)
    return pl.pallas_call(
        _odd_out_kernel,
        grid=(m // ROW_TILE,),
        in_specs=[
            pl.BlockSpec((ROW_TILE, d), row),
            pl.BlockSpec((ROW_TILE, e), row),
            pl.BlockSpec((ROW_TILE, e), row),
            pl.BlockSpec((e, e), fixed),
            pl.BlockSpec((e, e), fixed),
            pl.BlockSpec((e, d), fixed),
        ],
        out_specs=pl.BlockSpec((ROW_TILE, d), row),
        out_shape=jax.ShapeDtypeStruct((m, d), F32),
        compiler_params=_cparams("parallel"),
        name="odd_out_proj",
    )(x2d, y2d, z2d, w1, w2, wo)


def _rope_tables(pos):
    half = NSA_DH // 2
    inv = ROPE_THETA ** (-jnp.arange(half, dtype=F32) / half)
    ang = pos.astype(F32)[:, None] * inv[None, :]
    cos, sin = jnp.cos(ang), jnp.sin(ang)
    reps = LANES // NSA_DH
    return jnp.tile(cos, (1, 2 * reps)), jnp.tile(jnp.concatenate([-sin, sin], axis=1), (1, reps))


def _even_weights(norm_g, w_in, w_out, q_norm, k_norm, cmp_pos, cmp_w):
    assert w_in.shape[1] == ZB_COL0 + B_WIDTH
    w_bf = w_in.astype(BF16)
    w_zb = w_bf[:, ZB_COL0:]
    reps = LANES // NSA_DH
    qg = jnp.tile(q_norm[None, :], (1, reps))
    kg = jnp.tile(k_norm, (1, reps))
    cw = cmp_w.astype(BF16)
    row_blocks = []
    for c in range(2):
        for g in range(NSA_KV_HEADS):
            r0 = (c * NSA_KV_HEADS + g) * NSA_DH
            row_blocks.append(jnp.pad(cw[c], ((0, 0), (0, 0), (r0, 2 * LANES - NSA_DH - r0))))
    w_bd = jnp.concatenate(row_blocks, axis=1)
    pe = jnp.broadcast_to(cmp_pos.transpose(1, 0, 2)[:, :, None, :], (L_CMP, 2, NSA_KV_HEADS, NSA_DH))
    pe = pe.reshape(L_CMP, 2 * LANES)
    wa = w_out[:A_WIDTH].astype(BF16)
    wb = w_out[A_WIDTH:].astype(BF16)
    return norm_g[None, :], w_bf, w_zb, qg, kg, w_bd, pe, wa, wb


def _even_layer(x, pos0, s_ret, ew, gn_gain, next_odd, cache=None, page_table=None, win_past=None):
    batch, seq, d = x.shape
    gain, w_bf, w_zb, qg, kg, w_bd, pe, wa, wb = ew
    x2d = x.reshape(batch * seq, d)
    pos = pos0 + jnp.arange(seq, dtype=jnp.int32)
    cos_t, sin_t = _rope_tables(pos)
    if seq >= EVEN_IN_TILE:
        n_table_blocks = seq // EVEN_IN_TILE
    else:
        cos_t = jnp.tile(cos_t, (EVEN_IN_TILE // seq, 1))
        sin_t = jnp.tile(sin_t, (EVEN_IN_TILE // seq, 1))
        n_table_blocks = 1
    g, dh = NSA_KV_HEADS, NSA_DH
    if cache is None:
        qa, ka, va, za, qn, full_t, win_t, zb, gates, cmp_rows = _even_in(
            x2d, gain, w_bf, w_zb, cos_t, sin_t, qg, kg, n_table_blocks, seq_tiles=seq // EVEN_IN_TILE)
        oa, s_fin = _retention(qa, ka, va, za, gn_gain[None, :], s_ret, batch, seq)
        cmp_kv = _compress(cmp_rows.reshape(batch, seq, 2 * LANES), pe, w_bd)
        overlap, expand = _selection_tables(seq // CMP_STRIDE, seq)
        ob = _nsa_prompt(qn, gates, zb, cmp_kv, full_t, win_t, overlap, expand, batch, seq)
        keep = min(WINDOW, seq)
        y, *uz = _even_out(x2d, oa, ob, wa, wb, next_odd)
        full_rows = full_t.reshape(batch, N_FULL_KV, g, dh, seq).transpose(0, 4, 1, 2, 3)
        win_rows = win_t[:, :, seq - keep:].reshape(batch, N_WIN_KV, g, dh, keep).transpose(0, 4, 1, 2, 3)
        return y.reshape(batch, seq, d), s_fin, full_rows, win_rows, uz
    else:
        qa, ka, va, za, qn, full_new, win_new, zb, gates = _even_in(x2d, gain, w_bf, w_zb, cos_t, sin_t, qg, kg,
                                                                     n_table_blocks)
        oa, s_fin = _retention(qa, ka, va, za, gn_gain[None, :], s_ret, batch, seq)
        n_pages = page_table.shape[1]
        past_len = n_pages * PAGE_SIZE
        overlap, expand = _selection_tables(past_len // CMP_STRIDE, past_len + KEY_TILE)
        ob, win_out_t = _nsa_sample(page_table, qn, gates, zb, full_new, win_new, win_past, cache, pe, w_bd,
                                    overlap, expand, batch, seq, n_pages)
    y, *uz = _even_out(x2d, oa, ob, wa, wb, next_odd)
    w_buf = win_out_t.shape[3]
    full_t = lax.optimization_barrier(full_new.reshape(batch, seq, KV_ROW).transpose(1, 2, 0))
    full_rows = full_t.reshape(seq, N_FULL_KV, g, dh, batch).transpose(4, 0, 1, 2, 3)
    return (y.reshape(batch, seq, d), s_fin, full_rows,
            win_out_t.reshape(batch, N_WIN_KV, g, dh, w_buf).transpose(0, 4, 1, 2, 3), uz)


def _odd_layer(x, uz, s_re, s_im, s5p, d_row, w1, w2, wo, tt):
    batch, seq, d = x.shape
    x2d = x.reshape(batch * seq, d)
    u, zs = uz
    e = u.shape[1]
    a_re, a_im, bmat, cmat = s5p
    n_groups, n_state = s_re.shape[1], s_re.shape[2]
    y, f_re, f_im = _s5(u.reshape(batch, seq, e), s_re.reshape(batch, n_groups * n_state),
                        s_im.reshape(batch, n_groups * n_state), a_re, a_im, bmat, cmat, d_row, tt)
    out = _odd_out(x2d, y.reshape(batch * seq, e), zs, w1, w2, wo).reshape(batch, seq, d)
    return out, f_re.reshape(batch, n_groups, n_state), f_im.reshape(batch, n_groups, n_state)


def kernel(x_prompt, x_sample, cache_nsa_kv, cache_nsa_win, state_ret, state_ssm_re, state_ssm_im, page_table,
           norm_even, w_in_even, w_out_even, ret_gn_gain, nsa_q_norm, nsa_k_norm, nsa_cmp_pos, nsa_cmp_w,
           norm_odd, w_in_odd, ssm_lambda_re, ssm_lambda_im, ssm_b_re, ssm_b_im, ssm_c_re, ssm_c_im, ssm_d,
           ssm_log_step, glu_w1, glu_w2, w_out_odd):
    bp, seq_p, _ = x_prompt.shape
    db, seq_s, _ = x_sample.shape
    n_pages = page_table.shape[1]
    past_len = n_pages * PAGE_SIZE
    depth = norm_even.shape[0] + norm_odd.shape[0]
    yp, ys = x_prompt, x_sample
    ret_p, ret_s, kv_p, kv_s, win_p, win_s = [], [], [], [], [], []
    sre_p, sim_p, sre_s, sim_s = [], [], [], []
    for layer in range(depth):
        li = layer // 2
        if layer % 2 == 0:
            ew = _even_weights(norm_even[li], w_in_even[li], w_out_even[li], nsa_q_norm[li], nsa_k_norm[li],
                               nsa_cmp_pos[li], nsa_cmp_w[li])
            next_odd = None
            if layer + 1 < depth:
                next_odd = (norm_odd[li][None, :], w_in_odd[li].astype(BF16))
            s0 = jnp.zeros((bp, RET_HEADS, RET_DK, RET_DV), F32)
            yp, sr, kvr, wr, uz_p = _even_layer(yp, 0, s0, ew, ret_gn_gain[li], next_odd)
            ret_p.append(sr); kv_p.append(kvr); win_p.append(wr)
            cache_t = cache_nsa_kv[li].transpose(0, 2, 3, 4, 1).reshape(
                cache_nsa_kv.shape[1], N_FULL_KV, LANES, PAGE_SIZE)
            win_t = cache_nsa_win[li].transpose(0, 2, 3, 4, 1).reshape(
                db, N_WIN_KV, LANES, cache_nsa_win.shape[2])
            ys, sr2, kvr2, wr2, uz_s = _even_layer(ys, past_len, state_ret[li], ew, ret_gn_gain[li], next_odd,
                                                   cache=cache_t, page_table=page_table, win_past=win_t)
            ret_s.append(sr2); kv_s.append(kvr2); win_s.append(wr2)
        else:
            s5p = _s5_params(ssm_lambda_re[li], ssm_lambda_im[li], ssm_b_re[li], ssm_b_im[li],
                             ssm_c_re[li], ssm_c_im[li], ssm_log_step[li])
            w1, w2, wo = glu_w1[li].astype(BF16), glu_w2[li].astype(BF16), w_out_odd[li].astype(BF16)
            d_row = ssm_d[li][None, :]
            n_groups = ssm_lambda_re.shape[1]
            z0 = jnp.zeros((bp, n_groups, S5_STATE), F32)
            yp, fr, fi = _odd_layer(yp, uz_p, z0, z0, s5p, d_row, w1, w2, wo, min(S5_TIME_TILE, seq_p))
            sre_p.append(fr); sim_p.append(fi)
            ys, fr2, fi2 = _odd_layer(ys, uz_s, state_ssm_re[li], state_ssm_im[li], s5p, d_row,
                                      w1, w2, wo, min(S5_TIME_TILE, seq_s))
            sre_s.append(fr2); sim_s.append(fi2)
    return (yp, ys, jnp.stack(ret_p), jnp.stack(ret_s), jnp.stack(kv_p), jnp.stack(kv_s), jnp.stack(win_p),
            jnp.stack(win_s), jnp.stack(sre_p), jnp.stack(sim_p), jnp.stack(sre_s), jnp.stack(sim_s))
```
